```python
import jax, jax.numpy as jnp
from jax import lax
import numpy as np

D_MODEL = 2048
BATCH = 4
SEQ = 2048
DEPTH = 1
DEC_BATCH = 128
DEC_SEQ = 1
PAST_LEN = 16384
PAGE_SIZE = 128

N_META = 16
RMS_EPS = 1e-6
RW_WIDTH = D_MODEL // 2
RW_HEAD = 64
RW_HEADS = RW_WIDTH // RW_HEAD
RW_DECAY_LORA = 64
RW_AAA_LORA = 64
RW_GATE_LORA = 128
RW_GN_EPS = RW_HEAD * 1e-5
GLA_WIDTH = D_MODEL // 2
GLA_HEADS = 4
GLA_KDIM = GLA_WIDTH // 2
GLA_DK = GLA_KDIM // GLA_HEADS
GLA_DV = GLA_WIDTH // GLA_HEADS
GLA_GATE_LORA = 16
GLA_GATE_NORMALIZER = 16.0
GLA_CHUNK = 64
N_GROUPS = 4
EXPERTS_PER_GROUP = 8
N_EXPERTS = N_GROUPS * EXPERTS_PER_GROUP
EXPERT_TOP_K = 2
D_EXPERT = 512
W_SHIFT = 3 * RW_WIDTH + RW_DECAY_LORA + RW_AAA_LORA + RW_GATE_LORA
SHIFT_SPLITS = (RW_WIDTH, RW_WIDTH + RW_DECAY_LORA, 2 * RW_WIDTH + RW_DECAY_LORA,
                3 * RW_WIDTH + RW_DECAY_LORA, 3 * RW_WIDTH + RW_DECAY_LORA + RW_AAA_LORA)
W_GLA = 2 * GLA_KDIM + 2 * GLA_WIDTH + GLA_GATE_LORA
GLA_SPLITS = (GLA_KDIM, 2 * GLA_KDIM, 2 * GLA_KDIM + GLA_WIDTH, 2 * GLA_KDIM + GLA_WIDTH + GLA_GATE_LORA)
W_IN = W_SHIFT + W_GLA + 2 * D_MODEL

kernel_name = "rwkv7_gla_gated_hmoe_step"

F32 = jnp.float32


def rmsnorm(x, w):
    x32 = x.astype(F32)
    y = x32 * lax.rsqrt(jnp.mean(x32 * x32, -1, keepdims=True) + RMS_EPS) * w.astype(F32)
    return y.astype(x.dtype)


def rwkv7_branch(zs, prev, S0, mu, w0, w2, a0, a2, g2, k_k, k_a, r_k, lnx_w, lnx_b):
    B, L, _ = zs.shape
    shifted = jnp.concatenate([prev[:, None].astype(zs.dtype), zs[:, :-1]], axis=1)
    z = zs + (shifted - zs) * mu
    r, wd, k, v, ad, gd = jnp.split(z, SHIFT_SPLITS, axis=-1)
    w_log = -jax.nn.softplus(-(w0 + jnp.tanh(wd) @ w2).astype(F32)) - 0.5
    decay = jnp.exp(-jnp.exp(w_log))
    a = jax.nn.sigmoid((a0 + ad @ a2).astype(F32))
    g = jax.nn.sigmoid(gd) @ g2
    hd = lambda t: t.astype(F32).reshape(B, L, RW_HEADS, RW_HEAD)
    kk = hd(k * k_k)
    kk = kk * lax.rsqrt(jnp.maximum(jnp.sum(kk * kk, -1, keepdims=True), 1e-24))
    a_h = hd(a)
    k_h = hd(k) * (1.0 + (a_h - 1.0) * k_a.astype(F32).reshape(RW_HEADS, RW_HEAD))
    r_h, v_h, w_h = hd(r), hd(v), hd(decay)

    def step(S, inp):
        r_t, w_t, k_t, v_t, kk_t, b_t = inp
        sa = jnp.einsum('bhvk,bhk->bhv', S, -kk_t)
        S = S * w_t[:, :, None, :] + sa[..., None] * b_t[:, :, None, :] + v_t[..., None] * k_t[:, :, None, :]
        return S, jnp.einsum('bhvk,bhk->bhv', S, r_t)

    xs = tuple(jnp.moveaxis(t, 1, 0) for t in (r_h, w_h, k_h, v_h, kk, kk * a_h))
    S_fin, ys = lax.scan(step, S0.astype(F32), xs)
    y = jnp.moveaxis(ys, 0, 1)
    mean = jnp.mean(y, -1, keepdims=True)
    var = jnp.mean(jnp.square(y - mean), -1, keepdims=True)
    yn = ((y - mean) * lax.rsqrt(var + RW_GN_EPS)).reshape(B, L, RW_WIDTH) * lnx_w.astype(F32) + lnx_b.astype(F32)
    bonus = jnp.sum(r_h * k_h * r_k.astype(F32), -1, keepdims=True) * v_h
    o = (yn + bonus.reshape(B, L, RW_WIDTH)) * g.astype(F32)
    return o.astype(zs.dtype), S_fin.astype(S0.dtype)


def gla_chunked(q, k, v, logg, S0):
    B, L, H, _ = q.shape
    n = L // GLA_CHUNK
    blk = lambda t: t.reshape(B, n, GLA_CHUNK, H, t.shape[-1]).transpose(1, 0, 3, 2, 4)
    mask = jnp.tril(jnp.ones((GLA_CHUNK, GLA_CHUNK), bool))

    def step(S, inp):
        qc, kc, vc, gc = inp
        b = jnp.cumsum(gc, axis=-2)
        qt = qc * jnp.exp(b)
        A = jnp.where(mask, jnp.einsum('bhik,bhjk->bhij', qt, kc * jnp.exp(-b)), 0.0)
        o = jnp.einsum('bhij,bhjv->bhiv', A, vc) + jnp.einsum('bhik,bhkv->bhiv', qt, S)
        bl = b[..., -1:, :]
        S = jnp.exp(bl)[:, :, 0, :, None] * S + jnp.einsum('bhjk,bhjv->bhkv', kc * jnp.exp(bl - b), vc)
        return S, o

    S_fin, o = lax.scan(step, S0, (blk(q), blk(k), blk(v), blk(logg)))
    o = o.transpose(1, 0, 3, 2, 4).reshape(B, L, H, v.shape[-1])
    return o, S_fin


def gla_recurrent(q, k, v, logg, S0):
    def step(S, inp):
        q_t, k_t, v_t, g_t = inp
        S = jnp.exp(g_t)[..., None] * S + k_t[..., None] * v_t[:, :, None, :]
        return S, jnp.einsum('bhk,bhkv->bhv', q_t, S)
    xs = tuple(jnp.moveaxis(t, 1, 0) for t in (q, k, v, logg))
    S_fin, o = lax.scan(step, S0, xs)
    return jnp.moveaxis(o, 0, 1), S_fin


def gla_branch(zg, S0, gk_up, gk_b, norm_w, prompt):
    B, L, _ = zg.shape
    q, k, v, gkd, og = jnp.split(zg, GLA_SPLITS, axis=-1)
    logg = jax.nn.log_sigmoid((gkd @ gk_up + gk_b).astype(F32)) / GLA_GATE_NORMALIZER
    hk = lambda t: t.astype(F32).reshape(B, L, GLA_HEADS, GLA_DK)
    qh = hk(q) * (GLA_DK ** -0.5)
    kh, gh = hk(k), hk(logg)
    vh = v.astype(F32).reshape(B, L, GLA_HEADS, GLA_DV)
    if prompt:
        pad = (-L) % GLA_CHUNK
        pf = lambda t: jnp.pad(t, ((0, 0), (pad, 0), (0, 0), (0, 0)))
        o, S_fin = gla_chunked(pf(qh), pf(kh), pf(vh), pf(gh), S0.astype(F32))
        o = o[:, pad:]
    else:
        o, S_fin = gla_recurrent(qh, kh, vh, gh, S0.astype(F32))
    o = o * lax.rsqrt(jnp.mean(o * o, -1, keepdims=True) + RMS_EPS) * norm_w.astype(F32)
    o = o.reshape(B, L, GLA_WIDTH) * jax.nn.silu(og.astype(F32))
    return o.astype(zg.dtype), S_fin.astype(S0.dtype)


def hier_moe(x, w_group, b_group, w_router, b_router, w1, w3, w2):
    B, L, D = x.shape
    xt = x.reshape(-1, D)
    p_group = jax.nn.softmax((xt @ w_group + b_group).astype(F32), -1)
    g_sel = jnp.argmax(p_group, -1)
    p_g = jnp.take_along_axis(p_group, g_sel[:, None], -1)
    logits = (xt @ w_router + b_router).astype(F32).reshape(-1, N_GROUPS, EXPERTS_PER_GROUP)
    logit_sel = jnp.take_along_axis(logits, g_sel[:, None, None], axis=1)[:, 0]
    top_p, top_i = lax.top_k(jax.nn.softmax(logit_sel, -1), EXPERT_TOP_K)
    weights = p_g * top_p / jnp.sum(top_p, -1, keepdims=True)
    idx = g_sel[:, None] * EXPERTS_PER_GROUP + top_i
    gate = jnp.sum(jax.nn.one_hot(idx, N_EXPERTS, dtype=F32) * weights[..., None], axis=1).astype(x.dtype)
    y = jnp.zeros_like(xt)
    for e in range(N_EXPERTS):
        h = jax.nn.silu(xt @ w1[e]) * (xt @ w3[e])
        y = y + gate[:, e:e + 1] * (h @ w2[e])
    return y.reshape(B, L, D)


def trunk_layer(h, S_rw, prev, S_gla, lp, prompt):
    xn = rmsnorm(h, lp['norm1_w'])
    z = xn @ lp['w_in']
    zs = z[..., :W_SHIFT]
    zg = z[..., W_SHIFT:W_SHIFT + W_GLA]
    gate_r = z[..., W_SHIFT + W_GLA:W_SHIFT + W_GLA + D_MODEL]
    gate_g = z[..., W_SHIFT + W_GLA + D_MODEL:]
    o_r, S_rw_new = rwkv7_branch(zs, prev, S_rw, lp['mu_shift'], lp['rw_w0'], lp['rw_w2'], lp['rw_a0'],
                                 lp['rw_a2'], lp['rw_g2'], lp['rw_k_k'], lp['rw_k_a'], lp['rw_r_k'],
                                 lp['rw_lnx_w'], lp['rw_lnx_b'])
    o_g, S_gla_new = gla_branch(zg, S_gla, lp['gla_gk_up'], lp['gla_gk_b'], lp['gla_norm_w'], prompt)
    m = jax.nn.sigmoid(gate_r) * (o_r @ lp['p_rwkv']) + jax.nn.sigmoid(gate_g) * (o_g @ lp['p_gla'])
    h = h + m @ lp['w_out']
    h = h + hier_moe(rmsnorm(h, lp['norm2_w']), lp['moe_w_group'], lp['moe_b_group'], lp['moe_w_router'],
                     lp['moe_b_router'], lp['moe_w1'], lp['moe_w3'], lp['moe_w2'])
    return h, S_rw_new, zs[:, -1].astype(prev.dtype), S_gla_new


def setup_inputs(seed: int = 0) -> dict:
    key = jax.random.key(seed)
    ks = iter(jax.random.split(key, 48))
    nrm = lambda shape, scale: jax.random.normal(next(ks), shape, F32) * scale
    uni = lambda shape, lo, hi: jax.random.uniform(next(ks), shape, F32, lo, hi)
    L = DEPTH
    return {
        'x_prompt': nrm((BATCH, SEQ, D_MODEL), 1.0),
        'x_sample': nrm((DEC_BATCH, DEC_SEQ, D_MODEL), 1.0),
        'state_rwkv': nrm((L, DEC_BATCH, RW_HEADS, RW_HEAD, RW_HEAD), 0.5),
        'state_shift': nrm((L, DEC_BATCH, W_SHIFT), 1.0),
        'state_gla': nrm((L, DEC_BATCH, GLA_HEADS, GLA_DK, GLA_DV), 0.5),
        'meta_tokens': nrm((N_META, D_MODEL), 1.0),
        'norm1_w': 1.0 + nrm((L, D_MODEL), 0.02),
        'w_in': nrm((L, D_MODEL, W_IN), D_MODEL ** -0.5),
        'mu_shift': uni((L, W_SHIFT), 0.2, 0.8),
        'rw_w0': jnp.linspace(-6.5, -1.5, RW_WIDTH, dtype=F32)[None] + nrm((L, RW_WIDTH), 0.1),
        'rw_w2': nrm((L, RW_DECAY_LORA, RW_WIDTH), 0.1 * RW_DECAY_LORA ** -0.5),
        'rw_a0': nrm((L, RW_WIDTH), 0.1),
        'rw_a2': nrm((L, RW_AAA_LORA, RW_WIDTH), RW_AAA_LORA ** -0.5),
        'rw_g2': nrm((L, RW_GATE_LORA, RW_WIDTH), RW_GATE_LORA ** -0.5),
        'rw_k_k': 0.85 + nrm((L, RW_WIDTH), 0.02),
        'rw_k_a': 1.0 + nrm((L, RW_WIDTH), 0.02),
        'rw_r_k': nrm((L, RW_HEADS, RW_HEAD), 0.1),
        'rw_lnx_w': 1.0 + nrm((L, RW_WIDTH), 0.02),
        'rw_lnx_b': nrm((L, RW_WIDTH), 0.02),
        'gla_gk_up': nrm((L, GLA_GATE_LORA, GLA_KDIM), GLA_GATE_LORA ** -0.5),
        'gla_gk_b': nrm((L, GLA_KDIM), 0.1),
        'gla_norm_w': 1.0 + nrm((L, GLA_DV), 0.02),
        'p_rwkv': nrm((L, RW_WIDTH, D_MODEL), RW_WIDTH ** -0.5),
        'p_gla': nrm((L, GLA_WIDTH, D_MODEL), GLA_WIDTH ** -0.5),
        'w_out': nrm((L, D_MODEL, D_MODEL), D_MODEL ** -0.5),
        'norm2_w': 1.0 + nrm((L, D_MODEL), 0.02),
        'moe_w_group': nrm((L, D_MODEL, N_GROUPS), D_MODEL ** -0.5),
        'moe_b_group': nrm((L, N_GROUPS), 0.01),
        'moe_w_router': nrm((L, D_MODEL, N_EXPERTS), D_MODEL ** -0.5),
        'moe_b_router': nrm((L, N_EXPERTS), 0.01),
        'moe_w1': nrm((L, N_EXPERTS, D_MODEL, D_EXPERT), D_MODEL ** -0.5),
        'moe_w3': nrm((L, N_EXPERTS, D_MODEL, D_EXPERT), D_MODEL ** -0.5),
        'moe_w2': nrm((L, N_EXPERTS, D_EXPERT, D_MODEL), D_EXPERT ** -0.5),
        'final_norm_w': 1.0 + nrm((D_MODEL,), 0.02),
    }


def reference(x_prompt, x_sample, state_rwkv, state_shift, state_gla, meta_tokens, norm1_w, w_in, mu_shift,
              rw_w0, rw_w2, rw_a0, rw_a2, rw_g2, rw_k_k, rw_k_a, rw_r_k, rw_lnx_w, rw_lnx_b, gla_gk_up, gla_gk_b,
              gla_norm_w, p_rwkv, p_gla, w_out, norm2_w, moe_w_group, moe_b_group, moe_w_router, moe_b_router,
              moe_w1, moe_w3, moe_w2, final_norm_w):
    B = x_prompt.shape[0]
    meta = jnp.broadcast_to(meta_tokens.astype(x_prompt.dtype)[None], (B, N_META, D_MODEL))
    h_p = jnp.concatenate([meta, x_prompt], axis=1)
    h_s = x_sample
    p_rw, p_sh, p_gl, s_rw, s_sh, s_gl = [], [], [], [], [], []
    for l in range(DEPTH):
        lp = {
            'norm1_w': norm1_w[l], 'w_in': w_in[l], 'mu_shift': mu_shift[l], 'rw_w0': rw_w0[l],
            'rw_w2': rw_w2[l], 'rw_a0': rw_a0[l], 'rw_a2': rw_a2[l], 'rw_g2': rw_g2[l], 'rw_k_k': rw_k_k[l],
            'rw_k_a': rw_k_a[l], 'rw_r_k': rw_r_k[l], 'rw_lnx_w': rw_lnx_w[l], 'rw_lnx_b': rw_lnx_b[l],
            'gla_gk_up': gla_gk_up[l], 'gla_gk_b': gla_gk_b[l], 'gla_norm_w': gla_norm_w[l],
            'p_rwkv': p_rwkv[l], 'p_gla': p_gla[l], 'w_out': w_out[l], 'norm2_w': norm2_w[l],
            'moe_w_group': moe_w_group[l], 'moe_b_group': moe_b_group[l], 'moe_w_router': moe_w_router[l],
            'moe_b_router': moe_b_router[l], 'moe_w1': moe_w1[l], 'moe_w3': moe_w3[l], 'moe_w2': moe_w2[l],
        }
        S_rw0 = jnp.zeros((B, RW_HEADS, RW_HEAD, RW_HEAD), state_rwkv.dtype)
        prev0 = jnp.zeros((B, W_SHIFT), state_shift.dtype)
        S_gl0 = jnp.zeros((B, GLA_HEADS, GLA_DK, GLA_DV), state_gla.dtype)
        h_p, a1, a2, a3 = trunk_layer(h_p, S_rw0, prev0, S_gl0, lp, True)
        h_s, b1, b2, b3 = trunk_layer(h_s, state_rwkv[l], state_shift[l], state_gla[l], lp, False)
        p_rw.append(a1); p_sh.append(a2); p_gl.append(a3)
        s_rw.append(b1); s_sh.append(b2); s_gl.append(b3)
    y_prompt = rmsnorm(h_p, final_norm_w)[:, N_META:]
    y_sample = rmsnorm(h_s, final_norm_w)
    return (y_prompt, y_sample, jnp.stack(p_rw), jnp.stack(p_sh), jnp.stack(p_gl),
            jnp.stack(s_rw), jnp.stack(s_sh), jnp.stack(s_gl))
```

```python
import functools
import math

import numpy as np
import jax
import jax.numpy as jnp
from jax import lax
from jax.experimental import pallas as pl
from jax.experimental.pallas import tpu as pltpu

F32 = jnp.float32
BF16 = jnp.bfloat16

D_MODEL = 2048
N_BATCH = 4
SEQ = 2048
N_SAMPLE = 128
N_META = 16
RMS_EPS = 1e-6

RW_WIDTH = 1024
RW_HEAD = 64
RW_HEADS = 16
RW_GN_EPS = RW_HEAD * 1e-5
W_SHIFT = 3328
GLA_HEADS = 4
GLA_DK = 128
GLA_DV = 256
GLA_KDIM = 512
GLA_WIDTH = 1024
GLA_LORA = 16
GLA_NORMALIZER = 16.0
W_GLA_PAD = 2304
OG_BLOCK = 2 * D_MODEL // GLA_WIDTH
N_GROUPS = 4
EXPERTS_PER_GROUP = 8
N_EXPERTS = 32
D_EXPERT = 512

CHUNK = 64
N_CHUNKS = SEQ // CHUNK
META_PAD = CHUNK - N_META
T_MAIN = N_BATCH * SEQ
ROW_SAMPLE = T_MAIN
ROW_META = T_MAIN + N_SAMPLE
T_EXT = 256
T_ALL = T_MAIN + T_EXT
LANES = 128

TM_DENSE = 256
TM_MM = 1056
TM_MOE = 256
NT_MOE = (2 * T_ALL) // TM_MOE + N_EXPERTS
NEG_BIG = -1e30

VMEM_LIMIT = 56 * 1024 * 1024

_SHIFT_PERM = np.concatenate([np.arange(0, 1024), np.arange(1088, 2112), np.arange(2112, 3136),
                              np.arange(1024, 1088), np.arange(3136, 3328)])
_SHIFT_INV = np.argsort(_SHIFT_PERM)


def _cparams(n_axes=1):
    return pltpu.CompilerParams(dimension_semantics=("arbitrary",) * n_axes, vmem_limit_bytes=VMEM_LIMIT)


def _bdot(a, b):
    return jnp.dot(a.astype(BF16), b.astype(BF16), preferred_element_type=F32)


def _bdot_nt(a, b):
    return lax.dot_general(a.astype(BF16), b.astype(BF16), (((1,), (1,)), ((), ())), preferred_element_type=F32)


def _bdot_tn(a, b):
    return lax.dot_general(a.astype(BF16), b.astype(BF16), (((0,), (0,)), ((), ())), preferred_element_type=F32)


def _split_dot(m_bf16, x):
    hi = x.astype(BF16)
    lo = (x - hi.astype(F32)).astype(BF16)
    return (jnp.dot(m_bf16, hi, preferred_element_type=F32) + jnp.dot(m_bf16, lo, preferred_element_type=F32))


def _iota(shape, dim):
    return lax.broadcasted_iota(jnp.int32, shape, dim)


def _tri_incl(n):
    return (_iota((n, n), 0) >= _iota((n, n), 1)).astype(BF16)


def _seg_sum(x, width):
    m, n = x.shape
    nb = n // LANES
    bd = ((_iota((LANES, LANES), 0) // width) == (_iota((LANES, LANES), 1) // width)).astype(BF16)
    xs = jnp.concatenate([x[:, j * LANES:(j + 1) * LANES] for j in range(nb)], axis=0)
    hi = xs.astype(BF16)
    lo = (xs - hi.astype(F32)).astype(BF16)
    s = jnp.dot(hi, bd, preferred_element_type=F32) + jnp.dot(lo, bd, preferred_element_type=F32)
    return jnp.concatenate([s[j * m:(j + 1) * m] for j in range(nb)], axis=1)


def _col_of_row(row):
    n = row.shape[-1]
    eye = _iota((n, n), 0) == _iota((n, n), 1)
    return jnp.sum(jnp.where(eye, jnp.broadcast_to(row, (n, n)), 0.0), axis=-1, keepdims=True)


def _norm1_kernel(xp_ref, xe_ref, w_ref, o_ref):
    i = pl.program_id(0)

    def f(x):
        ms = jnp.mean(x * x, axis=-1, keepdims=True)
        return (x * lax.rsqrt(ms + RMS_EPS) * w_ref[...]).astype(BF16)

    @pl.when(i < T_MAIN // TM_DENSE)
    def _():
        o_ref[...] = f(xp_ref[...])

    @pl.when(i == T_MAIN // TM_DENSE)
    def _():
        o_ref[...] = f(xe_ref[...])


def _norm1(xp, xe, w):
    nmain = T_MAIN // TM_DENSE
    return pl.pallas_call(
        _norm1_kernel,
        grid=(T_ALL // TM_DENSE,),
        in_specs=[pl.BlockSpec((TM_DENSE, D_MODEL), lambda i: (jnp.minimum(i, nmain - 1), 0)),
                  pl.BlockSpec((T_EXT, D_MODEL), lambda i: (0, 0)),
                  pl.BlockSpec((1, D_MODEL), lambda i: (0, 0))],
        out_specs=pl.BlockSpec((TM_DENSE, D_MODEL), lambda i: (i, 0)),
        out_shape=jax.ShapeDtypeStruct((T_ALL, D_MODEL), BF16),
        compiler_params=_cparams(),
        name="norm1",
    )(xp, xe, w)


def _mm_kernel(x_ref, w_ref, o_ref, *, silu_from):
    z = jnp.dot(x_ref[...], w_ref[...], preferred_element_type=F32)
    if silu_from is None:
        o_ref[...] = z.astype(o_ref.dtype)
    else:
        s = jax.nn.sigmoid(z)
        o_ref[...] = jnp.where(pl.program_id(1) >= silu_from, z * s, s).astype(o_ref.dtype)


def _proj(x, w, tn, out_dtype, silu_from=None, name="proj"):
    t, k = x.shape
    n = w.shape[1]
    return pl.pallas_call(
        functools.partial(_mm_kernel, silu_from=silu_from),
        grid=(t // TM_MM, n // tn),
        in_specs=[pl.BlockSpec((TM_MM, k), lambda i, j: (i, 0)),
                  pl.BlockSpec((k, tn), lambda i, j: (0, j))],
        out_specs=pl.BlockSpec((TM_MM, tn), lambda i, j: (i, j)),
        out_shape=jax.ShapeDtypeStruct((t, n), out_dtype),
        compiler_params=_cparams(2),
        name=name,
    )(x, w)


def _rwkv_pre(z, w0, w2a, a0, g2, k_k, k_a):
    r = z[:, 0:1024]
    k = z[:, 1024:2048]
    v = z[:, 2048:3072]
    wa = z[:, 3072:3200]
    gd = z[:, 3200:3328]
    lane = _iota(wa.shape, 1)
    wa = jnp.where(lane < 64, jnp.tanh(wa), wa)
    up = _bdot(wa, w2a)
    lw = -math.exp(-0.5) * jax.nn.sigmoid(w0 + up[:, :1024])
    a = jax.nn.sigmoid(a0 + up[:, 1024:])
    g = _bdot(jax.nn.sigmoid(gd), g2)
    kk = k * k_k
    kk = kk * lax.rsqrt(jnp.maximum(_seg_sum(kk * kk, RW_HEAD), 1e-24))
    k_h = k * (1.0 + (a - 1.0) * k_a)
    return r, lw, k_h, v, kk, kk * a, g


def _rwkv_post(y, r, k_h, v, g, r_k, lnx_w, lnx_b):
    mean = _seg_sum(y, RW_HEAD) * (1.0 / RW_HEAD)
    d = y - mean
    var = _seg_sum(d * d, RW_HEAD) * (1.0 / RW_HEAD)
    yn = d * lax.rsqrt(var + RW_GN_EPS) * lnx_w + lnx_b
    bonus = _seg_sum(r * k_h * r_k, RW_HEAD) * v
    return (yn + bonus) * g


def _stack2(x):
    lane = _iota(x.shape, 1)
    return jnp.concatenate([jnp.where(lane < RW_HEAD, x, 0.0), jnp.where(lane >= RW_HEAD, x, 0.0)], axis=0)


def _unstack2(x):
    c = x.shape[0] // 2
    return x[:c] + x[c:]


def _rwkv_chunk_kernel(zs_ref, mu_ref, w0_ref, w2a_ref, a0_ref, g2_ref, kk_ref, ka_ref, rk_ref, lw_ref, lb_ref,
                       o_ref, hout_ref, prev_sc, h_sc, hmeta_sc, prevmeta_sc, y_sc, *, n_chunks):
    i = pl.program_id(0)
    is_meta = i == 0
    c = lax.rem(jnp.maximum(i - 1, 0), n_chunks)
    first = jnp.logical_and(i >= 1, c == 0)

    @pl.when(is_meta)
    def _():
        h_sc[...] = jnp.zeros_like(h_sc)
        prev_sc[...] = jnp.zeros_like(prev_sc)

    @pl.when(first)
    def _():
        h_sc[...] = hmeta_sc[...]
        prev_sc[...] = prevmeta_sc[...]

    zs = zs_ref[...]
    rowi = _iota((CHUNK, 1), 0)
    sh = pltpu.roll(zs, 1, 0)
    sh = jnp.where(rowi == 0, prev_sc[...], sh)
    prev_sc[...] = zs[CHUNK - 1:CHUNK, :]
    z = zs + (sh - zs) * mu_ref[...]
    r, lw, k_h, v, kk, b, g = _rwkv_pre(z, w0_ref[...], w2a_ref[...], a0_ref[...], g2_ref[...], kk_ref[...],
                                        ka_ref[...])
    lw = jnp.where(jnp.logical_and(is_meta, rowi < META_PAD), 0.0, lw)

    cl = _split_dot(_tri_incl(CHUNK), lw)
    cl_end = cl[CHUNK - 1:CHUNK, :]
    e_neg = jnp.exp(-cl)
    e_end = jnp.exp(cl_end - cl)
    kkt = kk * jnp.exp(cl - lw)
    rt = r * jnp.exp(cl)
    bt = b * e_neg
    kt = k_h * e_neg
    bh = b * e_end
    kh = k_h * e_end
    e_c = jnp.exp(cl_end)

    n2 = 2 * CHUNK
    tok_r = jnp.bitwise_and(_iota((n2, n2), 0), CHUNK - 1)
    tok_c = jnp.bitwise_and(_iota((n2, n2), 1), CHUNK - 1)
    strict = tok_r > tok_c
    incl = tok_r >= tok_c

    for j in range(RW_HEADS // 2):
        sl = slice(j * LANES, (j + 1) * LANES)
        s_kkt = _stack2(kkt[:, sl])
        s_rt = _stack2(rt[:, sl])
        s_v = _stack2(v[:, sl]).astype(BF16)
        rb = jnp.concatenate([_stack2(bt[:, sl]), _stack2(kt[:, sl])], axis=0).astype(BF16)
        aa = _bdot_nt(s_kkt, rb)
        mm = _bdot_nt(s_rt, rb)
        a_b = jnp.where(strict, aa[:, :n2], 0.0).astype(BF16)
        a_k = jnp.where(strict, aa[:, n2:], 0.0)
        m_rb = jnp.where(incl, mm[:, :n2], 0.0)
        m_rk = jnp.where(incl, mm[:, n2:], 0.0)
        x = jnp.concatenate([s_kkt, _bdot(a_k, s_v)], axis=1)
        x = x - _bdot(a_b, x)
        p = a_b
        for _ in range(5):
            p = _bdot(p, p).astype(BF16)
            x = x + _bdot(p, x)
        zq = _bdot(m_rb, x)
        q_eff = _unstack2(s_rt - zq[:, :LANES])
        y_in = _unstack2(_bdot(m_rk, s_v) - zq[:, LANES:])
        zb = _bdot_tn(_stack2(bh[:, sl]), x)
        h_add = _bdot_tn(_stack2(kh[:, sl]), s_v) - zb[:, LANES:]
        h0 = h_sc[j]
        y_sc[:, sl] = _bdot(q_eff, h0) + y_in
        h_new = _col_of_row(e_c[:, sl]) * h0 - _bdot(zb[:, :LANES], h0) + h_add
        h_sc[j] = h_new

        @pl.when(jnp.logical_and(i >= 1, c == n_chunks - 1))
        def _():
            hout_ref[0, :, sl] = _unstack2(h_new)

    o = _rwkv_post(y_sc[...], r, k_h, v, g, rk_ref[...], lw_ref[...], lb_ref[...])
    o_ref[...] = o.astype(o_ref.dtype)

    @pl.when(is_meta)
    def _():
        hmeta_sc[...] = h_sc[...]
        prevmeta_sc[...] = prev_sc[...]


def _rwkv_prompt(zr, p, n_batch=N_BATCH, n_chunks=N_CHUNKS, meta_block=ROW_META // CHUNK):
    row = lambda n: pl.BlockSpec((1, n), lambda i: (0, 0))
    full = lambda a, b: pl.BlockSpec((a, b), lambda i: (0, 0))
    blk = lambda i: jnp.where(i == 0, meta_block, i - 1)
    return pl.pallas_call(
        functools.partial(_rwkv_chunk_kernel, n_chunks=n_chunks),
        grid=(1 + n_batch * n_chunks,),
        in_specs=[pl.BlockSpec((CHUNK, W_SHIFT), lambda i: (blk(i), 0)),
                  row(W_SHIFT), row(RW_WIDTH), full(LANES, 2 * RW_WIDTH), row(RW_WIDTH), full(LANES, RW_WIDTH),
                  row(RW_WIDTH), row(RW_WIDTH), row(RW_WIDTH), row(RW_WIDTH), row(RW_WIDTH)],
        out_specs=[pl.BlockSpec((CHUNK, RW_WIDTH), lambda i: (jnp.maximum(i - 1, 0), 0)),
                   pl.BlockSpec((1, RW_HEAD, RW_WIDTH), lambda i: (jnp.maximum(i - 1, 0) // n_chunks, 0, 0))],
        out_shape=[jax.ShapeDtypeStruct((n_batch * n_chunks * CHUNK, RW_WIDTH), BF16),
                   jax.ShapeDtypeStruct((n_batch, RW_HEAD, RW_WIDTH), F32)],
        scratch_shapes=[pltpu.VMEM((1, W_SHIFT), F32),
                        pltpu.VMEM((RW_HEADS // 2, LANES, LANES), F32),
                        pltpu.VMEM((RW_HEADS // 2, LANES, LANES), F32),
                        pltpu.VMEM((1, W_SHIFT), F32),
                        pltpu.VMEM((CHUNK, RW_WIDTH), F32)],
        compiler_params=_cparams(),
        name="rwkv_prompt",
    )(zr, p['mu'], p['w0'], p['w2a'], p['a0'], p['g2'], p['k_k'], p['k_a'], p['r_k'], p['lnx_w'], p['lnx_b'])


def _rwkv_sample_pre_kernel(zs_ref, prev_ref, mu_ref, w0_ref, w2a_ref, a0_ref, g2_ref, kk_ref, ka_ref,
                            r_ref, w_ref, kh_ref, v_ref, nkk_ref, b_ref, g_ref):
    zs = zs_ref[...]
    z = zs + (prev_ref[...] - zs) * mu_ref[...]
    r, lw, k_h, v, kk, b, g = _rwkv_pre(z, w0_ref[...], w2a_ref[...], a0_ref[...], g2_ref[...], kk_ref[...],
                                        ka_ref[...])
    r_ref[...] = r
    w_ref[...] = jnp.exp(lw)
    kh_ref[...] = k_h
    v_ref[...] = v
    nkk_ref[...] = -kk
    b_ref[...] = b
    g_ref[...] = g


def _rwkv_sample_pre(zr, prev, p):
    row = lambda n: pl.BlockSpec((1, n), lambda i: (0, 0))
    full = lambda a, b: pl.BlockSpec((a, b), lambda i: (0, 0))
    vec = jax.ShapeDtypeStruct((N_SAMPLE, RW_WIDTH), F32)
    return pl.pallas_call(
        _rwkv_sample_pre_kernel,
        grid=(1,),
        in_specs=[pl.BlockSpec((N_SAMPLE, W_SHIFT), lambda i: (ROW_SAMPLE // N_SAMPLE, 0)),
                  full(N_SAMPLE, W_SHIFT),
                  row(W_SHIFT), row(RW_WIDTH), full(LANES, 2 * RW_WIDTH), row(RW_WIDTH), full(LANES, RW_WIDTH),
                  row(RW_WIDTH), row(RW_WIDTH)],
        out_specs=[full(N_SAMPLE, RW_WIDTH)] * 7,
        out_shape=[vec] * 7,
        compiler_params=_cparams(),
        name="rwkv_sample_pre",
    )(zr, prev, p['mu'], p['w0'], p['w2a'], p['a0'], p['g2'], p['k_k'], p['k_a'])


def _rwkv_sample_step_kernel(s_ref, r_ref, w_ref, kh_ref, v_ref, nkk_ref, b_ref, so_ref, y_ref):
    s = s_ref[0]
    eye = _iota((RW_HEAD, RW_HEAD), 0) == _iota((RW_HEAD, RW_HEAD), 1)
    sa = jnp.sum(s * nkk_ref[0], axis=-1, keepdims=True)
    v_col = jnp.sum(jnp.where(eye, v_ref[0], 0.0), axis=-1, keepdims=True)
    s_new = s * w_ref[0] + sa * b_ref[0] + v_col * kh_ref[0]
    so_ref[0] = s_new
    y_col = jnp.sum(s_new * r_ref[0], axis=-1, keepdims=True)
    y_ref[0] = jnp.sum(jnp.where(eye, y_col, 0.0), axis=-2, keepdims=True)


def _rwkv_sample_step(state, r, w, kh, v, nkk, b):
    hv = lambda t: t.reshape(N_SAMPLE, RW_HEADS, 1, RW_HEAD)
    vspec = pl.BlockSpec((1, RW_HEADS, 1, RW_HEAD), lambda i: (i, 0, 0, 0))
    sspec = pl.BlockSpec((1, RW_HEADS, RW_HEAD, RW_HEAD), lambda i: (i, 0, 0, 0))
    s_new, y = pl.pallas_call(
        _rwkv_sample_step_kernel,
        grid=(N_SAMPLE,),
        in_specs=[sspec] + [vspec] * 6,
        out_specs=[sspec, vspec],
        out_shape=[jax.ShapeDtypeStruct(state.shape, F32),
                   jax.ShapeDtypeStruct((N_SAMPLE, RW_HEADS, 1, RW_HEAD), F32)],
        compiler_params=_cparams(),
        name="rwkv_sample_step",
    )(state, hv(r), hv(w), hv(kh), hv(v), hv(nkk), hv(b))
    return s_new, y.reshape(N_SAMPLE, RW_WIDTH)


def _rwkv_sample_post_kernel(y_ref, r_ref, kh_ref, v_ref, g_ref, rk_ref, lw_ref, lb_ref, o_ref):
    o = _rwkv_post(y_ref[...], r_ref[...], kh_ref[...], v_ref[...], g_ref[...], rk_ref[...], lw_ref[...],
                   lb_ref[...])
    o_ref[0:N_SAMPLE, :] = o.astype(o_ref.dtype)
    o_ref[N_SAMPLE:T_EXT, :] = jnp.zeros((T_EXT - N_SAMPLE, RW_WIDTH), o_ref.dtype)


def _rwkv_sample_post(y, r, kh, v, g, p):
    row = lambda n: pl.BlockSpec((1, n), lambda i: (0, 0))
    full = lambda a, b: pl.BlockSpec((a, b), lambda i: (0, 0))
    return pl.pallas_call(
        _rwkv_sample_post_kernel,
        grid=(1,),
        in_specs=[full(N_SAMPLE, RW_WIDTH)] * 5 + [row(RW_WIDTH)] * 3,
        out_specs=full(T_EXT, RW_WIDTH),
        out_shape=jax.ShapeDtypeStruct((T_EXT, RW_WIDTH), BF16),
        compiler_params=_cparams(),
        name="rwkv_sample_post",
    )(y, r, kh, v, g, p['r_k'], p['lnx_w'], p['lnx_b'])


def _gla_logg(zg, gk_up, gk_b):
    gkd = zg[:, 2 * GLA_KDIM + GLA_WIDTH:2 * GLA_KDIM + GLA_WIDTH + LANES]
    x = _bdot(gkd, gk_up) + gk_b
    return (jnp.minimum(x, 0.0) - jnp.log(1.0 + jnp.exp(-jnp.abs(x)))) * (1.0 / GLA_NORMALIZER)


def _gla_post(o, og_act, norm_w):
    outs = []
    for h in range(GLA_HEADS):
        oh = o[:, h * GLA_DV:(h + 1) * GLA_DV]
        ms = jnp.mean(oh * oh, axis=-1, keepdims=True)
        outs.append(oh * lax.rsqrt(ms + RMS_EPS) * norm_w)
    return jnp.concatenate(outs, axis=1) * og_act


def _gla_chunk_kernel(zg_ref, og_ref, gkup_ref, gkb_ref, nw_ref, o_ref, sout_ref, s_sc, smeta_sc, *, n_chunks):
    i = pl.program_id(0)
    is_meta = i == 0
    c = lax.rem(jnp.maximum(i - 1, 0), n_chunks)
    first = jnp.logical_and(i >= 1, c == 0)

    @pl.when(is_meta)
    def _():
        s_sc[...] = jnp.zeros_like(s_sc)

    @pl.when(first)
    def _():
        s_sc[...] = smeta_sc[...]

    zg = zg_ref[...]
    rowi = _iota((CHUNK, 1), 0)
    logg = _gla_logg(zg, gkup_ref[...], gkb_ref[...])
    logg = jnp.where(jnp.logical_and(is_meta, rowi < META_PAD), 0.0, logg)
    bcum = _split_dot(_tri_incl(CHUNK), logg)
    b_end = bcum[CHUNK - 1:CHUNK, :]
    qt = zg[:, 0:GLA_KDIM] * (GLA_DK ** -0.5) * jnp.exp(bcum)
    kt = zg[:, GLA_KDIM:2 * GLA_KDIM] * jnp.exp(-bcum)
    ke = zg[:, GLA_KDIM:2 * GLA_KDIM] * jnp.exp(b_end - bcum)
    e_end = jnp.exp(b_end)
    causal = _iota((CHUNK, CHUNK), 0) >= _iota((CHUNK, CHUNK), 1)
    outs = []
    for h in range(GLA_HEADS):
        ks = slice(h * GLA_DK, (h + 1) * GLA_DK)
        vh = zg[:, 2 * GLA_KDIM + h * GLA_DV:2 * GLA_KDIM + (h + 1) * GLA_DV].astype(BF16)
        a = jnp.where(causal, _bdot_nt(qt[:, ks], kt[:, ks]), 0.0)
        s0 = s_sc[h]
        outs.append(_bdot(a, vh) + _bdot(qt[:, ks], s0))
        s_new = _col_of_row(e_end[:, ks]) * s0 + _bdot_tn(ke[:, ks], vh)
        s_sc[h] = s_new

        @pl.when(jnp.logical_and(i >= 1, c == n_chunks - 1))
        def _():
            sout_ref[0, h] = s_new

    o = _gla_post(jnp.concatenate(outs, axis=1), og_ref[...].astype(F32), nw_ref[...])
    o_ref[...] = o.astype(o_ref.dtype)

    @pl.when(is_meta)
    def _():
        smeta_sc[...] = s_sc[...]


def _gla_prompt(zg, zb, p, n_batch=N_BATCH, n_chunks=N_CHUNKS, meta_block=ROW_META // CHUNK, og_block=OG_BLOCK):
    full = lambda a, b: pl.BlockSpec((a, b), lambda i: (0, 0))
    blk = lambda i: jnp.where(i == 0, meta_block, i - 1)
    return pl.pallas_call(
        functools.partial(_gla_chunk_kernel, n_chunks=n_chunks),
        grid=(1 + n_batch * n_chunks,),
        in_specs=[pl.BlockSpec((CHUNK, W_GLA_PAD), lambda i: (blk(i), 0)),
                  pl.BlockSpec((CHUNK, GLA_WIDTH), lambda i: (blk(i), og_block)),
                  full(LANES, GLA_KDIM), full(1, GLA_KDIM), full(1, GLA_DV)],
        out_specs=[pl.BlockSpec((CHUNK, GLA_WIDTH), lambda i: (jnp.maximum(i - 1, 0), 0)),
                   pl.BlockSpec((1, GLA_HEADS, GLA_DK, GLA_DV),
                                lambda i: (jnp.maximum(i - 1, 0) // n_chunks, 0, 0, 0))],
        out_shape=[jax.ShapeDtypeStruct((n_batch * n_chunks * CHUNK, GLA_WIDTH), BF16),
                   jax.ShapeDtypeStruct((n_batch, GLA_HEADS, GLA_DK, GLA_DV), F32)],
        scratch_shapes=[pltpu.VMEM((GLA_HEADS, GLA_DK, GLA_DV), F32),
                        pltpu.VMEM((GLA_HEADS, GLA_DK, GLA_DV), F32)],
        compiler_params=_cparams(),
        name="gla_prompt",
    )(zg, zb, p['gk_up'], p['gk_b'], p['norm_w'])


def _gla_sample_pre_kernel(zg_ref, gkup_ref, gkb_ref, q_ref, k_ref, eg_ref):
    zg = zg_ref[...]
    logg = _gla_logg(zg, gkup_ref[...], gkb_ref[...])
    q_ref[...] = zg[:, 0:GLA_KDIM] * (GLA_DK ** -0.5)
    k_ref[...] = zg[:, GLA_KDIM:2 * GLA_KDIM]
    eg_ref[...] = jnp.exp(logg)


def _gla_sample_pre(zg, p):
    full = lambda a, b: pl.BlockSpec((a, b), lambda i: (0, 0))
    vec = jax.ShapeDtypeStruct((N_SAMPLE, GLA_KDIM), F32)
    return pl.pallas_call(
        _gla_sample_pre_kernel,
        grid=(1,),
        in_specs=[pl.BlockSpec((N_SAMPLE, W_GLA_PAD), lambda i: (ROW_SAMPLE // N_SAMPLE, 0)),
                  full(LANES, GLA_KDIM), full(1, GLA_KDIM)],
        out_specs=[full(N_SAMPLE, GLA_KDIM)] * 3,
        out_shape=[vec] * 3,
        compiler_params=_cparams(),
        name="gla_sample_pre",
    )(zg, p['gk_up'], p['gk_b'])


def _gla_sample_step_kernel(s_ref, q_ref, k_ref, eg_ref, v_ref, so_ref, o_ref):
    s = s_ref[0]
    eye = _iota((GLA_DK, GLA_DK), 0) == _iota((GLA_DK, GLA_DK), 1)
    col = lambda ref: jnp.sum(jnp.where(eye, ref[0], 0.0), axis=-1, keepdims=True)
    s_new = col(eg_ref) * s + col(k_ref) * v_ref[0]
    so_ref[0] = s_new
    o_ref[0] = jnp.sum(col(q_ref) * s_new, axis=-2, keepdims=True)


def _gla_sample_step(state, q, k, eg, v):
    kv = lambda t: t.reshape(N_SAMPLE, GLA_HEADS, 1, GLA_DK)
    kspec = pl.BlockSpec((1, GLA_HEADS, 1, GLA_DK), lambda i: (i, 0, 0, 0))
    vspec = pl.BlockSpec((1, GLA_HEADS, 1, GLA_DV), lambda i: (i, 0, 0, 0))
    sspec = pl.BlockSpec((1, GLA_HEADS, GLA_DK, GLA_DV), lambda i: (i, 0, 0, 0))
    s_new, o = pl.pallas_call(
        _gla_sample_step_kernel,
        grid=(N_SAMPLE,),
        in_specs=[sspec, kspec, kspec, kspec, vspec],
        out_specs=[sspec, vspec],
        out_shape=[jax.ShapeDtypeStruct(state.shape, F32),
                   jax.ShapeDtypeStruct((N_SAMPLE, GLA_HEADS, 1, GLA_DV), F32)],
        compiler_params=_cparams(),
        name="gla_sample_step",
    )(state, kv(q), kv(k), kv(eg), v.reshape(N_SAMPLE, GLA_HEADS, 1, GLA_DV))
    return s_new, o.reshape(N_SAMPLE, GLA_WIDTH)


def _gla_sample_post_kernel(o_ref, og_ref, nw_ref, out_ref):
    o = _gla_post(o_ref[...], og_ref[0:N_SAMPLE, :].astype(F32), nw_ref[...])
    out_ref[0:N_SAMPLE, :] = o.astype(out_ref.dtype)
    out_ref[N_SAMPLE:T_EXT, :] = jnp.zeros((T_EXT - N_SAMPLE, GLA_WIDTH), out_ref.dtype)


def _gla_sample_post(o, zb, p, og_block=OG_BLOCK):
    full = lambda a, b: pl.BlockSpec((a, b), lambda i: (0, 0))
    return pl.pallas_call(
        _gla_sample_post_kernel,
        grid=(1,),
        in_specs=[full(N_SAMPLE, GLA_WIDTH),
                  pl.BlockSpec((T_EXT, GLA_WIDTH), lambda i: (T_MAIN // T_EXT, og_block)),
                  full(1, GLA_DV)],
        out_specs=full(T_EXT, GLA_WIDTH),
        out_shape=jax.ShapeDtypeStruct((T_EXT, GLA_WIDTH), BF16),
        compiler_params=_cparams(),
        name="gla_sample_post",
    )(o, zb, p['norm_w'])


def _mix_route_kernel(or_ref, ore_ref, og_ref, oge_ref, gates_ref, xp_ref, xe_ref, pr_ref, pg_ref, wo_ref, n2_ref,
                      wrh_ref, wrl_ref, br_ref, h1_ref, xn_ref, info_ref, cnt_ref, carry_sc):
    i = pl.program_id(0)
    is_main = i < T_MAIN // TM_DENSE

    @pl.when(i == 0)
    def _():
        carry_sc[...] = jnp.zeros_like(carry_sc)

    sig_r = gates_ref[:, 0:D_MODEL].astype(F32)
    sig_g = gates_ref[:, D_MODEL:2 * D_MODEL].astype(F32)
    o_r = jnp.where(is_main, or_ref[...], ore_ref[...])
    o_g = jnp.where(is_main, og_ref[...], oge_ref[...])
    m = (sig_r * jnp.dot(o_r, pr_ref[...], preferred_element_type=F32)
         + sig_g * jnp.dot(o_g, pg_ref[...], preferred_element_type=F32))
    h = jnp.where(is_main, xp_ref[...], xe_ref[...])
    h1 = h + jnp.dot(m.astype(BF16), wo_ref[...], preferred_element_type=F32)
    h1_ref[...] = h1
    ms = jnp.mean(h1 * h1, axis=-1, keepdims=True)
    xn = h1 * lax.rsqrt(ms + RMS_EPS) * n2_ref[...]
    xn_ref[...] = xn
    xh = xn.astype(BF16)
    xl = (xn - xh.astype(F32)).astype(BF16)
    lg = (jnp.dot(xh, wrh_ref[...], preferred_element_type=F32)
          + jnp.dot(xh, wrl_ref[...], preferred_element_type=F32)
          + jnp.dot(xl, wrh_ref[...], preferred_element_type=F32)) + br_ref[...]

    lane = _iota(lg.shape, 1)
    lanef = lane.astype(F32)
    is_g = jnp.logical_and(lane >= N_EXPERTS, lane < N_EXPERTS + N_GROUPS)
    gl = jnp.where(is_g, lg, NEG_BIG)
    gmax = jnp.max(gl, axis=-1, keepdims=True)
    gsel = jnp.min(jnp.where(jnp.logical_and(is_g, gl == gmax), lanef, 1e9), axis=-1, keepdims=True) - N_EXPERTS
    p_g = 1.0 / jnp.sum(jnp.exp(gl - gmax), axis=-1, keepdims=True)
    grp = (lane // EXPERTS_PER_GROUP).astype(F32)
    in_grp = jnp.logical_and(lane < N_EXPERTS, grp == gsel)
    el = jnp.where(in_grp, lg, NEG_BIG)
    m1 = jnp.max(el, axis=-1, keepdims=True)
    i1 = jnp.min(jnp.where(jnp.logical_and(in_grp, el == m1), lanef, 1e9), axis=-1, keepdims=True)
    in2 = jnp.logical_and(in_grp, lanef != i1)
    el2 = jnp.where(in2, lg, NEG_BIG)
    m2 = jnp.max(el2, axis=-1, keepdims=True)
    i2 = jnp.min(jnp.where(jnp.logical_and(in2, el2 == m2), lanef, 1e9), axis=-1, keepdims=True)
    e2 = jnp.exp(m2 - m1)
    w1 = p_g / (1.0 + e2)
    w2 = p_g * e2 / (1.0 + e2)

    oh1 = lanef == i1
    oh2 = lanef == i2
    cnt = jnp.where(jnp.logical_or(oh1, oh2), 1.0, 0.0)
    tm = cnt.shape[0]
    lstrict = (_iota((tm, tm), 0) > _iota((tm, tm), 1)).astype(BF16)
    before = jnp.dot(lstrict, cnt.astype(BF16), preferred_element_type=F32) + carry_sc[...]
    rank1 = jnp.sum(jnp.where(oh1, before, 0.0), axis=-1, keepdims=True)
    rank2 = jnp.sum(jnp.where(oh2, before, 0.0), axis=-1, keepdims=True)
    carry_sc[...] = carry_sc[...] + jnp.sum(cnt, axis=0, keepdims=True)
    cnt_ref[...] = carry_sc[...]
    info = jnp.where(lane == 0, i1, jnp.where(lane == 1, i2, jnp.where(lane == 2, w1, jnp.where(
        lane == 3, w2, jnp.where(lane == 4, rank1, jnp.where(lane == 5, rank2, 0.0))))))
    info_ref[...] = info


def _mix_route(o_r, o_r_ext, o_g, o_g_ext, zb, xp, xe, p):
    nmain = T_MAIN // TM_DENSE
    tile = lambda n: pl.BlockSpec((TM_DENSE, n), lambda i: (i, 0))
    main = lambda n: pl.BlockSpec((TM_DENSE, n), lambda i: (jnp.minimum(i, nmain - 1), 0))
    ext = lambda n: pl.BlockSpec((T_EXT, n), lambda i: (0, 0))
    const = lambda a, b: pl.BlockSpec((a, b), lambda i: (0, 0), pipeline_mode=pl.Buffered(1))
    return pl.pallas_call(
        _mix_route_kernel,
        grid=(T_ALL // TM_DENSE,),
        in_specs=[main(RW_WIDTH), ext(RW_WIDTH), main(GLA_WIDTH), ext(GLA_WIDTH), tile(2 * D_MODEL),
                  main(D_MODEL), ext(D_MODEL),
                  const(RW_WIDTH, D_MODEL), const(GLA_WIDTH, D_MODEL), const(D_MODEL, D_MODEL),
                  const(1, D_MODEL), const(D_MODEL, LANES), const(D_MODEL, LANES), const(1, LANES)],
        out_specs=[tile(D_MODEL), tile(D_MODEL), tile(LANES), pl.BlockSpec((1, LANES), lambda i: (0, 0))],
        out_shape=[jax.ShapeDtypeStruct((T_ALL, D_MODEL), F32),
                   jax.ShapeDtypeStruct((T_ALL, D_MODEL), F32),
                   jax.ShapeDtypeStruct((T_ALL, LANES), F32),
                   jax.ShapeDtypeStruct((1, LANES), F32)],
        scratch_shapes=[pltpu.VMEM((1, LANES), F32)],
        compiler_params=_cparams(),
        name="mix_route",
    )(o_r, o_r_ext, o_g, o_g_ext, zb, xp, xe, p['p_rwkv'], p['p_gla'], p['w_out'], p['norm2_w'], p['wr_hi'],
      p['wr_lo'], p['b_route'])


def _moe_kernel(te_ref, nt_ref, src_ref, xn_hbm, w1_ref, w3_ref, w2_ref, o_ref, xbuf, w1b, w3b, w2b, sem):
    i = pl.program_id(0)

    @pl.when(i < nt_ref[0])
    def _():
        def row_copy(t):
            return pltpu.make_async_copy(xn_hbm.at[pl.ds(src_ref[0, 0, t], 1), :], xbuf.at[pl.ds(t, 1), :], sem)

        def issue(t, carry):
            row_copy(t).start()
            return carry

        lax.fori_loop(0, TM_MOE, issue, 0)

        changed = jnp.logical_or(i == 0, te_ref[i] != te_ref[jnp.maximum(i - 1, 0)])

        @pl.when(changed)
        def _():
            w1b[...] = w1_ref[0].astype(BF16)
            w3b[...] = w3_ref[0].astype(BF16)
            w2b[...] = w2_ref[0].astype(BF16)

        def drain(t, carry):
            row_copy(t).wait()
            return carry

        lax.fori_loop(0, TM_MOE, drain, 0)

        xb = xbuf[...].astype(BF16)
        h1 = jnp.dot(xb, w1b[...], preferred_element_type=F32)
        h3 = jnp.dot(xb, w3b[...], preferred_element_type=F32)
        hh = (h1 * jax.nn.sigmoid(h1) * h3).astype(BF16)
        o_ref[...] = jnp.dot(hh, w2b[...], preferred_element_type=F32)

    @pl.when(i >= nt_ref[0])
    def _():
        o_ref[...] = jnp.zeros_like(o_ref)


def _moe(tile_expert, n_tiles, src, xn, w1, w3, w2):
    grid_spec = pltpu.PrefetchScalarGridSpec(
        num_scalar_prefetch=2,
        grid=(NT_MOE,),
        in_specs=[pl.BlockSpec((1, 1, TM_MOE), lambda i, te, nt: (i, 0, 0), memory_space=pltpu.SMEM),
                  pl.BlockSpec(memory_space=pl.ANY),
                  pl.BlockSpec((1, D_MODEL, D_EXPERT), lambda i, te, nt: (te[i], 0, 0)),
                  pl.BlockSpec((1, D_MODEL, D_EXPERT), lambda i, te, nt: (te[i], 0, 0)),
                  pl.BlockSpec((1, D_EXPERT, D_MODEL), lambda i, te, nt: (te[i], 0, 0))],
        out_specs=pl.BlockSpec((TM_MOE, D_MODEL), lambda i, te, nt: (i, 0)),
        scratch_shapes=[pltpu.VMEM((TM_MOE, D_MODEL), F32),
                        pltpu.VMEM((D_MODEL, D_EXPERT), BF16),
                        pltpu.VMEM((D_MODEL, D_EXPERT), BF16),
                        pltpu.VMEM((D_EXPERT, D_MODEL), BF16),
                        pltpu.SemaphoreType.DMA(())],
    )
    return pl.pallas_call(
        _moe_kernel,
        grid_spec=grid_spec,
        out_shape=jax.ShapeDtypeStruct((NT_MOE * TM_MOE, D_MODEL), F32),
        compiler_params=_cparams(),
        name="moe_experts",
    )(tile_expert, n_tiles, src, xn, w1, w3, w2)


def _combine_kernel(dst_ref, h1_ref, info_ref, fw_ref, eo_hbm, yp_ref, ye_ref, buf, sem):
    i = pl.program_id(0)
    nmain = T_MAIN // TM_DENSE

    def row_copy(t):
        return pltpu.make_async_copy(eo_hbm.at[pl.ds(dst_ref[0, 0, t], 1), :], buf.at[pl.ds(t, 1), :], sem)

    def issue(t, carry):
        row_copy(t).start()
        return carry

    lax.fori_loop(0, 2 * TM_DENSE, issue, 0)

    def drain(t, carry):
        row_copy(t).wait()
        return carry

    lax.fori_loop(0, 2 * TM_DENSE, drain, 0)

    info = info_ref[...]
    y = h1_ref[...] + info[:, 2:3] * buf[0:TM_DENSE, :] + info[:, 3:4] * buf[TM_DENSE:2 * TM_DENSE, :]
    ms = jnp.mean(y * y, axis=-1, keepdims=True)
    out = y * lax.rsqrt(ms + RMS_EPS) * fw_ref[...]

    @pl.when(i < nmain)
    def _():
        yp_ref[...] = out

    @pl.when(i == nmain)
    def _():
        ye_ref[...] = out


def _combine(dst, h1, info, fw, eo):
    nmain = T_MAIN // TM_DENSE
    return pl.pallas_call(
        _combine_kernel,
        grid=(T_ALL // TM_DENSE,),
        in_specs=[pl.BlockSpec((1, 1, 2 * TM_DENSE), lambda i: (i, 0, 0), memory_space=pltpu.SMEM),
                  pl.BlockSpec((TM_DENSE, D_MODEL), lambda i: (i, 0)),
                  pl.BlockSpec((TM_DENSE, LANES), lambda i: (i, 0)),
                  pl.BlockSpec((1, D_MODEL), lambda i: (0, 0)),
                  pl.BlockSpec(memory_space=pl.ANY)],
        out_specs=[pl.BlockSpec((TM_DENSE, D_MODEL), lambda i: (jnp.minimum(i, nmain - 1), 0)),
                   pl.BlockSpec((T_EXT, D_MODEL), lambda i: (0, 0))],
        out_shape=[jax.ShapeDtypeStruct((T_MAIN, D_MODEL), F32),
                   jax.ShapeDtypeStruct((T_EXT, D_MODEL), F32)],
        scratch_shapes=[pltpu.VMEM((2 * TM_DENSE, D_MODEL), F32), pltpu.SemaphoreType.DMA(())],
        compiler_params=_cparams(),
        name="moe_combine",
    )(dst, h1, info, fw, eo)


def _route_tables(info, counts):
    cnt = counts[0, :N_EXPERTS].astype(jnp.int32)
    tiles_e = (cnt + TM_MOE - 1) // TM_MOE
    tile_end = jnp.cumsum(tiles_e)
    tile_start = tile_end - tiles_e
    n_tiles = tile_end[-1]
    tile_idx = jnp.arange(NT_MOE, dtype=jnp.int32)
    tile_expert = jnp.minimum(jnp.sum((tile_end[None, :] <= tile_idx[:, None]).astype(jnp.int32), axis=1),
                              N_EXPERTS - 1)
    last_e = tile_expert[jnp.maximum(n_tiles - 1, 0)]
    tile_expert = jnp.where(jnp.arange(NT_MOE) < n_tiles, tile_expert, last_e).astype(jnp.int32)
    e1 = info[:, 0].astype(jnp.int32)
    e2 = info[:, 1].astype(jnp.int32)
    d1 = tile_start[e1] * TM_MOE + info[:, 4].astype(jnp.int32)
    d2 = tile_start[e2] * TM_MOE + info[:, 5].astype(jnp.int32)
    tok = jnp.arange(T_ALL, dtype=jnp.int32)
    src = jnp.zeros((NT_MOE * TM_MOE,), jnp.int32).at[d1].set(tok).at[d2].set(tok)
    nt = T_ALL // TM_DENSE
    dst = jnp.concatenate([d1.reshape(nt, 1, TM_DENSE), d2.reshape(nt, 1, TM_DENSE)], axis=2)
    return tile_expert, n_tiles.reshape(1).astype(jnp.int32), src.reshape(NT_MOE, 1, TM_MOE), dst


def kernel(x_prompt, x_sample, state_rwkv, state_shift, state_gla, meta_tokens, norm1_w, w_in, mu_shift, rw_w0,
           rw_w2, rw_a0, rw_a2, rw_g2, rw_k_k, rw_k_a, rw_r_k, rw_lnx_w, rw_lnx_b, gla_gk_up, gla_gk_b, gla_norm_w,
           p_rwkv, p_gla, w_out, norm2_w, moe_w_group, moe_b_group, moe_w_router, moe_b_router, moe_w1, moe_w3,
           moe_w2, final_norm_w):
    w = w_in[0]
    w_r = jnp.concatenate([w[:, 0:1024], w[:, 1088:3136], w[:, 1024:1088], w[:, 3136:3328]], axis=1).astype(BF16)
    w_g = jnp.concatenate([w[:, 3328:5392], jnp.zeros((D_MODEL, W_GLA_PAD - 2064), F32)], axis=1).astype(BF16)
    w_b = jnp.concatenate([w[:, 6416:10512], w[:, 5392:6416]], axis=1).astype(BF16)
    row = lambda t: t.reshape(1, -1)
    w2a = jnp.zeros((LANES, 2 * RW_WIDTH), F32)
    w2a = w2a.at[0:64, 0:RW_WIDTH].set(rw_w2[0]).at[64:128, RW_WIDTH:].set(rw_a2[0])
    rw = dict(mu=row(mu_shift[0][_SHIFT_PERM]), w0=row(rw_w0[0]), w2a=w2a, a0=row(rw_a0[0]), g2=rw_g2[0],
              k_k=row(rw_k_k[0]), k_a=row(rw_k_a[0]), r_k=row(rw_r_k[0]), lnx_w=row(rw_lnx_w[0]),
              lnx_b=row(rw_lnx_b[0]))
    gk_up = jnp.zeros((LANES, GLA_KDIM), F32).at[0:GLA_LORA].set(gla_gk_up[0])
    gl = dict(gk_up=gk_up, gk_b=row(gla_gk_b[0]), norm_w=row(gla_norm_w[0]))
    w_route = jnp.zeros((D_MODEL, LANES), F32)
    w_route = w_route.at[:, 0:N_EXPERTS].set(moe_w_router[0]).at[:, N_EXPERTS:N_EXPERTS + N_GROUPS].set(moe_w_group[0])
    wr_hi = w_route.astype(BF16)
    wr_lo = (w_route - wr_hi.astype(F32)).astype(BF16)
    b_route = jnp.zeros((1, LANES), F32)
    b_route = b_route.at[0, 0:N_EXPERTS].set(moe_b_router[0]).at[0, N_EXPERTS:N_EXPERTS + N_GROUPS].set(moe_b_group[0])
    mix = dict(p_rwkv=p_rwkv[0].astype(BF16), p_gla=p_gla[0].astype(BF16), w_out=w_out[0].astype(BF16),
               norm2_w=row(norm2_w[0]), wr_hi=wr_hi, wr_lo=wr_lo, b_route=b_route)

    xp = x_prompt.reshape(T_MAIN, D_MODEL)
    xe = jnp.concatenate([x_sample[:, 0, :], jnp.zeros((META_PAD, D_MODEL), F32), meta_tokens,
                          jnp.zeros((T_ALL - ROW_META - CHUNK, D_MODEL), F32)], axis=0)

    xn = _norm1(xp, xe, row(norm1_w[0]))
    zr = _proj(xn, w_r, 1664, F32, name="proj_rwkv")
    zg = _proj(xn, w_g, 1152, F32, name="proj_gla")
    zb = _proj(xn, w_b, 1024, BF16, silu_from=4, name="proj_gates")

    o_r, h_fin = _rwkv_prompt(zr, rw)
    o_g, s_fin = _gla_prompt(zg, zb, gl)

    r_s, w_s, kh_s, v_s, nkk_s, b_s, g_s = _rwkv_sample_pre(zr, state_shift[0][:, _SHIFT_PERM], rw)
    rw_new, y_s = _rwkv_sample_step(state_rwkv[0], r_s, w_s, kh_s, v_s, nkk_s, b_s)
    o_r_ext = _rwkv_sample_post(y_s, r_s, kh_s, v_s, g_s, rw)
    q_s, k_s, eg_s = _gla_sample_pre(zg, gl)
    v_gs = zg[ROW_SAMPLE:ROW_SAMPLE + N_SAMPLE, 2 * GLA_KDIM:2 * GLA_KDIM + GLA_WIDTH]
    gla_new, og_s = _gla_sample_step(state_gla[0], q_s, k_s, eg_s, v_gs)
    o_g_ext = _gla_sample_post(og_s, zb, gl)

    h1, xn2, info, counts = _mix_route(o_r, o_r_ext, o_g, o_g_ext, zb, xp, xe, mix)
    tile_expert, n_tiles, src, dst = _route_tables(info, counts)
    eo = _moe(tile_expert, n_tiles, src, xn2, moe_w1[0], moe_w3[0], moe_w2[0])
    y_p, y_e = _combine(dst, h1, info, row(final_norm_w), eo)

    y_prompt = y_p.reshape(N_BATCH, SEQ, D_MODEL)
    y_sample = y_e[0:N_SAMPLE].reshape(N_SAMPLE, 1, D_MODEL)
    new_rwkv_prompt = h_fin.reshape(N_BATCH, RW_HEAD, RW_HEADS, RW_HEAD).transpose(0, 2, 3, 1)[None]
    shift_rows = jnp.concatenate([zr[SEQ - 1:T_MAIN:SEQ], zr[ROW_SAMPLE:ROW_SAMPLE + N_SAMPLE]], axis=0)
    shift_rows = shift_rows[:, _SHIFT_INV]
    return (y_prompt, y_sample, new_rwkv_prompt, shift_rows[None, 0:N_BATCH], s_fin[None],
            rw_new[None], shift_rows[None, N_BATCH:], gla_new[None])
```

```python
import functools
import math

import numpy as np
import jax
import jax.numpy as jnp
from jax import lax
from jax.experimental import pallas as pl
from jax.experimental.pallas import tpu as pltpu

F32 = jnp.float32
BF16 = jnp.bfloat16

D_MODEL = 2048
N_BATCH = 4
SEQ = 2048
N_SAMPLE = 128
N_META = 16
RMS_EPS = 1e-6

RW_WIDTH = 1024
RW_HEAD = 64
RW_HEADS = 16
RW_GN_EPS = RW_HEAD * 1e-5
W_SHIFT = 3328
GLA_HEADS = 4
GLA_DK = 128
GLA_DV = 256
GLA_KDIM = 512
GLA_WIDTH = 1024
GLA_LORA = 16
GLA_NORMALIZER = 16.0
W_GLA_PAD = 2304
OG_BLOCK = 2 * D_MODEL // GLA_WIDTH
N_GROUPS = 4
EXPERTS_PER_GROUP = 8
N_EXPERTS = 32
D_EXPERT = 512

CHUNK = 64
N_CHUNKS = SEQ // CHUNK
META_PAD = CHUNK - N_META
T_MAIN = N_BATCH * SEQ
ROW_SAMPLE = T_MAIN
ROW_META = T_MAIN + N_SAMPLE
T_EXT = 256
T_ALL = T_MAIN + T_EXT
LANES = 128

TM_DENSE = 256
TM_MM = 1056
TM_MOE = 256
NT_MOE = (2 * T_ALL) // TM_MOE + N_EXPERTS
NEG_BIG = -1e30

VMEM_LIMIT = 56 * 1024 * 1024

_SHIFT_PERM = np.concatenate([np.arange(0, 1024), np.arange(1088, 2112), np.arange(2112, 3136),
                              np.arange(1024, 1088), np.arange(3136, 3328)])
_SHIFT_INV = np.argsort(_SHIFT_PERM)


def _cparams(n_axes=1):
    return pltpu.CompilerParams(dimension_semantics=("arbitrary",) * n_axes, vmem_limit_bytes=VMEM_LIMIT)


def _bdot(a, b):
    return jnp.dot(a.astype(BF16), b.astype(BF16), preferred_element_type=F32)


def _bdot_nt(a, b):
    return lax.dot_general(a.astype(BF16), b.astype(BF16), (((1,), (1,)), ((), ())), preferred_element_type=F32)


def _bdot_tn(a, b):
    return lax.dot_general(a.astype(BF16), b.astype(BF16), (((0,), (0,)), ((), ())), preferred_element_type=F32)


def _split_dot(m_bf16, x):
    hi = x.astype(BF16)
    lo = (x - hi.astype(F32)).astype(BF16)
    return (jnp.dot(m_bf16, hi, preferred_element_type=F32) + jnp.dot(m_bf16, lo, preferred_element_type=F32))


def _iota(shape, dim):
    return lax.broadcasted_iota(jnp.int32, shape, dim)


def _tri_incl(n):
    return (_iota((n, n), 0) >= _iota((n, n), 1)).astype(BF16)


def _seg_sum(x, width):
    m, n = x.shape
    nb = n // LANES
    bd = ((_iota((LANES, LANES), 0) // width) == (_iota((LANES, LANES), 1) // width)).astype(BF16)
    xs = jnp.concatenate([x[:, j * LANES:(j + 1) * LANES] for j in range(nb)], axis=0)
    hi = xs.astype(BF16)
    lo = (xs - hi.astype(F32)).astype(BF16)
    s = jnp.dot(hi, bd, preferred_element_type=F32) + jnp.dot(lo, bd, preferred_element_type=F32)
    return jnp.concatenate([s[j * m:(j + 1) * m] for j in range(nb)], axis=1)


def _col_of_row(row):
    n = row.shape[-1]
    eye = _iota((n, n), 0) == _iota((n, n), 1)
    return jnp.sum(jnp.where(eye, jnp.broadcast_to(row, (n, n)), 0.0), axis=-1, keepdims=True)


def _norm1_kernel(xp_ref, xe_ref, w_ref, o_ref):
    i = pl.program_id(0)

    def f(x):
        ms = jnp.mean(x * x, axis=-1, keepdims=True)
        return (x * lax.rsqrt(ms + RMS_EPS) * w_ref[...]).astype(BF16)

    @pl.when(i < T_MAIN // TM_DENSE)
    def _():
        o_ref[...] = f(xp_ref[...])

    @pl.when(i == T_MAIN // TM_DENSE)
    def _():
        o_ref[...] = f(xe_ref[...])


def _norm1(xp, xe, w):
    nmain = T_MAIN // TM_DENSE
    return pl.pallas_call(
        _norm1_kernel,
        grid=(T_ALL // TM_DENSE,),
        in_specs=[pl.BlockSpec((TM_DENSE, D_MODEL), lambda i: (jnp.minimum(i, nmain - 1), 0)),
                  pl.BlockSpec((T_EXT, D_MODEL), lambda i: (0, 0)),
                  pl.BlockSpec((1, D_MODEL), lambda i: (0, 0))],
        out_specs=pl.BlockSpec((TM_DENSE, D_MODEL), lambda i: (i, 0)),
        out_shape=jax.ShapeDtypeStruct((T_ALL, D_MODEL), BF16),
        compiler_params=_cparams(),
        name="norm1",
    )(xp, xe, w)


def _mm_kernel(x_ref, w_ref, o_ref, *, silu_from):
    z = jnp.dot(x_ref[...], w_ref[...], preferred_element_type=F32)
    if silu_from is None:
        o_ref[...] = z.astype(o_ref.dtype)
    else:
        s = jax.nn.sigmoid(z)
        o_ref[...] = jnp.where(pl.program_id(1) >= silu_from, z * s, s).astype(o_ref.dtype)


def _proj(x, w, tn, out_dtype, silu_from=None, name="proj"):
    t, k = x.shape
    n = w.shape[1]
    return pl.pallas_call(
        functools.partial(_mm_kernel, silu_from=silu_from),
        grid=(t // TM_MM, n // tn),
        in_specs=[pl.BlockSpec((TM_MM, k), lambda i, j: (i, 0)),
                  pl.BlockSpec((k, tn), lambda i, j: (0, j))],
        out_specs=pl.BlockSpec((TM_MM, tn), lambda i, j: (i, j)),
        out_shape=jax.ShapeDtypeStruct((t, n), out_dtype),
        compiler_params=_cparams(2),
        name=name,
    )(x, w)


def _rwkv_pre(z, w0, w2a, a0, g2, k_k, k_a):
    r = z[:, 0:1024]
    k = z[:, 1024:2048]
    v = z[:, 2048:3072]
    wa = z[:, 3072:3200]
    gd = z[:, 3200:3328]
    lane = _iota(wa.shape, 1)
    wa = jnp.where(lane < 64, jnp.tanh(wa), wa)
    up = _bdot(wa, w2a)
    lw = -math.exp(-0.5) * jax.nn.sigmoid(w0 + up[:, :1024])
    a = jax.nn.sigmoid(a0 + up[:, 1024:])
    g = _bdot(jax.nn.sigmoid(gd), g2)
    kk = k * k_k
    kk = kk * lax.rsqrt(jnp.maximum(_seg_sum(kk * kk, RW_HEAD), 1e-24))
    k_h = k * (1.0 + (a - 1.0) * k_a)
    return r, lw, k_h, v, kk, kk * a, g


def _rwkv_post(y, r, k_h, v, g, r_k, lnx_w, lnx_b):
    mean = _seg_sum(y, RW_HEAD) * (1.0 / RW_HEAD)
    d = y - mean
    var = _seg_sum(d * d, RW_HEAD) * (1.0 / RW_HEAD)
    yn = d * lax.rsqrt(var + RW_GN_EPS) * lnx_w + lnx_b
    bonus = _seg_sum(r * k_h * r_k, RW_HEAD) * v
    return (yn + bonus) * g


def _stack2(x):
    lane = _iota(x.shape, 1)
    return jnp.concatenate([jnp.where(lane < RW_HEAD, x, 0.0), jnp.where(lane >= RW_HEAD, x, 0.0)], axis=0)


def _unstack2(x):
    c = x.shape[0] // 2
    return x[:c] + x[c:]


def _rwkv_chunk_kernel(zs_ref, mu_ref, w0_ref, w2a_ref, a0_ref, g2_ref, kk_ref, ka_ref, rk_ref, lw_ref, lb_ref,
                       o_ref, hout_ref, prev_sc, h_sc, hmeta_sc, prevmeta_sc, y_sc, *, n_chunks):
    i = pl.program_id(0)
    is_meta = i == 0
    c = lax.rem(jnp.maximum(i - 1, 0), n_chunks)
    first = jnp.logical_and(i >= 1, c == 0)

    @pl.when(is_meta)
    def _():
        h_sc[...] = jnp.zeros_like(h_sc)
        prev_sc[...] = jnp.zeros_like(prev_sc)

    @pl.when(first)
    def _():
        h_sc[...] = hmeta_sc[...]
        prev_sc[...] = prevmeta_sc[...]

    zs = zs_ref[...]
    rowi = _iota((CHUNK, 1), 0)
    sh = pltpu.roll(zs, 1, 0)
    sh = jnp.where(rowi == 0, prev_sc[...], sh)
    prev_sc[...] = zs[CHUNK - 1:CHUNK, :]
    z = zs + (sh - zs) * mu_ref[...]
    r, lw, k_h, v, kk, b, g = _rwkv_pre(z, w0_ref[...], w2a_ref[...], a0_ref[...], g2_ref[...], kk_ref[...],
                                        ka_ref[...])
    lw = jnp.where(jnp.logical_and(is_meta, rowi < META_PAD), 0.0, lw)

    cl = _split_dot(_tri_incl(CHUNK), lw)
    cl_end = cl[CHUNK - 1:CHUNK, :]
    e_neg = jnp.exp(-cl)
    e_end = jnp.exp(cl_end - cl)
    kkt = kk * jnp.exp(cl - lw)
    rt = r * jnp.exp(cl)
    bt = b * e_neg
    kt = k_h * e_neg
    bh = b * e_end
    kh = k_h * e_end
    e_c = jnp.exp(cl_end)

    n2 = 2 * CHUNK
    tok_r = jnp.bitwise_and(_iota((n2, n2), 0), CHUNK - 1)
    tok_c = jnp.bitwise_and(_iota((n2, n2), 1), CHUNK - 1)
    strict = tok_r > tok_c
    incl = tok_r >= tok_c

    pairs = range(RW_HEADS // 2)
    sls = [slice(j * LANES, (j + 1) * LANES) for j in pairs]
    s_kkt = [_stack2(kkt[:, sl]) for sl in sls]
    s_rt = [_stack2(rt[:, sl]) for sl in sls]
    s_v = [_stack2(v[:, sl]).astype(BF16) for sl in sls]
    rb = [jnp.concatenate([_stack2(bt[:, sl]), _stack2(kt[:, sl])], axis=0).astype(BF16) for sl in sls]
    aa = [_bdot_nt(s_kkt[j], rb[j]) for j in pairs]
    mm = [_bdot_nt(s_rt[j], rb[j]) for j in pairs]
    h_kv = [_bdot_tn(_stack2(kh[:, sls[j]]), s_v[j]) for j in pairs]
    a_b = [jnp.where(strict, aa[j][:, :n2], 0.0).astype(BF16) for j in pairs]
    a_k = [jnp.where(strict, aa[j][:, n2:], 0.0) for j in pairs]
    m_rb = [jnp.where(incl, mm[j][:, :n2], 0.0).astype(BF16) for j in pairs]
    m_rk = [jnp.where(incl, mm[j][:, n2:], 0.0) for j in pairs]
    akv = [_bdot(a_k[j], s_v[j]) for j in pairs]
    y_kv = [_bdot(m_rk[j], s_v[j]) for j in pairs]
    x = [jnp.concatenate([s_kkt[j], akv[j]], axis=1) for j in pairs]
    p = a_b
    x = [x[j] - _bdot(p[j], x[j]) for j in pairs]
    for _ in range(5):
        p = [_bdot(p[j], p[j]).astype(BF16) for j in pairs]
        x = [x[j] + _bdot(p[j], x[j]) for j in pairs]
    zq = [_bdot(m_rb[j], x[j]) for j in pairs]
    zb = [_bdot_tn(_stack2(bh[:, sls[j]]), x[j]) for j in pairs]
    h0 = [h_sc[j] for j in pairs]
    for j in pairs:
        q_eff = _unstack2(s_rt[j] - zq[j][:, :LANES])
        y_in = _unstack2(y_kv[j] - zq[j][:, LANES:])
        y_sc[:, sls[j]] = _bdot(q_eff, h0[j]) + y_in
    h_new = [_col_of_row(e_c[:, sls[j]]) * h0[j] - _bdot(zb[j][:, :LANES], h0[j]) + (h_kv[j] - zb[j][:, LANES:])
             for j in pairs]
    for j in pairs:
        h_sc[j] = h_new[j]

    @pl.when(jnp.logical_and(i >= 1, c == n_chunks - 1))
    def _():
        for j in pairs:
            hout_ref[0, :, sls[j]] = _unstack2(h_new[j])

    o = _rwkv_post(y_sc[...], r, k_h, v, g, rk_ref[...], lw_ref[...], lb_ref[...])
    o_ref[...] = o.astype(o_ref.dtype)

    @pl.when(is_meta)
    def _():
        hmeta_sc[...] = h_sc[...]
        prevmeta_sc[...] = prev_sc[...]


def _rwkv_prompt(zr, p, n_batch=N_BATCH, n_chunks=N_CHUNKS, meta_block=ROW_META // CHUNK):
    row = lambda n: pl.BlockSpec((1, n), lambda i: (0, 0))
    full = lambda a, b: pl.BlockSpec((a, b), lambda i: (0, 0))
    blk = lambda i: jnp.where(i == 0, meta_block, i - 1)
    return pl.pallas_call(
        functools.partial(_rwkv_chunk_kernel, n_chunks=n_chunks),
        grid=(1 + n_batch * n_chunks,),
        in_specs=[pl.BlockSpec((CHUNK, W_SHIFT), lambda i: (blk(i), 0)),
                  row(W_SHIFT), row(RW_WIDTH), full(LANES, 2 * RW_WIDTH), row(RW_WIDTH), full(LANES, RW_WIDTH),
                  row(RW_WIDTH), row(RW_WIDTH), row(RW_WIDTH), row(RW_WIDTH), row(RW_WIDTH)],
        out_specs=[pl.BlockSpec((CHUNK, RW_WIDTH), lambda i: (jnp.maximum(i - 1, 0), 0)),
                   pl.BlockSpec((1, RW_HEAD, RW_WIDTH), lambda i: (jnp.maximum(i - 1, 0) // n_chunks, 0, 0))],
        out_shape=[jax.ShapeDtypeStruct((n_batch * n_chunks * CHUNK, RW_WIDTH), BF16),
                   jax.ShapeDtypeStruct((n_batch, RW_HEAD, RW_WIDTH), F32)],
        scratch_shapes=[pltpu.VMEM((1, W_SHIFT), F32),
                        pltpu.VMEM((RW_HEADS // 2, LANES, LANES), F32),
                        pltpu.VMEM((RW_HEADS // 2, LANES, LANES), F32),
                        pltpu.VMEM((1, W_SHIFT), F32),
                        pltpu.VMEM((CHUNK, RW_WIDTH), F32)],
        compiler_params=_cparams(),
        name="rwkv_prompt",
    )(zr, p['mu'], p['w0'], p['w2a'], p['a0'], p['g2'], p['k_k'], p['k_a'], p['r_k'], p['lnx_w'], p['lnx_b'])


def _rwkv_sample_pre_kernel(zs_ref, prev_ref, mu_ref, w0_ref, w2a_ref, a0_ref, g2_ref, kk_ref, ka_ref,
                            r_ref, kh_ref, v_ref, g_ref, rt_ref, wt_ref, kht_ref, vt_ref, nkkt_ref, bt_ref):
    zs = zs_ref[...]
    z = zs + (prev_ref[...] - zs) * mu_ref[...]
    r, lw, k_h, v, kk, b, g = _rwkv_pre(z, w0_ref[...], w2a_ref[...], a0_ref[...], g2_ref[...], kk_ref[...],
                                        ka_ref[...])
    r_ref[...] = r
    kh_ref[...] = k_h
    v_ref[...] = v
    g_ref[...] = g
    rt_ref[...] = r.T
    wt_ref[...] = jnp.exp(lw).T
    kht_ref[...] = k_h.T
    vt_ref[...] = v.T
    nkkt_ref[...] = (-kk).T
    bt_ref[...] = b.T


def _rwkv_sample_pre(zr, prev, p):
    row = lambda n: pl.BlockSpec((1, n), lambda i: (0, 0))
    full = lambda a, b: pl.BlockSpec((a, b), lambda i: (0, 0))
    vec = jax.ShapeDtypeStruct((N_SAMPLE, RW_WIDTH), F32)
    vec_t = jax.ShapeDtypeStruct((RW_WIDTH, N_SAMPLE), F32)
    return pl.pallas_call(
        _rwkv_sample_pre_kernel,
        grid=(1,),
        in_specs=[pl.BlockSpec((N_SAMPLE, W_SHIFT), lambda i: (ROW_SAMPLE // N_SAMPLE, 0)),
                  full(N_SAMPLE, W_SHIFT),
                  row(W_SHIFT), row(RW_WIDTH), full(LANES, 2 * RW_WIDTH), row(RW_WIDTH), full(LANES, RW_WIDTH),
                  row(RW_WIDTH), row(RW_WIDTH)],
        out_specs=[full(N_SAMPLE, RW_WIDTH)] * 4 + [full(RW_WIDTH, N_SAMPLE)] * 6,
        out_shape=[vec] * 4 + [vec_t] * 6,
        compiler_params=_cparams(),
        name="rwkv_sample_pre",
    )(zr, prev, p['mu'], p['w0'], p['w2a'], p['a0'], p['g2'], p['k_k'], p['k_a'])


def _rwkv_sample_step_kernel(s_ref, r_ref, w_ref, kh_ref, v_ref, nkk_ref, b_ref, so_ref, y_ref):
    s = s_ref[0]
    sa = jnp.sum(s * nkk_ref[0][None], axis=1, keepdims=True)
    s_new = s * w_ref[0][None] + sa * b_ref[0][None] + v_ref[0] * kh_ref[0][None]
    so_ref[0] = s_new
    y_ref[0] = jnp.sum(s_new * r_ref[0][None], axis=1, keepdims=True)


def _rwkv_sample_step(state_t, r_t, w_t, kh_t, v_t, nkk_t, b_t):
    kvec = lambda t: t.reshape(RW_HEADS, RW_HEAD, N_SAMPLE)
    kspec = pl.BlockSpec((1, RW_HEAD, N_SAMPLE), lambda i: (i, 0, 0))
    vspec = pl.BlockSpec((1, RW_HEAD, 1, N_SAMPLE), lambda i: (i, 0, 0, 0))
    sspec = pl.BlockSpec((1, RW_HEAD, RW_HEAD, N_SAMPLE), lambda i: (i, 0, 0, 0))
    s_new, y = pl.pallas_call(
        _rwkv_sample_step_kernel,
        grid=(RW_HEADS,),
        in_specs=[sspec, kspec, kspec, kspec, vspec, kspec, kspec],
        out_specs=[sspec, vspec],
        out_shape=[jax.ShapeDtypeStruct(state_t.shape, F32),
                   jax.ShapeDtypeStruct((RW_HEADS, RW_HEAD, 1, N_SAMPLE), F32)],
        compiler_params=_cparams(),
        name="rwkv_sample_step",
    )(state_t, kvec(r_t), kvec(w_t), kvec(kh_t), v_t.reshape(RW_HEADS, RW_HEAD, 1, N_SAMPLE), kvec(nkk_t), kvec(b_t))
    return s_new, y.reshape(RW_WIDTH, N_SAMPLE)


def _rwkv_sample_post_kernel(yt_ref, r_ref, kh_ref, v_ref, g_ref, rk_ref, lw_ref, lb_ref, o_ref):
    o = _rwkv_post(yt_ref[...].T, r_ref[...], kh_ref[...], v_ref[...], g_ref[...], rk_ref[...], lw_ref[...],
                   lb_ref[...])
    o_ref[0:N_SAMPLE, :] = o.astype(o_ref.dtype)
    o_ref[N_SAMPLE:T_EXT, :] = jnp.zeros((T_EXT - N_SAMPLE, RW_WIDTH), o_ref.dtype)


def _rwkv_sample_post(y_t, r, kh, v, g, p):
    row = lambda n: pl.BlockSpec((1, n), lambda i: (0, 0))
    full = lambda a, b: pl.BlockSpec((a, b), lambda i: (0, 0))
    return pl.pallas_call(
        _rwkv_sample_post_kernel,
        grid=(1,),
        in_specs=[full(RW_WIDTH, N_SAMPLE)] + [full(N_SAMPLE, RW_WIDTH)] * 4 + [row(RW_WIDTH)] * 3,
        out_specs=full(T_EXT, RW_WIDTH),
        out_shape=jax.ShapeDtypeStruct((T_EXT, RW_WIDTH), BF16),
        compiler_params=_cparams(),
        name="rwkv_sample_post",
    )(y_t, r, kh, v, g, p['r_k'], p['lnx_w'], p['lnx_b'])


def _gla_logg(zg, gk_up, gk_b):
    gkd = zg[:, 2 * GLA_KDIM + GLA_WIDTH:2 * GLA_KDIM + GLA_WIDTH + LANES]
    x = _bdot(gkd, gk_up) + gk_b
    return (jnp.minimum(x, 0.0) - jnp.log(1.0 + jnp.exp(-jnp.abs(x)))) * (1.0 / GLA_NORMALIZER)


def _gla_post(o, og_act, norm_w):
    outs = []
    for h in range(GLA_HEADS):
        oh = o[:, h * GLA_DV:(h + 1) * GLA_DV]
        ms = jnp.mean(oh * oh, axis=-1, keepdims=True)
        outs.append(oh * lax.rsqrt(ms + RMS_EPS) * norm_w)
    return jnp.concatenate(outs, axis=1) * og_act


def _gla_chunk_kernel(zg_ref, og_ref, gkup_ref, gkb_ref, nw_ref, o_ref, sout_ref, s_sc, smeta_sc, *, n_chunks):
    i = pl.program_id(0)
    is_meta = i == 0
    c = lax.rem(jnp.maximum(i - 1, 0), n_chunks)
    first = jnp.logical_and(i >= 1, c == 0)

    @pl.when(is_meta)
    def _():
        s_sc[...] = jnp.zeros_like(s_sc)

    @pl.when(first)
    def _():
        s_sc[...] = smeta_sc[...]

    zg = zg_ref[...]
    rowi = _iota((CHUNK, 1), 0)
    logg = _gla_logg(zg, gkup_ref[...], gkb_ref[...])
    logg = jnp.where(jnp.logical_and(is_meta, rowi < META_PAD), 0.0, logg)
    bcum = _split_dot(_tri_incl(CHUNK), logg)
    b_end = bcum[CHUNK - 1:CHUNK, :]
    qt = zg[:, 0:GLA_KDIM] * (GLA_DK ** -0.5) * jnp.exp(bcum)
    kt = zg[:, GLA_KDIM:2 * GLA_KDIM] * jnp.exp(-bcum)
    ke = zg[:, GLA_KDIM:2 * GLA_KDIM] * jnp.exp(b_end - bcum)
    e_end = jnp.exp(b_end)
    causal = _iota((CHUNK, CHUNK), 0) >= _iota((CHUNK, CHUNK), 1)
    heads = range(GLA_HEADS)
    ks = [slice(h * GLA_DK, (h + 1) * GLA_DK) for h in heads]
    vh = [zg[:, 2 * GLA_KDIM + h * GLA_DV:2 * GLA_KDIM + (h + 1) * GLA_DV].astype(BF16) for h in heads]
    a = [jnp.where(causal, _bdot_nt(qt[:, ks[h]], kt[:, ks[h]]), 0.0) for h in heads]
    s0 = [s_sc[h] for h in heads]
    o_inter = [_bdot(qt[:, ks[h]], s0[h]) for h in heads]
    s_add = [_bdot_tn(ke[:, ks[h]], vh[h]) for h in heads]
    outs = [_bdot(a[h], vh[h]) + o_inter[h] for h in heads]
    s_new = [_col_of_row(e_end[:, ks[h]]) * s0[h] + s_add[h] for h in heads]
    for h in heads:
        s_sc[h] = s_new[h]

    @pl.when(jnp.logical_and(i >= 1, c == n_chunks - 1))
    def _():
        for h in heads:
            sout_ref[0, h] = s_new[h]

    o = _gla_post(jnp.concatenate(outs, axis=1), og_ref[...].astype(F32), nw_ref[...])
    o_ref[...] = o.astype(o_ref.dtype)

    @pl.when(is_meta)
    def _():
        smeta_sc[...] = s_sc[...]


def _gla_prompt(zg, zb, p, n_batch=N_BATCH, n_chunks=N_CHUNKS, meta_block=ROW_META // CHUNK, og_block=OG_BLOCK):
    full = lambda a, b: pl.BlockSpec((a, b), lambda i: (0, 0))
    blk = lambda i: jnp.where(i == 0, meta_block, i - 1)
    return pl.pallas_call(
        functools.partial(_gla_chunk_kernel, n_chunks=n_chunks),
        grid=(1 + n_batch * n_chunks,),
        in_specs=[pl.BlockSpec((CHUNK, W_GLA_PAD), lambda i: (blk(i), 0)),
                  pl.BlockSpec((CHUNK, GLA_WIDTH), lambda i: (blk(i), og_block)),
                  full(LANES, GLA_KDIM), full(1, GLA_KDIM), full(1, GLA_DV)],
        out_specs=[pl.BlockSpec((CHUNK, GLA_WIDTH), lambda i: (jnp.maximum(i - 1, 0), 0)),
                   pl.BlockSpec((1, GLA_HEADS, GLA_DK, GLA_DV),
                                lambda i: (jnp.maximum(i - 1, 0) // n_chunks, 0, 0, 0))],
        out_shape=[jax.ShapeDtypeStruct((n_batch * n_chunks * CHUNK, GLA_WIDTH), BF16),
                   jax.ShapeDtypeStruct((n_batch, GLA_HEADS, GLA_DK, GLA_DV), F32)],
        scratch_shapes=[pltpu.VMEM((GLA_HEADS, GLA_DK, GLA_DV), F32),
                        pltpu.VMEM((GLA_HEADS, GLA_DK, GLA_DV), F32)],
        compiler_params=_cparams(),
        name="gla_prompt",
    )(zg, zb, p['gk_up'], p['gk_b'], p['norm_w'])


def _gla_sample_pre_kernel(zg_ref, gkup_ref, gkb_ref, q_ref, k_ref, eg_ref):
    zg = zg_ref[...]
    logg = _gla_logg(zg, gkup_ref[...], gkb_ref[...])
    q_ref[...] = zg[:, 0:GLA_KDIM] * (GLA_DK ** -0.5)
    k_ref[...] = zg[:, GLA_KDIM:2 * GLA_KDIM]
    eg_ref[...] = jnp.exp(logg)


def _gla_sample_pre(zg, p):
    full = lambda a, b: pl.BlockSpec((a, b), lambda i: (0, 0))
    vec = jax.ShapeDtypeStruct((N_SAMPLE, GLA_KDIM), F32)
    return pl.pallas_call(
        _gla_sample_pre_kernel,
        grid=(1,),
        in_specs=[pl.BlockSpec((N_SAMPLE, W_GLA_PAD), lambda i: (ROW_SAMPLE // N_SAMPLE, 0)),
                  full(LANES, GLA_KDIM), full(1, GLA_KDIM)],
        out_specs=[full(N_SAMPLE, GLA_KDIM)] * 3,
        out_shape=[vec] * 3,
        compiler_params=_cparams(),
        name="gla_sample_pre",
    )(zg, p['gk_up'], p['gk_b'])


def _gla_sample_step_kernel(s_ref, q_ref, k_ref, eg_ref, v_ref, so_ref, o_ref):
    s = s_ref[0]
    eye = _iota((GLA_DK, GLA_DK), 0) == _iota((GLA_DK, GLA_DK), 1)
    col = lambda ref: jnp.sum(jnp.where(eye, ref[0], 0.0), axis=-1, keepdims=True)
    s_new = col(eg_ref) * s + col(k_ref) * v_ref[0]
    so_ref[0] = s_new
    o_ref[0] = jnp.sum(col(q_ref) * s_new, axis=-2, keepdims=True)


def _gla_sample_step(state, q, k, eg, v):
    kv = lambda t: t.reshape(N_SAMPLE, GLA_HEADS, 1, GLA_DK)
    kspec = pl.BlockSpec((1, GLA_HEADS, 1, GLA_DK), lambda i: (i, 0, 0, 0))
    vspec = pl.BlockSpec((1, GLA_HEADS, 1, GLA_DV), lambda i: (i, 0, 0, 0))
    sspec = pl.BlockSpec((1, GLA_HEADS, GLA_DK, GLA_DV), lambda i: (i, 0, 0, 0))
    s_new, o = pl.pallas_call(
        _gla_sample_step_kernel,
        grid=(N_SAMPLE,),
        in_specs=[sspec, kspec, kspec, kspec, vspec],
        out_specs=[sspec, vspec],
        out_shape=[jax.ShapeDtypeStruct(state.shape, F32),
                   jax.ShapeDtypeStruct((N_SAMPLE, GLA_HEADS, 1, GLA_DV), F32)],
        compiler_params=_cparams(),
        name="gla_sample_step",
    )(state, kv(q), kv(k), kv(eg), v.reshape(N_SAMPLE, GLA_HEADS, 1, GLA_DV))
    return s_new, o.reshape(N_SAMPLE, GLA_WIDTH)


def _gla_sample_post_kernel(o_ref, og_ref, nw_ref, out_ref):
    o = _gla_post(o_ref[...], og_ref[0:N_SAMPLE, :].astype(F32), nw_ref[...])
    out_ref[0:N_SAMPLE, :] = o.astype(out_ref.dtype)
    out_ref[N_SAMPLE:T_EXT, :] = jnp.zeros((T_EXT - N_SAMPLE, GLA_WIDTH), out_ref.dtype)


def _gla_sample_post(o, zb, p, og_block=OG_BLOCK):
    full = lambda a, b: pl.BlockSpec((a, b), lambda i: (0, 0))
    return pl.pallas_call(
        _gla_sample_post_kernel,
        grid=(1,),
        in_specs=[full(N_SAMPLE, GLA_WIDTH),
                  pl.BlockSpec((T_EXT, GLA_WIDTH), lambda i: (T_MAIN // T_EXT, og_block)),
                  full(1, GLA_DV)],
        out_specs=full(T_EXT, GLA_WIDTH),
        out_shape=jax.ShapeDtypeStruct((T_EXT, GLA_WIDTH), BF16),
        compiler_params=_cparams(),
        name="gla_sample_post",
    )(o, zb, p['norm_w'])


def _mix_route_kernel(or_ref, ore_ref, og_ref, oge_ref, gates_ref, xp_ref, xe_ref, pr_ref, pg_ref, wo_ref, n2_ref,
                      wrh_ref, wrl_ref, br_ref, h1_ref, xn_ref, info_ref, cnt_ref, carry_sc):
    i = pl.program_id(0)
    is_main = i < T_MAIN // TM_DENSE

    @pl.when(i == 0)
    def _():
        carry_sc[...] = jnp.zeros_like(carry_sc)

    sig_r = gates_ref[:, 0:D_MODEL].astype(F32)
    sig_g = gates_ref[:, D_MODEL:2 * D_MODEL].astype(F32)
    o_r = jnp.where(is_main, or_ref[...], ore_ref[...])
    o_g = jnp.where(is_main, og_ref[...], oge_ref[...])
    m = (sig_r * jnp.dot(o_r, pr_ref[...], preferred_element_type=F32)
         + sig_g * jnp.dot(o_g, pg_ref[...], preferred_element_type=F32))
    h = jnp.where(is_main, xp_ref[...], xe_ref[...])
    h1 = h + jnp.dot(m.astype(BF16), wo_ref[...], preferred_element_type=F32)
    h1_ref[...] = h1
    ms = jnp.mean(h1 * h1, axis=-1, keepdims=True)
    xn = h1 * lax.rsqrt(ms + RMS_EPS) * n2_ref[...]
    xn_ref[...] = xn
    xh = xn.astype(BF16)
    xl = (xn - xh.astype(F32)).astype(BF16)
    lg = (jnp.dot(xh, wrh_ref[...], preferred_element_type=F32)
          + jnp.dot(xh, wrl_ref[...], preferred_element_type=F32)
          + jnp.dot(xl, wrh_ref[...], preferred_element_type=F32)) + br_ref[...]

    lane = _iota(lg.shape, 1)
    lanef = lane.astype(F32)
    is_g = jnp.logical_and(lane >= N_EXPERTS, lane < N_EXPERTS + N_GROUPS)
    gl = jnp.where(is_g, lg, NEG_BIG)
    gmax = jnp.max(gl, axis=-1, keepdims=True)
    gsel = jnp.min(jnp.where(jnp.logical_and(is_g, gl == gmax), lanef, 1e9), axis=-1, keepdims=True) - N_EXPERTS
    p_g = 1.0 / jnp.sum(jnp.exp(gl - gmax), axis=-1, keepdims=True)
    grp = (lane // EXPERTS_PER_GROUP).astype(F32)
    in_grp = jnp.logical_and(lane < N_EXPERTS, grp == gsel)
    el = jnp.where(in_grp, lg, NEG_BIG)
    m1 = jnp.max(el, axis=-1, keepdims=True)
    i1 = jnp.min(jnp.where(jnp.logical_and(in_grp, el == m1), lanef, 1e9), axis=-1, keepdims=True)
    in2 = jnp.logical_and(in_grp, lanef != i1)
    el2 = jnp.where(in2, lg, NEG_BIG)
    m2 = jnp.max(el2, axis=-1, keepdims=True)
    i2 = jnp.min(jnp.where(jnp.logical_and(in2, el2 == m2), lanef, 1e9), axis=-1, keepdims=True)
    e2 = jnp.exp(m2 - m1)
    w1 = p_g / (1.0 + e2)
    w2 = p_g * e2 / (1.0 + e2)

    oh1 = lanef == i1
    oh2 = lanef == i2
    cnt = jnp.where(jnp.logical_or(oh1, oh2), 1.0, 0.0)
    tm = cnt.shape[0]
    lstrict = (_iota((tm, tm), 0) > _iota((tm, tm), 1)).astype(BF16)
    before = jnp.dot(lstrict, cnt.astype(BF16), preferred_element_type=F32) + carry_sc[...]
    rank1 = jnp.sum(jnp.where(oh1, before, 0.0), axis=-1, keepdims=True)
    rank2 = jnp.sum(jnp.where(oh2, before, 0.0), axis=-1, keepdims=True)
    carry_sc[...] = carry_sc[...] + jnp.sum(cnt, axis=0, keepdims=True)
    cnt_ref[...] = carry_sc[...]
    info = jnp.where(lane == 0, i1, jnp.where(lane == 1, i2, jnp.where(lane == 2, w1, jnp.where(
        lane == 3, w2, jnp.where(lane == 4, rank1, jnp.where(lane == 5, rank2, 0.0))))))
    info_ref[...] = info


def _mix_route(o_r, o_r_ext, o_g, o_g_ext, zb, xp, xe, p):
    nmain = T_MAIN // TM_DENSE
    tile = lambda n: pl.BlockSpec((TM_DENSE, n), lambda i: (i, 0))
    main = lambda n: pl.BlockSpec((TM_DENSE, n), lambda i: (jnp.minimum(i, nmain - 1), 0))
    ext = lambda n: pl.BlockSpec((T_EXT, n), lambda i: (0, 0))
    const = lambda a, b: pl.BlockSpec((a, b), lambda i: (0, 0), pipeline_mode=pl.Buffered(1))
    return pl.pallas_call(
        _mix_route_kernel,
        grid=(T_ALL // TM_DENSE,),
        in_specs=[main(RW_WIDTH), ext(RW_WIDTH), main(GLA_WIDTH), ext(GLA_WIDTH), tile(2 * D_MODEL),
                  main(D_MODEL), ext(D_MODEL),
                  const(RW_WIDTH, D_MODEL), const(GLA_WIDTH, D_MODEL), const(D_MODEL, D_MODEL),
                  const(1, D_MODEL), const(D_MODEL, LANES), const(D_MODEL, LANES), const(1, LANES)],
        out_specs=[tile(D_MODEL), tile(D_MODEL), tile(LANES), pl.BlockSpec((1, LANES), lambda i: (0, 0))],
        out_shape=[jax.ShapeDtypeStruct((T_ALL, D_MODEL), F32),
                   jax.ShapeDtypeStruct((T_ALL, D_MODEL), F32),
                   jax.ShapeDtypeStruct((T_ALL, LANES), F32),
                   jax.ShapeDtypeStruct((1, LANES), F32)],
        scratch_shapes=[pltpu.VMEM((1, LANES), F32)],
        compiler_params=_cparams(),
        name="mix_route",
    )(o_r, o_r_ext, o_g, o_g_ext, zb, xp, xe, p['p_rwkv'], p['p_gla'], p['w_out'], p['norm2_w'], p['wr_hi'],
      p['wr_lo'], p['b_route'])


def _moe_kernel(te_ref, nt_ref, src_ref, xn_hbm, w1_ref, w3_ref, w2_ref, o_ref, xbuf, w1b, w3b, w2b, sem):
    i = pl.program_id(0)

    @pl.when(i < nt_ref[0])
    def _():
        def row_copy(t):
            return pltpu.make_async_copy(xn_hbm.at[pl.ds(src_ref[0, 0, t], 1), :], xbuf.at[pl.ds(t, 1), :], sem)

        def issue(t, carry):
            row_copy(t).start()
            return carry

        lax.fori_loop(0, TM_MOE, issue, 0)

        changed = jnp.logical_or(i == 0, te_ref[i] != te_ref[jnp.maximum(i - 1, 0)])

        @pl.when(changed)
        def _():
            w1b[...] = w1_ref[0].astype(BF16)
            w3b[...] = w3_ref[0].astype(BF16)
            w2b[...] = w2_ref[0].astype(BF16)

        def drain(t, carry):
            row_copy(t).wait()
            return carry

        lax.fori_loop(0, TM_MOE, drain, 0)

        xb = xbuf[...].astype(BF16)
        h1 = jnp.dot(xb, w1b[...], preferred_element_type=F32)
        h3 = jnp.dot(xb, w3b[...], preferred_element_type=F32)
        hh = (h1 * jax.nn.sigmoid(h1) * h3).astype(BF16)
        o_ref[...] = jnp.dot(hh, w2b[...], preferred_element_type=F32)

    @pl.when(i >= nt_ref[0])
    def _():
        o_ref[...] = jnp.zeros_like(o_ref)


def _moe(tile_expert, n_tiles, src, xn, w1, w3, w2):
    grid_spec = pltpu.PrefetchScalarGridSpec(
        num_scalar_prefetch=2,
        grid=(NT_MOE,),
        in_specs=[pl.BlockSpec((1, 1, TM_MOE), lambda i, te, nt: (i, 0, 0), memory_space=pltpu.SMEM),
                  pl.BlockSpec(memory_space=pl.ANY),
                  pl.BlockSpec((1, D_MODEL, D_EXPERT), lambda i, te, nt: (te[i], 0, 0)),
                  pl.BlockSpec((1, D_MODEL, D_EXPERT), lambda i, te, nt: (te[i], 0, 0)),
                  pl.BlockSpec((1, D_EXPERT, D_MODEL), lambda i, te, nt: (te[i], 0, 0))],
        out_specs=pl.BlockSpec((TM_MOE, D_MODEL), lambda i, te, nt: (i, 0)),
        scratch_shapes=[pltpu.VMEM((TM_MOE, D_MODEL), F32),
                        pltpu.VMEM((D_MODEL, D_EXPERT), BF16),
                        pltpu.VMEM((D_MODEL, D_EXPERT), BF16),
                        pltpu.VMEM((D_EXPERT, D_MODEL), BF16),
                        pltpu.SemaphoreType.DMA(())],
    )
    return pl.pallas_call(
        _moe_kernel,
        grid_spec=grid_spec,
        out_shape=jax.ShapeDtypeStruct((NT_MOE * TM_MOE, D_MODEL), F32),
        compiler_params=_cparams(),
        name="moe_experts",
    )(tile_expert, n_tiles, src, xn, w1, w3, w2)


def _combine_kernel(dst_ref, h1_ref, info_ref, fw_ref, eo_hbm, yp_ref, ye_ref, buf, sem):
    i = pl.program_id(0)
    nmain = T_MAIN // TM_DENSE

    def row_copy(t):
        return pltpu.make_async_copy(eo_hbm.at[pl.ds(dst_ref[0, 0, t], 1), :], buf.at[pl.ds(t, 1), :], sem)

    def issue(t, carry):
        row_copy(t).start()
        return carry

    lax.fori_loop(0, 2 * TM_DENSE, issue, 0)

    def drain(t, carry):
        row_copy(t).wait()
        return carry

    lax.fori_loop(0, 2 * TM_DENSE, drain, 0)

    info = info_ref[...]
    y = h1_ref[...] + info[:, 2:3] * buf[0:TM_DENSE, :] + info[:, 3:4] * buf[TM_DENSE:2 * TM_DENSE, :]
    ms = jnp.mean(y * y, axis=-1, keepdims=True)
    out = y * lax.rsqrt(ms + RMS_EPS) * fw_ref[...]

    @pl.when(i < nmain)
    def _():
        yp_ref[...] = out

    @pl.when(i == nmain)
    def _():
        ye_ref[...] = out


def _combine(dst, h1, info, fw, eo):
    nmain = T_MAIN // TM_DENSE
    return pl.pallas_call(
        _combine_kernel,
        grid=(T_ALL // TM_DENSE,),
        in_specs=[pl.BlockSpec((1, 1, 2 * TM_DENSE), lambda i: (i, 0, 0), memory_space=pltpu.SMEM),
                  pl.BlockSpec((TM_DENSE, D_MODEL), lambda i: (i, 0)),
                  pl.BlockSpec((TM_DENSE, LANES), lambda i: (i, 0)),
                  pl.BlockSpec((1, D_MODEL), lambda i: (0, 0)),
                  pl.BlockSpec(memory_space=pl.ANY)],
        out_specs=[pl.BlockSpec((TM_DENSE, D_MODEL), lambda i: (jnp.minimum(i, nmain - 1), 0)),
                   pl.BlockSpec((T_EXT, D_MODEL), lambda i: (0, 0))],
        out_shape=[jax.ShapeDtypeStruct((T_MAIN, D_MODEL), F32),
                   jax.ShapeDtypeStruct((T_EXT, D_MODEL), F32)],
        scratch_shapes=[pltpu.VMEM((2 * TM_DENSE, D_MODEL), F32), pltpu.SemaphoreType.DMA(())],
        compiler_params=_cparams(),
        name="moe_combine",
    )(dst, h1, info, fw, eo)


def _route_tables(info, counts):
    cnt = counts[0, :N_EXPERTS].astype(jnp.int32)
    tiles_e = (cnt + TM_MOE - 1) // TM_MOE
    tile_end = jnp.cumsum(tiles_e)
    tile_start = tile_end - tiles_e
    n_tiles = tile_end[-1]
    tile_idx = jnp.arange(NT_MOE, dtype=jnp.int32)
    tile_expert = jnp.minimum(jnp.sum((tile_end[None, :] <= tile_idx[:, None]).astype(jnp.int32), axis=1),
                              N_EXPERTS - 1)
    last_e = tile_expert[jnp.maximum(n_tiles - 1, 0)]
    tile_expert = jnp.where(jnp.arange(NT_MOE) < n_tiles, tile_expert, last_e).astype(jnp.int32)
    e1 = info[:, 0].astype(jnp.int32)
    e2 = info[:, 1].astype(jnp.int32)
    d1 = tile_start[e1] * TM_MOE + info[:, 4].astype(jnp.int32)
    d2 = tile_start[e2] * TM_MOE + info[:, 5].astype(jnp.int32)
    tok = jnp.arange(T_ALL, dtype=jnp.int32)
    src = jnp.zeros((NT_MOE * TM_MOE,), jnp.int32).at[d1].set(tok).at[d2].set(tok)
    nt = T_ALL // TM_DENSE
    dst = jnp.concatenate([d1.reshape(nt, 1, TM_DENSE), d2.reshape(nt, 1, TM_DENSE)], axis=2)
    return tile_expert, n_tiles.reshape(1).astype(jnp.int32), src.reshape(NT_MOE, 1, TM_MOE), dst


def kernel(x_prompt, x_sample, state_rwkv, state_shift, state_gla, meta_tokens, norm1_w, w_in, mu_shift, rw_w0,
           rw_w2, rw_a0, rw_a2, rw_g2, rw_k_k, rw_k_a, rw_r_k, rw_lnx_w, rw_lnx_b, gla_gk_up, gla_gk_b, gla_norm_w,
           p_rwkv, p_gla, w_out, norm2_w, moe_w_group, moe_b_group, moe_w_router, moe_b_router, moe_w1, moe_w3,
           moe_w2, final_norm_w):
    w = w_in[0]
    w_r = jnp.concatenate([w[:, 0:1024], w[:, 1088:3136], w[:, 1024:1088], w[:, 3136:3328]], axis=1).astype(BF16)
    w_g = jnp.concatenate([w[:, 3328:5392], jnp.zeros((D_MODEL, W_GLA_PAD - 2064), F32)], axis=1).astype(BF16)
    w_b = jnp.concatenate([w[:, 6416:10512], w[:, 5392:6416]], axis=1).astype(BF16)
    row = lambda t: t.reshape(1, -1)
    w2a = jnp.zeros((LANES, 2 * RW_WIDTH), F32)
    w2a = w2a.at[0:64, 0:RW_WIDTH].set(rw_w2[0]).at[64:128, RW_WIDTH:].set(rw_a2[0])
    rw = dict(mu=row(mu_shift[0][_SHIFT_PERM]), w0=row(rw_w0[0]), w2a=w2a, a0=row(rw_a0[0]), g2=rw_g2[0],
              k_k=row(rw_k_k[0]), k_a=row(rw_k_a[0]), r_k=row(rw_r_k[0]), lnx_w=row(rw_lnx_w[0]),
              lnx_b=row(rw_lnx_b[0]))
    gk_up = jnp.zeros((LANES, GLA_KDIM), F32).at[0:GLA_LORA].set(gla_gk_up[0])
    gl = dict(gk_up=gk_up, gk_b=row(gla_gk_b[0]), norm_w=row(gla_norm_w[0]))
    w_route = jnp.zeros((D_MODEL, LANES), F32)
    w_route = w_route.at[:, 0:N_EXPERTS].set(moe_w_router[0]).at[:, N_EXPERTS:N_EXPERTS + N_GROUPS].set(moe_w_group[0])
    wr_hi = w_route.astype(BF16)
    wr_lo = (w_route - wr_hi.astype(F32)).astype(BF16)
    b_route = jnp.zeros((1, LANES), F32)
    b_route = b_route.at[0, 0:N_EXPERTS].set(moe_b_router[0]).at[0, N_EXPERTS:N_EXPERTS + N_GROUPS].set(moe_b_group[0])
    mix = dict(p_rwkv=p_rwkv[0].astype(BF16), p_gla=p_gla[0].astype(BF16), w_out=w_out[0].astype(BF16),
               norm2_w=row(norm2_w[0]), wr_hi=wr_hi, wr_lo=wr_lo, b_route=b_route)

    xp = x_prompt.reshape(T_MAIN, D_MODEL)
    xe = jnp.concatenate([x_sample[:, 0, :], jnp.zeros((META_PAD, D_MODEL), F32), meta_tokens,
                          jnp.zeros((T_ALL - ROW_META - CHUNK, D_MODEL), F32)], axis=0)

    xn = _norm1(xp, xe, row(norm1_w[0]))
    zr = _proj(xn, w_r, 1664, F32, name="proj_rwkv")
    zg = _proj(xn, w_g, 1152, F32, name="proj_gla")
    zb = _proj(xn, w_b, 1024, BF16, silu_from=4, name="proj_gates")

    o_r, h_fin = _rwkv_prompt(zr, rw)
    o_g, s_fin = _gla_prompt(zg, zb, gl)

    r_s, kh_s, v_s, g_s, r_t, w_t, kh_t, v_t, nkk_t, b_t = _rwkv_sample_pre(zr, state_shift[0][:, _SHIFT_PERM], rw)
    rw_new_t, y_t = _rwkv_sample_step(state_rwkv[0].transpose(1, 2, 3, 0), r_t, w_t, kh_t, v_t, nkk_t, b_t)
    rw_new = rw_new_t.transpose(3, 0, 1, 2)
    o_r_ext = _rwkv_sample_post(y_t, r_s, kh_s, v_s, g_s, rw)
    q_s, k_s, eg_s = _gla_sample_pre(zg, gl)
    v_gs = zg[ROW_SAMPLE:ROW_SAMPLE + N_SAMPLE, 2 * GLA_KDIM:2 * GLA_KDIM + GLA_WIDTH]
    gla_new, og_s = _gla_sample_step(state_gla[0], q_s, k_s, eg_s, v_gs)
    o_g_ext = _gla_sample_post(og_s, zb, gl)

    h1, xn2, info, counts = _mix_route(o_r, o_r_ext, o_g, o_g_ext, zb, xp, xe, mix)
    tile_expert, n_tiles, src, dst = _route_tables(info, counts)
    eo = _moe(tile_expert, n_tiles, src, xn2, moe_w1[0], moe_w3[0], moe_w2[0])
    y_p, y_e = _combine(dst, h1, info, row(final_norm_w), eo)

    y_prompt = y_p.reshape(N_BATCH, SEQ, D_MODEL)
    y_sample = y_e[0:N_SAMPLE].reshape(N_SAMPLE, 1, D_MODEL)
    new_rwkv_prompt = h_fin.reshape(N_BATCH, RW_HEAD, RW_HEADS, RW_HEAD).transpose(0, 2, 3, 1)[None]
    shift_rows = jnp.concatenate([zr[SEQ - 1:T_MAIN:SEQ], zr[ROW_SAMPLE:ROW_SAMPLE + N_SAMPLE]], axis=0)
    shift_rows = shift_rows[:, _SHIFT_INV]
    return (y_prompt, y_sample, new_rwkv_prompt, shift_rows[None, 0:N_BATCH], s_fin[None],
            rw_new[None], shift_rows[None, N_BATCH:], gla_new[None])
```

```python
import functools
import math

import numpy as np
import jax
import jax.numpy as jnp
from jax import lax
from jax.experimental import pallas as pl
from jax.experimental.pallas import tpu as pltpu

F32 = jnp.float32
BF16 = jnp.bfloat16

D_MODEL = 2048
N_BATCH = 4
SEQ = 2048
N_SAMPLE = 128
N_META = 16
RMS_EPS = 1e-6

RW_WIDTH = 1024
RW_HEAD = 64
RW_HEADS = 16
RW_GN_EPS = RW_HEAD * 1e-5
W_SHIFT = 3328
GLA_HEADS = 4
GLA_DK = 128
GLA_DV = 256
GLA_KDIM = 512
GLA_WIDTH = 1024
GLA_LORA = 16
GLA_NORMALIZER = 16.0
W_GLA_PAD = 2304
OG_BLOCK = 2 * D_MODEL // GLA_WIDTH
N_GROUPS = 4
EXPERTS_PER_GROUP = 8
N_EXPERTS = 32
D_EXPERT = 512

CHUNK = 64
N_CHUNKS = SEQ // CHUNK
META_PAD = CHUNK - N_META
T_MAIN = N_BATCH * SEQ
ROW_SAMPLE = T_MAIN
ROW_META = T_MAIN + N_SAMPLE
T_EXT = 256
T_ALL = T_MAIN + T_EXT
LANES = 128

TM_DENSE = 256
TM_MM = 1056
TM_MOE = 256
NT_MOE = (2 * T_ALL) // TM_MOE + N_EXPERTS
NEG_BIG = -1e30

VMEM_LIMIT = 56 * 1024 * 1024

_SHIFT_PERM = np.concatenate([np.arange(0, 1024), np.arange(1088, 2112), np.arange(2112, 3136),
                              np.arange(1024, 1088), np.arange(3136, 3328)])
_SHIFT_INV = np.argsort(_SHIFT_PERM)


def _cparams(n_axes=1):
    return pltpu.CompilerParams(dimension_semantics=("arbitrary",) * n_axes, vmem_limit_bytes=VMEM_LIMIT)


def _bdot(a, b):
    return jnp.dot(a.astype(BF16), b.astype(BF16), preferred_element_type=F32)


def _bdot_nt(a, b):
    return lax.dot_general(a.astype(BF16), b.astype(BF16), (((1,), (1,)), ((), ())), preferred_element_type=F32)


def _bdot_tn(a, b):
    return lax.dot_general(a.astype(BF16), b.astype(BF16), (((0,), (0,)), ((), ())), preferred_element_type=F32)


def _split_dot(m_bf16, x):
    hi = x.astype(BF16)
    lo = (x - hi.astype(F32)).astype(BF16)
    return (jnp.dot(m_bf16, hi, preferred_element_type=F32) + jnp.dot(m_bf16, lo, preferred_element_type=F32))


def _iota(shape, dim):
    return lax.broadcasted_iota(jnp.int32, shape, dim)


def _tri_incl(n):
    return (_iota((n, n), 0) >= _iota((n, n), 1)).astype(BF16)


def _seg_sum(x, width):
    m, n = x.shape
    nb = n // LANES
    bd = ((_iota((LANES, LANES), 0) // width) == (_iota((LANES, LANES), 1) // width)).astype(BF16)
    xs = jnp.concatenate([x[:, j * LANES:(j + 1) * LANES] for j in range(nb)], axis=0)
    hi = xs.astype(BF16)
    lo = (xs - hi.astype(F32)).astype(BF16)
    s = jnp.dot(hi, bd, preferred_element_type=F32) + jnp.dot(lo, bd, preferred_element_type=F32)
    return jnp.concatenate([s[j * m:(j + 1) * m] for j in range(nb)], axis=1)


def _col_of_row(row):
    n = row.shape[-1]
    eye = _iota((n, n), 0) == _iota((n, n), 1)
    return jnp.sum(jnp.where(eye, jnp.broadcast_to(row, (n, n)), 0.0), axis=-1, keepdims=True)


def _norm1_kernel(xp_ref, xe_ref, w_ref, o_ref):
    i = pl.program_id(0)

    def f(x):
        ms = jnp.mean(x * x, axis=-1, keepdims=True)
        return (x * lax.rsqrt(ms + RMS_EPS) * w_ref[...]).astype(BF16)

    @pl.when(i < T_MAIN // TM_DENSE)
    def _():
        o_ref[...] = f(xp_ref[...])

    @pl.when(i == T_MAIN // TM_DENSE)
    def _():
        o_ref[...] = f(xe_ref[...])


def _norm1(xp, xe, w):
    nmain = T_MAIN // TM_DENSE
    return pl.pallas_call(
        _norm1_kernel,
        grid=(T_ALL // TM_DENSE,),
        in_specs=[pl.BlockSpec((TM_DENSE, D_MODEL), lambda i: (jnp.minimum(i, nmain - 1), 0)),
                  pl.BlockSpec((T_EXT, D_MODEL), lambda i: (0, 0)),
                  pl.BlockSpec((1, D_MODEL), lambda i: (0, 0))],
        out_specs=pl.BlockSpec((TM_DENSE, D_MODEL), lambda i: (i, 0)),
        out_shape=jax.ShapeDtypeStruct((T_ALL, D_MODEL), BF16),
        compiler_params=_cparams(),
        name="norm1",
    )(xp, xe, w)


def _mm_kernel(x_ref, w_ref, o_ref, *, silu_from):
    z = jnp.dot(x_ref[...], w_ref[...], preferred_element_type=F32)
    if silu_from is None:
        o_ref[...] = z.astype(o_ref.dtype)
    else:
        s = jax.nn.sigmoid(z)
        o_ref[...] = jnp.where(pl.program_id(1) >= silu_from, z * s, s).astype(o_ref.dtype)


def _proj(x, w, tn, out_dtype, silu_from=None, name="proj"):
    t, k = x.shape
    n = w.shape[1]
    return pl.pallas_call(
        functools.partial(_mm_kernel, silu_from=silu_from),
        grid=(t // TM_MM, n // tn),
        in_specs=[pl.BlockSpec((TM_MM, k), lambda i, j: (i, 0)),
                  pl.BlockSpec((k, tn), lambda i, j: (0, j))],
        out_specs=pl.BlockSpec((TM_MM, tn), lambda i, j: (i, j)),
        out_shape=jax.ShapeDtypeStruct((t, n), out_dtype),
        compiler_params=_cparams(2),
        name=name,
    )(x, w)


def _rwkv_pre(z, w0, w2a, a0, g2, k_k, k_a):
    r = z[:, 0:1024]
    k = z[:, 1024:2048]
    v = z[:, 2048:3072]
    wa = z[:, 3072:3200]
    gd = z[:, 3200:3328]
    lane = _iota(wa.shape, 1)
    wa = jnp.where(lane < 64, jnp.tanh(wa), wa)
    up = _bdot(wa, w2a)
    lw = -math.exp(-0.5) * jax.nn.sigmoid(w0 + up[:, :1024])
    a = jax.nn.sigmoid(a0 + up[:, 1024:])
    g = _bdot(jax.nn.sigmoid(gd), g2)
    kk = k * k_k
    kk = kk * lax.rsqrt(jnp.maximum(_seg_sum(kk * kk, RW_HEAD), 1e-24))
    k_h = k * (1.0 + (a - 1.0) * k_a)
    return r, lw, k_h, v, kk, kk * a, g


def _rwkv_post(y, r, k_h, v, g, r_k, lnx_w, lnx_b):
    mean = _seg_sum(y, RW_HEAD) * (1.0 / RW_HEAD)
    d = y - mean
    var = _seg_sum(d * d, RW_HEAD) * (1.0 / RW_HEAD)
    yn = d * lax.rsqrt(var + RW_GN_EPS) * lnx_w + lnx_b
    bonus = _seg_sum(r * k_h * r_k, RW_HEAD) * v
    return (yn + bonus) * g


def _stack2(x):
    lane = _iota(x.shape, 1)
    return jnp.concatenate([jnp.where(lane < RW_HEAD, x, 0.0), jnp.where(lane >= RW_HEAD, x, 0.0)], axis=0)


def _unstack2(x):
    c = x.shape[0] // 2
    return x[:c] + x[c:]


def _rwkv_chunk_kernel(zs_ref, mu_ref, w0_ref, w2a_ref, a0_ref, g2_ref, kk_ref, ka_ref, rk_ref, lw_ref, lb_ref,
                       o_ref, hout_ref, prev_sc, h_sc, hmeta_sc, prevmeta_sc, y_sc, *, n_chunks):
    i = pl.program_id(0)
    is_meta = i == 0
    c = lax.rem(jnp.maximum(i - 1, 0), n_chunks)
    first = jnp.logical_and(i >= 1, c == 0)

    @pl.when(is_meta)
    def _():
        h_sc[...] = jnp.zeros_like(h_sc)
        prev_sc[...] = jnp.zeros_like(prev_sc)

    @pl.when(first)
    def _():
        h_sc[...] = hmeta_sc[...]
        prev_sc[...] = prevmeta_sc[...]

    zs = zs_ref[...]
    rowi = _iota((CHUNK, 1), 0)
    sh = pltpu.roll(zs, 1, 0)
    sh = jnp.where(rowi == 0, prev_sc[...], sh)
    prev_sc[...] = zs[CHUNK - 1:CHUNK, :]
    z = zs + (sh - zs) * mu_ref[...]
    r, lw, k_h, v, kk, b, g = _rwkv_pre(z, w0_ref[...], w2a_ref[...], a0_ref[...], g2_ref[...], kk_ref[...],
                                        ka_ref[...])
    lw = jnp.where(jnp.logical_and(is_meta, rowi < META_PAD), 0.0, lw)

    cl = _split_dot(_tri_incl(CHUNK), lw)
    cl_end = cl[CHUNK - 1:CHUNK, :]
    e_neg = jnp.exp(-cl)
    e_end = jnp.exp(cl_end - cl)
    kkt = kk * jnp.exp(cl - lw)
    rt = r * jnp.exp(cl)
    bt = b * e_neg
    kt = k_h * e_neg
    bh = b * e_end
    kh = k_h * e_end
    e_c = jnp.exp(cl_end)

    n2 = 2 * CHUNK
    tok_r = jnp.bitwise_and(_iota((n2, n2), 0), CHUNK - 1)
    tok_c = jnp.bitwise_and(_iota((n2, n2), 1), CHUNK - 1)
    strict = tok_r > tok_c
    incl = tok_r >= tok_c

    pairs = range(RW_HEADS // 2)
    sls = [slice(j * LANES, (j + 1) * LANES) for j in pairs]
    s_kkt = [_stack2(kkt[:, sl]) for sl in sls]
    s_rt = [_stack2(rt[:, sl]) for sl in sls]
    s_v = [_stack2(v[:, sl]).astype(BF16) for sl in sls]
    rb = [jnp.concatenate([_stack2(bt[:, sl]), _stack2(kt[:, sl])], axis=0).astype(BF16) for sl in sls]
    aa = [_bdot_nt(s_kkt[j], rb[j]) for j in pairs]
    mm = [_bdot_nt(s_rt[j], rb[j]) for j in pairs]
    h_kv = [_bdot_tn(_stack2(kh[:, sls[j]]), s_v[j]) for j in pairs]
    a_b = [jnp.where(strict, aa[j][:, :n2], 0.0).astype(BF16) for j in pairs]
    a_k = [jnp.where(strict, aa[j][:, n2:], 0.0) for j in pairs]
    m_rb = [jnp.where(incl, mm[j][:, :n2], 0.0).astype(BF16) for j in pairs]
    m_rk = [jnp.where(incl, mm[j][:, n2:], 0.0) for j in pairs]
    akv = [_bdot(a_k[j], s_v[j]) for j in pairs]
    y_kv = [_bdot(m_rk[j], s_v[j]) for j in pairs]
    x = [jnp.concatenate([s_kkt[j], akv[j]], axis=1) for j in pairs]
    p = a_b
    x = [x[j] - _bdot(p[j], x[j]) for j in pairs]
    for _ in range(5):
        p = [_bdot(p[j], p[j]).astype(BF16) for j in pairs]
        x = [x[j] + _bdot(p[j], x[j]) for j in pairs]
    zq = [_bdot(m_rb[j], x[j]) for j in pairs]
    zb = [_bdot_tn(_stack2(bh[:, sls[j]]), x[j]) for j in pairs]
    h0 = [h_sc[j] for j in pairs]
    for j in pairs:
        q_eff = _unstack2(s_rt[j] - zq[j][:, :LANES])
        y_in = _unstack2(y_kv[j] - zq[j][:, LANES:])
        y_sc[:, sls[j]] = _bdot(q_eff, h0[j]) + y_in
    h_new = [_col_of_row(e_c[:, sls[j]]) * h0[j] - _bdot(zb[j][:, :LANES], h0[j]) + (h_kv[j] - zb[j][:, LANES:])
             for j in pairs]
    for j in pairs:
        h_sc[j] = h_new[j]

    @pl.when(jnp.logical_and(i >= 1, c == n_chunks - 1))
    def _():
        for j in pairs:
            hout_ref[0, :, sls[j]] = _unstack2(h_new[j])

    o = _rwkv_post(y_sc[...], r, k_h, v, g, rk_ref[...], lw_ref[...], lb_ref[...])
    o_ref[...] = o.astype(o_ref.dtype)

    @pl.when(is_meta)
    def _():
        hmeta_sc[...] = h_sc[...]
        prevmeta_sc[...] = prev_sc[...]


def _rwkv_prompt(zr, p, n_batch=N_BATCH, n_chunks=N_CHUNKS, meta_block=ROW_META // CHUNK):
    row = lambda n: pl.BlockSpec((1, n), lambda i: (0, 0))
    full = lambda a, b: pl.BlockSpec((a, b), lambda i: (0, 0))
    blk = lambda i: jnp.where(i == 0, meta_block, i - 1)
    return pl.pallas_call(
        functools.partial(_rwkv_chunk_kernel, n_chunks=n_chunks),
        grid=(1 + n_batch * n_chunks,),
        in_specs=[pl.BlockSpec((CHUNK, W_SHIFT), lambda i: (blk(i), 0)),
                  row(W_SHIFT), row(RW_WIDTH), full(LANES, 2 * RW_WIDTH), row(RW_WIDTH), full(LANES, RW_WIDTH),
                  row(RW_WIDTH), row(RW_WIDTH), row(RW_WIDTH), row(RW_WIDTH), row(RW_WIDTH)],
        out_specs=[pl.BlockSpec((CHUNK, RW_WIDTH), lambda i: (jnp.maximum(i - 1, 0), 0)),
                   pl.BlockSpec((1, RW_HEAD, RW_WIDTH), lambda i: (jnp.maximum(i - 1, 0) // n_chunks, 0, 0))],
        out_shape=[jax.ShapeDtypeStruct((n_batch * n_chunks * CHUNK, RW_WIDTH), BF16),
                   jax.ShapeDtypeStruct((n_batch, RW_HEAD, RW_WIDTH), F32)],
        scratch_shapes=[pltpu.VMEM((1, W_SHIFT), F32),
                        pltpu.VMEM((RW_HEADS // 2, LANES, LANES), F32),
                        pltpu.VMEM((RW_HEADS // 2, LANES, LANES), F32),
                        pltpu.VMEM((1, W_SHIFT), F32),
                        pltpu.VMEM((CHUNK, RW_WIDTH), F32)],
        compiler_params=_cparams(),
        name="rwkv_prompt",
    )(zr, p['mu'], p['w0'], p['w2a'], p['a0'], p['g2'], p['k_k'], p['k_a'], p['r_k'], p['lnx_w'], p['lnx_b'])


def _rwkv_sample_pre_kernel(zs_ref, prev_ref, mu_ref, w0_ref, w2a_ref, a0_ref, g2_ref, kk_ref, ka_ref,
                            r_ref, kh_ref, v_ref, g_ref, rt_ref, wt_ref, kht_ref, vt_ref, nkkt_ref, bt_ref):
    zs = zs_ref[...]
    z = zs + (prev_ref[...] - zs) * mu_ref[...]
    r, lw, k_h, v, kk, b, g = _rwkv_pre(z, w0_ref[...], w2a_ref[...], a0_ref[...], g2_ref[...], kk_ref[...],
                                        ka_ref[...])
    r_ref[...] = r
    kh_ref[...] = k_h
    v_ref[...] = v
    g_ref[...] = g
    rt_ref[...] = r.T
    wt_ref[...] = jnp.exp(lw).T
    kht_ref[...] = k_h.T
    vt_ref[...] = v.T
    nkkt_ref[...] = (-kk).T
    bt_ref[...] = b.T


def _rwkv_sample_pre(zr, prev, p):
    row = lambda n: pl.BlockSpec((1, n), lambda i: (0, 0))
    full = lambda a, b: pl.BlockSpec((a, b), lambda i: (0, 0))
    vec = jax.ShapeDtypeStruct((N_SAMPLE, RW_WIDTH), F32)
    vec_t = jax.ShapeDtypeStruct((RW_WIDTH, N_SAMPLE), F32)
    return pl.pallas_call(
        _rwkv_sample_pre_kernel,
        grid=(1,),
        in_specs=[pl.BlockSpec((N_SAMPLE, W_SHIFT), lambda i: (ROW_SAMPLE // N_SAMPLE, 0)),
                  full(N_SAMPLE, W_SHIFT),
                  row(W_SHIFT), row(RW_WIDTH), full(LANES, 2 * RW_WIDTH), row(RW_WIDTH), full(LANES, RW_WIDTH),
                  row(RW_WIDTH), row(RW_WIDTH)],
        out_specs=[full(N_SAMPLE, RW_WIDTH)] * 4 + [full(RW_WIDTH, N_SAMPLE)] * 6,
        out_shape=[vec] * 4 + [vec_t] * 6,
        compiler_params=_cparams(),
        name="rwkv_sample_pre",
    )(zr, prev, p['mu'], p['w0'], p['w2a'], p['a0'], p['g2'], p['k_k'], p['k_a'])


def _rwkv_sample_step_kernel(s_ref, r_ref, w_ref, kh_ref, v_ref, nkk_ref, b_ref, so_ref, y_ref):
    s = s_ref[0]
    sa = jnp.sum(s * nkk_ref[0][None], axis=1, keepdims=True)
    s_new = s * w_ref[0][None] + sa * b_ref[0][None] + v_ref[0] * kh_ref[0][None]
    so_ref[0] = s_new
    y_ref[0] = jnp.sum(s_new * r_ref[0][None], axis=1, keepdims=True)


def _rwkv_sample_step(state_t, r_t, w_t, kh_t, v_t, nkk_t, b_t):
    kvec = lambda t: t.reshape(RW_HEADS, RW_HEAD, N_SAMPLE)
    kspec = pl.BlockSpec((1, RW_HEAD, N_SAMPLE), lambda i: (i, 0, 0))
    vspec = pl.BlockSpec((1, RW_HEAD, 1, N_SAMPLE), lambda i: (i, 0, 0, 0))
    sspec = pl.BlockSpec((1, RW_HEAD, RW_HEAD, N_SAMPLE), lambda i: (i, 0, 0, 0))
    s_new, y = pl.pallas_call(
        _rwkv_sample_step_kernel,
        grid=(RW_HEADS,),
        in_specs=[sspec, kspec, kspec, kspec, vspec, kspec, kspec],
        out_specs=[sspec, vspec],
        out_shape=[jax.ShapeDtypeStruct(state_t.shape, F32),
                   jax.ShapeDtypeStruct((RW_HEADS, RW_HEAD, 1, N_SAMPLE), F32)],
        compiler_params=_cparams(),
        name="rwkv_sample_step",
    )(state_t, kvec(r_t), kvec(w_t), kvec(kh_t), v_t.reshape(RW_HEADS, RW_HEAD, 1, N_SAMPLE), kvec(nkk_t), kvec(b_t))
    return s_new, y.reshape(RW_WIDTH, N_SAMPLE)


def _rwkv_sample_post_kernel(yt_ref, r_ref, kh_ref, v_ref, g_ref, rk_ref, lw_ref, lb_ref, o_ref):
    o = _rwkv_post(yt_ref[...].T, r_ref[...], kh_ref[...], v_ref[...], g_ref[...], rk_ref[...], lw_ref[...],
                   lb_ref[...])
    o_ref[0:N_SAMPLE, :] = o.astype(o_ref.dtype)
    o_ref[N_SAMPLE:T_EXT, :] = jnp.zeros((T_EXT - N_SAMPLE, RW_WIDTH), o_ref.dtype)


def _rwkv_sample_post(y_t, r, kh, v, g, p):
    row = lambda n: pl.BlockSpec((1, n), lambda i: (0, 0))
    full = lambda a, b: pl.BlockSpec((a, b), lambda i: (0, 0))
    return pl.pallas_call(
        _rwkv_sample_post_kernel,
        grid=(1,),
        in_specs=[full(RW_WIDTH, N_SAMPLE)] + [full(N_SAMPLE, RW_WIDTH)] * 4 + [row(RW_WIDTH)] * 3,
        out_specs=full(T_EXT, RW_WIDTH),
        out_shape=jax.ShapeDtypeStruct((T_EXT, RW_WIDTH), BF16),
        compiler_params=_cparams(),
        name="rwkv_sample_post",
    )(y_t, r, kh, v, g, p['r_k'], p['lnx_w'], p['lnx_b'])


def _gla_logg(zg, gk_up, gk_b):
    gkd = zg[:, 2 * GLA_KDIM + GLA_WIDTH:2 * GLA_KDIM + GLA_WIDTH + LANES]
    x = _bdot(gkd, gk_up) + gk_b
    return (jnp.minimum(x, 0.0) - jnp.log(1.0 + jnp.exp(-jnp.abs(x)))) * (1.0 / GLA_NORMALIZER)


def _gla_post(o, og_act, norm_w):
    outs = []
    for h in range(GLA_HEADS):
        oh = o[:, h * GLA_DV:(h + 1) * GLA_DV]
        ms = jnp.mean(oh * oh, axis=-1, keepdims=True)
        outs.append(oh * lax.rsqrt(ms + RMS_EPS) * norm_w)
    return jnp.concatenate(outs, axis=1) * og_act


def _gla_chunk_kernel(zg_ref, og_ref, gkup_ref, gkb_ref, nw_ref, o_ref, sout_ref, s_sc, smeta_sc, *, n_chunks):
    i = pl.program_id(0)
    is_meta = i == 0
    c = lax.rem(jnp.maximum(i - 1, 0), n_chunks)
    first = jnp.logical_and(i >= 1, c == 0)

    @pl.when(is_meta)
    def _():
        s_sc[...] = jnp.zeros_like(s_sc)

    @pl.when(first)
    def _():
        s_sc[...] = smeta_sc[...]

    zg = zg_ref[...]
    rowi = _iota((CHUNK, 1), 0)
    logg = _gla_logg(zg, gkup_ref[...], gkb_ref[...])
    logg = jnp.where(jnp.logical_and(is_meta, rowi < META_PAD), 0.0, logg)
    bcum = _split_dot(_tri_incl(CHUNK), logg)
    b_end = bcum[CHUNK - 1:CHUNK, :]
    qt = zg[:, 0:GLA_KDIM] * (GLA_DK ** -0.5) * jnp.exp(bcum)
    kt = zg[:, GLA_KDIM:2 * GLA_KDIM] * jnp.exp(-bcum)
    ke = zg[:, GLA_KDIM:2 * GLA_KDIM] * jnp.exp(b_end - bcum)
    e_end = jnp.exp(b_end)
    causal = _iota((CHUNK, CHUNK), 0) >= _iota((CHUNK, CHUNK), 1)
    heads = range(GLA_HEADS)
    ks = [slice(h * GLA_DK, (h + 1) * GLA_DK) for h in heads]
    vh = [zg[:, 2 * GLA_KDIM + h * GLA_DV:2 * GLA_KDIM + (h + 1) * GLA_DV].astype(BF16) for h in heads]
    a = [jnp.where(causal, _bdot_nt(qt[:, ks[h]], kt[:, ks[h]]), 0.0) for h in heads]
    s0 = [s_sc[h] for h in heads]
    o_inter = [_bdot(qt[:, ks[h]], s0[h]) for h in heads]
    s_add = [_bdot_tn(ke[:, ks[h]], vh[h]) for h in heads]
    outs = [_bdot(a[h], vh[h]) + o_inter[h] for h in heads]
    s_new = [_col_of_row(e_end[:, ks[h]]) * s0[h] + s_add[h] for h in heads]
    for h in heads:
        s_sc[h] = s_new[h]

    @pl.when(jnp.logical_and(i >= 1, c == n_chunks - 1))
    def _():
        for h in heads:
            sout_ref[0, h] = s_new[h]

    o = _gla_post(jnp.concatenate(outs, axis=1), og_ref[...].astype(F32), nw_ref[...])
    o_ref[...] = o.astype(o_ref.dtype)

    @pl.when(is_meta)
    def _():
        smeta_sc[...] = s_sc[...]


def _gla_prompt(zg, zb, p, n_batch=N_BATCH, n_chunks=N_CHUNKS, meta_block=ROW_META // CHUNK, og_block=OG_BLOCK):
    full = lambda a, b: pl.BlockSpec((a, b), lambda i: (0, 0))
    blk = lambda i: jnp.where(i == 0, meta_block, i - 1)
    return pl.pallas_call(
        functools.partial(_gla_chunk_kernel, n_chunks=n_chunks),
        grid=(1 + n_batch * n_chunks,),
        in_specs=[pl.BlockSpec((CHUNK, W_GLA_PAD), lambda i: (blk(i), 0)),
                  pl.BlockSpec((CHUNK, GLA_WIDTH), lambda i: (blk(i), og_block)),
                  full(LANES, GLA_KDIM), full(1, GLA_KDIM), full(1, GLA_DV)],
        out_specs=[pl.BlockSpec((CHUNK, GLA_WIDTH), lambda i: (jnp.maximum(i - 1, 0), 0)),
                   pl.BlockSpec((1, GLA_HEADS, GLA_DK, GLA_DV),
                                lambda i: (jnp.maximum(i - 1, 0) // n_chunks, 0, 0, 0))],
        out_shape=[jax.ShapeDtypeStruct((n_batch * n_chunks * CHUNK, GLA_WIDTH), BF16),
                   jax.ShapeDtypeStruct((n_batch, GLA_HEADS, GLA_DK, GLA_DV), F32)],
        scratch_shapes=[pltpu.VMEM((GLA_HEADS, GLA_DK, GLA_DV), F32),
                        pltpu.VMEM((GLA_HEADS, GLA_DK, GLA_DV), F32)],
        compiler_params=_cparams(),
        name="gla_prompt",
    )(zg, zb, p['gk_up'], p['gk_b'], p['norm_w'])


def _gla_sample_pre_kernel(zg_ref, gkup_ref, gkb_ref, q_ref, k_ref, eg_ref):
    zg = zg_ref[...]
    logg = _gla_logg(zg, gkup_ref[...], gkb_ref[...])
    q_ref[...] = zg[:, 0:GLA_KDIM] * (GLA_DK ** -0.5)
    k_ref[...] = zg[:, GLA_KDIM:2 * GLA_KDIM]
    eg_ref[...] = jnp.exp(logg)


def _gla_sample_pre(zg, p):
    full = lambda a, b: pl.BlockSpec((a, b), lambda i: (0, 0))
    vec = jax.ShapeDtypeStruct((N_SAMPLE, GLA_KDIM), F32)
    return pl.pallas_call(
        _gla_sample_pre_kernel,
        grid=(1,),
        in_specs=[pl.BlockSpec((N_SAMPLE, W_GLA_PAD), lambda i: (ROW_SAMPLE // N_SAMPLE, 0)),
                  full(LANES, GLA_KDIM), full(1, GLA_KDIM)],
        out_specs=[full(N_SAMPLE, GLA_KDIM)] * 3,
        out_shape=[vec] * 3,
        compiler_params=_cparams(),
        name="gla_sample_pre",
    )(zg, p['gk_up'], p['gk_b'])


def _gla_sample_step_kernel(s_ref, q_ref, k_ref, eg_ref, v_ref, so_ref, o_ref):
    s = s_ref[0]
    eye = _iota((GLA_DK, GLA_DK), 0) == _iota((GLA_DK, GLA_DK), 1)
    col = lambda ref: jnp.sum(jnp.where(eye, ref[0], 0.0), axis=-1, keepdims=True)
    s_new = col(eg_ref) * s + col(k_ref) * v_ref[0]
    so_ref[0] = s_new
    o_ref[0] = jnp.sum(col(q_ref) * s_new, axis=-2, keepdims=True)


def _gla_sample_step(state, q, k, eg, v):
    kv = lambda t: t.reshape(N_SAMPLE, GLA_HEADS, 1, GLA_DK)
    kspec = pl.BlockSpec((1, GLA_HEADS, 1, GLA_DK), lambda i: (i, 0, 0, 0))
    vspec = pl.BlockSpec((1, GLA_HEADS, 1, GLA_DV), lambda i: (i, 0, 0, 0))
    sspec = pl.BlockSpec((1, GLA_HEADS, GLA_DK, GLA_DV), lambda i: (i, 0, 0, 0))
    s_new, o = pl.pallas_call(
        _gla_sample_step_kernel,
        grid=(N_SAMPLE,),
        in_specs=[sspec, kspec, kspec, kspec, vspec],
        out_specs=[sspec, vspec],
        out_shape=[jax.ShapeDtypeStruct(state.shape, F32),
                   jax.ShapeDtypeStruct((N_SAMPLE, GLA_HEADS, 1, GLA_DV), F32)],
        compiler_params=_cparams(),
        name="gla_sample_step",
    )(state, kv(q), kv(k), kv(eg), v.reshape(N_SAMPLE, GLA_HEADS, 1, GLA_DV))
    return s_new, o.reshape(N_SAMPLE, GLA_WIDTH)


def _gla_sample_post_kernel(o_ref, og_ref, nw_ref, out_ref):
    o = _gla_post(o_ref[...], og_ref[0:N_SAMPLE, :].astype(F32), nw_ref[...])
    out_ref[0:N_SAMPLE, :] = o.astype(out_ref.dtype)
    out_ref[N_SAMPLE:T_EXT, :] = jnp.zeros((T_EXT - N_SAMPLE, GLA_WIDTH), out_ref.dtype)


def _gla_sample_post(o, zb, p, og_block=OG_BLOCK):
    full = lambda a, b: pl.BlockSpec((a, b), lambda i: (0, 0))
    return pl.pallas_call(
        _gla_sample_post_kernel,
        grid=(1,),
        in_specs=[full(N_SAMPLE, GLA_WIDTH),
                  pl.BlockSpec((T_EXT, GLA_WIDTH), lambda i: (T_MAIN // T_EXT, og_block)),
                  full(1, GLA_DV)],
        out_specs=full(T_EXT, GLA_WIDTH),
        out_shape=jax.ShapeDtypeStruct((T_EXT, GLA_WIDTH), BF16),
        compiler_params=_cparams(),
        name="gla_sample_post",
    )(o, zb, p['norm_w'])


def _mix_route_kernel(or_ref, ore_ref, og_ref, oge_ref, gates_ref, xp_ref, xe_ref, pr_ref, pg_ref, wo_ref, n2_ref,
                      wrh_ref, wrl_ref, br_ref, h1_ref, xn_ref, info_ref, cnt_ref, carry_sc):
    i = pl.program_id(0)
    is_main = i < T_MAIN // TM_DENSE

    @pl.when(i == 0)
    def _():
        carry_sc[...] = jnp.zeros_like(carry_sc)

    sig_r = gates_ref[:, 0:D_MODEL].astype(F32)
    sig_g = gates_ref[:, D_MODEL:2 * D_MODEL].astype(F32)
    o_r = jnp.where(is_main, or_ref[...], ore_ref[...])
    o_g = jnp.where(is_main, og_ref[...], oge_ref[...])
    m = (sig_r * jnp.dot(o_r, pr_ref[...], preferred_element_type=F32)
         + sig_g * jnp.dot(o_g, pg_ref[...], preferred_element_type=F32))
    h = jnp.where(is_main, xp_ref[...], xe_ref[...])
    h1 = h + jnp.dot(m.astype(BF16), wo_ref[...], preferred_element_type=F32)
    h1_ref[...] = h1
    ms = jnp.mean(h1 * h1, axis=-1, keepdims=True)
    xn = h1 * lax.rsqrt(ms + RMS_EPS) * n2_ref[...]
    xn_ref[...] = xn
    xh = xn.astype(BF16)
    xl = (xn - xh.astype(F32)).astype(BF16)
    lg = (jnp.dot(xh, wrh_ref[...], preferred_element_type=F32)
          + jnp.dot(xh, wrl_ref[...], preferred_element_type=F32)
          + jnp.dot(xl, wrh_ref[...], preferred_element_type=F32)) + br_ref[...]

    lane = _iota(lg.shape, 1)
    lanef = lane.astype(F32)
    is_g = jnp.logical_and(lane >= N_EXPERTS, lane < N_EXPERTS + N_GROUPS)
    gl = jnp.where(is_g, lg, NEG_BIG)
    gmax = jnp.max(gl, axis=-1, keepdims=True)
    gsel = jnp.min(jnp.where(jnp.logical_and(is_g, gl == gmax), lanef, 1e9), axis=-1, keepdims=True) - N_EXPERTS
    p_g = 1.0 / jnp.sum(jnp.exp(gl - gmax), axis=-1, keepdims=True)
    grp = (lane // EXPERTS_PER_GROUP).astype(F32)
    in_grp = jnp.logical_and(lane < N_EXPERTS, grp == gsel)
    el = jnp.where(in_grp, lg, NEG_BIG)
    m1 = jnp.max(el, axis=-1, keepdims=True)
    i1 = jnp.min(jnp.where(jnp.logical_and(in_grp, el == m1), lanef, 1e9), axis=-1, keepdims=True)
    in2 = jnp.logical_and(in_grp, lanef != i1)
    el2 = jnp.where(in2, lg, NEG_BIG)
    m2 = jnp.max(el2, axis=-1, keepdims=True)
    i2 = jnp.min(jnp.where(jnp.logical_and(in2, el2 == m2), lanef, 1e9), axis=-1, keepdims=True)
    e2 = jnp.exp(m2 - m1)
    w1 = p_g / (1.0 + e2)
    w2 = p_g * e2 / (1.0 + e2)

    oh1 = lanef == i1
    oh2 = lanef == i2
    cnt = jnp.where(jnp.logical_or(oh1, oh2), 1.0, 0.0)
    tm = cnt.shape[0]
    lstrict = (_iota((tm, tm), 0) > _iota((tm, tm), 1)).astype(BF16)
    before = jnp.dot(lstrict, cnt.astype(BF16), preferred_element_type=F32) + carry_sc[...]
    rank1 = jnp.sum(jnp.where(oh1, before, 0.0), axis=-1, keepdims=True)
    rank2 = jnp.sum(jnp.where(oh2, before, 0.0), axis=-1, keepdims=True)
    carry_sc[...] = carry_sc[...] + jnp.sum(cnt, axis=0, keepdims=True)
    cnt_ref[...] = carry_sc[...]
    info = jnp.where(lane == 0, i1, jnp.where(lane == 1, i2, jnp.where(lane == 2, w1, jnp.where(
        lane == 3, w2, jnp.where(lane == 4, rank1, jnp.where(lane == 5, rank2, 0.0))))))
    info_ref[...] = info


def _mix_route(o_r, o_r_ext, o_g, o_g_ext, zb, xp, xe, p):
    nmain = T_MAIN // TM_DENSE
    tile = lambda n: pl.BlockSpec((TM_DENSE, n), lambda i: (i, 0))
    main = lambda n: pl.BlockSpec((TM_DENSE, n), lambda i: (jnp.minimum(i, nmain - 1), 0))
    ext = lambda n: pl.BlockSpec((T_EXT, n), lambda i: (0, 0))
    const = lambda a, b: pl.BlockSpec((a, b), lambda i: (0, 0), pipeline_mode=pl.Buffered(1))
    return pl.pallas_call(
        _mix_route_kernel,
        grid=(T_ALL // TM_DENSE,),
        in_specs=[main(RW_WIDTH), ext(RW_WIDTH), main(GLA_WIDTH), ext(GLA_WIDTH), tile(2 * D_MODEL),
                  main(D_MODEL), ext(D_MODEL),
                  const(RW_WIDTH, D_MODEL), const(GLA_WIDTH, D_MODEL), const(D_MODEL, D_MODEL),
                  const(1, D_MODEL), const(D_MODEL, LANES), const(D_MODEL, LANES), const(1, LANES)],
        out_specs=[tile(D_MODEL), tile(D_MODEL), tile(LANES), pl.BlockSpec((1, LANES), lambda i: (0, 0))],
        out_shape=[jax.ShapeDtypeStruct((T_ALL, D_MODEL), F32),
                   jax.ShapeDtypeStruct((T_ALL, D_MODEL), F32),
                   jax.ShapeDtypeStruct((T_ALL, LANES), F32),
                   jax.ShapeDtypeStruct((1, LANES), F32)],
        scratch_shapes=[pltpu.VMEM((1, LANES), F32)],
        compiler_params=_cparams(),
        name="mix_route",
    )(o_r, o_r_ext, o_g, o_g_ext, zb, xp, xe, p['p_rwkv'], p['p_gla'], p['w_out'], p['norm2_w'], p['wr_hi'],
      p['wr_lo'], p['b_route'])


def _dispatch_kernel(info_ref, cnt_ref, xq_ref, xs_hbm, d_ref, te_ref, nt_ref,
                     d_vm, d_sm, tab_vm, tab_sm, zbuf, sem_t, sem_x):
    i = pl.program_id(0)
    lane1 = _iota((1, LANES), 1)
    cnt = jnp.where(lane1 < N_EXPERTS, cnt_ref[...], 0.0)
    tiles = jnp.floor((cnt + (TM_MOE - 1)) * (1.0 / TM_MOE))
    upper = (_iota((LANES, LANES), 0) < _iota((LANES, LANES), 1)).astype(BF16)
    tile_start = jnp.dot(jnp.broadcast_to(tiles, (8, LANES)).astype(BF16), upper,
                         preferred_element_type=F32)[0:1, :]
    base = tile_start * TM_MOE

    info = info_ref[...]
    lanef = _iota(info.shape, 1).astype(F32)
    pick = lambda col: jnp.sum(jnp.where(lanef == info[:, col:col + 1], base, 0.0), axis=-1, keepdims=True)
    d1 = pick(0) + info[:, 4:5]
    d2 = pick(1) + info[:, 5:6]
    tm = info.shape[0]
    eye = _iota((tm, tm), 0) == _iota((tm, tm), 1)
    to_row = lambda col: jnp.sum(jnp.where(eye, col, 0.0), axis=0, keepdims=True)
    d_row = jnp.concatenate([to_row(d1), to_row(d2)], axis=1).astype(jnp.int32)
    d_ref[0] = d_row
    d_vm[...] = d_row

    @pl.when(i == 0)
    def _():
        tile_end = tile_start + tiles
        n_tiles = jnp.sum(tiles, axis=-1, keepdims=True)
        eye_l = _iota((LANES, LANES), 0) == _iota((LANES, LANES), 1)
        end_col = jnp.sum(jnp.where(eye_l, tile_end, 0.0), axis=-1, keepdims=True)
        rowl = _iota((LANES, LANES), 0)
        tile_f = _iota((LANES, LANES), 1).astype(F32)
        te = jnp.sum(jnp.where(jnp.logical_and(rowl < N_EXPERTS, end_col <= tile_f), 1.0, 0.0), axis=0,
                     keepdims=True)
        last_e = jnp.max(jnp.where(tiles > 0.0, lane1.astype(F32), 0.0), axis=-1, keepdims=True)
        te = jnp.where(lane1.astype(F32) < n_tiles, jnp.minimum(te, N_EXPERTS - 1.0), last_e)
        te_ref[...] = te.astype(jnp.int32)
        nt_ref[...] = jnp.broadcast_to(n_tiles, (1, LANES)).astype(jnp.int32)
        zbuf[...] = jnp.zeros_like(zbuf)
        pad_lo = (base + cnt).astype(jnp.int32)
        pad_hi = (base + tiles * TM_MOE).astype(jnp.int32)
        nt_row = jnp.broadcast_to(n_tiles, (1, LANES)).astype(jnp.int32)
        tab_vm[...] = jnp.concatenate([pad_lo, pad_hi, nt_row, jnp.zeros((5, LANES), jnp.int32)], axis=0)
        tab_copy = pltpu.make_async_copy(tab_vm, tab_sm, sem_t)
        tab_copy.start()
        tab_copy.wait()

        def zero_row(r):
            return pltpu.make_async_copy(zbuf.at[pl.ds(0, 1), :], xs_hbm.at[pl.ds(r, 1), :], sem_x)

        def zero_tile(t):
            return pltpu.make_async_copy(zbuf, xs_hbm.at[pl.ds(t * TM_MOE, TM_MOE), :], sem_x)

        def each(fn):
            def body(r, carry):
                fn(r)
                return carry
            return body

        for e in range(N_EXPERTS):
            lax.fori_loop(tab_sm[0, e], tab_sm[1, e], each(lambda r: zero_row(r).start()), 0)
        lax.fori_loop(tab_sm[2, 0], NT_MOE, each(lambda t: zero_tile(t).start()), 0)
        for e in range(N_EXPERTS):
            lax.fori_loop(tab_sm[0, e], tab_sm[1, e], each(lambda r: zero_row(r).wait()), 0)
        lax.fori_loop(tab_sm[2, 0], NT_MOE, each(lambda t: zero_tile(t).wait()), 0)

    d_copy = pltpu.make_async_copy(d_vm, d_sm, sem_t)
    d_copy.start()
    d_copy.wait()

    def row_copy(t, half):
        return pltpu.make_async_copy(xq_ref.at[pl.ds(t, 1), :], xs_hbm.at[pl.ds(d_sm[0, half * tm + t], 1), :], sem_x)

    def issue(t, carry):
        row_copy(t, 0).start()
        row_copy(t, 1).start()
        return carry

    def drain(t, carry):
        row_copy(t, 0).wait()
        row_copy(t, 1).wait()
        return carry

    lax.fori_loop(0, tm, issue, 0, unroll=8)
    lax.fori_loop(0, tm, drain, 0, unroll=8)


def _dispatch(info, counts, xq):
    nt = T_ALL // TM_DENSE
    return pl.pallas_call(
        _dispatch_kernel,
        grid=(nt,),
        in_specs=[pl.BlockSpec((TM_DENSE, LANES), lambda i: (i, 0)),
                  pl.BlockSpec((1, LANES), lambda i: (0, 0)),
                  pl.BlockSpec((TM_DENSE, D_MODEL), lambda i: (i, 0))],
        out_specs=[pl.BlockSpec(memory_space=pl.ANY),
                   pl.BlockSpec((1, 1, 2 * TM_DENSE), lambda i: (i, 0, 0)),
                   pl.BlockSpec((1, LANES), lambda i: (0, 0)),
                   pl.BlockSpec((1, LANES), lambda i: (0, 0))],
        out_shape=[jax.ShapeDtypeStruct((NT_MOE * TM_MOE, D_MODEL), F32),
                   jax.ShapeDtypeStruct((nt, 1, 2 * TM_DENSE), jnp.int32),
                   jax.ShapeDtypeStruct((1, LANES), jnp.int32),
                   jax.ShapeDtypeStruct((1, LANES), jnp.int32)],
        scratch_shapes=[pltpu.VMEM((1, 2 * TM_DENSE), jnp.int32),
                        pltpu.SMEM((1, 2 * TM_DENSE), jnp.int32),
                        pltpu.VMEM((8, LANES), jnp.int32),
                        pltpu.SMEM((8, LANES), jnp.int32),
                        pltpu.VMEM((TM_MOE, D_MODEL), F32),
                        pltpu.SemaphoreType.DMA(()),
                        pltpu.SemaphoreType.DMA(())],
        compiler_params=_cparams(),
        name="moe_dispatch",
    )(info, counts, xq)


def _moe_kernel(te_ref, nt_ref, xs_ref, w1_ref, w3_ref, w2_ref, o_ref, w1b, w3b, w2b):
    i = pl.program_id(0)

    @pl.when(i < nt_ref[0])
    def _():
        changed = jnp.logical_or(i == 0, te_ref[i] != te_ref[jnp.maximum(i - 1, 0)])

        @pl.when(changed)
        def _():
            w1b[...] = w1_ref[0].astype(BF16)
            w3b[...] = w3_ref[0].astype(BF16)
            w2b[...] = w2_ref[0].astype(BF16)

        xb = xs_ref[...].astype(BF16)
        h1 = jnp.dot(xb, w1b[...], preferred_element_type=F32)
        h3 = jnp.dot(xb, w3b[...], preferred_element_type=F32)
        hh = (h1 * jax.nn.sigmoid(h1) * h3).astype(BF16)
        o_ref[...] = jnp.dot(hh, w2b[...], preferred_element_type=F32)

    @pl.when(i >= nt_ref[0])
    def _():
        o_ref[...] = jnp.zeros_like(o_ref)


def _moe(tile_expert, n_tiles, xs, w1, w3, w2):
    grid_spec = pltpu.PrefetchScalarGridSpec(
        num_scalar_prefetch=2,
        grid=(NT_MOE,),
        in_specs=[pl.BlockSpec((TM_MOE, D_MODEL), lambda i, te, nt: (jnp.minimum(i, nt[0] - 1), 0)),
                  pl.BlockSpec((1, D_MODEL, D_EXPERT), lambda i, te, nt: (te[i], 0, 0)),
                  pl.BlockSpec((1, D_MODEL, D_EXPERT), lambda i, te, nt: (te[i], 0, 0)),
                  pl.BlockSpec((1, D_EXPERT, D_MODEL), lambda i, te, nt: (te[i], 0, 0))],
        out_specs=pl.BlockSpec((TM_MOE, D_MODEL), lambda i, te, nt: (i, 0)),
        scratch_shapes=[pltpu.VMEM((D_MODEL, D_EXPERT), BF16),
                        pltpu.VMEM((D_MODEL, D_EXPERT), BF16),
                        pltpu.VMEM((D_EXPERT, D_MODEL), BF16)],
    )
    return pl.pallas_call(
        _moe_kernel,
        grid_spec=grid_spec,
        out_shape=jax.ShapeDtypeStruct((NT_MOE * TM_MOE, D_MODEL), F32),
        compiler_params=_cparams(),
        name="moe_experts",
    )(tile_expert, n_tiles, xs, w1, w3, w2)


def _combine_kernel(dcur_ref, dnext_ref, h1_ref, info_ref, fw_ref, eo_hbm, yp_ref, ye_ref, buf, sem):
    i = pl.program_id(0)
    nmain = T_MAIN // TM_DENSE
    slot = lax.rem(i, 2)

    def row_copy(d_ref, s, t):
        return pltpu.make_async_copy(eo_hbm.at[pl.ds(d_ref[0, 0, t], 1), :], buf.at[s, pl.ds(t, 1), :], sem.at[s])

    def issue_all(d_ref, s):
        def body(t, carry):
            row_copy(d_ref, s, t).start()
            return carry
        lax.fori_loop(0, 2 * TM_DENSE, body, 0, unroll=8)

    @pl.when(i == 0)
    def _():
        issue_all(dcur_ref, slot)

    @pl.when(i + 1 < pl.num_programs(0))
    def _():
        issue_all(dnext_ref, 1 - slot)

    def drain(t, carry):
        row_copy(dcur_ref, slot, t).wait()
        return carry

    lax.fori_loop(0, 2 * TM_DENSE, drain, 0, unroll=8)

    info = info_ref[...]
    y = h1_ref[...] + info[:, 2:3] * buf[slot, 0:TM_DENSE, :] + info[:, 3:4] * buf[slot, TM_DENSE:2 * TM_DENSE, :]
    ms = jnp.mean(y * y, axis=-1, keepdims=True)
    out = y * lax.rsqrt(ms + RMS_EPS) * fw_ref[...]

    @pl.when(i < nmain)
    def _():
        yp_ref[...] = out

    @pl.when(i == nmain)
    def _():
        ye_ref[...] = out


def _combine(dst, h1, info, fw, eo):
    nmain = T_MAIN // TM_DENSE
    return pl.pallas_call(
        _combine_kernel,
        grid=(T_ALL // TM_DENSE,),
        in_specs=[pl.BlockSpec((1, 1, 2 * TM_DENSE), lambda i: (i, 0, 0), memory_space=pltpu.SMEM),
                  pl.BlockSpec((1, 1, 2 * TM_DENSE), lambda i: (jnp.minimum(i + 1, nmain), 0, 0),
                               memory_space=pltpu.SMEM),
                  pl.BlockSpec((TM_DENSE, D_MODEL), lambda i: (i, 0)),
                  pl.BlockSpec((TM_DENSE, LANES), lambda i: (i, 0)),
                  pl.BlockSpec((1, D_MODEL), lambda i: (0, 0)),
                  pl.BlockSpec(memory_space=pl.ANY)],
        out_specs=[pl.BlockSpec((TM_DENSE, D_MODEL), lambda i: (jnp.minimum(i, nmain - 1), 0)),
                   pl.BlockSpec((T_EXT, D_MODEL), lambda i: (0, 0))],
        out_shape=[jax.ShapeDtypeStruct((T_MAIN, D_MODEL), F32),
                   jax.ShapeDtypeStruct((T_EXT, D_MODEL), F32)],
        scratch_shapes=[pltpu.VMEM((2, 2 * TM_DENSE, D_MODEL), F32), pltpu.SemaphoreType.DMA((2,))],
        compiler_params=_cparams(),
        name="moe_combine",
    )(dst, dst, h1, info, fw, eo)


def kernel(x_prompt, x_sample, state_rwkv, state_shift, state_gla, meta_tokens, norm1_w, w_in, mu_shift, rw_w0,
           rw_w2, rw_a0, rw_a2, rw_g2, rw_k_k, rw_k_a, rw_r_k, rw_lnx_w, rw_lnx_b, gla_gk_up, gla_gk_b, gla_norm_w,
           p_rwkv, p_gla, w_out, norm2_w, moe_w_group, moe_b_group, moe_w_router, moe_b_router, moe_w1, moe_w3,
           moe_w2, final_norm_w):
    w = w_in[0]
    w_r = jnp.concatenate([w[:, 0:1024], w[:, 1088:3136], w[:, 1024:1088], w[:, 3136:3328]], axis=1).astype(BF16)
    w_g = jnp.concatenate([w[:, 3328:5392], jnp.zeros((D_MODEL, W_GLA_PAD - 2064), F32)], axis=1).astype(BF16)
    w_b = jnp.concatenate([w[:, 6416:10512], w[:, 5392:6416]], axis=1).astype(BF16)
    row = lambda t: t.reshape(1, -1)
    w2a = jnp.zeros((LANES, 2 * RW_WIDTH), F32)
    w2a = w2a.at[0:64, 0:RW_WIDTH].set(rw_w2[0]).at[64:128, RW_WIDTH:].set(rw_a2[0])
    rw = dict(mu=row(mu_shift[0][_SHIFT_PERM]), w0=row(rw_w0[0]), w2a=w2a, a0=row(rw_a0[0]), g2=rw_g2[0],
              k_k=row(rw_k_k[0]), k_a=row(rw_k_a[0]), r_k=row(rw_r_k[0]), lnx_w=row(rw_lnx_w[0]),
              lnx_b=row(rw_lnx_b[0]))
    gk_up = jnp.zeros((LANES, GLA_KDIM), F32).at[0:GLA_LORA].set(gla_gk_up[0])
    gl = dict(gk_up=gk_up, gk_b=row(gla_gk_b[0]), norm_w=row(gla_norm_w[0]))
    w_route = jnp.zeros((D_MODEL, LANES), F32)
    w_route = w_route.at[:, 0:N_EXPERTS].set(moe_w_router[0]).at[:, N_EXPERTS:N_EXPERTS + N_GROUPS].set(moe_w_group[0])
    wr_hi = w_route.astype(BF16)
    wr_lo = (w_route - wr_hi.astype(F32)).astype(BF16)
    b_route = jnp.zeros((1, LANES), F32)
    b_route = b_route.at[0, 0:N_EXPERTS].set(moe_b_router[0]).at[0, N_EXPERTS:N_EXPERTS + N_GROUPS].set(moe_b_group[0])
    mix = dict(p_rwkv=p_rwkv[0].astype(BF16), p_gla=p_gla[0].astype(BF16), w_out=w_out[0].astype(BF16),
               norm2_w=row(norm2_w[0]), wr_hi=wr_hi, wr_lo=wr_lo, b_route=b_route)

    xp = x_prompt.reshape(T_MAIN, D_MODEL)
    xe = jnp.concatenate([x_sample[:, 0, :], jnp.zeros((META_PAD, D_MODEL), F32), meta_tokens,
                          jnp.zeros((T_ALL - ROW_META - CHUNK, D_MODEL), F32)], axis=0)

    xn = _norm1(xp, xe, row(norm1_w[0]))
    zr = _proj(xn, w_r, 1664, F32, name="proj_rwkv")
    zg = _proj(xn, w_g, 1152, F32, name="proj_gla")
    zb = _proj(xn, w_b, 1024, BF16, silu_from=4, name="proj_gates")

    o_r, h_fin = _rwkv_prompt(zr, rw)
    o_g, s_fin = _gla_prompt(zg, zb, gl)

    r_s, kh_s, v_s, g_s, r_t, w_t, kh_t, v_t, nkk_t, b_t = _rwkv_sample_pre(zr, state_shift[0][:, _SHIFT_PERM], rw)
    rw_new_t, y_t = _rwkv_sample_step(state_rwkv[0].transpose(1, 2, 3, 0), r_t, w_t, kh_t, v_t, nkk_t, b_t)
    rw_new = rw_new_t.transpose(3, 0, 1, 2)
    o_r_ext = _rwkv_sample_post(y_t, r_s, kh_s, v_s, g_s, rw)
    q_s, k_s, eg_s = _gla_sample_pre(zg, gl)
    v_gs = zg[ROW_SAMPLE:ROW_SAMPLE + N_SAMPLE, 2 * GLA_KDIM:2 * GLA_KDIM + GLA_WIDTH]
    gla_new, og_s = _gla_sample_step(state_gla[0], q_s, k_s, eg_s, v_gs)
    o_g_ext = _gla_sample_post(og_s, zb, gl)

    h1, xn2, info, counts = _mix_route(o_r, o_r_ext, o_g, o_g_ext, zb, xp, xe, mix)
    xs, dst, tile_expert, n_tiles = _dispatch(info, counts, xn2)
    eo = _moe(tile_expert[0, :NT_MOE], n_tiles[0, :1], xs, moe_w1[0], moe_w3[0], moe_w2[0])
    y_p, y_e = _combine(dst, h1, info, row(final_norm_w), eo)

    y_prompt = y_p.reshape(N_BATCH, SEQ, D_MODEL)
    y_sample = y_e[0:N_SAMPLE].reshape(N_SAMPLE, 1, D_MODEL)
    new_rwkv_prompt = h_fin.reshape(N_BATCH, RW_HEAD, RW_HEADS, RW_HEAD).transpose(0, 2, 3, 1)[None]
    shift_rows = jnp.concatenate([zr[SEQ - 1:T_MAIN:SEQ], zr[ROW_SAMPLE:ROW_SAMPLE + N_SAMPLE]], axis=0)
    shift_rows = shift_rows[:, _SHIFT_INV]
    return (y_prompt, y_sample, new_rwkv_prompt, shift_rows[None, 0:N_BATCH], s_fin[None],
            rw_new[None], shift_rows[None, N_BATCH:], gla_new[None])
```

```python
import functools
import math

import numpy as np
import jax
import jax.numpy as jnp
from jax import lax
from jax.experimental import pallas as pl
from jax.experimental.pallas import tpu as pltpu

F32 = jnp.float32
BF16 = jnp.bfloat16

D_MODEL = 2048
N_BATCH = 4
SEQ = 2048
N_SAMPLE = 128
N_META = 16
RMS_EPS = 1e-6

RW_WIDTH = 1024
RW_HEAD = 64
RW_HEADS = 16
RW_GN_EPS = RW_HEAD * 1e-5
W_SHIFT = 3328
GLA_HEADS = 4
GLA_DK = 128
GLA_DV = 256
GLA_KDIM = 512
GLA_WIDTH = 1024
GLA_LORA = 16
GLA_NORMALIZER = 16.0
W_GLA_PAD = 2304
OG_BLOCK = 2 * D_MODEL // GLA_WIDTH
N_GROUPS = 4
EXPERTS_PER_GROUP = 8
N_EXPERTS = 32
D_EXPERT = 512

CHUNK = 64
N_CHUNKS = SEQ // CHUNK
META_PAD = CHUNK - N_META
T_MAIN = N_BATCH * SEQ
ROW_SAMPLE = T_MAIN
ROW_META = T_MAIN + N_SAMPLE
T_EXT = 256
T_ALL = T_MAIN + T_EXT
LANES = 128

TM_DENSE = 256
TM_MM = 1056
TM_MOE = 256
NT_MOE = (2 * T_ALL) // TM_MOE + N_EXPERTS
NEG_BIG = -1e30

VMEM_LIMIT = 56 * 1024 * 1024

_SHIFT_PERM = np.concatenate([np.arange(0, 1024), np.arange(1088, 2112), np.arange(2112, 3136),
                              np.arange(1024, 1088), np.arange(3136, 3328)])
_SHIFT_INV = np.argsort(_SHIFT_PERM)


def _cparams(n_axes=1):
    return pltpu.CompilerParams(dimension_semantics=("arbitrary",) * n_axes, vmem_limit_bytes=VMEM_LIMIT)


def _bdot(a, b):
    return jnp.dot(a.astype(BF16), b.astype(BF16), preferred_element_type=F32)


def _bdot_nt(a, b):
    return lax.dot_general(a.astype(BF16), b.astype(BF16), (((1,), (1,)), ((), ())), preferred_element_type=F32)


def _bdot_tn(a, b):
    return lax.dot_general(a.astype(BF16), b.astype(BF16), (((0,), (0,)), ((), ())), preferred_element_type=F32)


def _split_dot(m_bf16, x):
    hi = x.astype(BF16)
    lo = (x - hi.astype(F32)).astype(BF16)
    return (jnp.dot(m_bf16, hi, preferred_element_type=F32) + jnp.dot(m_bf16, lo, preferred_element_type=F32))


def _iota(shape, dim):
    return lax.broadcasted_iota(jnp.int32, shape, dim)


def _tri_incl(n):
    return (_iota((n, n), 0) >= _iota((n, n), 1)).astype(BF16)


def _seg_sum(x, width):
    m, n = x.shape
    nb = n // LANES
    bd = ((_iota((LANES, LANES), 0) // width) == (_iota((LANES, LANES), 1) // width)).astype(BF16)
    xs = jnp.concatenate([x[:, j * LANES:(j + 1) * LANES] for j in range(nb)], axis=0)
    hi = xs.astype(BF16)
    lo = (xs - hi.astype(F32)).astype(BF16)
    s = jnp.dot(hi, bd, preferred_element_type=F32) + jnp.dot(lo, bd, preferred_element_type=F32)
    return jnp.concatenate([s[j * m:(j + 1) * m] for j in range(nb)], axis=1)


def _col_of_row(row):
    n = row.shape[-1]
    eye = _iota((n, n), 0) == _iota((n, n), 1)
    return jnp.sum(jnp.where(eye, jnp.broadcast_to(row, (n, n)), 0.0), axis=-1, keepdims=True)


def _norm1_kernel(xp_ref, xe_ref, w_ref, o_ref):
    i = pl.program_id(0)

    def f(x):
        ms = jnp.mean(x * x, axis=-1, keepdims=True)
        return (x * lax.rsqrt(ms + RMS_EPS) * w_ref[...]).astype(BF16)

    @pl.when(i < T_MAIN // TM_DENSE)
    def _():
        o_ref[...] = f(xp_ref[...])

    @pl.when(i == T_MAIN // TM_DENSE)
    def _():
        o_ref[...] = f(xe_ref[...])


def _norm1(xp, xe, w):
    nmain = T_MAIN // TM_DENSE
    return pl.pallas_call(
        _norm1_kernel,
        grid=(T_ALL // TM_DENSE,),
        in_specs=[pl.BlockSpec((TM_DENSE, D_MODEL), lambda i: (jnp.minimum(i, nmain - 1), 0)),
                  pl.BlockSpec((T_EXT, D_MODEL), lambda i: (0, 0)),
                  pl.BlockSpec((1, D_MODEL), lambda i: (0, 0))],
        out_specs=pl.BlockSpec((TM_DENSE, D_MODEL), lambda i: (i, 0)),
        out_shape=jax.ShapeDtypeStruct((T_ALL, D_MODEL), BF16),
        compiler_params=_cparams(),
        name="norm1",
    )(xp, xe, w)


def _proj_kernel(x_ref, wt_hbm, o_ref, wraw, wbf, sem, *, tiles, silu_from):
    j = pl.program_id(0)
    m = pl.program_id(1)
    tn = wbf.shape[0]

    def copies(jj):
        return [pltpu.make_async_copy(wt_hbm.at[pl.ds(src, n), :], wraw.at[pl.ds(dst, n), :], sem)
                for (src, n, dst) in tiles[jj]]

    @pl.when(jnp.logical_and(j == 0, m == 0))
    def _():
        for c in copies(0):
            c.start()

    for jj in range(len(tiles)):
        @pl.when(jnp.logical_and(j == jj, m == 0))
        def _(jj=jj):
            for c in copies(jj):
                c.wait()
            edge = 0
            for lo, hi in sorted((dst, dst + n) for (_, n, dst) in tiles[jj]) + [(tn, tn)]:
                if lo > edge:
                    wbf[edge:lo, :] = jnp.zeros((lo - edge, wbf.shape[1]), BF16)
                if hi > lo:
                    wbf[lo:hi, :] = wraw[lo:hi, :].astype(BF16)
                edge = hi
            if jj + 1 < len(tiles):
                for c in copies(jj + 1):
                    c.start()

    z = lax.dot_general(x_ref[...], wbf[...], (((1,), (1,)), ((), ())), preferred_element_type=F32)
    if silu_from is None:
        o_ref[...] = z.astype(o_ref.dtype)
    else:
        s = jax.nn.sigmoid(z)
        o_ref[...] = jnp.where(j >= silu_from, z * s, s).astype(o_ref.dtype)


def _proj(x, w_t, tiles, tn, out_dtype, silu_from=None, name="proj"):
    t, k = x.shape
    return pl.pallas_call(
        functools.partial(_proj_kernel, tiles=tiles, silu_from=silu_from),
        grid=(len(tiles), t // TM_MM),
        in_specs=[pl.BlockSpec((TM_MM, k), lambda j, m: (m, 0)),
                  pl.BlockSpec(memory_space=pl.ANY)],
        out_specs=pl.BlockSpec((TM_MM, tn), lambda j, m: (m, j)),
        out_shape=jax.ShapeDtypeStruct((t, tn * len(tiles)), out_dtype),
        scratch_shapes=[pltpu.VMEM((tn, k), F32), pltpu.VMEM((tn, k), BF16), pltpu.SemaphoreType.DMA(())],
        compiler_params=_cparams(2),
        name=name,
    )(x, w_t)


_RWKV_TILES = (((0, 1024, 0), (1088, 640, 1024)),
               ((1728, 1408, 0), (1024, 64, 1408), (3136, 192, 1472)))
_GLA_TILES = (((3328, 1152, 0),), ((4480, 912, 0),))
_GATE_TILES = tuple(((6416 + 1024 * j, 1024, 0),) for j in range(4)) + (((5392, 1024, 0),),)


def _rwkv_pre(z, w0, w2a, a0, g2, k_k, k_a):
    r = z[:, 0:1024]
    k = z[:, 1024:2048]
    v = z[:, 2048:3072]
    wa = z[:, 3072:3200]
    gd = z[:, 3200:3328]
    lane = _iota(wa.shape, 1)
    wa = jnp.where(lane < 64, jnp.tanh(wa), wa)
    up = _bdot(wa, w2a)
    lw = -math.exp(-0.5) * jax.nn.sigmoid(w0 + up[:, :1024])
    a = jax.nn.sigmoid(a0 + up[:, 1024:])
    g = _bdot(jax.nn.sigmoid(gd), g2)
    kk = k * k_k
    kk = kk * lax.rsqrt(jnp.maximum(_seg_sum(kk * kk, RW_HEAD), 1e-24))
    k_h = k * (1.0 + (a - 1.0) * k_a)
    return r, lw, k_h, v, kk, kk * a, g


def _rwkv_post(y, r, k_h, v, g, r_k, lnx_w, lnx_b):
    mean = _seg_sum(y, RW_HEAD) * (1.0 / RW_HEAD)
    d = y - mean
    var = _seg_sum(d * d, RW_HEAD) * (1.0 / RW_HEAD)
    yn = d * lax.rsqrt(var + RW_GN_EPS) * lnx_w + lnx_b
    bonus = _seg_sum(r * k_h * r_k, RW_HEAD) * v
    return (yn + bonus) * g


def _stack2(x):
    lane = _iota(x.shape, 1)
    return jnp.concatenate([jnp.where(lane < RW_HEAD, x, 0.0), jnp.where(lane >= RW_HEAD, x, 0.0)], axis=0)


def _unstack2(x):
    c = x.shape[0] // 2
    return x[:c] + x[c:]


def _rwkv_chunk_kernel(zs_ref, mu_ref, w0_ref, w2a_ref, a0_ref, g2_ref, kk_ref, ka_ref, rk_ref, lw_ref, lb_ref,
                       o_ref, hout_ref, prev_sc, h_sc, hmeta_sc, prevmeta_sc, y_sc, *, n_chunks):
    i = pl.program_id(0)
    is_meta = i == 0
    c = lax.rem(jnp.maximum(i - 1, 0), n_chunks)
    first = jnp.logical_and(i >= 1, c == 0)

    @pl.when(is_meta)
    def _():
        h_sc[...] = jnp.zeros_like(h_sc)
        prev_sc[...] = jnp.zeros_like(prev_sc)

    @pl.when(first)
    def _():
        h_sc[...] = hmeta_sc[...]
        prev_sc[...] = prevmeta_sc[...]

    zs = zs_ref[...]
    rowi = _iota((CHUNK, 1), 0)
    sh = pltpu.roll(zs, 1, 0)
    sh = jnp.where(rowi == 0, prev_sc[...], sh)
    prev_sc[...] = zs[CHUNK - 1:CHUNK, :]
    z = zs + (sh - zs) * mu_ref[...]
    r, lw, k_h, v, kk, b, g = _rwkv_pre(z, w0_ref[...], w2a_ref[...], a0_ref[...], g2_ref[...], kk_ref[...],
                                        ka_ref[...])
    lw = jnp.where(jnp.logical_and(is_meta, rowi < META_PAD), 0.0, lw)

    cl = _split_dot(_tri_incl(CHUNK), lw)
    cl_end = cl[CHUNK - 1:CHUNK, :]
    e_neg = jnp.exp(-cl)
    e_end = jnp.exp(cl_end - cl)
    kkt = kk * jnp.exp(cl - lw)
    rt = r * jnp.exp(cl)
    bt = b * e_neg
    kt = k_h * e_neg
    bh = b * e_end
    kh = k_h * e_end
    e_c = jnp.exp(cl_end)

    n2 = 2 * CHUNK
    tok_r = jnp.bitwise_and(_iota((n2, n2), 0), CHUNK - 1)
    tok_c = jnp.bitwise_and(_iota((n2, n2), 1), CHUNK - 1)
    strict = tok_r > tok_c
    incl = tok_r >= tok_c

    pairs = range(RW_HEADS // 2)
    sls = [slice(j * LANES, (j + 1) * LANES) for j in pairs]
    s_kkt = [_stack2(kkt[:, sl]) for sl in sls]
    s_rt = [_stack2(rt[:, sl]) for sl in sls]
    s_v = [_stack2(v[:, sl]).astype(BF16) for sl in sls]
    rb = [jnp.concatenate([_stack2(bt[:, sl]), _stack2(kt[:, sl])], axis=0).astype(BF16) for sl in sls]
    aa = [_bdot_nt(s_kkt[j], rb[j]) for j in pairs]
    mm = [_bdot_nt(s_rt[j], rb[j]) for j in pairs]
    h_kv = [_bdot_tn(_stack2(kh[:, sls[j]]), s_v[j]) for j in pairs]
    a_b = [jnp.where(strict, aa[j][:, :n2], 0.0).astype(BF16) for j in pairs]
    a_k = [jnp.where(strict, aa[j][:, n2:], 0.0) for j in pairs]
    m_rb = [jnp.where(incl, mm[j][:, :n2], 0.0).astype(BF16) for j in pairs]
    m_rk = [jnp.where(incl, mm[j][:, n2:], 0.0) for j in pairs]
    akv = [_bdot(a_k[j], s_v[j]) for j in pairs]
    y_kv = [_bdot(m_rk[j], s_v[j]) for j in pairs]
    x = [jnp.concatenate([s_kkt[j], akv[j]], axis=1) for j in pairs]
    p = a_b
    x = [x[j] - _bdot(p[j], x[j]) for j in pairs]
    for _ in range(5):
        p = [_bdot(p[j], p[j]).astype(BF16) for j in pairs]
        x = [x[j] + _bdot(p[j], x[j]) for j in pairs]
    zq = [_bdot(m_rb[j], x[j]) for j in pairs]
    zb = [_bdot_tn(_stack2(bh[:, sls[j]]), x[j]) for j in pairs]
    h0 = [h_sc[j] for j in pairs]
    for j in pairs:
        q_eff = _unstack2(s_rt[j] - zq[j][:, :LANES])
        y_in = _unstack2(y_kv[j] - zq[j][:, LANES:])
        y_sc[:, sls[j]] = _bdot(q_eff, h0[j]) + y_in
    h_new = [_col_of_row(e_c[:, sls[j]]) * h0[j] - _bdot(zb[j][:, :LANES], h0[j]) + (h_kv[j] - zb[j][:, LANES:])
             for j in pairs]
    for j in pairs:
        h_sc[j] = h_new[j]

    @pl.when(jnp.logical_and(i >= 1, c == n_chunks - 1))
    def _():
        for j in pairs:
            hout_ref[0, :, sls[j]] = _unstack2(h_new[j])

    o = _rwkv_post(y_sc[...], r, k_h, v, g, rk_ref[...], lw_ref[...], lb_ref[...])
    o_ref[...] = o.astype(o_ref.dtype)

    @pl.when(is_meta)
    def _():
        hmeta_sc[...] = h_sc[...]
        prevmeta_sc[...] = prev_sc[...]


def _rwkv_prompt(zr, p, n_batch=N_BATCH, n_chunks=N_CHUNKS, meta_block=ROW_META // CHUNK):
    row = lambda n: pl.BlockSpec((1, n), lambda i: (0, 0))
    full = lambda a, b: pl.BlockSpec((a, b), lambda i: (0, 0))
    blk = lambda i: jnp.where(i == 0, meta_block, i - 1)
    return pl.pallas_call(
        functools.partial(_rwkv_chunk_kernel, n_chunks=n_chunks),
        grid=(1 + n_batch * n_chunks,),
        in_specs=[pl.BlockSpec((CHUNK, W_SHIFT), lambda i: (blk(i), 0)),
                  row(W_SHIFT), row(RW_WIDTH), full(LANES, 2 * RW_WIDTH), row(RW_WIDTH), full(LANES, RW_WIDTH),
                  row(RW_WIDTH), row(RW_WIDTH), row(RW_WIDTH), row(RW_WIDTH), row(RW_WIDTH)],
        out_specs=[pl.BlockSpec((CHUNK, RW_WIDTH), lambda i: (jnp.maximum(i - 1, 0), 0)),
                   pl.BlockSpec((1, RW_HEAD, RW_WIDTH), lambda i: (jnp.maximum(i - 1, 0) // n_chunks, 0, 0))],
        out_shape=[jax.ShapeDtypeStruct((n_batch * n_chunks * CHUNK, RW_WIDTH), BF16),
                   jax.ShapeDtypeStruct((n_batch, RW_HEAD, RW_WIDTH), F32)],
        scratch_shapes=[pltpu.VMEM((1, W_SHIFT), F32),
                        pltpu.VMEM((RW_HEADS // 2, LANES, LANES), F32),
                        pltpu.VMEM((RW_HEADS // 2, LANES, LANES), F32),
                        pltpu.VMEM((1, W_SHIFT), F32),
                        pltpu.VMEM((CHUNK, RW_WIDTH), F32)],
        compiler_params=_cparams(),
        name="rwkv_prompt",
    )(zr, p['mu'], p['w0'], p['w2a'], p['a0'], p['g2'], p['k_k'], p['k_a'], p['r_k'], p['lnx_w'], p['lnx_b'])


def _rwkv_sample_pre_kernel(zs_ref, prev_ref, mu_ref, w0_ref, w2a_ref, a0_ref, g2_ref, kk_ref, ka_ref,
                            r_ref, kh_ref, v_ref, g_ref, rt_ref, wt_ref, kht_ref, vt_ref, nkkt_ref, bt_ref):
    zs = zs_ref[...]
    z = zs + (prev_ref[...] - zs) * mu_ref[...]
    r, lw, k_h, v, kk, b, g = _rwkv_pre(z, w0_ref[...], w2a_ref[...], a0_ref[...], g2_ref[...], kk_ref[...],
                                        ka_ref[...])
    r_ref[...] = r
    kh_ref[...] = k_h
    v_ref[...] = v
    g_ref[...] = g
    rt_ref[...] = r.T
    wt_ref[...] = jnp.exp(lw).T
    kht_ref[...] = k_h.T
    vt_ref[...] = v.T
    nkkt_ref[...] = (-kk).T
    bt_ref[...] = b.T


def _rwkv_sample_pre(zr, prev, p):
    row = lambda n: pl.BlockSpec((1, n), lambda i: (0, 0))
    full = lambda a, b: pl.BlockSpec((a, b), lambda i: (0, 0))
    vec = jax.ShapeDtypeStruct((N_SAMPLE, RW_WIDTH), F32)
    vec_t = jax.ShapeDtypeStruct((RW_WIDTH, N_SAMPLE), F32)
    return pl.pallas_call(
        _rwkv_sample_pre_kernel,
        grid=(1,),
        in_specs=[pl.BlockSpec((N_SAMPLE, W_SHIFT), lambda i: (ROW_SAMPLE // N_SAMPLE, 0)),
                  full(N_SAMPLE, W_SHIFT),
                  row(W_SHIFT), row(RW_WIDTH), full(LANES, 2 * RW_WIDTH), row(RW_WIDTH), full(LANES, RW_WIDTH),
                  row(RW_WIDTH), row(RW_WIDTH)],
        out_specs=[full(N_SAMPLE, RW_WIDTH)] * 4 + [full(RW_WIDTH, N_SAMPLE)] * 6,
        out_shape=[vec] * 4 + [vec_t] * 6,
        compiler_params=_cparams(),
        name="rwkv_sample_pre",
    )(zr, prev, p['mu'], p['w0'], p['w2a'], p['a0'], p['g2'], p['k_k'], p['k_a'])


def _rwkv_sample_step_kernel(s_ref, r_ref, w_ref, kh_ref, v_ref, nkk_ref, b_ref, so_ref, y_ref):
    s = s_ref[0]
    sa = jnp.sum(s * nkk_ref[0][None], axis=1, keepdims=True)
    s_new = s * w_ref[0][None] + sa * b_ref[0][None] + v_ref[0] * kh_ref[0][None]
    so_ref[0] = s_new
    y_ref[0] = jnp.sum(s_new * r_ref[0][None], axis=1, keepdims=True)


def _rwkv_sample_step(state_t, r_t, w_t, kh_t, v_t, nkk_t, b_t):
    kvec = lambda t: t.reshape(RW_HEADS, RW_HEAD, N_SAMPLE)
    kspec = pl.BlockSpec((1, RW_HEAD, N_SAMPLE), lambda i: (i, 0, 0))
    vspec = pl.BlockSpec((1, RW_HEAD, 1, N_SAMPLE), lambda i: (i, 0, 0, 0))
    sspec = pl.BlockSpec((1, RW_HEAD, RW_HEAD, N_SAMPLE), lambda i: (i, 0, 0, 0))
    s_new, y = pl.pallas_call(
        _rwkv_sample_step_kernel,
        grid=(RW_HEADS,),
        in_specs=[sspec, kspec, kspec, kspec, vspec, kspec, kspec],
        out_specs=[sspec, vspec],
        out_shape=[jax.ShapeDtypeStruct(state_t.shape, F32),
                   jax.ShapeDtypeStruct((RW_HEADS, RW_HEAD, 1, N_SAMPLE), F32)],
        compiler_params=_cparams(),
        name="rwkv_sample_step",
    )(state_t, kvec(r_t), kvec(w_t), kvec(kh_t), v_t.reshape(RW_HEADS, RW_HEAD, 1, N_SAMPLE), kvec(nkk_t), kvec(b_t))
    return s_new, y.reshape(RW_WIDTH, N_SAMPLE)


def _rwkv_sample_post_kernel(yt_ref, r_ref, kh_ref, v_ref, g_ref, rk_ref, lw_ref, lb_ref, o_ref):
    o = _rwkv_post(yt_ref[...].T, r_ref[...], kh_ref[...], v_ref[...], g_ref[...], rk_ref[...], lw_ref[...],
                   lb_ref[...])
    o_ref[0:N_SAMPLE, :] = o.astype(o_ref.dtype)
    o_ref[N_SAMPLE:T_EXT, :] = jnp.zeros((T_EXT - N_SAMPLE, RW_WIDTH), o_ref.dtype)


def _rwkv_sample_post(y_t, r, kh, v, g, p):
    row = lambda n: pl.BlockSpec((1, n), lambda i: (0, 0))
    full = lambda a, b: pl.BlockSpec((a, b), lambda i: (0, 0))
    return pl.pallas_call(
        _rwkv_sample_post_kernel,
        grid=(1,),
        in_specs=[full(RW_WIDTH, N_SAMPLE)] + [full(N_SAMPLE, RW_WIDTH)] * 4 + [row(RW_WIDTH)] * 3,
        out_specs=full(T_EXT, RW_WIDTH),
        out_shape=jax.ShapeDtypeStruct((T_EXT, RW_WIDTH), BF16),
        compiler_params=_cparams(),
        name="rwkv_sample_post",
    )(y_t, r, kh, v, g, p['r_k'], p['lnx_w'], p['lnx_b'])


def _gla_logg(zg, gk_up, gk_b):
    gkd = zg[:, 2 * GLA_KDIM + GLA_WIDTH:2 * GLA_KDIM + GLA_WIDTH + LANES]
    x = _bdot(gkd, gk_up) + gk_b
    return (jnp.minimum(x, 0.0) - jnp.log(1.0 + jnp.exp(-jnp.abs(x)))) * (1.0 / GLA_NORMALIZER)


def _gla_post(o, og_act, norm_w):
    outs = []
    for h in range(GLA_HEADS):
        oh = o[:, h * GLA_DV:(h + 1) * GLA_DV]
        ms = jnp.mean(oh * oh, axis=-1, keepdims=True)
        outs.append(oh * lax.rsqrt(ms + RMS_EPS) * norm_w)
    return jnp.concatenate(outs, axis=1) * og_act


def _gla_chunk_kernel(zg_ref, og_ref, gkup_ref, gkb_ref, nw_ref, o_ref, sout_ref, s_sc, smeta_sc, *, n_chunks):
    i = pl.program_id(0)
    is_meta = i == 0
    c = lax.rem(jnp.maximum(i - 1, 0), n_chunks)
    first = jnp.logical_and(i >= 1, c == 0)

    @pl.when(is_meta)
    def _():
        s_sc[...] = jnp.zeros_like(s_sc)

    @pl.when(first)
    def _():
        s_sc[...] = smeta_sc[...]

    zg = zg_ref[...]
    rowi = _iota((CHUNK, 1), 0)
    logg = _gla_logg(zg, gkup_ref[...], gkb_ref[...])
    logg = jnp.where(jnp.logical_and(is_meta, rowi < META_PAD), 0.0, logg)
    bcum = _split_dot(_tri_incl(CHUNK), logg)
    b_end = bcum[CHUNK - 1:CHUNK, :]
    qt = zg[:, 0:GLA_KDIM] * (GLA_DK ** -0.5) * jnp.exp(bcum)
    kt = zg[:, GLA_KDIM:2 * GLA_KDIM] * jnp.exp(-bcum)
    ke = zg[:, GLA_KDIM:2 * GLA_KDIM] * jnp.exp(b_end - bcum)
    e_end = jnp.exp(b_end)
    causal = _iota((CHUNK, CHUNK), 0) >= _iota((CHUNK, CHUNK), 1)
    heads = range(GLA_HEADS)
    ks = [slice(h * GLA_DK, (h + 1) * GLA_DK) for h in heads]
    vh = [zg[:, 2 * GLA_KDIM + h * GLA_DV:2 * GLA_KDIM + (h + 1) * GLA_DV].astype(BF16) for h in heads]
    a = [jnp.where(causal, _bdot_nt(qt[:, ks[h]], kt[:, ks[h]]), 0.0) for h in heads]
    s0 = [s_sc[h] for h in heads]
    o_inter = [_bdot(qt[:, ks[h]], s0[h]) for h in heads]
    s_add = [_bdot_tn(ke[:, ks[h]], vh[h]) for h in heads]
    outs = [_bdot(a[h], vh[h]) + o_inter[h] for h in heads]
    s_new = [_col_of_row(e_end[:, ks[h]]) * s0[h] + s_add[h] for h in heads]
    for h in heads:
        s_sc[h] = s_new[h]

    @pl.when(jnp.logical_and(i >= 1, c == n_chunks - 1))
    def _():
        for h in heads:
            sout_ref[0, h] = s_new[h]

    o = _gla_post(jnp.concatenate(outs, axis=1), og_ref[...].astype(F32), nw_ref[...])
    o_ref[...] = o.astype(o_ref.dtype)

    @pl.when(is_meta)
    def _():
        smeta_sc[...] = s_sc[...]


def _gla_prompt(zg, zb, p, n_batch=N_BATCH, n_chunks=N_CHUNKS, meta_block=ROW_META // CHUNK, og_block=OG_BLOCK):
    full = lambda a, b: pl.BlockSpec((a, b), lambda i: (0, 0))
    blk = lambda i: jnp.where(i == 0, meta_block, i - 1)
    return pl.pallas_call(
        functools.partial(_gla_chunk_kernel, n_chunks=n_chunks),
        grid=(1 + n_batch * n_chunks,),
        in_specs=[pl.BlockSpec((CHUNK, W_GLA_PAD), lambda i: (blk(i), 0)),
                  pl.BlockSpec((CHUNK, GLA_WIDTH), lambda i: (blk(i), og_block)),
                  full(LANES, GLA_KDIM), full(1, GLA_KDIM), full(1, GLA_DV)],
        out_specs=[pl.BlockSpec((CHUNK, GLA_WIDTH), lambda i: (jnp.maximum(i - 1, 0), 0)),
                   pl.BlockSpec((1, GLA_HEADS, GLA_DK, GLA_DV),
                                lambda i: (jnp.maximum(i - 1, 0) // n_chunks, 0, 0, 0))],
        out_shape=[jax.ShapeDtypeStruct((n_batch * n_chunks * CHUNK, GLA_WIDTH), BF16),
                   jax.ShapeDtypeStruct((n_batch, GLA_HEADS, GLA_DK, GLA_DV), F32)],
        scratch_shapes=[pltpu.VMEM((GLA_HEADS, GLA_DK, GLA_DV), F32),
                        pltpu.VMEM((GLA_HEADS, GLA_DK, GLA_DV), F32)],
        compiler_params=_cparams(),
        name="gla_prompt",
    )(zg, zb, p['gk_up'], p['gk_b'], p['norm_w'])


def _gla_sample_pre_kernel(zg_ref, gkup_ref, gkb_ref, q_ref, k_ref, eg_ref):
    zg = zg_ref[...]
    logg = _gla_logg(zg, gkup_ref[...], gkb_ref[...])
    q_ref[...] = zg[:, 0:GLA_KDIM] * (GLA_DK ** -0.5)
    k_ref[...] = zg[:, GLA_KDIM:2 * GLA_KDIM]
    eg_ref[...] = jnp.exp(logg)


def _gla_sample_pre(zg, p):
    full = lambda a, b: pl.BlockSpec((a, b), lambda i: (0, 0))
    vec = jax.ShapeDtypeStruct((N_SAMPLE, GLA_KDIM), F32)
    return pl.pallas_call(
        _gla_sample_pre_kernel,
        grid=(1,),
        in_specs=[pl.BlockSpec((N_SAMPLE, W_GLA_PAD), lambda i: (ROW_SAMPLE // N_SAMPLE, 0)),
                  full(LANES, GLA_KDIM), full(1, GLA_KDIM)],
        out_specs=[full(N_SAMPLE, GLA_KDIM)] * 3,
        out_shape=[vec] * 3,
        compiler_params=_cparams(),
        name="gla_sample_pre",
    )(zg, p['gk_up'], p['gk_b'])


def _gla_sample_step_kernel(s_ref, q_ref, k_ref, eg_ref, v_ref, so_ref, o_ref):
    s = s_ref[0]
    eye = _iota((GLA_DK, GLA_DK), 0) == _iota((GLA_DK, GLA_DK), 1)
    col = lambda ref: jnp.sum(jnp.where(eye, ref[0], 0.0), axis=-1, keepdims=True)
    s_new = col(eg_ref) * s + col(k_ref) * v_ref[0]
    so_ref[0] = s_new
    o_ref[0] = jnp.sum(col(q_ref) * s_new, axis=-2, keepdims=True)


def _gla_sample_step(state, q, k, eg, v):
    kv = lambda t: t.reshape(N_SAMPLE, GLA_HEADS, 1, GLA_DK)
    kspec = pl.BlockSpec((1, GLA_HEADS, 1, GLA_DK), lambda i: (i, 0, 0, 0))
    vspec = pl.BlockSpec((1, GLA_HEADS, 1, GLA_DV), lambda i: (i, 0, 0, 0))
    sspec = pl.BlockSpec((1, GLA_HEADS, GLA_DK, GLA_DV), lambda i: (i, 0, 0, 0))
    s_new, o = pl.pallas_call(
        _gla_sample_step_kernel,
        grid=(N_SAMPLE,),
        in_specs=[sspec, kspec, kspec, kspec, vspec],
        out_specs=[sspec, vspec],
        out_shape=[jax.ShapeDtypeStruct(state.shape, F32),
                   jax.ShapeDtypeStruct((N_SAMPLE, GLA_HEADS, 1, GLA_DV), F32)],
        compiler_params=_cparams(),
        name="gla_sample_step",
    )(state, kv(q), kv(k), kv(eg), v.reshape(N_SAMPLE, GLA_HEADS, 1, GLA_DV))
    return s_new, o.reshape(N_SAMPLE, GLA_WIDTH)


def _gla_sample_post_kernel(o_ref, og_ref, nw_ref, out_ref):
    o = _gla_post(o_ref[...], og_ref[0:N_SAMPLE, :].astype(F32), nw_ref[...])
    out_ref[0:N_SAMPLE, :] = o.astype(out_ref.dtype)
    out_ref[N_SAMPLE:T_EXT, :] = jnp.zeros((T_EXT - N_SAMPLE, GLA_WIDTH), out_ref.dtype)


def _gla_sample_post(o, zb, p, og_block=OG_BLOCK):
    full = lambda a, b: pl.BlockSpec((a, b), lambda i: (0, 0))
    return pl.pallas_call(
        _gla_sample_post_kernel,
        grid=(1,),
        in_specs=[full(N_SAMPLE, GLA_WIDTH),
                  pl.BlockSpec((T_EXT, GLA_WIDTH), lambda i: (T_MAIN // T_EXT, og_block)),
                  full(1, GLA_DV)],
        out_specs=full(T_EXT, GLA_WIDTH),
        out_shape=jax.ShapeDtypeStruct((T_EXT, GLA_WIDTH), BF16),
        compiler_params=_cparams(),
        name="gla_sample_post",
    )(o, zb, p['norm_w'])


def _mix_route_kernel(or_ref, ore_ref, og_ref, oge_ref, gates_ref, xp_ref, xe_ref, pr_ref, pg_ref, wo_ref, n2_ref,
                      wrh_ref, wrl_ref, br_ref, h1_ref, xn_ref, info_ref, cnt_ref, carry_sc):
    i = pl.program_id(0)
    is_main = i < T_MAIN // TM_DENSE

    @pl.when(i == 0)
    def _():
        carry_sc[...] = jnp.zeros_like(carry_sc)

    sig_r = gates_ref[:, 0:D_MODEL].astype(F32)
    sig_g = gates_ref[:, D_MODEL:2 * D_MODEL].astype(F32)
    o_r = jnp.where(is_main, or_ref[...], ore_ref[...])
    o_g = jnp.where(is_main, og_ref[...], oge_ref[...])
    m = (sig_r * jnp.dot(o_r, pr_ref[...], preferred_element_type=F32)
         + sig_g * jnp.dot(o_g, pg_ref[...], preferred_element_type=F32))
    h = jnp.where(is_main, xp_ref[...], xe_ref[...])
    h1 = h + jnp.dot(m.astype(BF16), wo_ref[...], preferred_element_type=F32)
    h1_ref[...] = h1
    ms = jnp.mean(h1 * h1, axis=-1, keepdims=True)
    xn = h1 * lax.rsqrt(ms + RMS_EPS) * n2_ref[...]
    xn_ref[...] = xn
    xh = xn.astype(BF16)
    xl = (xn - xh.astype(F32)).astype(BF16)
    lg = (jnp.dot(xh, wrh_ref[...], preferred_element_type=F32)
          + jnp.dot(xh, wrl_ref[...], preferred_element_type=F32)
          + jnp.dot(xl, wrh_ref[...], preferred_element_type=F32)) + br_ref[...]

    lane = _iota(lg.shape, 1)
    lanef = lane.astype(F32)
    is_g = jnp.logical_and(lane >= N_EXPERTS, lane < N_EXPERTS + N_GROUPS)
    gl = jnp.where(is_g, lg, NEG_BIG)
    gmax = jnp.max(gl, axis=-1, keepdims=True)
    gsel = jnp.min(jnp.where(jnp.logical_and(is_g, gl == gmax), lanef, 1e9), axis=-1, keepdims=True) - N_EXPERTS
    p_g = 1.0 / jnp.sum(jnp.exp(gl - gmax), axis=-1, keepdims=True)
    grp = (lane // EXPERTS_PER_GROUP).astype(F32)
    in_grp = jnp.logical_and(lane < N_EXPERTS, grp == gsel)
    el = jnp.where(in_grp, lg, NEG_BIG)
    m1 = jnp.max(el, axis=-1, keepdims=True)
    i1 = jnp.min(jnp.where(jnp.logical_and(in_grp, el == m1), lanef, 1e9), axis=-1, keepdims=True)
    in2 = jnp.logical_and(in_grp, lanef != i1)
    el2 = jnp.where(in2, lg, NEG_BIG)
    m2 = jnp.max(el2, axis=-1, keepdims=True)
    i2 = jnp.min(jnp.where(jnp.logical_and(in2, el2 == m2), lanef, 1e9), axis=-1, keepdims=True)
    e2 = jnp.exp(m2 - m1)
    w1 = p_g / (1.0 + e2)
    w2 = p_g * e2 / (1.0 + e2)

    oh1 = lanef == i1
    oh2 = lanef == i2
    cnt = jnp.where(jnp.logical_or(oh1, oh2), 1.0, 0.0)
    tm = cnt.shape[0]
    lstrict = (_iota((tm, tm), 0) > _iota((tm, tm), 1)).astype(BF16)
    before = jnp.dot(lstrict, cnt.astype(BF16), preferred_element_type=F32) + carry_sc[...]
    rank1 = jnp.sum(jnp.where(oh1, before, 0.0), axis=-1, keepdims=True)
    rank2 = jnp.sum(jnp.where(oh2, before, 0.0), axis=-1, keepdims=True)
    carry_sc[...] = carry_sc[...] + jnp.sum(cnt, axis=0, keepdims=True)
    cnt_ref[...] = carry_sc[...]
    info = jnp.where(lane == 0, i1, jnp.where(lane == 1, i2, jnp.where(lane == 2, w1, jnp.where(
        lane == 3, w2, jnp.where(lane == 4, rank1, jnp.where(lane == 5, rank2, 0.0))))))
    info_ref[...] = info


def _mix_route(o_r, o_r_ext, o_g, o_g_ext, zb, xp, xe, p):
    nmain = T_MAIN // TM_DENSE
    tile = lambda n: pl.BlockSpec((TM_DENSE, n), lambda i: (i, 0))
    main = lambda n: pl.BlockSpec((TM_DENSE, n), lambda i: (jnp.minimum(i, nmain - 1), 0))
    ext = lambda n: pl.BlockSpec((T_EXT, n), lambda i: (0, 0))
    const = lambda a, b: pl.BlockSpec((a, b), lambda i: (0, 0), pipeline_mode=pl.Buffered(1))
    return pl.pallas_call(
        _mix_route_kernel,
        grid=(T_ALL // TM_DENSE,),
        in_specs=[main(RW_WIDTH), ext(RW_WIDTH), main(GLA_WIDTH), ext(GLA_WIDTH), tile(2 * D_MODEL),
                  main(D_MODEL), ext(D_MODEL),
                  const(RW_WIDTH, D_MODEL), const(GLA_WIDTH, D_MODEL), const(D_MODEL, D_MODEL),
                  const(1, D_MODEL), const(D_MODEL, LANES), const(D_MODEL, LANES), const(1, LANES)],
        out_specs=[tile(D_MODEL), tile(D_MODEL), tile(LANES), pl.BlockSpec((1, LANES), lambda i: (0, 0))],
        out_shape=[jax.ShapeDtypeStruct((T_ALL, D_MODEL), F32),
                   jax.ShapeDtypeStruct((T_ALL, D_MODEL), F32),
                   jax.ShapeDtypeStruct((T_ALL, LANES), F32),
                   jax.ShapeDtypeStruct((1, LANES), F32)],
        scratch_shapes=[pltpu.VMEM((1, LANES), F32)],
        compiler_params=_cparams(),
        name="mix_route",
    )(o_r, o_r_ext, o_g, o_g_ext, zb, xp, xe, p['p_rwkv'], p['p_gla'], p['w_out'], p['norm2_w'], p['wr_hi'],
      p['wr_lo'], p['b_route'])


def _route_tables_kernel(info_ref, cnt_ref, src_ref, d_ref, te_ref, nt_ref, d_vm, d_sm, zero_vm, sem):
    i = pl.program_id(0)
    lane1 = _iota((1, LANES), 1)
    cnt = jnp.where(lane1 < N_EXPERTS, cnt_ref[...], 0.0)
    tiles = jnp.floor((cnt + (TM_MOE - 1)) * (1.0 / TM_MOE))
    upper = (_iota((LANES, LANES), 0) < _iota((LANES, LANES), 1)).astype(BF16)
    tile_start = jnp.dot(jnp.broadcast_to(tiles, (8, LANES)).astype(BF16), upper,
                         preferred_element_type=F32)[0:1, :]
    base = tile_start * TM_MOE

    info = info_ref[...]
    lanef = _iota(info.shape, 1).astype(F32)
    pick = lambda col: jnp.sum(jnp.where(lanef == info[:, col:col + 1], base, 0.0), axis=-1, keepdims=True)
    d1 = pick(0) + info[:, 4:5]
    d2 = pick(1) + info[:, 5:6]
    tm = info.shape[0]
    eye = _iota((tm, tm), 0) == _iota((tm, tm), 1)
    to_row = lambda col: jnp.sum(jnp.where(eye, col, 0.0), axis=0, keepdims=True)
    d_row = jnp.concatenate([to_row(d1), to_row(d2)], axis=1).astype(jnp.int32)
    d_ref[0] = d_row
    d_vm[...] = d_row

    @pl.when(i == 0)
    def _():
        tile_end = tile_start + tiles
        n_tiles = jnp.sum(tiles, axis=-1, keepdims=True)
        eye_l = _iota((LANES, LANES), 0) == _iota((LANES, LANES), 1)
        end_col = jnp.sum(jnp.where(eye_l, tile_end, 0.0), axis=-1, keepdims=True)
        rowl = _iota((LANES, LANES), 0)
        tile_f = _iota((LANES, LANES), 1).astype(F32)
        te = jnp.sum(jnp.where(jnp.logical_and(rowl < N_EXPERTS, end_col <= tile_f), 1.0, 0.0), axis=0,
                     keepdims=True)
        last_e = jnp.max(jnp.where(tiles > 0.0, lane1.astype(F32), 0.0), axis=-1, keepdims=True)
        te = jnp.where(lane1.astype(F32) < n_tiles, jnp.minimum(te, N_EXPERTS - 1.0), last_e)
        te_ref[...] = te.astype(jnp.int32)
        nt_ref[...] = jnp.broadcast_to(n_tiles, (1, LANES)).astype(jnp.int32)
        zero_vm[...] = jnp.zeros_like(zero_vm)
        init = pltpu.make_async_copy(zero_vm, src_ref, sem)
        init.start()
        init.wait()

    d_copy = pltpu.make_async_copy(d_vm, d_sm, sem)
    d_copy.start()
    d_copy.wait()

    def place(t, carry):
        tok = i * tm + t
        for half in range(2):
            d = d_sm[0, half * tm + t]
            src_ref[d // LANES, lax.rem(d, LANES)] = tok
        return carry

    lax.fori_loop(0, tm, place, 0, unroll=8)


def _route_tables(info, counts):
    nt = T_ALL // TM_DENSE
    src_rows = NT_MOE * TM_MOE // LANES
    return pl.pallas_call(
        _route_tables_kernel,
        grid=(nt,),
        in_specs=[pl.BlockSpec((TM_DENSE, LANES), lambda i: (i, 0)),
                  pl.BlockSpec((1, LANES), lambda i: (0, 0))],
        out_specs=[pl.BlockSpec(memory_space=pltpu.SMEM),
                   pl.BlockSpec((1, 1, 2 * TM_DENSE), lambda i: (i, 0, 0)),
                   pl.BlockSpec((1, LANES), lambda i: (0, 0)),
                   pl.BlockSpec((1, LANES), lambda i: (0, 0))],
        out_shape=[jax.ShapeDtypeStruct((src_rows, LANES), jnp.int32),
                   jax.ShapeDtypeStruct((nt, 1, 2 * TM_DENSE), jnp.int32),
                   jax.ShapeDtypeStruct((1, LANES), jnp.int32),
                   jax.ShapeDtypeStruct((1, LANES), jnp.int32)],
        scratch_shapes=[pltpu.VMEM((1, 2 * TM_DENSE), jnp.int32),
                        pltpu.SMEM((1, 2 * TM_DENSE), jnp.int32),
                        pltpu.VMEM((src_rows, LANES), jnp.int32),
                        pltpu.SemaphoreType.DMA(())],
        compiler_params=_cparams(),
        name="moe_route_tables",
    )(info, counts)


def _moe_kernel(te_ref, nt_ref, scur_ref, snext_ref, xn_hbm, w1_ref, w3_ref, w2_ref, o_ref, xbuf, w1b, w3b, w2b,
                sem):
    i = pl.program_id(0)
    n_tiles = nt_ref[0]
    slot = lax.rem(i, 2)

    def row_copy(s_ref, s, t):
        return pltpu.make_async_copy(xn_hbm.at[pl.ds(s_ref[0, 0, t], 1), :], xbuf.at[s, pl.ds(t, 1), :], sem.at[s])

    def gather(s_ref, s):
        def body(t, carry):
            row_copy(s_ref, s, t).start()
            return carry
        lax.fori_loop(0, TM_MOE, body, 0, unroll=8)

    @pl.when(jnp.logical_and(i == 0, i < n_tiles))
    def _():
        gather(scur_ref, slot)

    @pl.when(i + 1 < n_tiles)
    def _():
        gather(snext_ref, 1 - slot)

    @pl.when(i < n_tiles)
    def _():
        changed = jnp.logical_or(i == 0, te_ref[i] != te_ref[jnp.maximum(i - 1, 0)])

        @pl.when(changed)
        def _():
            w1b[...] = w1_ref[0].astype(BF16)
            w3b[...] = w3_ref[0].astype(BF16)
            w2b[...] = w2_ref[0].astype(BF16)

        def drain(t, carry):
            row_copy(scur_ref, slot, t).wait()
            return carry

        lax.fori_loop(0, TM_MOE, drain, 0, unroll=8)
        xb = xbuf[slot].astype(BF16)
        h1 = jnp.dot(xb, w1b[...], preferred_element_type=F32)
        h3 = jnp.dot(xb, w3b[...], preferred_element_type=F32)
        hh = (h1 * jax.nn.sigmoid(h1) * h3).astype(BF16)
        o_ref[...] = jnp.dot(hh, w2b[...], preferred_element_type=F32)

    @pl.when(i >= nt_ref[0])
    def _():
        o_ref[...] = jnp.zeros_like(o_ref)


def _moe(tile_expert, n_tiles, src, xn, w1, w3, w2):
    grid_spec = pltpu.PrefetchScalarGridSpec(
        num_scalar_prefetch=2,
        grid=(NT_MOE,),
        in_specs=[pl.BlockSpec((1, 1, TM_MOE), lambda i, te, nt: (i, 0, 0), memory_space=pltpu.SMEM),
                  pl.BlockSpec((1, 1, TM_MOE), lambda i, te, nt: (jnp.minimum(i + 1, NT_MOE - 1), 0, 0),
                               memory_space=pltpu.SMEM),
                  pl.BlockSpec(memory_space=pl.ANY),
                  pl.BlockSpec((1, D_MODEL, D_EXPERT), lambda i, te, nt: (te[i], 0, 0)),
                  pl.BlockSpec((1, D_MODEL, D_EXPERT), lambda i, te, nt: (te[i], 0, 0)),
                  pl.BlockSpec((1, D_EXPERT, D_MODEL), lambda i, te, nt: (te[i], 0, 0))],
        out_specs=pl.BlockSpec((TM_MOE, D_MODEL), lambda i, te, nt: (i, 0)),
        scratch_shapes=[pltpu.VMEM((2, TM_MOE, D_MODEL), F32),
                        pltpu.VMEM((D_MODEL, D_EXPERT), BF16),
                        pltpu.VMEM((D_MODEL, D_EXPERT), BF16),
                        pltpu.VMEM((D_EXPERT, D_MODEL), BF16),
                        pltpu.SemaphoreType.DMA((2,))],
    )
    return pl.pallas_call(
        _moe_kernel,
        grid_spec=grid_spec,
        out_shape=jax.ShapeDtypeStruct((NT_MOE * TM_MOE, D_MODEL), F32),
        compiler_params=_cparams(),
        name="moe_experts",
    )(tile_expert, n_tiles, src.reshape(NT_MOE, 1, TM_MOE), src.reshape(NT_MOE, 1, TM_MOE), xn, w1, w3, w2)


def _combine_kernel(dcur_ref, dnext_ref, h1_ref, info_ref, fw_ref, eo_hbm, yp_ref, ye_ref, buf, sem):
    i = pl.program_id(0)
    nmain = T_MAIN // TM_DENSE
    slot = lax.rem(i, 2)

    def row_copy(d_ref, s, t):
        return pltpu.make_async_copy(eo_hbm.at[pl.ds(d_ref[0, 0, t], 1), :], buf.at[s, pl.ds(t, 1), :], sem.at[s])

    def issue_all(d_ref, s):
        def body(t, carry):
            row_copy(d_ref, s, t).start()
            return carry
        lax.fori_loop(0, 2 * TM_DENSE, body, 0, unroll=8)

    @pl.when(i == 0)
    def _():
        issue_all(dcur_ref, slot)

    @pl.when(i + 1 < pl.num_programs(0))
    def _():
        issue_all(dnext_ref, 1 - slot)

    def drain(t, carry):
        row_copy(dcur_ref, slot, t).wait()
        return carry

    lax.fori_loop(0, 2 * TM_DENSE, drain, 0, unroll=8)

    info = info_ref[...]
    y = h1_ref[...] + info[:, 2:3] * buf[slot, 0:TM_DENSE, :] + info[:, 3:4] * buf[slot, TM_DENSE:2 * TM_DENSE, :]
    ms = jnp.mean(y * y, axis=-1, keepdims=True)
    out = y * lax.rsqrt(ms + RMS_EPS) * fw_ref[...]

    @pl.when(i < nmain)
    def _():
        yp_ref[...] = out

    @pl.when(i == nmain)
    def _():
        ye_ref[...] = out


def _combine(dst, h1, info, fw, eo):
    nmain = T_MAIN // TM_DENSE
    return pl.pallas_call(
        _combine_kernel,
        grid=(T_ALL // TM_DENSE,),
        in_specs=[pl.BlockSpec((1, 1, 2 * TM_DENSE), lambda i: (i, 0, 0), memory_space=pltpu.SMEM),
                  pl.BlockSpec((1, 1, 2 * TM_DENSE), lambda i: (jnp.minimum(i + 1, nmain), 0, 0),
                               memory_space=pltpu.SMEM),
                  pl.BlockSpec((TM_DENSE, D_MODEL), lambda i: (i, 0)),
                  pl.BlockSpec((TM_DENSE, LANES), lambda i: (i, 0)),
                  pl.BlockSpec((1, D_MODEL), lambda i: (0, 0)),
                  pl.BlockSpec(memory_space=pl.ANY)],
        out_specs=[pl.BlockSpec((TM_DENSE, D_MODEL), lambda i: (jnp.minimum(i, nmain - 1), 0)),
                   pl.BlockSpec((T_EXT, D_MODEL), lambda i: (0, 0))],
        out_shape=[jax.ShapeDtypeStruct((T_MAIN, D_MODEL), F32),
                   jax.ShapeDtypeStruct((T_EXT, D_MODEL), F32)],
        scratch_shapes=[pltpu.VMEM((2, 2 * TM_DENSE, D_MODEL), F32), pltpu.SemaphoreType.DMA((2,))],
        compiler_params=_cparams(),
        name="moe_combine",
    )(dst, dst, h1, info, fw, eo)


def kernel(x_prompt, x_sample, state_rwkv, state_shift, state_gla, meta_tokens, norm1_w, w_in, mu_shift, rw_w0,
           rw_w2, rw_a0, rw_a2, rw_g2, rw_k_k, rw_k_a, rw_r_k, rw_lnx_w, rw_lnx_b, gla_gk_up, gla_gk_b, gla_norm_w,
           p_rwkv, p_gla, w_out, norm2_w, moe_w_group, moe_b_group, moe_w_router, moe_b_router, moe_w1, moe_w3,
           moe_w2, final_norm_w):
    w_t = w_in[0].T
    row = lambda t: t.reshape(1, -1)
    w2a = jnp.zeros((LANES, 2 * RW_WIDTH), F32)
    w2a = w2a.at[0:64, 0:RW_WIDTH].set(rw_w2[0]).at[64:128, RW_WIDTH:].set(rw_a2[0])
    rw = dict(mu=row(mu_shift[0][_SHIFT_PERM]), w0=row(rw_w0[0]), w2a=w2a, a0=row(rw_a0[0]), g2=rw_g2[0],
              k_k=row(rw_k_k[0]), k_a=row(rw_k_a[0]), r_k=row(rw_r_k[0]), lnx_w=row(rw_lnx_w[0]),
              lnx_b=row(rw_lnx_b[0]))
    gk_up = jnp.zeros((LANES, GLA_KDIM), F32).at[0:GLA_LORA].set(gla_gk_up[0])
    gl = dict(gk_up=gk_up, gk_b=row(gla_gk_b[0]), norm_w=row(gla_norm_w[0]))
    w_route = jnp.zeros((D_MODEL, LANES), F32)
    w_route = w_route.at[:, 0:N_EXPERTS].set(moe_w_router[0]).at[:, N_EXPERTS:N_EXPERTS + N_GROUPS].set(moe_w_group[0])
    wr_hi = w_route.astype(BF16)
    wr_lo = (w_route - wr_hi.astype(F32)).astype(BF16)
    b_route = jnp.zeros((1, LANES), F32)
    b_route = b_route.at[0, 0:N_EXPERTS].set(moe_b_router[0]).at[0, N_EXPERTS:N_EXPERTS + N_GROUPS].set(moe_b_group[0])
    mix = dict(p_rwkv=p_rwkv[0].astype(BF16), p_gla=p_gla[0].astype(BF16), w_out=w_out[0].astype(BF16),
               norm2_w=row(norm2_w[0]), wr_hi=wr_hi, wr_lo=wr_lo, b_route=b_route)

    xp = x_prompt.reshape(T_MAIN, D_MODEL)
    xe = jnp.concatenate([x_sample[:, 0, :], jnp.zeros((META_PAD, D_MODEL), F32), meta_tokens,
                          jnp.zeros((T_ALL - ROW_META - CHUNK, D_MODEL), F32)], axis=0)

    xn = _norm1(xp, xe, row(norm1_w[0]))
    zr = _proj(xn, w_t, _RWKV_TILES, 1664, F32, name="proj_rwkv")
    zg = _proj(xn, w_t, _GLA_TILES, 1152, F32, name="proj_gla")
    zb = _proj(xn, w_t, _GATE_TILES, 1024, BF16, silu_from=4, name="proj_gates")

    o_r, h_fin = _rwkv_prompt(zr, rw)
    o_g, s_fin = _gla_prompt(zg, zb, gl)

    r_s, kh_s, v_s, g_s, r_t, w_t, kh_t, v_t, nkk_t, b_t = _rwkv_sample_pre(zr, state_shift[0][:, _SHIFT_PERM], rw)
    rw_new_t, y_t = _rwkv_sample_step(state_rwkv[0].transpose(1, 2, 3, 0), r_t, w_t, kh_t, v_t, nkk_t, b_t)
    rw_new = rw_new_t.transpose(3, 0, 1, 2)
    o_r_ext = _rwkv_sample_post(y_t, r_s, kh_s, v_s, g_s, rw)
    q_s, k_s, eg_s = _gla_sample_pre(zg, gl)
    v_gs = zg[ROW_SAMPLE:ROW_SAMPLE + N_SAMPLE, 2 * GLA_KDIM:2 * GLA_KDIM + GLA_WIDTH]
    gla_new, og_s = _gla_sample_step(state_gla[0], q_s, k_s, eg_s, v_gs)
    o_g_ext = _gla_sample_post(og_s, zb, gl)

    h1, xn2, info, counts = _mix_route(o_r, o_r_ext, o_g, o_g_ext, zb, xp, xe, mix)
    src, dst, tile_expert, n_tiles = _route_tables(info, counts)
    eo = _moe(tile_expert[0, :NT_MOE], n_tiles[0, :1], src, xn2, moe_w1[0], moe_w3[0], moe_w2[0])
    y_p, y_e = _combine(dst, h1, info, row(final_norm_w), eo)

    y_prompt = y_p.reshape(N_BATCH, SEQ, D_MODEL)
    y_sample = y_e[0:N_SAMPLE].reshape(N_SAMPLE, 1, D_MODEL)
    new_rwkv_prompt = h_fin.reshape(N_BATCH, RW_HEAD, RW_HEADS, RW_HEAD).transpose(0, 2, 3, 1)[None]
    shift_rows = jnp.concatenate([zr[SEQ - 1:T_MAIN:SEQ], zr[ROW_SAMPLE:ROW_SAMPLE + N_SAMPLE]], axis=0)
    shift_rows = shift_rows[:, _SHIFT_INV]
    return (y_prompt, y_sample, new_rwkv_prompt, shift_rows[None, 0:N_BATCH], s_fin[None],
            rw_new[None], shift_rows[None, N_BATCH:], gla_new[None])
```

```python
import functools
import math

import numpy as np
import jax
import jax.numpy as jnp
from jax import lax
from jax.experimental import pallas as pl
from jax.experimental.pallas import tpu as pltpu

F32 = jnp.float32
BF16 = jnp.bfloat16

D_MODEL = 2048
N_BATCH = 4
SEQ = 2048
N_SAMPLE = 128
N_META = 16
RMS_EPS = 1e-6

RW_WIDTH = 1024
RW_HEAD = 64
RW_HEADS = 16
RW_GN_EPS = RW_HEAD * 1e-5
W_SHIFT = 3328
GLA_HEADS = 4
GLA_DK = 128
GLA_DV = 256
GLA_KDIM = 512
GLA_WIDTH = 1024
GLA_LORA = 16
GLA_NORMALIZER = 16.0
W_GLA_PAD = 2304
OG_BLOCK = 2 * D_MODEL // GLA_WIDTH
N_GROUPS = 4
EXPERTS_PER_GROUP = 8
N_EXPERTS = 32
D_EXPERT = 512

CHUNK = 64
N_CHUNKS = SEQ // CHUNK
META_PAD = CHUNK - N_META
T_MAIN = N_BATCH * SEQ
ROW_SAMPLE = T_MAIN
ROW_META = T_MAIN + N_SAMPLE
T_EXT = 256
T_ALL = T_MAIN + T_EXT
LANES = 128

TM_DENSE = 256
TM_MM = 1056
TM_MOE = 256
NT_MOE = (2 * T_ALL) // TM_MOE + N_EXPERTS
NEG_BIG = -1e30

VMEM_LIMIT = 56 * 1024 * 1024

_SHIFT_PERM = np.concatenate([np.arange(0, 1024), np.arange(1088, 2112), np.arange(2112, 3136),
                              np.arange(1024, 1088), np.arange(3136, 3328)])
_SHIFT_INV = np.argsort(_SHIFT_PERM)


def _cparams(n_axes=1):
    return pltpu.CompilerParams(dimension_semantics=("arbitrary",) * n_axes, vmem_limit_bytes=VMEM_LIMIT)


def _bdot(a, b):
    return jnp.dot(a.astype(BF16), b.astype(BF16), preferred_element_type=F32)


def _bdot_nt(a, b):
    return lax.dot_general(a.astype(BF16), b.astype(BF16), (((1,), (1,)), ((), ())), preferred_element_type=F32)


def _bdot_tn(a, b):
    return lax.dot_general(a.astype(BF16), b.astype(BF16), (((0,), (0,)), ((), ())), preferred_element_type=F32)


def _split_dot(m_bf16, x):
    hi = x.astype(BF16)
    lo = (x - hi.astype(F32)).astype(BF16)
    return (jnp.dot(m_bf16, hi, preferred_element_type=F32) + jnp.dot(m_bf16, lo, preferred_element_type=F32))


def _iota(shape, dim):
    return lax.broadcasted_iota(jnp.int32, shape, dim)


def _tri_incl(n):
    return (_iota((n, n), 0) >= _iota((n, n), 1)).astype(BF16)


def _seg_sum(x, width):
    m, n = x.shape
    nb = n // LANES
    bd = ((_iota((LANES, LANES), 0) // width) == (_iota((LANES, LANES), 1) // width)).astype(BF16)
    xs = jnp.concatenate([x[:, j * LANES:(j + 1) * LANES] for j in range(nb)], axis=0)
    hi = xs.astype(BF16)
    lo = (xs - hi.astype(F32)).astype(BF16)
    s = jnp.dot(hi, bd, preferred_element_type=F32) + jnp.dot(lo, bd, preferred_element_type=F32)
    return jnp.concatenate([s[j * m:(j + 1) * m] for j in range(nb)], axis=1)


def _col_of_row(row):
    n = row.shape[-1]
    eye = _iota((n, n), 0) == _iota((n, n), 1)
    return jnp.sum(jnp.where(eye, jnp.broadcast_to(row, (n, n)), 0.0), axis=-1, keepdims=True)


def _norm1_kernel(xp_ref, xe_ref, w_ref, o_ref):
    i = pl.program_id(0)

    def f(x):
        ms = jnp.mean(x * x, axis=-1, keepdims=True)
        return (x * lax.rsqrt(ms + RMS_EPS) * w_ref[...]).astype(BF16)

    @pl.when(i < T_MAIN // TM_DENSE)
    def _():
        o_ref[...] = f(xp_ref[...])

    @pl.when(i == T_MAIN // TM_DENSE)
    def _():
        o_ref[...] = f(xe_ref[...])


def _norm1(xp, xe, w):
    nmain = T_MAIN // TM_DENSE
    return pl.pallas_call(
        _norm1_kernel,
        grid=(T_ALL // TM_DENSE,),
        in_specs=[pl.BlockSpec((TM_DENSE, D_MODEL), lambda i: (jnp.minimum(i, nmain - 1), 0)),
                  pl.BlockSpec((T_EXT, D_MODEL), lambda i: (0, 0)),
                  pl.BlockSpec((1, D_MODEL), lambda i: (0, 0))],
        out_specs=pl.BlockSpec((TM_DENSE, D_MODEL), lambda i: (i, 0)),
        out_shape=jax.ShapeDtypeStruct((T_ALL, D_MODEL), BF16),
        compiler_params=_cparams(),
        name="norm1",
    )(xp, xe, w)


def _proj_kernel(x_ref, wt_hbm, o_ref, wraw, wbf, sem, *, tiles, silu_from):
    j = pl.program_id(0)
    m = pl.program_id(1)
    tn = wbf.shape[0]

    def copies(jj):
        return [pltpu.make_async_copy(wt_hbm.at[pl.ds(src, n), :], wraw.at[pl.ds(dst, n), :], sem)
                for (src, n, dst) in tiles[jj]]

    @pl.when(jnp.logical_and(j == 0, m == 0))
    def _():
        for c in copies(0):
            c.start()

    for jj in range(len(tiles)):
        @pl.when(jnp.logical_and(j == jj, m == 0))
        def _(jj=jj):
            for c in copies(jj):
                c.wait()
            edge = 0
            for lo, hi in sorted((dst, dst + n) for (_, n, dst) in tiles[jj]) + [(tn, tn)]:
                if lo > edge:
                    wbf[edge:lo, :] = jnp.zeros((lo - edge, wbf.shape[1]), BF16)
                if hi > lo:
                    wbf[lo:hi, :] = wraw[lo:hi, :].astype(BF16)
                edge = hi
            if jj + 1 < len(tiles):
                for c in copies(jj + 1):
                    c.start()

    z = lax.dot_general(x_ref[...], wbf[...], (((1,), (1,)), ((), ())), preferred_element_type=F32)
    if silu_from is None:
        o_ref[...] = z.astype(o_ref.dtype)
    else:
        s = jax.nn.sigmoid(z)
        o_ref[...] = jnp.where(j >= silu_from, z * s, s).astype(o_ref.dtype)


def _proj(x, w_t, tiles, tn, out_dtype, silu_from=None, name="proj"):
    t, k = x.shape
    return pl.pallas_call(
        functools.partial(_proj_kernel, tiles=tiles, silu_from=silu_from),
        grid=(len(tiles), t // TM_MM),
        in_specs=[pl.BlockSpec((TM_MM, k), lambda j, m: (m, 0)),
                  pl.BlockSpec(memory_space=pl.ANY)],
        out_specs=pl.BlockSpec((TM_MM, tn), lambda j, m: (m, j)),
        out_shape=jax.ShapeDtypeStruct((t, tn * len(tiles)), out_dtype),
        scratch_shapes=[pltpu.VMEM((tn, k), F32), pltpu.VMEM((tn, k), BF16), pltpu.SemaphoreType.DMA(())],
        compiler_params=_cparams(2),
        name=name,
    )(x, w_t)


_RWKV_TILES = (((0, 1024, 0), (1088, 640, 1024)),
               ((1728, 1408, 0), (1024, 64, 1408), (3136, 192, 1472)))
_GLA_TILES = (((3328, 1152, 0),), ((4480, 912, 0),))
_GATE_TILES = tuple(((6416 + 1024 * j, 1024, 0),) for j in range(4)) + (((5392, 1024, 0),),)


def _rwkv_pre(z, w0, w2a, a0, g2, k_k, k_a):
    r = z[:, 0:1024]
    k = z[:, 1024:2048]
    v = z[:, 2048:3072]
    wa = z[:, 3072:3200]
    gd = z[:, 3200:3328]
    lane = _iota(wa.shape, 1)
    wa = jnp.where(lane < 64, jnp.tanh(wa), wa)
    up = _bdot(wa, w2a)
    lw = -math.exp(-0.5) * jax.nn.sigmoid(w0 + up[:, :1024])
    a = jax.nn.sigmoid(a0 + up[:, 1024:])
    g = _bdot(jax.nn.sigmoid(gd), g2)
    kk = k * k_k
    kk = kk * lax.rsqrt(jnp.maximum(_seg_sum(kk * kk, RW_HEAD), 1e-24))
    k_h = k * (1.0 + (a - 1.0) * k_a)
    return r, lw, k_h, v, kk, kk * a, g


def _rwkv_post(y, r, k_h, v, g, r_k, lnx_w, lnx_b):
    mean = _seg_sum(y, RW_HEAD) * (1.0 / RW_HEAD)
    d = y - mean
    var = _seg_sum(d * d, RW_HEAD) * (1.0 / RW_HEAD)
    yn = d * lax.rsqrt(var + RW_GN_EPS) * lnx_w + lnx_b
    bonus = _seg_sum(r * k_h * r_k, RW_HEAD) * v
    return (yn + bonus) * g


def _stack2(x):
    lane = _iota(x.shape, 1)
    return jnp.concatenate([jnp.where(lane < RW_HEAD, x, 0.0), jnp.where(lane >= RW_HEAD, x, 0.0)], axis=0)


def _unstack2(x):
    c = x.shape[0] // 2
    return x[:c] + x[c:]


def _rwkv_chunk_kernel(zs_ref, mu_ref, w0_ref, w2a_ref, a0_ref, g2_ref, kk_ref, ka_ref, rk_ref, lw_ref, lb_ref,
                       o_ref, hout_ref, prev_sc, h_sc, hmeta_sc, prevmeta_sc, y_sc, *, n_chunks):
    i = pl.program_id(0)
    is_meta = i == 0
    c = lax.rem(jnp.maximum(i - 1, 0), n_chunks)
    first = jnp.logical_and(i >= 1, c == 0)

    @pl.when(is_meta)
    def _():
        h_sc[...] = jnp.zeros_like(h_sc)
        prev_sc[...] = jnp.zeros_like(prev_sc)

    @pl.when(first)
    def _():
        h_sc[...] = hmeta_sc[...]
        prev_sc[...] = prevmeta_sc[...]

    zs = zs_ref[...]
    rowi = _iota((CHUNK, 1), 0)
    sh = pltpu.roll(zs, 1, 0)
    sh = jnp.where(rowi == 0, prev_sc[...], sh)
    prev_sc[...] = zs[CHUNK - 1:CHUNK, :]
    z = zs + (sh - zs) * mu_ref[...]
    r, lw, k_h, v, kk, b, g = _rwkv_pre(z, w0_ref[...], w2a_ref[...], a0_ref[...], g2_ref[...], kk_ref[...],
                                        ka_ref[...])
    lw = jnp.where(jnp.logical_and(is_meta, rowi < META_PAD), 0.0, lw)

    cl = _split_dot(_tri_incl(CHUNK), lw)
    cl_end = cl[CHUNK - 1:CHUNK, :]
    e_neg = jnp.exp(-cl)
    e_end = jnp.exp(cl_end - cl)
    kkt = kk * jnp.exp(cl - lw)
    rt = r * jnp.exp(cl)
    bt = b * e_neg
    kt = k_h * e_neg
    bh = b * e_end
    kh = k_h * e_end
    e_c = jnp.exp(cl_end)

    n2 = 2 * CHUNK
    tok_r = jnp.bitwise_and(_iota((n2, n2), 0), CHUNK - 1)
    tok_c = jnp.bitwise_and(_iota((n2, n2), 1), CHUNK - 1)
    strict = tok_r > tok_c
    incl = tok_r >= tok_c

    pairs = range(RW_HEADS // 2)
    sls = [slice(j * LANES, (j + 1) * LANES) for j in pairs]
    s_kkt = [_stack2(kkt[:, sl]) for sl in sls]
    s_rt = [_stack2(rt[:, sl]) for sl in sls]
    s_v = [_stack2(v[:, sl]).astype(BF16) for sl in sls]
    rb = [jnp.concatenate([_stack2(bt[:, sl]), _stack2(kt[:, sl])], axis=0).astype(BF16) for sl in sls]
    aa = [_bdot_nt(s_kkt[j], rb[j]) for j in pairs]
    mm = [_bdot_nt(s_rt[j], rb[j]) for j in pairs]
    h_kv = [_bdot_tn(_stack2(kh[:, sls[j]]), s_v[j]) for j in pairs]
    a_b = [jnp.where(strict, aa[j][:, :n2], 0.0).astype(BF16) for j in pairs]
    a_k = [jnp.where(strict, aa[j][:, n2:], 0.0) for j in pairs]
    m_rb = [jnp.where(incl, mm[j][:, :n2], 0.0).astype(BF16) for j in pairs]
    m_rk = [jnp.where(incl, mm[j][:, n2:], 0.0) for j in pairs]
    akv = [_bdot(a_k[j], s_v[j]) for j in pairs]
    y_kv = [_bdot(m_rk[j], s_v[j]) for j in pairs]
    x = [jnp.concatenate([s_kkt[j], akv[j]], axis=1) for j in pairs]
    p = a_b
    x = [x[j] - _bdot(p[j], x[j]) for j in pairs]
    for _ in range(5):
        p = [_bdot(p[j], p[j]).astype(BF16) for j in pairs]
        x = [x[j] + _bdot(p[j], x[j]) for j in pairs]
    zq = [_bdot(m_rb[j], x[j]) for j in pairs]
    zb = [_bdot_tn(_stack2(bh[:, sls[j]]), x[j]) for j in pairs]
    h0 = [h_sc[j] for j in pairs]
    for j in pairs:
        q_eff = _unstack2(s_rt[j] - zq[j][:, :LANES])
        y_in = _unstack2(y_kv[j] - zq[j][:, LANES:])
        y_sc[:, sls[j]] = _bdot(q_eff, h0[j]) + y_in
    h_new = [_col_of_row(e_c[:, sls[j]]) * h0[j] - _bdot(zb[j][:, :LANES], h0[j]) + (h_kv[j] - zb[j][:, LANES:])
             for j in pairs]
    for j in pairs:
        h_sc[j] = h_new[j]

    @pl.when(jnp.logical_and(i >= 1, c == n_chunks - 1))
    def _():
        for j in pairs:
            hout_ref[0, :, sls[j]] = _unstack2(h_new[j])

    o = _rwkv_post(y_sc[...], r, k_h, v, g, rk_ref[...], lw_ref[...], lb_ref[...])
    o_ref[...] = o.astype(o_ref.dtype)

    @pl.when(is_meta)
    def _():
        hmeta_sc[...] = h_sc[...]
        prevmeta_sc[...] = prev_sc[...]


def _rwkv_prompt(zr, p, n_batch=N_BATCH, n_chunks=N_CHUNKS, meta_block=ROW_META // CHUNK):
    row = lambda n: pl.BlockSpec((1, n), lambda i: (0, 0))
    full = lambda a, b: pl.BlockSpec((a, b), lambda i: (0, 0))
    blk = lambda i: jnp.where(i == 0, meta_block, i - 1)
    return pl.pallas_call(
        functools.partial(_rwkv_chunk_kernel, n_chunks=n_chunks),
        grid=(1 + n_batch * n_chunks,),
        in_specs=[pl.BlockSpec((CHUNK, W_SHIFT), lambda i: (blk(i), 0)),
                  row(W_SHIFT), row(RW_WIDTH), full(LANES, 2 * RW_WIDTH), row(RW_WIDTH), full(LANES, RW_WIDTH),
                  row(RW_WIDTH), row(RW_WIDTH), row(RW_WIDTH), row(RW_WIDTH), row(RW_WIDTH)],
        out_specs=[pl.BlockSpec((CHUNK, RW_WIDTH), lambda i: (jnp.maximum(i - 1, 0), 0)),
                   pl.BlockSpec((1, RW_HEAD, RW_WIDTH), lambda i: (jnp.maximum(i - 1, 0) // n_chunks, 0, 0))],
        out_shape=[jax.ShapeDtypeStruct((n_batch * n_chunks * CHUNK, RW_WIDTH), BF16),
                   jax.ShapeDtypeStruct((n_batch, RW_HEAD, RW_WIDTH), F32)],
        scratch_shapes=[pltpu.VMEM((1, W_SHIFT), F32),
                        pltpu.VMEM((RW_HEADS // 2, LANES, LANES), F32),
                        pltpu.VMEM((RW_HEADS // 2, LANES, LANES), F32),
                        pltpu.VMEM((1, W_SHIFT), F32),
                        pltpu.VMEM((CHUNK, RW_WIDTH), F32)],
        compiler_params=_cparams(),
        name="rwkv_prompt",
    )(zr, p['mu'], p['w0'], p['w2a'], p['a0'], p['g2'], p['k_k'], p['k_a'], p['r_k'], p['lnx_w'], p['lnx_b'])


def _rwkv_sample_pre_kernel(zs_ref, prev_ref, mu_ref, w0_ref, w2a_ref, a0_ref, g2_ref, kk_ref, ka_ref,
                            r_ref, kh_ref, v_ref, g_ref, rt_ref, wt_ref, kht_ref, vt_ref, nkkt_ref, bt_ref):
    zs = zs_ref[...]
    z = zs + (prev_ref[...] - zs) * mu_ref[...]
    r, lw, k_h, v, kk, b, g = _rwkv_pre(z, w0_ref[...], w2a_ref[...], a0_ref[...], g2_ref[...], kk_ref[...],
                                        ka_ref[...])
    r_ref[...] = r
    kh_ref[...] = k_h
    v_ref[...] = v
    g_ref[...] = g
    rt_ref[...] = r.T
    wt_ref[...] = jnp.exp(lw).T
    kht_ref[...] = k_h.T
    vt_ref[...] = v.T
    nkkt_ref[...] = (-kk).T
    bt_ref[...] = b.T


def _rwkv_sample_pre(zr, prev, p):
    row = lambda n: pl.BlockSpec((1, n), lambda i: (0, 0))
    full = lambda a, b: pl.BlockSpec((a, b), lambda i: (0, 0))
    vec = jax.ShapeDtypeStruct((N_SAMPLE, RW_WIDTH), F32)
    vec_t = jax.ShapeDtypeStruct((RW_WIDTH, N_SAMPLE), F32)
    return pl.pallas_call(
        _rwkv_sample_pre_kernel,
        grid=(1,),
        in_specs=[pl.BlockSpec((N_SAMPLE, W_SHIFT), lambda i: (ROW_SAMPLE // N_SAMPLE, 0)),
                  full(N_SAMPLE, W_SHIFT),
                  row(W_SHIFT), row(RW_WIDTH), full(LANES, 2 * RW_WIDTH), row(RW_WIDTH), full(LANES, RW_WIDTH),
                  row(RW_WIDTH), row(RW_WIDTH)],
        out_specs=[full(N_SAMPLE, RW_WIDTH)] * 4 + [full(RW_WIDTH, N_SAMPLE)] * 6,
        out_shape=[vec] * 4 + [vec_t] * 6,
        compiler_params=_cparams(),
        name="rwkv_sample_pre",
    )(zr, prev, p['mu'], p['w0'], p['w2a'], p['a0'], p['g2'], p['k_k'], p['k_a'])


def _rwkv_sample_step_kernel(s_ref, r_ref, w_ref, kh_ref, v_ref, nkk_ref, b_ref, so_ref, y_ref):
    s = s_ref[0]
    sa = jnp.sum(s * nkk_ref[0][None], axis=1, keepdims=True)
    s_new = s * w_ref[0][None] + sa * b_ref[0][None] + v_ref[0] * kh_ref[0][None]
    so_ref[0] = s_new
    y_ref[0] = jnp.sum(s_new * r_ref[0][None], axis=1, keepdims=True)


def _rwkv_sample_step(state_t, r_t, w_t, kh_t, v_t, nkk_t, b_t):
    kvec = lambda t: t.reshape(RW_HEADS, RW_HEAD, N_SAMPLE)
    kspec = pl.BlockSpec((1, RW_HEAD, N_SAMPLE), lambda i: (i, 0, 0))
    vspec = pl.BlockSpec((1, RW_HEAD, 1, N_SAMPLE), lambda i: (i, 0, 0, 0))
    sspec = pl.BlockSpec((1, RW_HEAD, RW_HEAD, N_SAMPLE), lambda i: (i, 0, 0, 0))
    s_new, y = pl.pallas_call(
        _rwkv_sample_step_kernel,
        grid=(RW_HEADS,),
        in_specs=[sspec, kspec, kspec, kspec, vspec, kspec, kspec],
        out_specs=[sspec, vspec],
        out_shape=[jax.ShapeDtypeStruct(state_t.shape, F32),
                   jax.ShapeDtypeStruct((RW_HEADS, RW_HEAD, 1, N_SAMPLE), F32)],
        compiler_params=_cparams(),
        name="rwkv_sample_step",
    )(state_t, kvec(r_t), kvec(w_t), kvec(kh_t), v_t.reshape(RW_HEADS, RW_HEAD, 1, N_SAMPLE), kvec(nkk_t), kvec(b_t))
    return s_new, y.reshape(RW_WIDTH, N_SAMPLE)


def _rwkv_sample_post_kernel(yt_ref, r_ref, kh_ref, v_ref, g_ref, rk_ref, lw_ref, lb_ref, o_ref):
    o = _rwkv_post(yt_ref[...].T, r_ref[...], kh_ref[...], v_ref[...], g_ref[...], rk_ref[...], lw_ref[...],
                   lb_ref[...])
    o_ref[0:N_SAMPLE, :] = o.astype(o_ref.dtype)
    o_ref[N_SAMPLE:T_EXT, :] = jnp.zeros((T_EXT - N_SAMPLE, RW_WIDTH), o_ref.dtype)


def _rwkv_sample_post(y_t, r, kh, v, g, p):
    row = lambda n: pl.BlockSpec((1, n), lambda i: (0, 0))
    full = lambda a, b: pl.BlockSpec((a, b), lambda i: (0, 0))
    return pl.pallas_call(
        _rwkv_sample_post_kernel,
        grid=(1,),
        in_specs=[full(RW_WIDTH, N_SAMPLE)] + [full(N_SAMPLE, RW_WIDTH)] * 4 + [row(RW_WIDTH)] * 3,
        out_specs=full(T_EXT, RW_WIDTH),
        out_shape=jax.ShapeDtypeStruct((T_EXT, RW_WIDTH), BF16),
        compiler_params=_cparams(),
        name="rwkv_sample_post",
    )(y_t, r, kh, v, g, p['r_k'], p['lnx_w'], p['lnx_b'])


def _gla_logg(zg, gk_up, gk_b):
    gkd = zg[:, 2 * GLA_KDIM + GLA_WIDTH:2 * GLA_KDIM + GLA_WIDTH + LANES]
    x = _bdot(gkd, gk_up) + gk_b
    return (jnp.minimum(x, 0.0) - jnp.log(1.0 + jnp.exp(-jnp.abs(x)))) * (1.0 / GLA_NORMALIZER)


def _gla_post(o, og_act, norm_w):
    outs = []
    for h in range(GLA_HEADS):
        oh = o[:, h * GLA_DV:(h + 1) * GLA_DV]
        ms = jnp.mean(oh * oh, axis=-1, keepdims=True)
        outs.append(oh * lax.rsqrt(ms + RMS_EPS) * norm_w)
    return jnp.concatenate(outs, axis=1) * og_act


def _gla_chunk_kernel(zg_ref, og_ref, gkup_ref, gkb_ref, nw_ref, o_ref, sout_ref, s_sc, smeta_sc, *, n_chunks):
    i = pl.program_id(0)
    is_meta = i == 0
    c = lax.rem(jnp.maximum(i - 1, 0), n_chunks)
    first = jnp.logical_and(i >= 1, c == 0)

    @pl.when(is_meta)
    def _():
        s_sc[...] = jnp.zeros_like(s_sc)

    @pl.when(first)
    def _():
        s_sc[...] = smeta_sc[...]

    zg = zg_ref[...]
    rowi = _iota((CHUNK, 1), 0)
    logg = _gla_logg(zg, gkup_ref[...], gkb_ref[...])
    logg = jnp.where(jnp.logical_and(is_meta, rowi < META_PAD), 0.0, logg)
    bcum = _split_dot(_tri_incl(CHUNK), logg)
    b_end = bcum[CHUNK - 1:CHUNK, :]
    qt = zg[:, 0:GLA_KDIM] * (GLA_DK ** -0.5) * jnp.exp(bcum)
    kt = zg[:, GLA_KDIM:2 * GLA_KDIM] * jnp.exp(-bcum)
    ke = zg[:, GLA_KDIM:2 * GLA_KDIM] * jnp.exp(b_end - bcum)
    e_end = jnp.exp(b_end)
    causal = _iota((CHUNK, CHUNK), 0) >= _iota((CHUNK, CHUNK), 1)
    heads = range(GLA_HEADS)
    ks = [slice(h * GLA_DK, (h + 1) * GLA_DK) for h in heads]
    vh = [zg[:, 2 * GLA_KDIM + h * GLA_DV:2 * GLA_KDIM + (h + 1) * GLA_DV].astype(BF16) for h in heads]
    a = [jnp.where(causal, _bdot_nt(qt[:, ks[h]], kt[:, ks[h]]), 0.0) for h in heads]
    s0 = [s_sc[h] for h in heads]
    o_inter = [_bdot(qt[:, ks[h]], s0[h]) for h in heads]
    s_add = [_bdot_tn(ke[:, ks[h]], vh[h]) for h in heads]
    outs = [_bdot(a[h], vh[h]) + o_inter[h] for h in heads]
    s_new = [_col_of_row(e_end[:, ks[h]]) * s0[h] + s_add[h] for h in heads]
    for h in heads:
        s_sc[h] = s_new[h]

    @pl.when(jnp.logical_and(i >= 1, c == n_chunks - 1))
    def _():
        for h in heads:
            sout_ref[0, h] = s_new[h]

    o = _gla_post(jnp.concatenate(outs, axis=1), og_ref[...].astype(F32), nw_ref[...])
    o_ref[...] = o.astype(o_ref.dtype)

    @pl.when(is_meta)
    def _():
        smeta_sc[...] = s_sc[...]


def _gla_prompt(zg, zb, p, n_batch=N_BATCH, n_chunks=N_CHUNKS, meta_block=ROW_META // CHUNK, og_block=OG_BLOCK):
    full = lambda a, b: pl.BlockSpec((a, b), lambda i: (0, 0))
    blk = lambda i: jnp.where(i == 0, meta_block, i - 1)
    return pl.pallas_call(
        functools.partial(_gla_chunk_kernel, n_chunks=n_chunks),
        grid=(1 + n_batch * n_chunks,),
        in_specs=[pl.BlockSpec((CHUNK, W_GLA_PAD), lambda i: (blk(i), 0)),
                  pl.BlockSpec((CHUNK, GLA_WIDTH), lambda i: (blk(i), og_block)),
                  full(LANES, GLA_KDIM), full(1, GLA_KDIM), full(1, GLA_DV)],
        out_specs=[pl.BlockSpec((CHUNK, GLA_WIDTH), lambda i: (jnp.maximum(i - 1, 0), 0)),
                   pl.BlockSpec((1, GLA_HEADS, GLA_DK, GLA_DV),
                                lambda i: (jnp.maximum(i - 1, 0) // n_chunks, 0, 0, 0))],
        out_shape=[jax.ShapeDtypeStruct((n_batch * n_chunks * CHUNK, GLA_WIDTH), BF16),
                   jax.ShapeDtypeStruct((n_batch, GLA_HEADS, GLA_DK, GLA_DV), F32)],
        scratch_shapes=[pltpu.VMEM((GLA_HEADS, GLA_DK, GLA_DV), F32),
                        pltpu.VMEM((GLA_HEADS, GLA_DK, GLA_DV), F32)],
        compiler_params=_cparams(),
        name="gla_prompt",
    )(zg, zb, p['gk_up'], p['gk_b'], p['norm_w'])


def _gla_sample_pre_kernel(zg_ref, gkup_ref, gkb_ref, q_ref, k_ref, eg_ref):
    zg = zg_ref[...]
    logg = _gla_logg(zg, gkup_ref[...], gkb_ref[...])
    q_ref[...] = zg[:, 0:GLA_KDIM] * (GLA_DK ** -0.5)
    k_ref[...] = zg[:, GLA_KDIM:2 * GLA_KDIM]
    eg_ref[...] = jnp.exp(logg)


def _gla_sample_pre(zg, p):
    full = lambda a, b: pl.BlockSpec((a, b), lambda i: (0, 0))
    vec = jax.ShapeDtypeStruct((N_SAMPLE, GLA_KDIM), F32)
    return pl.pallas_call(
        _gla_sample_pre_kernel,
        grid=(1,),
        in_specs=[pl.BlockSpec((N_SAMPLE, W_GLA_PAD), lambda i: (ROW_SAMPLE // N_SAMPLE, 0)),
                  full(LANES, GLA_KDIM), full(1, GLA_KDIM)],
        out_specs=[full(N_SAMPLE, GLA_KDIM)] * 3,
        out_shape=[vec] * 3,
        compiler_params=_cparams(),
        name="gla_sample_pre",
    )(zg, p['gk_up'], p['gk_b'])


def _gla_sample_step_kernel(s_ref, q_ref, k_ref, eg_ref, v_ref, so_ref, o_ref):
    s = s_ref[0]
    eye = _iota((GLA_DK, GLA_DK), 0) == _iota((GLA_DK, GLA_DK), 1)
    col = lambda ref: jnp.sum(jnp.where(eye, ref[0], 0.0), axis=-1, keepdims=True)
    s_new = col(eg_ref) * s + col(k_ref) * v_ref[0]
    so_ref[0] = s_new
    o_ref[0] = jnp.sum(col(q_ref) * s_new, axis=-2, keepdims=True)


def _gla_sample_step(state, q, k, eg, v):
    kv = lambda t: t.reshape(N_SAMPLE, GLA_HEADS, 1, GLA_DK)
    kspec = pl.BlockSpec((1, GLA_HEADS, 1, GLA_DK), lambda i: (i, 0, 0, 0))
    vspec = pl.BlockSpec((1, GLA_HEADS, 1, GLA_DV), lambda i: (i, 0, 0, 0))
    sspec = pl.BlockSpec((1, GLA_HEADS, GLA_DK, GLA_DV), lambda i: (i, 0, 0, 0))
    s_new, o = pl.pallas_call(
        _gla_sample_step_kernel,
        grid=(N_SAMPLE,),
        in_specs=[sspec, kspec, kspec, kspec, vspec],
        out_specs=[sspec, vspec],
        out_shape=[jax.ShapeDtypeStruct(state.shape, F32),
                   jax.ShapeDtypeStruct((N_SAMPLE, GLA_HEADS, 1, GLA_DV), F32)],
        compiler_params=_cparams(),
        name="gla_sample_step",
    )(state, kv(q), kv(k), kv(eg), v.reshape(N_SAMPLE, GLA_HEADS, 1, GLA_DV))
    return s_new, o.reshape(N_SAMPLE, GLA_WIDTH)


def _gla_sample_post_kernel(o_ref, og_ref, nw_ref, out_ref):
    o = _gla_post(o_ref[...], og_ref[0:N_SAMPLE, :].astype(F32), nw_ref[...])
    out_ref[0:N_SAMPLE, :] = o.astype(out_ref.dtype)
    out_ref[N_SAMPLE:T_EXT, :] = jnp.zeros((T_EXT - N_SAMPLE, GLA_WIDTH), out_ref.dtype)


def _gla_sample_post(o, zb, p, og_block=OG_BLOCK):
    full = lambda a, b: pl.BlockSpec((a, b), lambda i: (0, 0))
    return pl.pallas_call(
        _gla_sample_post_kernel,
        grid=(1,),
        in_specs=[full(N_SAMPLE, GLA_WIDTH),
                  pl.BlockSpec((T_EXT, GLA_WIDTH), lambda i: (T_MAIN // T_EXT, og_block)),
                  full(1, GLA_DV)],
        out_specs=full(T_EXT, GLA_WIDTH),
        out_shape=jax.ShapeDtypeStruct((T_EXT, GLA_WIDTH), BF16),
        compiler_params=_cparams(),
        name="gla_sample_post",
    )(o, zb, p['norm_w'])


def _mix_route_kernel(or_ref, ore_ref, og_ref, oge_ref, gates_ref, xp_ref, xe_ref, pr_ref, pg_ref, wo_ref, n2_ref,
                      wrh_ref, wrl_ref, br_ref, h1_ref, xn_ref, info_ref, cnt_ref, carry_sc):
    i = pl.program_id(0)
    is_main = i < T_MAIN // TM_DENSE

    @pl.when(i == 0)
    def _():
        carry_sc[...] = jnp.zeros_like(carry_sc)

    sig_r = gates_ref[:, 0:D_MODEL].astype(F32)
    sig_g = gates_ref[:, D_MODEL:2 * D_MODEL].astype(F32)
    o_r = jnp.where(is_main, or_ref[...], ore_ref[...])
    o_g = jnp.where(is_main, og_ref[...], oge_ref[...])
    m = (sig_r * jnp.dot(o_r, pr_ref[...], preferred_element_type=F32)
         + sig_g * jnp.dot(o_g, pg_ref[...], preferred_element_type=F32))
    h = jnp.where(is_main, xp_ref[...], xe_ref[...])
    h1 = h + jnp.dot(m.astype(BF16), wo_ref[...], preferred_element_type=F32)
    h1_ref[...] = h1
    ms = jnp.mean(h1 * h1, axis=-1, keepdims=True)
    xn = h1 * lax.rsqrt(ms + RMS_EPS) * n2_ref[...]
    xn_ref[...] = xn
    xh = xn.astype(BF16)
    xl = (xn - xh.astype(F32)).astype(BF16)
    lg = (jnp.dot(xh, wrh_ref[...], preferred_element_type=F32)
          + jnp.dot(xh, wrl_ref[...], preferred_element_type=F32)
          + jnp.dot(xl, wrh_ref[...], preferred_element_type=F32)) + br_ref[...]

    lane = _iota(lg.shape, 1)
    lanef = lane.astype(F32)
    is_g = jnp.logical_and(lane >= N_EXPERTS, lane < N_EXPERTS + N_GROUPS)
    gl = jnp.where(is_g, lg, NEG_BIG)
    gmax = jnp.max(gl, axis=-1, keepdims=True)
    gsel = jnp.min(jnp.where(jnp.logical_and(is_g, gl == gmax), lanef, 1e9), axis=-1, keepdims=True) - N_EXPERTS
    p_g = 1.0 / jnp.sum(jnp.exp(gl - gmax), axis=-1, keepdims=True)
    grp = (lane // EXPERTS_PER_GROUP).astype(F32)
    in_grp = jnp.logical_and(lane < N_EXPERTS, grp == gsel)
    el = jnp.where(in_grp, lg, NEG_BIG)
    m1 = jnp.max(el, axis=-1, keepdims=True)
    i1 = jnp.min(jnp.where(jnp.logical_and(in_grp, el == m1), lanef, 1e9), axis=-1, keepdims=True)
    in2 = jnp.logical_and(in_grp, lanef != i1)
    el2 = jnp.where(in2, lg, NEG_BIG)
    m2 = jnp.max(el2, axis=-1, keepdims=True)
    i2 = jnp.min(jnp.where(jnp.logical_and(in2, el2 == m2), lanef, 1e9), axis=-1, keepdims=True)
    e2 = jnp.exp(m2 - m1)
    w1 = p_g / (1.0 + e2)
    w2 = p_g * e2 / (1.0 + e2)

    oh1 = lanef == i1
    oh2 = lanef == i2
    cnt = jnp.where(jnp.logical_or(oh1, oh2), 1.0, 0.0)
    tm = cnt.shape[0]
    lstrict = (_iota((tm, tm), 0) > _iota((tm, tm), 1)).astype(BF16)
    before = jnp.dot(lstrict, cnt.astype(BF16), preferred_element_type=F32) + carry_sc[...]
    rank1 = jnp.sum(jnp.where(oh1, before, 0.0), axis=-1, keepdims=True)
    rank2 = jnp.sum(jnp.where(oh2, before, 0.0), axis=-1, keepdims=True)
    carry_sc[...] = carry_sc[...] + jnp.sum(cnt, axis=0, keepdims=True)
    cnt_ref[...] = carry_sc[...]
    info = jnp.where(lane == 0, i1, jnp.where(lane == 1, i2, jnp.where(lane == 2, w1, jnp.where(
        lane == 3, w2, jnp.where(lane == 4, rank1, jnp.where(lane == 5, rank2, 0.0))))))
    info_ref[...] = info


def _mix_route(o_r, o_r_ext, o_g, o_g_ext, zb, xp, xe, p):
    nmain = T_MAIN // TM_DENSE
    tile = lambda n: pl.BlockSpec((TM_DENSE, n), lambda i: (i, 0))
    main = lambda n: pl.BlockSpec((TM_DENSE, n), lambda i: (jnp.minimum(i, nmain - 1), 0))
    ext = lambda n: pl.BlockSpec((T_EXT, n), lambda i: (0, 0))
    const = lambda a, b: pl.BlockSpec((a, b), lambda i: (0, 0), pipeline_mode=pl.Buffered(1))
    return pl.pallas_call(
        _mix_route_kernel,
        grid=(T_ALL // TM_DENSE,),
        in_specs=[main(RW_WIDTH), ext(RW_WIDTH), main(GLA_WIDTH), ext(GLA_WIDTH), tile(2 * D_MODEL),
                  main(D_MODEL), ext(D_MODEL),
                  const(RW_WIDTH, D_MODEL), const(GLA_WIDTH, D_MODEL), const(D_MODEL, D_MODEL),
                  const(1, D_MODEL), const(D_MODEL, LANES), const(D_MODEL, LANES), const(1, LANES)],
        out_specs=[tile(D_MODEL), tile(D_MODEL), tile(LANES), pl.BlockSpec((1, LANES), lambda i: (0, 0))],
        out_shape=[jax.ShapeDtypeStruct((T_ALL, D_MODEL), F32),
                   jax.ShapeDtypeStruct((T_ALL, D_MODEL), F32),
                   jax.ShapeDtypeStruct((T_ALL, LANES), F32),
                   jax.ShapeDtypeStruct((1, LANES), F32)],
        scratch_shapes=[pltpu.VMEM((1, LANES), F32)],
        compiler_params=_cparams(),
        name="mix_route",
    )(o_r, o_r_ext, o_g, o_g_ext, zb, xp, xe, p['p_rwkv'], p['p_gla'], p['w_out'], p['norm2_w'], p['wr_hi'],
      p['wr_lo'], p['b_route'])


def _dispatch_kernel(info_ref, cnt_ref, xq_ref, xs_hbm, d_ref, te_ref, nt_ref,
                     d_vm, d_sm, tab_vm, tab_sm, zbuf, sem_t, sem_x):
    i = pl.program_id(0)
    lane1 = _iota((1, LANES), 1)
    cnt = jnp.where(lane1 < N_EXPERTS, cnt_ref[...], 0.0)
    tiles = jnp.floor((cnt + (TM_MOE - 1)) * (1.0 / TM_MOE))
    upper = (_iota((LANES, LANES), 0) < _iota((LANES, LANES), 1)).astype(BF16)
    tile_start = jnp.dot(jnp.broadcast_to(tiles, (8, LANES)).astype(BF16), upper,
                         preferred_element_type=F32)[0:1, :]
    base = tile_start * TM_MOE

    info = info_ref[...]
    lanef = _iota(info.shape, 1).astype(F32)
    pick = lambda col: jnp.sum(jnp.where(lanef == info[:, col:col + 1], base, 0.0), axis=-1, keepdims=True)
    d1 = pick(0) + info[:, 4:5]
    d2 = pick(1) + info[:, 5:6]
    tm = info.shape[0]
    eye = _iota((tm, tm), 0) == _iota((tm, tm), 1)
    to_row = lambda col: jnp.sum(jnp.where(eye, col, 0.0), axis=0, keepdims=True)
    d_row = jnp.concatenate([to_row(d1), to_row(d2)], axis=1).astype(jnp.int32)
    d_ref[0] = d_row
    d_vm[...] = d_row

    @pl.when(i == 0)
    def _():
        tile_end = tile_start + tiles
        n_tiles = jnp.sum(tiles, axis=-1, keepdims=True)
        eye_l = _iota((LANES, LANES), 0) == _iota((LANES, LANES), 1)
        end_col = jnp.sum(jnp.where(eye_l, tile_end, 0.0), axis=-1, keepdims=True)
        rowl = _iota((LANES, LANES), 0)
        tile_f = _iota((LANES, LANES), 1).astype(F32)
        te = jnp.sum(jnp.where(jnp.logical_and(rowl < N_EXPERTS, end_col <= tile_f), 1.0, 0.0), axis=0,
                     keepdims=True)
        last_e = jnp.max(jnp.where(tiles > 0.0, lane1.astype(F32), 0.0), axis=-1, keepdims=True)
        te = jnp.where(lane1.astype(F32) < n_tiles, jnp.minimum(te, N_EXPERTS - 1.0), last_e)
        te_ref[...] = te.astype(jnp.int32)
        nt_ref[...] = jnp.broadcast_to(n_tiles, (1, LANES)).astype(jnp.int32)
        zbuf[...] = jnp.zeros_like(zbuf)
        pad_lo = (base + cnt).astype(jnp.int32)
        pad_hi = (base + tiles * TM_MOE).astype(jnp.int32)
        nt_row = jnp.broadcast_to(n_tiles, (1, LANES)).astype(jnp.int32)
        tab_vm[...] = jnp.concatenate([pad_lo, pad_hi, nt_row, jnp.zeros((5, LANES), jnp.int32)], axis=0)
        tab_copy = pltpu.make_async_copy(tab_vm, tab_sm, sem_t)
        tab_copy.start()
        tab_copy.wait()

        def zero_row(r):
            return pltpu.make_async_copy(zbuf.at[pl.ds(0, 1), :], xs_hbm.at[pl.ds(r, 1), :], sem_x)

        def zero_tile(t):
            return pltpu.make_async_copy(zbuf, xs_hbm.at[pl.ds(t * TM_MOE, TM_MOE), :], sem_x)

        def each(fn):
            def body(r, carry):
                fn(r)
                return carry
            return body

        for e in range(N_EXPERTS):
            lax.fori_loop(tab_sm[0, e], tab_sm[1, e], each(lambda r: zero_row(r).start()), 0)
        lax.fori_loop(tab_sm[2, 0], NT_MOE, each(lambda t: zero_tile(t).start()), 0)
        for e in range(N_EXPERTS):
            lax.fori_loop(tab_sm[0, e], tab_sm[1, e], each(lambda r: zero_row(r).wait()), 0)
        lax.fori_loop(tab_sm[2, 0], NT_MOE, each(lambda t: zero_tile(t).wait()), 0)

    d_copy = pltpu.make_async_copy(d_vm, d_sm, sem_t)
    d_copy.start()
    d_copy.wait()

    for t in range(tm):
        for half in range(2):
            pltpu.make_async_copy(xq_ref.at[pl.ds(t, 1), :], xs_hbm.at[pl.ds(d_sm[0, half * tm + t], 1), :],
                                  sem_x).start()
    for half in range(2):
        pltpu.make_async_copy(xq_ref, xs_hbm.at[pl.ds(0, tm), :], sem_x).wait()


def _dispatch(info, counts, xq):
    nt = T_ALL // TM_DENSE
    return pl.pallas_call(
        _dispatch_kernel,
        grid=(nt,),
        in_specs=[pl.BlockSpec((TM_DENSE, LANES), lambda i: (i, 0)),
                  pl.BlockSpec((1, LANES), lambda i: (0, 0)),
                  pl.BlockSpec((TM_DENSE, D_MODEL), lambda i: (i, 0))],
        out_specs=[pl.BlockSpec(memory_space=pl.ANY),
                   pl.BlockSpec((1, 1, 2 * TM_DENSE), lambda i: (i, 0, 0)),
                   pl.BlockSpec((1, LANES), lambda i: (0, 0)),
                   pl.BlockSpec((1, LANES), lambda i: (0, 0))],
        out_shape=[jax.ShapeDtypeStruct((NT_MOE * TM_MOE, D_MODEL), F32),
                   jax.ShapeDtypeStruct((nt, 1, 2 * TM_DENSE), jnp.int32),
                   jax.ShapeDtypeStruct((1, LANES), jnp.int32),
                   jax.ShapeDtypeStruct((1, LANES), jnp.int32)],
        scratch_shapes=[pltpu.VMEM((1, 2 * TM_DENSE), jnp.int32),
                        pltpu.SMEM((1, 2 * TM_DENSE), jnp.int32),
                        pltpu.VMEM((8, LANES), jnp.int32),
                        pltpu.SMEM((8, LANES), jnp.int32),
                        pltpu.VMEM((TM_MOE, D_MODEL), F32),
                        pltpu.SemaphoreType.DMA(()),
                        pltpu.SemaphoreType.DMA(())],
        compiler_params=_cparams(),
        name="moe_dispatch",
    )(info, counts, xq)


def _moe_kernel(te_ref, nt_ref, xs_ref, w1_ref, w3_ref, w2_ref, o_ref, w1b, w3b, w2b):
    i = pl.program_id(0)

    @pl.when(i < nt_ref[0])
    def _():
        changed = jnp.logical_or(i == 0, te_ref[i] != te_ref[jnp.maximum(i - 1, 0)])

        @pl.when(changed)
        def _():
            w1b[...] = w1_ref[0].astype(BF16)
            w3b[...] = w3_ref[0].astype(BF16)
            w2b[...] = w2_ref[0].astype(BF16)

        xb = xs_ref[...].astype(BF16)
        h1 = jnp.dot(xb, w1b[...], preferred_element_type=F32)
        h3 = jnp.dot(xb, w3b[...], preferred_element_type=F32)
        hh = (h1 * jax.nn.sigmoid(h1) * h3).astype(BF16)
        o_ref[...] = jnp.dot(hh, w2b[...], preferred_element_type=F32)

    @pl.when(i >= nt_ref[0])
    def _():
        o_ref[...] = jnp.zeros_like(o_ref)


def _moe(tile_expert, n_tiles, xs, w1, w3, w2):
    grid_spec = pltpu.PrefetchScalarGridSpec(
        num_scalar_prefetch=2,
        grid=(NT_MOE,),
        in_specs=[pl.BlockSpec((TM_MOE, D_MODEL), lambda i, te, nt: (jnp.minimum(i, nt[0] - 1), 0)),
                  pl.BlockSpec((1, D_MODEL, D_EXPERT), lambda i, te, nt: (te[i], 0, 0)),
                  pl.BlockSpec((1, D_MODEL, D_EXPERT), lambda i, te, nt: (te[i], 0, 0)),
                  pl.BlockSpec((1, D_EXPERT, D_MODEL), lambda i, te, nt: (te[i], 0, 0))],
        out_specs=pl.BlockSpec((TM_MOE, D_MODEL), lambda i, te, nt: (i, 0)),
        scratch_shapes=[pltpu.VMEM((D_MODEL, D_EXPERT), BF16),
                        pltpu.VMEM((D_MODEL, D_EXPERT), BF16),
                        pltpu.VMEM((D_EXPERT, D_MODEL), BF16)],
    )
    return pl.pallas_call(
        _moe_kernel,
        grid_spec=grid_spec,
        out_shape=jax.ShapeDtypeStruct((NT_MOE * TM_MOE, D_MODEL), F32),
        compiler_params=_cparams(),
        name="moe_experts",
    )(tile_expert, n_tiles, xs, w1, w3, w2)


def _combine_kernel(dcur_ref, dnext_ref, h1_ref, info_ref, fw_ref, eo_hbm, yp_ref, ye_ref, buf, sem):
    i = pl.program_id(0)
    nmain = T_MAIN // TM_DENSE
    slot = lax.rem(i, 2)

    def row_copy(d_ref, s, t):
        return pltpu.make_async_copy(eo_hbm.at[pl.ds(d_ref[0, 0, t], 1), :], buf.at[s, pl.ds(t, 1), :], sem.at[s])

    def issue_all(d_ref, s):
        for t in range(2 * TM_DENSE):
            row_copy(d_ref, s, t).start()

    @pl.when(i == 0)
    def _():
        issue_all(dcur_ref, slot)

    @pl.when(i + 1 < pl.num_programs(0))
    def _():
        issue_all(dnext_ref, 1 - slot)

    pltpu.make_async_copy(eo_hbm.at[pl.ds(0, 2 * TM_DENSE), :], buf.at[slot], sem.at[slot]).wait()

    info = info_ref[...]
    y = h1_ref[...] + info[:, 2:3] * buf[slot, 0:TM_DENSE, :] + info[:, 3:4] * buf[slot, TM_DENSE:2 * TM_DENSE, :]
    ms = jnp.mean(y * y, axis=-1, keepdims=True)
    out = y * lax.rsqrt(ms + RMS_EPS) * fw_ref[...]

    @pl.when(i < nmain)
    def _():
        yp_ref[...] = out

    @pl.when(i == nmain)
    def _():
        ye_ref[...] = out


def _combine(dst, h1, info, fw, eo):
    nmain = T_MAIN // TM_DENSE
    return pl.pallas_call(
        _combine_kernel,
        grid=(T_ALL // TM_DENSE,),
        in_specs=[pl.BlockSpec((1, 1, 2 * TM_DENSE), lambda i: (i, 0, 0), memory_space=pltpu.SMEM),
                  pl.BlockSpec((1, 1, 2 * TM_DENSE), lambda i: (jnp.minimum(i + 1, nmain), 0, 0),
                               memory_space=pltpu.SMEM),
                  pl.BlockSpec((TM_DENSE, D_MODEL), lambda i: (i, 0)),
                  pl.BlockSpec((TM_DENSE, LANES), lambda i: (i, 0)),
                  pl.BlockSpec((1, D_MODEL), lambda i: (0, 0)),
                  pl.BlockSpec(memory_space=pl.ANY)],
        out_specs=[pl.BlockSpec((TM_DENSE, D_MODEL), lambda i: (jnp.minimum(i, nmain - 1), 0)),
                   pl.BlockSpec((T_EXT, D_MODEL), lambda i: (0, 0))],
        out_shape=[jax.ShapeDtypeStruct((T_MAIN, D_MODEL), F32),
                   jax.ShapeDtypeStruct((T_EXT, D_MODEL), F32)],
        scratch_shapes=[pltpu.VMEM((2, 2 * TM_DENSE, D_MODEL), F32), pltpu.SemaphoreType.DMA((2,))],
        compiler_params=_cparams(),
        name="moe_combine",
    )(dst, dst, h1, info, fw, eo)


def kernel(x_prompt, x_sample, state_rwkv, state_shift, state_gla, meta_tokens, norm1_w, w_in, mu_shift, rw_w0,
           rw_w2, rw_a0, rw_a2, rw_g2, rw_k_k, rw_k_a, rw_r_k, rw_lnx_w, rw_lnx_b, gla_gk_up, gla_gk_b, gla_norm_w,
           p_rwkv, p_gla, w_out, norm2_w, moe_w_group, moe_b_group, moe_w_router, moe_b_router, moe_w1, moe_w3,
           moe_w2, final_norm_w):
    w_t = w_in[0].T
    row = lambda t: t.reshape(1, -1)
    w2a = jnp.zeros((LANES, 2 * RW_WIDTH), F32)
    w2a = w2a.at[0:64, 0:RW_WIDTH].set(rw_w2[0]).at[64:128, RW_WIDTH:].set(rw_a2[0])
    rw = dict(mu=row(mu_shift[0][_SHIFT_PERM]), w0=row(rw_w0[0]), w2a=w2a, a0=row(rw_a0[0]), g2=rw_g2[0],
              k_k=row(rw_k_k[0]), k_a=row(rw_k_a[0]), r_k=row(rw_r_k[0]), lnx_w=row(rw_lnx_w[0]),
              lnx_b=row(rw_lnx_b[0]))
    gk_up = jnp.zeros((LANES, GLA_KDIM), F32).at[0:GLA_LORA].set(gla_gk_up[0])
    gl = dict(gk_up=gk_up, gk_b=row(gla_gk_b[0]), norm_w=row(gla_norm_w[0]))
    w_route = jnp.zeros((D_MODEL, LANES), F32)
    w_route = w_route.at[:, 0:N_EXPERTS].set(moe_w_router[0]).at[:, N_EXPERTS:N_EXPERTS + N_GROUPS].set(moe_w_group[0])
    wr_hi = w_route.astype(BF16)
    wr_lo = (w_route - wr_hi.astype(F32)).astype(BF16)
    b_route = jnp.zeros((1, LANES), F32)
    b_route = b_route.at[0, 0:N_EXPERTS].set(moe_b_router[0]).at[0, N_EXPERTS:N_EXPERTS + N_GROUPS].set(moe_b_group[0])
    mix = dict(p_rwkv=p_rwkv[0].astype(BF16), p_gla=p_gla[0].astype(BF16), w_out=w_out[0].astype(BF16),
               norm2_w=row(norm2_w[0]), wr_hi=wr_hi, wr_lo=wr_lo, b_route=b_route)

    xp = x_prompt.reshape(T_MAIN, D_MODEL)
    xe = jnp.concatenate([x_sample[:, 0, :], jnp.zeros((META_PAD, D_MODEL), F32), meta_tokens,
                          jnp.zeros((T_ALL - ROW_META - CHUNK, D_MODEL), F32)], axis=0)

    xn = _norm1(xp, xe, row(norm1_w[0]))
    zr = _proj(xn, w_t, _RWKV_TILES, 1664, F32, name="proj_rwkv")
    zg = _proj(xn, w_t, _GLA_TILES, 1152, F32, name="proj_gla")
    zb = _proj(xn, w_t, _GATE_TILES, 1024, BF16, silu_from=4, name="proj_gates")

    o_r, h_fin = _rwkv_prompt(zr, rw)
    o_g, s_fin = _gla_prompt(zg, zb, gl)

    r_s, kh_s, v_s, g_s, r_t, w_t, kh_t, v_t, nkk_t, b_t = _rwkv_sample_pre(zr, state_shift[0][:, _SHIFT_PERM], rw)
    rw_new_t, y_t = _rwkv_sample_step(state_rwkv[0].transpose(1, 2, 3, 0), r_t, w_t, kh_t, v_t, nkk_t, b_t)
    rw_new = rw_new_t.transpose(3, 0, 1, 2)
    o_r_ext = _rwkv_sample_post(y_t, r_s, kh_s, v_s, g_s, rw)
    q_s, k_s, eg_s = _gla_sample_pre(zg, gl)
    v_gs = zg[ROW_SAMPLE:ROW_SAMPLE + N_SAMPLE, 2 * GLA_KDIM:2 * GLA_KDIM + GLA_WIDTH]
    gla_new, og_s = _gla_sample_step(state_gla[0], q_s, k_s, eg_s, v_gs)
    o_g_ext = _gla_sample_post(og_s, zb, gl)

    h1, xn2, info, counts = _mix_route(o_r, o_r_ext, o_g, o_g_ext, zb, xp, xe, mix)
    xs, dst, tile_expert, n_tiles = _dispatch(info, counts, xn2)
    eo = _moe(tile_expert[0, :NT_MOE], n_tiles[0, :1], xs, moe_w1[0], moe_w3[0], moe_w2[0])
    y_p, y_e = _combine(dst, h1, info, row(final_norm_w), eo)

    y_prompt = y_p.reshape(N_BATCH, SEQ, D_MODEL)
    y_sample = y_e[0:N_SAMPLE].reshape(N_SAMPLE, 1, D_MODEL)
    new_rwkv_prompt = h_fin.reshape(N_BATCH, RW_HEAD, RW_HEADS, RW_HEAD).transpose(0, 2, 3, 1)[None]
    shift_rows = jnp.concatenate([zr[SEQ - 1:T_MAIN:SEQ], zr[ROW_SAMPLE:ROW_SAMPLE + N_SAMPLE]], axis=0)
    shift_rows = shift_rows[:, _SHIFT_INV]
    return (y_prompt, y_sample, new_rwkv_prompt, shift_rows[None, 0:N_BATCH], s_fin[None],
            rw_new[None], shift_rows[None, N_BATCH:], gla_new[None])
```

```python
import functools
import math

import numpy as np
import jax
import jax.numpy as jnp
from jax import lax
from jax.experimental import pallas as pl
from jax.experimental.pallas import tpu as pltpu

F32 = jnp.float32
BF16 = jnp.bfloat16

D_MODEL = 2048
N_BATCH = 4
SEQ = 2048
N_SAMPLE = 128
N_META = 16
RMS_EPS = 1e-6

RW_WIDTH = 1024
RW_HEAD = 64
RW_HEADS = 16
RW_GN_EPS = RW_HEAD * 1e-5
W_SHIFT = 3328
GLA_HEADS = 4
GLA_DK = 128
GLA_DV = 256
GLA_KDIM = 512
GLA_WIDTH = 1024
GLA_LORA = 16
GLA_NORMALIZER = 16.0
W_GLA_PAD = 2304
OG_BLOCK = 2 * D_MODEL // GLA_WIDTH
N_GROUPS = 4
EXPERTS_PER_GROUP = 8
N_EXPERTS = 32
D_EXPERT = 512

CHUNK = 64
N_CHUNKS = SEQ // CHUNK
META_PAD = CHUNK - N_META
T_MAIN = N_BATCH * SEQ
ROW_SAMPLE = T_MAIN
ROW_META = T_MAIN + N_SAMPLE
T_EXT = 256
T_ALL = T_MAIN + T_EXT
LANES = 128

TM_DENSE = 256
TM_MM = 1056
TM_MOE = 256
NT_MOE = (2 * T_ALL) // TM_MOE + N_EXPERTS
NEG_BIG = -1e30

VMEM_LIMIT = 56 * 1024 * 1024

_SHIFT_PERM = np.concatenate([np.arange(0, 1024), np.arange(1088, 2112), np.arange(2112, 3136),
                              np.arange(1024, 1088), np.arange(3136, 3328)])
_SHIFT_INV = np.argsort(_SHIFT_PERM)


def _row_tile(rows, index_map):
    return pl.BlockSpec((rows, None, D_MODEL), index_map)


def _cparams(n_axes=1):
    return pltpu.CompilerParams(dimension_semantics=("arbitrary",) * n_axes, vmem_limit_bytes=VMEM_LIMIT)


def _bdot(a, b):
    return jnp.dot(a.astype(BF16), b.astype(BF16), preferred_element_type=F32)


def _bdot_nt(a, b):
    return lax.dot_general(a.astype(BF16), b.astype(BF16), (((1,), (1,)), ((), ())), preferred_element_type=F32)


def _bdot_tn(a, b):
    return lax.dot_general(a.astype(BF16), b.astype(BF16), (((0,), (0,)), ((), ())), preferred_element_type=F32)


def _split_dot(m_bf16, x):
    hi = x.astype(BF16)
    lo = (x - hi.astype(F32)).astype(BF16)
    return (jnp.dot(m_bf16, hi, preferred_element_type=F32) + jnp.dot(m_bf16, lo, preferred_element_type=F32))


def _iota(shape, dim):
    return lax.broadcasted_iota(jnp.int32, shape, dim)


def _tri_incl(n):
    return (_iota((n, n), 0) >= _iota((n, n), 1)).astype(BF16)


def _seg_sum(x, width):
    m, n = x.shape
    nb = n // LANES
    bd = ((_iota((LANES, LANES), 0) // width) == (_iota((LANES, LANES), 1) // width)).astype(BF16)
    xs = jnp.concatenate([x[:, j * LANES:(j + 1) * LANES] for j in range(nb)], axis=0)
    hi = xs.astype(BF16)
    lo = (xs - hi.astype(F32)).astype(BF16)
    s = jnp.dot(hi, bd, preferred_element_type=F32) + jnp.dot(lo, bd, preferred_element_type=F32)
    return jnp.concatenate([s[j * m:(j + 1) * m] for j in range(nb)], axis=1)


def _col_of_row(row):
    n = row.shape[-1]
    eye = _iota((n, n), 0) == _iota((n, n), 1)
    return jnp.sum(jnp.where(eye, jnp.broadcast_to(row, (n, n)), 0.0), axis=-1, keepdims=True)


def _norm1_kernel(xp_ref, xe_ref, w_ref, o_ref):
    i = pl.program_id(0)

    def f(x):
        ms = jnp.mean(x * x, axis=-1, keepdims=True)
        return (x * lax.rsqrt(ms + RMS_EPS) * w_ref[...]).astype(BF16)

    @pl.when(i < T_MAIN // TM_DENSE)
    def _():
        o_ref[...] = f(xp_ref[...])

    @pl.when(i == T_MAIN // TM_DENSE)
    def _():
        o_ref[...] = f(xe_ref[...])


def _norm1(xp, xe, w):
    nmain = T_MAIN // TM_DENSE
    return pl.pallas_call(
        _norm1_kernel,
        grid=(T_ALL // TM_DENSE,),
        in_specs=[pl.BlockSpec((TM_DENSE, D_MODEL), lambda i: (jnp.minimum(i, nmain - 1), 0)),
                  pl.BlockSpec((T_EXT, D_MODEL), lambda i: (0, 0)),
                  pl.BlockSpec((1, D_MODEL), lambda i: (0, 0))],
        out_specs=pl.BlockSpec((TM_DENSE, D_MODEL), lambda i: (i, 0)),
        out_shape=jax.ShapeDtypeStruct((T_ALL, D_MODEL), BF16),
        compiler_params=_cparams(),
        name="norm1",
    )(xp, xe, w)


def _proj_kernel(x_ref, wt_hbm, o_ref, wraw, wbf, sem, *, tiles, silu_from):
    j = pl.program_id(0)
    m = pl.program_id(1)
    tn = wbf.shape[0]

    def copies(jj):
        return [pltpu.make_async_copy(wt_hbm.at[pl.ds(src, n), :], wraw.at[pl.ds(dst, n), :], sem)
                for (src, n, dst) in tiles[jj]]

    @pl.when(jnp.logical_and(j == 0, m == 0))
    def _():
        for c in copies(0):
            c.start()

    for jj in range(len(tiles)):
        @pl.when(jnp.logical_and(j == jj, m == 0))
        def _(jj=jj):
            for c in copies(jj):
                c.wait()
            edge = 0
            for lo, hi in sorted((dst, dst + n) for (_, n, dst) in tiles[jj]) + [(tn, tn)]:
                if lo > edge:
                    wbf[edge:lo, :] = jnp.zeros((lo - edge, wbf.shape[1]), BF16)
                if hi > lo:
                    wbf[lo:hi, :] = wraw[lo:hi, :].astype(BF16)
                edge = hi
            if jj + 1 < len(tiles):
                for c in copies(jj + 1):
                    c.start()

    z = lax.dot_general(x_ref[...], wbf[...], (((1,), (1,)), ((), ())), preferred_element_type=F32)
    if silu_from is None:
        o_ref[...] = z.astype(o_ref.dtype)
    else:
        s = jax.nn.sigmoid(z)
        o_ref[...] = jnp.where(j >= silu_from, z * s, s).astype(o_ref.dtype)


def _proj(x, w_t, tiles, tn, out_dtype, silu_from=None, name="proj"):
    t, k = x.shape
    return pl.pallas_call(
        functools.partial(_proj_kernel, tiles=tiles, silu_from=silu_from),
        grid=(len(tiles), t // TM_MM),
        in_specs=[pl.BlockSpec((TM_MM, k), lambda j, m: (m, 0)),
                  pl.BlockSpec(memory_space=pl.ANY)],
        out_specs=pl.BlockSpec((TM_MM, tn), lambda j, m: (m, j)),
        out_shape=jax.ShapeDtypeStruct((t, tn * len(tiles)), out_dtype),
        scratch_shapes=[pltpu.VMEM((tn, k), F32), pltpu.VMEM((tn, k), BF16), pltpu.SemaphoreType.DMA(())],
        compiler_params=_cparams(2),
        name=name,
    )(x, w_t)


_RWKV_TILES = (((0, 1024, 0), (1088, 640, 1024)),
               ((1728, 1408, 0), (1024, 64, 1408), (3136, 192, 1472)))
_GLA_TILES = (((3328, 1152, 0),), ((4480, 912, 0),))
_GATE_TILES = tuple(((6416 + 1024 * j, 1024, 0),) for j in range(4)) + (((5392, 1024, 0),),)


def _rwkv_pre(z, w0, w2a, a0, g2, k_k, k_a):
    r = z[:, 0:1024]
    k = z[:, 1024:2048]
    v = z[:, 2048:3072]
    wa = z[:, 3072:3200]
    gd = z[:, 3200:3328]
    lane = _iota(wa.shape, 1)
    wa = jnp.where(lane < 64, jnp.tanh(wa), wa)
    up = _bdot(wa, w2a)
    lw = -math.exp(-0.5) * jax.nn.sigmoid(w0 + up[:, :1024])
    a = jax.nn.sigmoid(a0 + up[:, 1024:])
    g = _bdot(jax.nn.sigmoid(gd), g2)
    kk = k * k_k
    kk = kk * lax.rsqrt(jnp.maximum(_seg_sum(kk * kk, RW_HEAD), 1e-24))
    k_h = k * (1.0 + (a - 1.0) * k_a)
    return r, lw, k_h, v, kk, kk * a, g


def _rwkv_post(y, r, k_h, v, g, r_k, lnx_w, lnx_b):
    mean = _seg_sum(y, RW_HEAD) * (1.0 / RW_HEAD)
    d = y - mean
    var = _seg_sum(d * d, RW_HEAD) * (1.0 / RW_HEAD)
    yn = d * lax.rsqrt(var + RW_GN_EPS) * lnx_w + lnx_b
    bonus = _seg_sum(r * k_h * r_k, RW_HEAD) * v
    return (yn + bonus) * g


def _stack2(x):
    lane = _iota(x.shape, 1)
    return jnp.concatenate([jnp.where(lane < RW_HEAD, x, 0.0), jnp.where(lane >= RW_HEAD, x, 0.0)], axis=0)


def _unstack2(x):
    c = x.shape[0] // 2
    return x[:c] + x[c:]


def _rwkv_chunk_kernel(zs_ref, mu_ref, w0_ref, w2a_ref, a0_ref, g2_ref, kk_ref, ka_ref, rk_ref, lw_ref, lb_ref,
                       o_ref, hout_ref, prev_sc, h_sc, hmeta_sc, prevmeta_sc, y_sc, *, n_chunks):
    i = pl.program_id(0)
    is_meta = i == 0
    c = lax.rem(jnp.maximum(i - 1, 0), n_chunks)
    first = jnp.logical_and(i >= 1, c == 0)

    @pl.when(is_meta)
    def _():
        h_sc[...] = jnp.zeros_like(h_sc)
        prev_sc[...] = jnp.zeros_like(prev_sc)

    @pl.when(first)
    def _():
        h_sc[...] = hmeta_sc[...]
        prev_sc[...] = prevmeta_sc[...]

    zs = zs_ref[...]
    rowi = _iota((CHUNK, 1), 0)
    sh = pltpu.roll(zs, 1, 0)
    sh = jnp.where(rowi == 0, prev_sc[...], sh)
    prev_sc[...] = zs[CHUNK - 1:CHUNK, :]
    z = zs + (sh - zs) * mu_ref[...]
    r, lw, k_h, v, kk, b, g = _rwkv_pre(z, w0_ref[...], w2a_ref[...], a0_ref[...], g2_ref[...], kk_ref[...],
                                        ka_ref[...])
    lw = jnp.where(jnp.logical_and(is_meta, rowi < META_PAD), 0.0, lw)

    cl = _split_dot(_tri_incl(CHUNK), lw)
    cl_end = cl[CHUNK - 1:CHUNK, :]
    e_neg = jnp.exp(-cl)
    e_end = jnp.exp(cl_end - cl)
    kkt = kk * jnp.exp(cl - lw)
    rt = r * jnp.exp(cl)
    bt = b * e_neg
    kt = k_h * e_neg
    bh = b * e_end
    kh = k_h * e_end
    e_c = jnp.exp(cl_end)

    n2 = 2 * CHUNK
    tok_r = jnp.bitwise_and(_iota((n2, n2), 0), CHUNK - 1)
    tok_c = jnp.bitwise_and(_iota((n2, n2), 1), CHUNK - 1)
    strict = tok_r > tok_c
    incl = tok_r >= tok_c

    pairs = range(RW_HEADS // 2)
    sls = [slice(j * LANES, (j + 1) * LANES) for j in pairs]
    s_kkt = [_stack2(kkt[:, sl]) for sl in sls]
    s_rt = [_stack2(rt[:, sl]) for sl in sls]
    s_v = [_stack2(v[:, sl]).astype(BF16) for sl in sls]
    rb = [jnp.concatenate([_stack2(bt[:, sl]), _stack2(kt[:, sl])], axis=0).astype(BF16) for sl in sls]
    aa = [_bdot_nt(s_kkt[j], rb[j]) for j in pairs]
    mm = [_bdot_nt(s_rt[j], rb[j]) for j in pairs]
    h_kv = [_bdot_tn(_stack2(kh[:, sls[j]]), s_v[j]) for j in pairs]
    a_b = [jnp.where(strict, aa[j][:, :n2], 0.0).astype(BF16) for j in pairs]
    a_k = [jnp.where(strict, aa[j][:, n2:], 0.0) for j in pairs]
    m_rb = [jnp.where(incl, mm[j][:, :n2], 0.0).astype(BF16) for j in pairs]
    m_rk = [jnp.where(incl, mm[j][:, n2:], 0.0) for j in pairs]
    akv = [_bdot(a_k[j], s_v[j]) for j in pairs]
    y_kv = [_bdot(m_rk[j], s_v[j]) for j in pairs]
    x = [jnp.concatenate([s_kkt[j], akv[j]], axis=1) for j in pairs]
    p = a_b
    x = [x[j] - _bdot(p[j], x[j]) for j in pairs]
    for _ in range(5):
        p = [_bdot(p[j], p[j]).astype(BF16) for j in pairs]
        x = [x[j] + _bdot(p[j], x[j]) for j in pairs]
    zq = [_bdot(m_rb[j], x[j]) for j in pairs]
    zb = [_bdot_tn(_stack2(bh[:, sls[j]]), x[j]) for j in pairs]
    h0 = [h_sc[j] for j in pairs]
    for j in pairs:
        q_eff = _unstack2(s_rt[j] - zq[j][:, :LANES])
        y_in = _unstack2(y_kv[j] - zq[j][:, LANES:])
        y_sc[:, sls[j]] = _bdot(q_eff, h0[j]) + y_in
    h_new = [_col_of_row(e_c[:, sls[j]]) * h0[j] - _bdot(zb[j][:, :LANES], h0[j]) + (h_kv[j] - zb[j][:, LANES:])
             for j in pairs]
    for j in pairs:
        h_sc[j] = h_new[j]

    @pl.when(jnp.logical_and(i >= 1, c == n_chunks - 1))
    def _():
        for j in pairs:
            hout_ref[0, :, sls[j]] = _unstack2(h_new[j])

    o = _rwkv_post(y_sc[...], r, k_h, v, g, rk_ref[...], lw_ref[...], lb_ref[...])
    o_ref[...] = o.astype(o_ref.dtype)

    @pl.when(is_meta)
    def _():
        hmeta_sc[...] = h_sc[...]
        prevmeta_sc[...] = prev_sc[...]


def _rwkv_prompt(zr, p, n_batch=N_BATCH, n_chunks=N_CHUNKS, meta_block=ROW_META // CHUNK):
    row = lambda n: pl.BlockSpec((1, n), lambda i: (0, 0))
    full = lambda a, b: pl.BlockSpec((a, b), lambda i: (0, 0))
    blk = lambda i: jnp.where(i == 0, meta_block, i - 1)
    return pl.pallas_call(
        functools.partial(_rwkv_chunk_kernel, n_chunks=n_chunks),
        grid=(1 + n_batch * n_chunks,),
        in_specs=[pl.BlockSpec((CHUNK, W_SHIFT), lambda i: (blk(i), 0)),
                  row(W_SHIFT), row(RW_WIDTH), full(LANES, 2 * RW_WIDTH), row(RW_WIDTH), full(LANES, RW_WIDTH),
                  row(RW_WIDTH), row(RW_WIDTH), row(RW_WIDTH), row(RW_WIDTH), row(RW_WIDTH)],
        out_specs=[pl.BlockSpec((CHUNK, RW_WIDTH), lambda i: (jnp.maximum(i - 1, 0), 0)),
                   pl.BlockSpec((1, RW_HEAD, RW_WIDTH), lambda i: (jnp.maximum(i - 1, 0) // n_chunks, 0, 0))],
        out_shape=[jax.ShapeDtypeStruct((n_batch * n_chunks * CHUNK, RW_WIDTH), BF16),
                   jax.ShapeDtypeStruct((n_batch, RW_HEAD, RW_WIDTH), F32)],
        scratch_shapes=[pltpu.VMEM((1, W_SHIFT), F32),
                        pltpu.VMEM((RW_HEADS // 2, LANES, LANES), F32),
                        pltpu.VMEM((RW_HEADS // 2, LANES, LANES), F32),
                        pltpu.VMEM((1, W_SHIFT), F32),
                        pltpu.VMEM((CHUNK, RW_WIDTH), F32)],
        compiler_params=_cparams(),
        name="rwkv_prompt",
    )(zr, p['mu'], p['w0'], p['w2a'], p['a0'], p['g2'], p['k_k'], p['k_a'], p['r_k'], p['lnx_w'], p['lnx_b'])


def _rwkv_sample_pre_kernel(zs_ref, prev_ref, mu_ref, w0_ref, w2a_ref, a0_ref, g2_ref, kk_ref, ka_ref,
                            r_ref, kh_ref, v_ref, g_ref, rt_ref, wt_ref, kht_ref, vt_ref, nkkt_ref, bt_ref):
    zs = zs_ref[...]
    z = zs + (prev_ref[...] - zs) * mu_ref[...]
    r, lw, k_h, v, kk, b, g = _rwkv_pre(z, w0_ref[...], w2a_ref[...], a0_ref[...], g2_ref[...], kk_ref[...],
                                        ka_ref[...])
    r_ref[...] = r
    kh_ref[...] = k_h
    v_ref[...] = v
    g_ref[...] = g
    rt_ref[...] = r.T
    wt_ref[...] = jnp.exp(lw).T
    kht_ref[...] = k_h.T
    vt_ref[...] = v.T
    nkkt_ref[...] = (-kk).T
    bt_ref[...] = b.T


def _rwkv_sample_pre(zr, prev, p):
    row = lambda n: pl.BlockSpec((1, n), lambda i: (0, 0))
    full = lambda a, b: pl.BlockSpec((a, b), lambda i: (0, 0))
    vec = jax.ShapeDtypeStruct((N_SAMPLE, RW_WIDTH), F32)
    vec_t = jax.ShapeDtypeStruct((RW_WIDTH, N_SAMPLE), F32)
    return pl.pallas_call(
        _rwkv_sample_pre_kernel,
        grid=(1,),
        in_specs=[pl.BlockSpec((N_SAMPLE, W_SHIFT), lambda i: (ROW_SAMPLE // N_SAMPLE, 0)),
                  full(N_SAMPLE, W_SHIFT),
                  row(W_SHIFT), row(RW_WIDTH), full(LANES, 2 * RW_WIDTH), row(RW_WIDTH), full(LANES, RW_WIDTH),
                  row(RW_WIDTH), row(RW_WIDTH)],
        out_specs=[full(N_SAMPLE, RW_WIDTH)] * 4 + [full(RW_WIDTH, N_SAMPLE)] * 6,
        out_shape=[vec] * 4 + [vec_t] * 6,
        compiler_params=_cparams(),
        name="rwkv_sample_pre",
    )(zr, prev, p['mu'], p['w0'], p['w2a'], p['a0'], p['g2'], p['k_k'], p['k_a'])


def _rwkv_sample_step_kernel(s_ref, r_ref, w_ref, kh_ref, v_ref, nkk_ref, b_ref, so_ref, y_ref):
    s = s_ref[0]
    sa = jnp.sum(s * nkk_ref[0][None], axis=1, keepdims=True)
    s_new = s * w_ref[0][None] + sa * b_ref[0][None] + v_ref[0] * kh_ref[0][None]
    so_ref[0] = s_new
    y_ref[0] = jnp.sum(s_new * r_ref[0][None], axis=1, keepdims=True)


def _rwkv_sample_step(state_t, r_t, w_t, kh_t, v_t, nkk_t, b_t):
    kvec = lambda t: t.reshape(RW_HEADS, RW_HEAD, N_SAMPLE)
    kspec = pl.BlockSpec((1, RW_HEAD, N_SAMPLE), lambda i: (i, 0, 0))
    vspec = pl.BlockSpec((1, RW_HEAD, 1, N_SAMPLE), lambda i: (i, 0, 0, 0))
    sspec = pl.BlockSpec((1, RW_HEAD, RW_HEAD, N_SAMPLE), lambda i: (i, 0, 0, 0))
    s_new, y = pl.pallas_call(
        _rwkv_sample_step_kernel,
        grid=(RW_HEADS,),
        in_specs=[sspec, kspec, kspec, kspec, vspec, kspec, kspec],
        out_specs=[sspec, vspec],
        out_shape=[jax.ShapeDtypeStruct(state_t.shape, F32),
                   jax.ShapeDtypeStruct((RW_HEADS, RW_HEAD, 1, N_SAMPLE), F32)],
        compiler_params=_cparams(),
        name="rwkv_sample_step",
    )(state_t, kvec(r_t), kvec(w_t), kvec(kh_t), v_t.reshape(RW_HEADS, RW_HEAD, 1, N_SAMPLE), kvec(nkk_t), kvec(b_t))
    return s_new, y.reshape(RW_WIDTH, N_SAMPLE)


def _rwkv_sample_post_kernel(yt_ref, r_ref, kh_ref, v_ref, g_ref, rk_ref, lw_ref, lb_ref, o_ref):
    o = _rwkv_post(yt_ref[...].T, r_ref[...], kh_ref[...], v_ref[...], g_ref[...], rk_ref[...], lw_ref[...],
                   lb_ref[...])
    o_ref[0:N_SAMPLE, :] = o.astype(o_ref.dtype)
    o_ref[N_SAMPLE:T_EXT, :] = jnp.zeros((T_EXT - N_SAMPLE, RW_WIDTH), o_ref.dtype)


def _rwkv_sample_post(y_t, r, kh, v, g, p):
    row = lambda n: pl.BlockSpec((1, n), lambda i: (0, 0))
    full = lambda a, b: pl.BlockSpec((a, b), lambda i: (0, 0))
    return pl.pallas_call(
        _rwkv_sample_post_kernel,
        grid=(1,),
        in_specs=[full(RW_WIDTH, N_SAMPLE)] + [full(N_SAMPLE, RW_WIDTH)] * 4 + [row(RW_WIDTH)] * 3,
        out_specs=full(T_EXT, RW_WIDTH),
        out_shape=jax.ShapeDtypeStruct((T_EXT, RW_WIDTH), BF16),
        compiler_params=_cparams(),
        name="rwkv_sample_post",
    )(y_t, r, kh, v, g, p['r_k'], p['lnx_w'], p['lnx_b'])


def _gla_logg(zg, gk_up, gk_b):
    gkd = zg[:, 2 * GLA_KDIM + GLA_WIDTH:2 * GLA_KDIM + GLA_WIDTH + LANES]
    x = _bdot(gkd, gk_up) + gk_b
    return (jnp.minimum(x, 0.0) - jnp.log(1.0 + jnp.exp(-jnp.abs(x)))) * (1.0 / GLA_NORMALIZER)


def _gla_post(o, og_act, norm_w):
    outs = []
    for h in range(GLA_HEADS):
        oh = o[:, h * GLA_DV:(h + 1) * GLA_DV]
        ms = jnp.mean(oh * oh, axis=-1, keepdims=True)
        outs.append(oh * lax.rsqrt(ms + RMS_EPS) * norm_w)
    return jnp.concatenate(outs, axis=1) * og_act


def _gla_chunk_kernel(zg_ref, og_ref, gkup_ref, gkb_ref, nw_ref, o_ref, sout_ref, s_sc, smeta_sc, *, n_chunks):
    i = pl.program_id(0)
    is_meta = i == 0
    c = lax.rem(jnp.maximum(i - 1, 0), n_chunks)
    first = jnp.logical_and(i >= 1, c == 0)

    @pl.when(is_meta)
    def _():
        s_sc[...] = jnp.zeros_like(s_sc)

    @pl.when(first)
    def _():
        s_sc[...] = smeta_sc[...]

    zg = zg_ref[...]
    rowi = _iota((CHUNK, 1), 0)
    logg = _gla_logg(zg, gkup_ref[...], gkb_ref[...])
    logg = jnp.where(jnp.logical_and(is_meta, rowi < META_PAD), 0.0, logg)
    bcum = _split_dot(_tri_incl(CHUNK), logg)
    b_end = bcum[CHUNK - 1:CHUNK, :]
    qt = zg[:, 0:GLA_KDIM] * (GLA_DK ** -0.5) * jnp.exp(bcum)
    kt = zg[:, GLA_KDIM:2 * GLA_KDIM] * jnp.exp(-bcum)
    ke = zg[:, GLA_KDIM:2 * GLA_KDIM] * jnp.exp(b_end - bcum)
    e_end = jnp.exp(b_end)
    causal = _iota((CHUNK, CHUNK), 0) >= _iota((CHUNK, CHUNK), 1)
    heads = range(GLA_HEADS)
    ks = [slice(h * GLA_DK, (h + 1) * GLA_DK) for h in heads]
    vh = [zg[:, 2 * GLA_KDIM + h * GLA_DV:2 * GLA_KDIM + (h + 1) * GLA_DV].astype(BF16) for h in heads]
    a = [jnp.where(causal, _bdot_nt(qt[:, ks[h]], kt[:, ks[h]]), 0.0) for h in heads]
    s0 = [s_sc[h] for h in heads]
    o_inter = [_bdot(qt[:, ks[h]], s0[h]) for h in heads]
    s_add = [_bdot_tn(ke[:, ks[h]], vh[h]) for h in heads]
    outs = [_bdot(a[h], vh[h]) + o_inter[h] for h in heads]
    s_new = [_col_of_row(e_end[:, ks[h]]) * s0[h] + s_add[h] for h in heads]
    for h in heads:
        s_sc[h] = s_new[h]

    @pl.when(jnp.logical_and(i >= 1, c == n_chunks - 1))
    def _():
        for h in heads:
            sout_ref[0, h] = s_new[h]

    o = _gla_post(jnp.concatenate(outs, axis=1), og_ref[...].astype(F32), nw_ref[...])
    o_ref[...] = o.astype(o_ref.dtype)

    @pl.when(is_meta)
    def _():
        smeta_sc[...] = s_sc[...]


def _gla_prompt(zg, zb, p, n_batch=N_BATCH, n_chunks=N_CHUNKS, meta_block=ROW_META // CHUNK, og_block=OG_BLOCK):
    full = lambda a, b: pl.BlockSpec((a, b), lambda i: (0, 0))
    blk = lambda i: jnp.where(i == 0, meta_block, i - 1)
    return pl.pallas_call(
        functools.partial(_gla_chunk_kernel, n_chunks=n_chunks),
        grid=(1 + n_batch * n_chunks,),
        in_specs=[pl.BlockSpec((CHUNK, W_GLA_PAD), lambda i: (blk(i), 0)),
                  pl.BlockSpec((CHUNK, GLA_WIDTH), lambda i: (blk(i), og_block)),
                  full(LANES, GLA_KDIM), full(1, GLA_KDIM), full(1, GLA_DV)],
        out_specs=[pl.BlockSpec((CHUNK, GLA_WIDTH), lambda i: (jnp.maximum(i - 1, 0), 0)),
                   pl.BlockSpec((1, GLA_HEADS, GLA_DK, GLA_DV),
                                lambda i: (jnp.maximum(i - 1, 0) // n_chunks, 0, 0, 0))],
        out_shape=[jax.ShapeDtypeStruct((n_batch * n_chunks * CHUNK, GLA_WIDTH), BF16),
                   jax.ShapeDtypeStruct((n_batch, GLA_HEADS, GLA_DK, GLA_DV), F32)],
        scratch_shapes=[pltpu.VMEM((GLA_HEADS, GLA_DK, GLA_DV), F32),
                        pltpu.VMEM((GLA_HEADS, GLA_DK, GLA_DV), F32)],
        compiler_params=_cparams(),
        name="gla_prompt",
    )(zg, zb, p['gk_up'], p['gk_b'], p['norm_w'])


def _gla_sample_pre_kernel(zg_ref, gkup_ref, gkb_ref, q_ref, k_ref, eg_ref):
    zg = zg_ref[...]
    logg = _gla_logg(zg, gkup_ref[...], gkb_ref[...])
    q_ref[...] = zg[:, 0:GLA_KDIM] * (GLA_DK ** -0.5)
    k_ref[...] = zg[:, GLA_KDIM:2 * GLA_KDIM]
    eg_ref[...] = jnp.exp(logg)


def _gla_sample_pre(zg, p):
    full = lambda a, b: pl.BlockSpec((a, b), lambda i: (0, 0))
    vec = jax.ShapeDtypeStruct((N_SAMPLE, GLA_KDIM), F32)
    return pl.pallas_call(
        _gla_sample_pre_kernel,
        grid=(1,),
        in_specs=[pl.BlockSpec((N_SAMPLE, W_GLA_PAD), lambda i: (ROW_SAMPLE // N_SAMPLE, 0)),
                  full(LANES, GLA_KDIM), full(1, GLA_KDIM)],
        out_specs=[full(N_SAMPLE, GLA_KDIM)] * 3,
        out_shape=[vec] * 3,
        compiler_params=_cparams(),
        name="gla_sample_pre",
    )(zg, p['gk_up'], p['gk_b'])


def _gla_sample_step_kernel(s_ref, q_ref, k_ref, eg_ref, v_ref, so_ref, o_ref):
    s = s_ref[0]
    eye = _iota((GLA_DK, GLA_DK), 0) == _iota((GLA_DK, GLA_DK), 1)
    col = lambda ref: jnp.sum(jnp.where(eye, ref[0], 0.0), axis=-1, keepdims=True)
    s_new = col(eg_ref) * s + col(k_ref) * v_ref[0]
    so_ref[0] = s_new
    o_ref[0] = jnp.sum(col(q_ref) * s_new, axis=-2, keepdims=True)


def _gla_sample_step(state, q, k, eg, v):
    kv = lambda t: t.reshape(N_SAMPLE, GLA_HEADS, 1, GLA_DK)
    kspec = pl.BlockSpec((1, GLA_HEADS, 1, GLA_DK), lambda i: (i, 0, 0, 0))
    vspec = pl.BlockSpec((1, GLA_HEADS, 1, GLA_DV), lambda i: (i, 0, 0, 0))
    sspec = pl.BlockSpec((1, GLA_HEADS, GLA_DK, GLA_DV), lambda i: (i, 0, 0, 0))
    s_new, o = pl.pallas_call(
        _gla_sample_step_kernel,
        grid=(N_SAMPLE,),
        in_specs=[sspec, kspec, kspec, kspec, vspec],
        out_specs=[sspec, vspec],
        out_shape=[jax.ShapeDtypeStruct(state.shape, F32),
                   jax.ShapeDtypeStruct((N_SAMPLE, GLA_HEADS, 1, GLA_DV), F32)],
        compiler_params=_cparams(),
        name="gla_sample_step",
    )(state, kv(q), kv(k), kv(eg), v.reshape(N_SAMPLE, GLA_HEADS, 1, GLA_DV))
    return s_new, o.reshape(N_SAMPLE, GLA_WIDTH)


def _gla_sample_post_kernel(o_ref, og_ref, nw_ref, out_ref):
    o = _gla_post(o_ref[...], og_ref[0:N_SAMPLE, :].astype(F32), nw_ref[...])
    out_ref[0:N_SAMPLE, :] = o.astype(out_ref.dtype)
    out_ref[N_SAMPLE:T_EXT, :] = jnp.zeros((T_EXT - N_SAMPLE, GLA_WIDTH), out_ref.dtype)


def _gla_sample_post(o, zb, p, og_block=OG_BLOCK):
    full = lambda a, b: pl.BlockSpec((a, b), lambda i: (0, 0))
    return pl.pallas_call(
        _gla_sample_post_kernel,
        grid=(1,),
        in_specs=[full(N_SAMPLE, GLA_WIDTH),
                  pl.BlockSpec((T_EXT, GLA_WIDTH), lambda i: (T_MAIN // T_EXT, og_block)),
                  full(1, GLA_DV)],
        out_specs=full(T_EXT, GLA_WIDTH),
        out_shape=jax.ShapeDtypeStruct((T_EXT, GLA_WIDTH), BF16),
        compiler_params=_cparams(),
        name="gla_sample_post",
    )(o, zb, p['norm_w'])


def _mix_route_kernel(or_ref, ore_ref, og_ref, oge_ref, gates_ref, xp_ref, xe_ref, pr_ref, pg_ref, wo_ref, n2_ref,
                      wrh_ref, wrl_ref, br_ref, h1_ref, xn_ref, info_ref, cnt_ref, carry_sc):
    i = pl.program_id(0)
    is_main = i < T_MAIN // TM_DENSE

    @pl.when(i == 0)
    def _():
        carry_sc[...] = jnp.zeros_like(carry_sc)

    sig_r = gates_ref[:, 0:D_MODEL].astype(F32)
    sig_g = gates_ref[:, D_MODEL:2 * D_MODEL].astype(F32)
    o_r = jnp.where(is_main, or_ref[...], ore_ref[...])
    o_g = jnp.where(is_main, og_ref[...], oge_ref[...])
    m = (sig_r * jnp.dot(o_r, pr_ref[...], preferred_element_type=F32)
         + sig_g * jnp.dot(o_g, pg_ref[...], preferred_element_type=F32))
    h = jnp.where(is_main, xp_ref[...], xe_ref[...])
    h1 = h + jnp.dot(m.astype(BF16), wo_ref[...], preferred_element_type=F32)
    h1_ref[...] = h1
    ms = jnp.mean(h1 * h1, axis=-1, keepdims=True)
    xn = h1 * lax.rsqrt(ms + RMS_EPS) * n2_ref[...]
    xn_ref[...] = xn
    xh = xn.astype(BF16)
    xl = (xn - xh.astype(F32)).astype(BF16)
    lg = (jnp.dot(xh, wrh_ref[...], preferred_element_type=F32)
          + jnp.dot(xh, wrl_ref[...], preferred_element_type=F32)
          + jnp.dot(xl, wrh_ref[...], preferred_element_type=F32)) + br_ref[...]

    lane = _iota(lg.shape, 1)
    lanef = lane.astype(F32)
    is_g = jnp.logical_and(lane >= N_EXPERTS, lane < N_EXPERTS + N_GROUPS)
    gl = jnp.where(is_g, lg, NEG_BIG)
    gmax = jnp.max(gl, axis=-1, keepdims=True)
    gsel = jnp.min(jnp.where(jnp.logical_and(is_g, gl == gmax), lanef, 1e9), axis=-1, keepdims=True) - N_EXPERTS
    p_g = 1.0 / jnp.sum(jnp.exp(gl - gmax), axis=-1, keepdims=True)
    grp = (lane // EXPERTS_PER_GROUP).astype(F32)
    in_grp = jnp.logical_and(lane < N_EXPERTS, grp == gsel)
    el = jnp.where(in_grp, lg, NEG_BIG)
    m1 = jnp.max(el, axis=-1, keepdims=True)
    i1 = jnp.min(jnp.where(jnp.logical_and(in_grp, el == m1), lanef, 1e9), axis=-1, keepdims=True)
    in2 = jnp.logical_and(in_grp, lanef != i1)
    el2 = jnp.where(in2, lg, NEG_BIG)
    m2 = jnp.max(el2, axis=-1, keepdims=True)
    i2 = jnp.min(jnp.where(jnp.logical_and(in2, el2 == m2), lanef, 1e9), axis=-1, keepdims=True)
    e2 = jnp.exp(m2 - m1)
    w1 = p_g / (1.0 + e2)
    w2 = p_g * e2 / (1.0 + e2)

    oh1 = lanef == i1
    oh2 = lanef == i2
    cnt = jnp.where(jnp.logical_or(oh1, oh2), 1.0, 0.0)
    tm = cnt.shape[0]
    lstrict = (_iota((tm, tm), 0) > _iota((tm, tm), 1)).astype(BF16)
    before = jnp.dot(lstrict, cnt.astype(BF16), preferred_element_type=F32) + carry_sc[...]
    rank1 = jnp.sum(jnp.where(oh1, before, 0.0), axis=-1, keepdims=True)
    rank2 = jnp.sum(jnp.where(oh2, before, 0.0), axis=-1, keepdims=True)
    carry_sc[...] = carry_sc[...] + jnp.sum(cnt, axis=0, keepdims=True)
    cnt_ref[...] = carry_sc[...]
    info = jnp.where(lane == 0, i1, jnp.where(lane == 1, i2, jnp.where(lane == 2, w1, jnp.where(
        lane == 3, w2, jnp.where(lane == 4, rank1, jnp.where(lane == 5, rank2, 0.0))))))
    info_ref[...] = info


def _mix_route(o_r, o_r_ext, o_g, o_g_ext, zb, xp, xe, p):
    nmain = T_MAIN // TM_DENSE
    tile = lambda n: pl.BlockSpec((TM_DENSE, n), lambda i: (i, 0))
    main = lambda n: pl.BlockSpec((TM_DENSE, n), lambda i: (jnp.minimum(i, nmain - 1), 0))
    ext = lambda n: pl.BlockSpec((T_EXT, n), lambda i: (0, 0))
    const = lambda a, b: pl.BlockSpec((a, b), lambda i: (0, 0), pipeline_mode=pl.Buffered(1))
    return pl.pallas_call(
        _mix_route_kernel,
        grid=(T_ALL // TM_DENSE,),
        in_specs=[main(RW_WIDTH), ext(RW_WIDTH), main(GLA_WIDTH), ext(GLA_WIDTH), tile(2 * D_MODEL),
                  main(D_MODEL), ext(D_MODEL),
                  const(RW_WIDTH, D_MODEL), const(GLA_WIDTH, D_MODEL), const(D_MODEL, D_MODEL),
                  const(1, D_MODEL), const(D_MODEL, LANES), const(D_MODEL, LANES), const(1, LANES)],
        out_specs=[tile(D_MODEL), _row_tile(TM_DENSE, lambda i: (i, 0, 0)), tile(LANES),
                   pl.BlockSpec((1, LANES), lambda i: (0, 0))],
        out_shape=[jax.ShapeDtypeStruct((T_ALL, D_MODEL), F32),
                   jax.ShapeDtypeStruct((T_ALL, 1, D_MODEL), F32),
                   jax.ShapeDtypeStruct((T_ALL, LANES), F32),
                   jax.ShapeDtypeStruct((1, LANES), F32)],
        scratch_shapes=[pltpu.VMEM((1, LANES), F32)],
        compiler_params=_cparams(),
        name="mix_route",
    )(o_r, o_r_ext, o_g, o_g_ext, zb, xp, xe, p['p_rwkv'], p['p_gla'], p['w_out'], p['norm2_w'], p['wr_hi'],
      p['wr_lo'], p['b_route'])


def _dispatch_kernel(info_ref, cnt_ref, xq_ref, xs_hbm, d_ref, te_ref, nt_ref,
                     d_vm, d_sm, tab_vm, tab_sm, zbuf, sem_t, sem_x):
    i = pl.program_id(0)
    lane1 = _iota((1, LANES), 1)
    cnt = jnp.where(lane1 < N_EXPERTS, cnt_ref[...], 0.0)
    tiles = jnp.floor((cnt + (TM_MOE - 1)) * (1.0 / TM_MOE))
    upper = (_iota((LANES, LANES), 0) < _iota((LANES, LANES), 1)).astype(BF16)
    tile_start = jnp.dot(jnp.broadcast_to(tiles, (8, LANES)).astype(BF16), upper,
                         preferred_element_type=F32)[0:1, :]
    base = tile_start * TM_MOE

    info = info_ref[...]
    lanef = _iota(info.shape, 1).astype(F32)
    pick = lambda col: jnp.sum(jnp.where(lanef == info[:, col:col + 1], base, 0.0), axis=-1, keepdims=True)
    d1 = pick(0) + info[:, 4:5]
    d2 = pick(1) + info[:, 5:6]
    tm = info.shape[0]
    eye = _iota((tm, tm), 0) == _iota((tm, tm), 1)
    to_row = lambda col: jnp.sum(jnp.where(eye, col, 0.0), axis=0, keepdims=True)
    d_row = jnp.concatenate([to_row(d1), to_row(d2)], axis=1).astype(jnp.int32)
    d_ref[0] = d_row
    d_vm[...] = d_row

    @pl.when(i == 0)
    def _():
        tile_end = tile_start + tiles
        n_tiles = jnp.sum(tiles, axis=-1, keepdims=True)
        eye_l = _iota((LANES, LANES), 0) == _iota((LANES, LANES), 1)
        end_col = jnp.sum(jnp.where(eye_l, tile_end, 0.0), axis=-1, keepdims=True)
        rowl = _iota((LANES, LANES), 0)
        tile_f = _iota((LANES, LANES), 1).astype(F32)
        te = jnp.sum(jnp.where(jnp.logical_and(rowl < N_EXPERTS, end_col <= tile_f), 1.0, 0.0), axis=0,
                     keepdims=True)
        last_e = jnp.max(jnp.where(tiles > 0.0, lane1.astype(F32), 0.0), axis=-1, keepdims=True)
        te = jnp.where(lane1.astype(F32) < n_tiles, jnp.minimum(te, N_EXPERTS - 1.0), last_e)
        te_ref[...] = te.astype(jnp.int32)
        nt_ref[...] = jnp.broadcast_to(n_tiles, (1, LANES)).astype(jnp.int32)
        zbuf[...] = jnp.zeros_like(zbuf)
        pad_lo = (base + cnt).astype(jnp.int32)
        pad_hi = (base + tiles * TM_MOE).astype(jnp.int32)
        nt_row = jnp.broadcast_to(n_tiles, (1, LANES)).astype(jnp.int32)
        tab_vm[...] = jnp.concatenate([pad_lo, pad_hi, nt_row, jnp.zeros((5, LANES), jnp.int32)], axis=0)
        tab_copy = pltpu.make_async_copy(tab_vm, tab_sm, sem_t)
        tab_copy.start()
        tab_copy.wait()

        def zero_row(r):
            return pltpu.make_async_copy(zbuf.at[pl.ds(0, 1)], xs_hbm.at[pl.ds(r, 1)], sem_x)

        def zero_tile(t):
            return pltpu.make_async_copy(zbuf, xs_hbm.at[pl.ds(t * TM_MOE, TM_MOE)], sem_x)

        def each(fn):
            def body(r, carry):
                fn(r)
                return carry
            return body

        for e in range(N_EXPERTS):
            lax.fori_loop(tab_sm[0, e], tab_sm[1, e], each(lambda r: zero_row(r).start()), 0)
        lax.fori_loop(tab_sm[2, 0], NT_MOE, each(lambda t: zero_tile(t).start()), 0)
        for e in range(N_EXPERTS):
            lax.fori_loop(tab_sm[0, e], tab_sm[1, e], each(lambda r: zero_row(r).wait()), 0)
        lax.fori_loop(tab_sm[2, 0], NT_MOE, each(lambda t: zero_tile(t).wait()), 0)

    d_copy = pltpu.make_async_copy(d_vm, d_sm, sem_t)
    d_copy.start()
    d_copy.wait()

    for t in range(tm):
        for half in range(2):
            pltpu.make_async_copy(xq_ref.at[pl.ds(t, 1)], xs_hbm.at[pl.ds(d_sm[0, half * tm + t], 1)],
                                  sem_x).start(priority=half)
    for half in range(2):
        pltpu.make_async_copy(zbuf, xs_hbm.at[pl.ds(0, tm)], sem_x).wait()


def _dispatch(info, counts, xq):
    nt = T_ALL // TM_DENSE
    return pl.pallas_call(
        _dispatch_kernel,
        grid=(nt,),
        in_specs=[pl.BlockSpec((TM_DENSE, LANES), lambda i: (i, 0)),
                  pl.BlockSpec((1, LANES), lambda i: (0, 0)),
                  pl.BlockSpec((TM_DENSE, 1, D_MODEL), lambda i: (i, 0, 0))],
        out_specs=[pl.BlockSpec(memory_space=pl.ANY),
                   pl.BlockSpec((1, 1, 2 * TM_DENSE), lambda i: (i, 0, 0)),
                   pl.BlockSpec((1, LANES), lambda i: (0, 0)),
                   pl.BlockSpec((1, LANES), lambda i: (0, 0))],
        out_shape=[jax.ShapeDtypeStruct((NT_MOE * TM_MOE, 1, D_MODEL), F32),
                   jax.ShapeDtypeStruct((nt, 1, 2 * TM_DENSE), jnp.int32),
                   jax.ShapeDtypeStruct((1, LANES), jnp.int32),
                   jax.ShapeDtypeStruct((1, LANES), jnp.int32)],
        scratch_shapes=[pltpu.VMEM((1, 2 * TM_DENSE), jnp.int32),
                        pltpu.SMEM((1, 2 * TM_DENSE), jnp.int32),
                        pltpu.VMEM((8, LANES), jnp.int32),
                        pltpu.SMEM((8, LANES), jnp.int32),
                        pltpu.VMEM((TM_MOE, 1, D_MODEL), F32),
                        pltpu.SemaphoreType.DMA(()),
                        pltpu.SemaphoreType.DMA(())],
        compiler_params=_cparams(),
        name="moe_dispatch",
    )(info, counts, xq)


def _moe_kernel(te_ref, nt_ref, xs_ref, w1_ref, w3_ref, w2_ref, o_ref, w1b, w3b, w2b):
    i = pl.program_id(0)

    @pl.when(i < nt_ref[0])
    def _():
        changed = jnp.logical_or(i == 0, te_ref[i] != te_ref[jnp.maximum(i - 1, 0)])

        @pl.when(changed)
        def _():
            w1b[...] = w1_ref[0].astype(BF16)
            w3b[...] = w3_ref[0].astype(BF16)
            w2b[...] = w2_ref[0].astype(BF16)

        xb = xs_ref[...].astype(BF16)
        h1 = jnp.dot(xb, w1b[...], preferred_element_type=F32)
        h3 = jnp.dot(xb, w3b[...], preferred_element_type=F32)
        hh = (h1 * jax.nn.sigmoid(h1) * h3).astype(BF16)
        o_ref[...] = jnp.dot(hh, w2b[...], preferred_element_type=F32)

    @pl.when(i >= nt_ref[0])
    def _():
        o_ref[...] = jnp.zeros_like(o_ref)


def _moe(tile_expert, n_tiles, xs, w1, w3, w2):
    grid_spec = pltpu.PrefetchScalarGridSpec(
        num_scalar_prefetch=2,
        grid=(NT_MOE,),
        in_specs=[_row_tile(TM_MOE, lambda i, te, nt: (jnp.minimum(i, nt[0] - 1), 0, 0)),
                  pl.BlockSpec((1, D_MODEL, D_EXPERT), lambda i, te, nt: (te[i], 0, 0)),
                  pl.BlockSpec((1, D_MODEL, D_EXPERT), lambda i, te, nt: (te[i], 0, 0)),
                  pl.BlockSpec((1, D_EXPERT, D_MODEL), lambda i, te, nt: (te[i], 0, 0))],
        out_specs=_row_tile(TM_MOE, lambda i, te, nt: (i, 0, 0)),
        scratch_shapes=[pltpu.VMEM((D_MODEL, D_EXPERT), BF16),
                        pltpu.VMEM((D_MODEL, D_EXPERT), BF16),
                        pltpu.VMEM((D_EXPERT, D_MODEL), BF16)],
    )
    return pl.pallas_call(
        _moe_kernel,
        grid_spec=grid_spec,
        out_shape=jax.ShapeDtypeStruct((NT_MOE * TM_MOE, 1, D_MODEL), F32),
        compiler_params=_cparams(),
        name="moe_experts",
    )(tile_expert, n_tiles, xs, w1, w3, w2)


def _combine_kernel(dcur_ref, dnext_ref, h1_ref, info_ref, fw_ref, eo_hbm, yp_ref, ye_ref, buf, sem):
    i = pl.program_id(0)
    nmain = T_MAIN // TM_DENSE
    slot = lax.rem(i, 2)

    def row_copy(d_ref, s, t):
        return pltpu.make_async_copy(eo_hbm.at[pl.ds(d_ref[0, 0, t], 1)], buf.at[s, pl.ds(t, 1)], sem.at[s])

    def issue_all(d_ref, s):
        for t in range(2 * TM_DENSE):
            row_copy(d_ref, s, t).start()

    @pl.when(i == 0)
    def _():
        issue_all(dcur_ref, slot)

    @pl.when(i + 1 < pl.num_programs(0))
    def _():
        issue_all(dnext_ref, 1 - slot)

    pltpu.make_async_copy(buf.at[slot], buf.at[slot], sem.at[slot]).wait()

    info = info_ref[...]
    y = (h1_ref[...] + info[:, 2:3] * buf[slot, 0:TM_DENSE, 0, :]
         + info[:, 3:4] * buf[slot, TM_DENSE:2 * TM_DENSE, 0, :])
    ms = jnp.mean(y * y, axis=-1, keepdims=True)
    out = y * lax.rsqrt(ms + RMS_EPS) * fw_ref[...]

    @pl.when(i < nmain)
    def _():
        yp_ref[...] = out

    @pl.when(i == nmain)
    def _():
        ye_ref[...] = out


def _combine(dst, h1, info, fw, eo):
    nmain = T_MAIN // TM_DENSE
    return pl.pallas_call(
        _combine_kernel,
        grid=(T_ALL // TM_DENSE,),
        in_specs=[pl.BlockSpec((1, 1, 2 * TM_DENSE), lambda i: (i, 0, 0), memory_space=pltpu.SMEM),
                  pl.BlockSpec((1, 1, 2 * TM_DENSE), lambda i: (jnp.minimum(i + 1, nmain), 0, 0),
                               memory_space=pltpu.SMEM),
                  pl.BlockSpec((TM_DENSE, D_MODEL), lambda i: (i, 0)),
                  pl.BlockSpec((TM_DENSE, LANES), lambda i: (i, 0)),
                  pl.BlockSpec((1, D_MODEL), lambda i: (0, 0)),
                  pl.BlockSpec(memory_space=pl.ANY)],
        out_specs=[pl.BlockSpec((TM_DENSE, D_MODEL), lambda i: (jnp.minimum(i, nmain - 1), 0)),
                   pl.BlockSpec((T_EXT, D_MODEL), lambda i: (0, 0))],
        out_shape=[jax.ShapeDtypeStruct((T_MAIN, D_MODEL), F32),
                   jax.ShapeDtypeStruct((T_EXT, D_MODEL), F32)],
        scratch_shapes=[pltpu.VMEM((2, 2 * TM_DENSE, 1, D_MODEL), F32), pltpu.SemaphoreType.DMA((2,))],
        compiler_params=_cparams(),
        name="moe_combine",
    )(dst, dst, h1, info, fw, eo)


def kernel(x_prompt, x_sample, state_rwkv, state_shift, state_gla, meta_tokens, norm1_w, w_in, mu_shift, rw_w0,
           rw_w2, rw_a0, rw_a2, rw_g2, rw_k_k, rw_k_a, rw_r_k, rw_lnx_w, rw_lnx_b, gla_gk_up, gla_gk_b, gla_norm_w,
           p_rwkv, p_gla, w_out, norm2_w, moe_w_group, moe_b_group, moe_w_router, moe_b_router, moe_w1, moe_w3,
           moe_w2, final_norm_w):
    w_t = w_in[0].T
    row = lambda t: t.reshape(1, -1)
    w2a = jnp.zeros((LANES, 2 * RW_WIDTH), F32)
    w2a = w2a.at[0:64, 0:RW_WIDTH].set(rw_w2[0]).at[64:128, RW_WIDTH:].set(rw_a2[0])
    rw = dict(mu=row(mu_shift[0][_SHIFT_PERM]), w0=row(rw_w0[0]), w2a=w2a, a0=row(rw_a0[0]), g2=rw_g2[0],
              k_k=row(rw_k_k[0]), k_a=row(rw_k_a[0]), r_k=row(rw_r_k[0]), lnx_w=row(rw_lnx_w[0]),
              lnx_b=row(rw_lnx_b[0]))
    gk_up = jnp.zeros((LANES, GLA_KDIM), F32).at[0:GLA_LORA].set(gla_gk_up[0])
    gl = dict(gk_up=gk_up, gk_b=row(gla_gk_b[0]), norm_w=row(gla_norm_w[0]))
    w_route = jnp.zeros((D_MODEL, LANES), F32)
    w_route = w_route.at[:, 0:N_EXPERTS].set(moe_w_router[0]).at[:, N_EXPERTS:N_EXPERTS + N_GROUPS].set(moe_w_group[0])
    wr_hi = w_route.astype(BF16)
    wr_lo = (w_route - wr_hi.astype(F32)).astype(BF16)
    b_route = jnp.zeros((1, LANES), F32)
    b_route = b_route.at[0, 0:N_EXPERTS].set(moe_b_router[0]).at[0, N_EXPERTS:N_EXPERTS + N_GROUPS].set(moe_b_group[0])
    mix = dict(p_rwkv=p_rwkv[0].astype(BF16), p_gla=p_gla[0].astype(BF16), w_out=w_out[0].astype(BF16),
               norm2_w=row(norm2_w[0]), wr_hi=wr_hi, wr_lo=wr_lo, b_route=b_route)

    xp = x_prompt.reshape(T_MAIN, D_MODEL)
    xe = jnp.concatenate([x_sample[:, 0, :], jnp.zeros((META_PAD, D_MODEL), F32), meta_tokens,
                          jnp.zeros((T_ALL - ROW_META - CHUNK, D_MODEL), F32)], axis=0)

    xn = _norm1(xp, xe, row(norm1_w[0]))
    zr = _proj(xn, w_t, _RWKV_TILES, 1664, F32, name="proj_rwkv")
    zg = _proj(xn, w_t, _GLA_TILES, 1152, F32, name="proj_gla")
    zb = _proj(xn, w_t, _GATE_TILES, 1024, BF16, silu_from=4, name="proj_gates")

    o_r, h_fin = _rwkv_prompt(zr, rw)
    o_g, s_fin = _gla_prompt(zg, zb, gl)

    r_s, kh_s, v_s, g_s, r_t, w_t, kh_t, v_t, nkk_t, b_t = _rwkv_sample_pre(zr, state_shift[0][:, _SHIFT_PERM], rw)
    rw_new_t, y_t = _rwkv_sample_step(state_rwkv[0].transpose(1, 2, 3, 0), r_t, w_t, kh_t, v_t, nkk_t, b_t)
    rw_new = rw_new_t.transpose(3, 0, 1, 2)
    o_r_ext = _rwkv_sample_post(y_t, r_s, kh_s, v_s, g_s, rw)
    q_s, k_s, eg_s = _gla_sample_pre(zg, gl)
    v_gs = zg[ROW_SAMPLE:ROW_SAMPLE + N_SAMPLE, 2 * GLA_KDIM:2 * GLA_KDIM + GLA_WIDTH]
    gla_new, og_s = _gla_sample_step(state_gla[0], q_s, k_s, eg_s, v_gs)
    o_g_ext = _gla_sample_post(og_s, zb, gl)

    h1, xn2, info, counts = _mix_route(o_r, o_r_ext, o_g, o_g_ext, zb, xp, xe, mix)
    xs, dst, tile_expert, n_tiles = _dispatch(info, counts, xn2)
    eo = _moe(tile_expert[0, :NT_MOE], n_tiles[0, :1], xs, moe_w1[0], moe_w3[0], moe_w2[0])
    y_p, y_e = _combine(dst, h1, info, row(final_norm_w), eo)

    y_prompt = y_p.reshape(N_BATCH, SEQ, D_MODEL)
    y_sample = y_e[0:N_SAMPLE].reshape(N_SAMPLE, 1, D_MODEL)
    new_rwkv_prompt = h_fin.reshape(N_BATCH, RW_HEAD, RW_HEADS, RW_HEAD).transpose(0, 2, 3, 1)[None]
    shift_rows = jnp.concatenate([zr[SEQ - 1:T_MAIN:SEQ], zr[ROW_SAMPLE:ROW_SAMPLE + N_SAMPLE]], axis=0)
    shift_rows = shift_rows[:, _SHIFT_INV]
    return (y_prompt, y_sample, new_rwkv_prompt, shift_rows[None, 0:N_BATCH], s_fin[None],
            rw_new[None], shift_rows[None, N_BATCH:], gla_new[None])
```

```python
import functools
import math

import numpy as np
import jax
import jax.numpy as jnp
from jax import lax
from jax.experimental import pallas as pl
from jax.experimental.pallas import tpu as pltpu

F32 = jnp.float32
BF16 = jnp.bfloat16

D_MODEL = 2048
N_BATCH = 4
SEQ = 2048
N_SAMPLE = 128
N_META = 16
RMS_EPS = 1e-6

RW_WIDTH = 1024
RW_HEAD = 64
RW_HEADS = 16
RW_GN_EPS = RW_HEAD * 1e-5
W_SHIFT = 3328
GLA_HEADS = 4
GLA_DK = 128
GLA_DV = 256
GLA_KDIM = 512
GLA_WIDTH = 1024
GLA_LORA = 16
GLA_NORMALIZER = 16.0
W_GLA_PAD = 2304
OG_BLOCK = 2 * D_MODEL // GLA_WIDTH
N_GROUPS = 4
EXPERTS_PER_GROUP = 8
N_EXPERTS = 32
D_EXPERT = 512

CHUNK = 64
N_CHUNKS = SEQ // CHUNK
META_PAD = CHUNK - N_META
T_MAIN = N_BATCH * SEQ
ROW_SAMPLE = T_MAIN
ROW_META = T_MAIN + N_SAMPLE
T_EXT = 256
T_ALL = T_MAIN + T_EXT
LANES = 128

TM_DENSE = 256
TM_MM = 1056
TM_MOE = 256
NT_MOE = (2 * T_ALL) // TM_MOE + N_EXPERTS
NEG_BIG = -1e30

VMEM_LIMIT = 56 * 1024 * 1024

_SHIFT_PERM = np.concatenate([np.arange(0, 1024), np.arange(1088, 2112), np.arange(2112, 3136),
                              np.arange(1024, 1088), np.arange(3136, 3328)])
_SHIFT_INV = np.argsort(_SHIFT_PERM)


def _cparams(n_axes=1):
    return pltpu.CompilerParams(dimension_semantics=("arbitrary",) * n_axes, vmem_limit_bytes=VMEM_LIMIT)


def _bdot(a, b):
    return jnp.dot(a.astype(BF16), b.astype(BF16), preferred_element_type=F32)


def _bdot_nt(a, b):
    return lax.dot_general(a.astype(BF16), b.astype(BF16), (((1,), (1,)), ((), ())), preferred_element_type=F32)


def _bdot_tn(a, b):
    return lax.dot_general(a.astype(BF16), b.astype(BF16), (((0,), (0,)), ((), ())), preferred_element_type=F32)


def _split_dot(m_bf16, x):
    hi = x.astype(BF16)
    lo = (x - hi.astype(F32)).astype(BF16)
    return (jnp.dot(m_bf16, hi, preferred_element_type=F32) + jnp.dot(m_bf16, lo, preferred_element_type=F32))


def _iota(shape, dim):
    return lax.broadcasted_iota(jnp.int32, shape, dim)


def _tri_incl(n):
    return (_iota((n, n), 0) >= _iota((n, n), 1)).astype(BF16)


def _seg_sum(x, width):
    m, n = x.shape
    nb = n // LANES
    bd = ((_iota((LANES, LANES), 0) // width) == (_iota((LANES, LANES), 1) // width)).astype(BF16)
    xs = jnp.concatenate([x[:, j * LANES:(j + 1) * LANES] for j in range(nb)], axis=0)
    hi = xs.astype(BF16)
    lo = (xs - hi.astype(F32)).astype(BF16)
    s = jnp.dot(hi, bd, preferred_element_type=F32) + jnp.dot(lo, bd, preferred_element_type=F32)
    return jnp.concatenate([s[j * m:(j + 1) * m] for j in range(nb)], axis=1)


def _col_of_row(row):
    n = row.shape[-1]
    eye = _iota((n, n), 0) == _iota((n, n), 1)
    return jnp.sum(jnp.where(eye, jnp.broadcast_to(row, (n, n)), 0.0), axis=-1, keepdims=True)


def _norm1_kernel(xp_ref, xe_ref, w_ref, o_ref):
    i = pl.program_id(0)

    def f(x):
        ms = jnp.mean(x * x, axis=-1, keepdims=True)
        return (x * lax.rsqrt(ms + RMS_EPS) * w_ref[...]).astype(BF16)

    @pl.when(i < T_MAIN // TM_DENSE)
    def _():
        o_ref[...] = f(xp_ref[...])

    @pl.when(i == T_MAIN // TM_DENSE)
    def _():
        o_ref[...] = f(xe_ref[...])


def _norm1(xp, xe, w):
    nmain = T_MAIN // TM_DENSE
    return pl.pallas_call(
        _norm1_kernel,
        grid=(T_ALL // TM_DENSE,),
        in_specs=[pl.BlockSpec((TM_DENSE, D_MODEL), lambda i: (jnp.minimum(i, nmain - 1), 0)),
                  pl.BlockSpec((T_EXT, D_MODEL), lambda i: (0, 0)),
                  pl.BlockSpec((1, D_MODEL), lambda i: (0, 0))],
        out_specs=pl.BlockSpec((TM_DENSE, D_MODEL), lambda i: (i, 0)),
        out_shape=jax.ShapeDtypeStruct((T_ALL, D_MODEL), BF16),
        compiler_params=_cparams(),
        name="norm1",
    )(xp, xe, w)


def _proj_kernel(x_ref, wt_hbm, o_ref, wraw, wbf, sem, *, tiles, silu_from):
    j = pl.program_id(0)
    m = pl.program_id(1)
    tn = wbf.shape[0]

    def copies(jj):
        return [pltpu.make_async_copy(wt_hbm.at[pl.ds(src, n), :], wraw.at[pl.ds(dst, n), :], sem)
                for (src, n, dst) in tiles[jj]]

    @pl.when(jnp.logical_and(j == 0, m == 0))
    def _():
        for c in copies(0):
            c.start()

    for jj in range(len(tiles)):
        @pl.when(jnp.logical_and(j == jj, m == 0))
        def _(jj=jj):
            for c in copies(jj):
                c.wait()
            edge = 0
            for lo, hi in sorted((dst, dst + n) for (_, n, dst) in tiles[jj]) + [(tn, tn)]:
                if lo > edge:
                    wbf[edge:lo, :] = jnp.zeros((lo - edge, wbf.shape[1]), BF16)
                if hi > lo:
                    wbf[lo:hi, :] = wraw[lo:hi, :].astype(BF16)
                edge = hi
            if jj + 1 < len(tiles):
                for c in copies(jj + 1):
                    c.start()

    z = lax.dot_general(x_ref[...], wbf[...], (((1,), (1,)), ((), ())), preferred_element_type=F32)
    if silu_from is None:
        o_ref[...] = z.astype(o_ref.dtype)
    else:
        s = jax.nn.sigmoid(z)
        o_ref[...] = jnp.where(j >= silu_from, z * s, s).astype(o_ref.dtype)


def _proj(x, w_t, tiles, tn, out_dtype, silu_from=None, name="proj"):
    t, k = x.shape
    return pl.pallas_call(
        functools.partial(_proj_kernel, tiles=tiles, silu_from=silu_from),
        grid=(len(tiles), t // TM_MM),
        in_specs=[pl.BlockSpec((TM_MM, k), lambda j, m: (m, 0)),
                  pl.BlockSpec(memory_space=pl.ANY)],
        out_specs=pl.BlockSpec((TM_MM, tn), lambda j, m: (m, j)),
        out_shape=jax.ShapeDtypeStruct((t, tn * len(tiles)), out_dtype),
        scratch_shapes=[pltpu.VMEM((tn, k), F32), pltpu.VMEM((tn, k), BF16), pltpu.SemaphoreType.DMA(())],
        compiler_params=_cparams(2),
        name=name,
    )(x, w_t)


_RWKV_TILES = (((0, 1024, 0), (1088, 640, 1024)),
               ((1728, 1408, 0), (1024, 64, 1408), (3136, 192, 1472)))
_GLA_TILES = (((3328, 1152, 0),), ((4480, 912, 0),))
_GATE_TILES = tuple(((6416 + 1024 * j, 1024, 0),) for j in range(4)) + (((5392, 1024, 0),),)


def _rwkv_pre(z, w0, w2a, a0, g2, k_k, k_a):
    r = z[:, 0:1024]
    k = z[:, 1024:2048]
    v = z[:, 2048:3072]
    wa = z[:, 3072:3200]
    gd = z[:, 3200:3328]
    lane = _iota(wa.shape, 1)
    wa = jnp.where(lane < 64, jnp.tanh(wa), wa)
    up = _bdot(wa, w2a)
    lw = -math.exp(-0.5) * jax.nn.sigmoid(w0 + up[:, :1024])
    a = jax.nn.sigmoid(a0 + up[:, 1024:])
    g = _bdot(jax.nn.sigmoid(gd), g2)
    kk = k * k_k
    kk = kk * lax.rsqrt(jnp.maximum(_seg_sum(kk * kk, RW_HEAD), 1e-24))
    k_h = k * (1.0 + (a - 1.0) * k_a)
    return r, lw, k_h, v, kk, kk * a, g


def _rwkv_post(y, r, k_h, v, g, r_k, lnx_w, lnx_b):
    mean = _seg_sum(y, RW_HEAD) * (1.0 / RW_HEAD)
    d = y - mean
    var = _seg_sum(d * d, RW_HEAD) * (1.0 / RW_HEAD)
    yn = d * lax.rsqrt(var + RW_GN_EPS) * lnx_w + lnx_b
    bonus = _seg_sum(r * k_h * r_k, RW_HEAD) * v
    return (yn + bonus) * g


def _stack2(x):
    lane = _iota(x.shape, 1)
    return jnp.concatenate([jnp.where(lane < RW_HEAD, x, 0.0), jnp.where(lane >= RW_HEAD, x, 0.0)], axis=0)


def _unstack2(x):
    c = x.shape[0] // 2
    return x[:c] + x[c:]


def _rwkv_chunk_kernel(zs_ref, mu_ref, w0_ref, w2a_ref, a0_ref, g2_ref, kk_ref, ka_ref, rk_ref, lw_ref, lb_ref,
                       o_ref, hout_ref, prev_sc, h_sc, hmeta_sc, prevmeta_sc, y_sc, *, n_chunks):
    i = pl.program_id(0)
    is_meta = i == 0
    c = lax.rem(jnp.maximum(i - 1, 0), n_chunks)
    first = jnp.logical_and(i >= 1, c == 0)

    @pl.when(is_meta)
    def _():
        h_sc[...] = jnp.zeros_like(h_sc)
        prev_sc[...] = jnp.zeros_like(prev_sc)

    @pl.when(first)
    def _():
        h_sc[...] = hmeta_sc[...]
        prev_sc[...] = prevmeta_sc[...]

    zs = zs_ref[...]
    rowi = _iota((CHUNK, 1), 0)
    sh = pltpu.roll(zs, 1, 0)
    sh = jnp.where(rowi == 0, prev_sc[...], sh)
    prev_sc[...] = zs[CHUNK - 1:CHUNK, :]
    z = zs + (sh - zs) * mu_ref[...]
    r, lw, k_h, v, kk, b, g = _rwkv_pre(z, w0_ref[...], w2a_ref[...], a0_ref[...], g2_ref[...], kk_ref[...],
                                        ka_ref[...])
    lw = jnp.where(jnp.logical_and(is_meta, rowi < META_PAD), 0.0, lw)

    cl = _split_dot(_tri_incl(CHUNK), lw)
    cl_end = cl[CHUNK - 1:CHUNK, :]
    e_neg = jnp.exp(-cl)
    e_end = jnp.exp(cl_end - cl)
    kkt = kk * jnp.exp(cl - lw)
    rt = r * jnp.exp(cl)
    bt = b * e_neg
    kt = k_h * e_neg
    bh = b * e_end
    kh = k_h * e_end
    e_c = jnp.exp(cl_end)

    n2 = 2 * CHUNK
    tok_r = jnp.bitwise_and(_iota((n2, n2), 0), CHUNK - 1)
    tok_c = jnp.bitwise_and(_iota((n2, n2), 1), CHUNK - 1)
    strict = tok_r > tok_c
    incl = tok_r >= tok_c

    pairs = range(RW_HEADS // 2)
    sls = [slice(j * LANES, (j + 1) * LANES) for j in pairs]
    s_kkt = [_stack2(kkt[:, sl]) for sl in sls]
    s_rt = [_stack2(rt[:, sl]) for sl in sls]
    s_v = [_stack2(v[:, sl]).astype(BF16) for sl in sls]
    rb = [jnp.concatenate([_stack2(bt[:, sl]), _stack2(kt[:, sl])], axis=0).astype(BF16) for sl in sls]
    aa = [_bdot_nt(s_kkt[j], rb[j]) for j in pairs]
    mm = [_bdot_nt(s_rt[j], rb[j]) for j in pairs]
    h_kv = [_bdot_tn(_stack2(kh[:, sls[j]]), s_v[j]) for j in pairs]
    a_b = [jnp.where(strict, aa[j][:, :n2], 0.0).astype(BF16) for j in pairs]
    a_k = [jnp.where(strict, aa[j][:, n2:], 0.0) for j in pairs]
    m_rb = [jnp.where(incl, mm[j][:, :n2], 0.0).astype(BF16) for j in pairs]
    m_rk = [jnp.where(incl, mm[j][:, n2:], 0.0) for j in pairs]
    akv = [_bdot(a_k[j], s_v[j]) for j in pairs]
    y_kv = [_bdot(m_rk[j], s_v[j]) for j in pairs]
    x = [jnp.concatenate([s_kkt[j], akv[j]], axis=1) for j in pairs]
    p = a_b
    x = [x[j] - _bdot(p[j], x[j]) for j in pairs]
    for _ in range(5):
        p = [_bdot(p[j], p[j]).astype(BF16) for j in pairs]
        x = [x[j] + _bdot(p[j], x[j]) for j in pairs]
    zq = [_bdot(m_rb[j], x[j]) for j in pairs]
    zb = [_bdot_tn(_stack2(bh[:, sls[j]]), x[j]) for j in pairs]
    h0 = [h_sc[j] for j in pairs]
    for j in pairs:
        q_eff = _unstack2(s_rt[j] - zq[j][:, :LANES])
        y_in = _unstack2(y_kv[j] - zq[j][:, LANES:])
        y_sc[:, sls[j]] = _bdot(q_eff, h0[j]) + y_in
    h_new = [_col_of_row(e_c[:, sls[j]]) * h0[j] - _bdot(zb[j][:, :LANES], h0[j]) + (h_kv[j] - zb[j][:, LANES:])
             for j in pairs]
    for j in pairs:
        h_sc[j] = h_new[j]

    @pl.when(jnp.logical_and(i >= 1, c == n_chunks - 1))
    def _():
        for j in pairs:
            hout_ref[0, :, sls[j]] = _unstack2(h_new[j])

    o = _rwkv_post(y_sc[...], r, k_h, v, g, rk_ref[...], lw_ref[...], lb_ref[...])
    o_ref[...] = o.astype(o_ref.dtype)

    @pl.when(is_meta)
    def _():
        hmeta_sc[...] = h_sc[...]
        prevmeta_sc[...] = prev_sc[...]


def _rwkv_prompt(zr, p, n_batch=N_BATCH, n_chunks=N_CHUNKS, meta_block=ROW_META // CHUNK):
    row = lambda n: pl.BlockSpec((1, n), lambda i: (0, 0))
    full = lambda a, b: pl.BlockSpec((a, b), lambda i: (0, 0))
    blk = lambda i: jnp.where(i == 0, meta_block, i - 1)
    return pl.pallas_call(
        functools.partial(_rwkv_chunk_kernel, n_chunks=n_chunks),
        grid=(1 + n_batch * n_chunks,),
        in_specs=[pl.BlockSpec((CHUNK, W_SHIFT), lambda i: (blk(i), 0)),
                  row(W_SHIFT), row(RW_WIDTH), full(LANES, 2 * RW_WIDTH), row(RW_WIDTH), full(LANES, RW_WIDTH),
                  row(RW_WIDTH), row(RW_WIDTH), row(RW_WIDTH), row(RW_WIDTH), row(RW_WIDTH)],
        out_specs=[pl.BlockSpec((CHUNK, RW_WIDTH), lambda i: (jnp.maximum(i - 1, 0), 0)),
                   pl.BlockSpec((1, RW_HEAD, RW_WIDTH), lambda i: (jnp.maximum(i - 1, 0) // n_chunks, 0, 0))],
        out_shape=[jax.ShapeDtypeStruct((n_batch * n_chunks * CHUNK, RW_WIDTH), BF16),
                   jax.ShapeDtypeStruct((n_batch, RW_HEAD, RW_WIDTH), F32)],
        scratch_shapes=[pltpu.VMEM((1, W_SHIFT), F32),
                        pltpu.VMEM((RW_HEADS // 2, LANES, LANES), F32),
                        pltpu.VMEM((RW_HEADS // 2, LANES, LANES), F32),
                        pltpu.VMEM((1, W_SHIFT), F32),
                        pltpu.VMEM((CHUNK, RW_WIDTH), F32)],
        compiler_params=_cparams(),
        name="rwkv_prompt",
    )(zr, p['mu'], p['w0'], p['w2a'], p['a0'], p['g2'], p['k_k'], p['k_a'], p['r_k'], p['lnx_w'], p['lnx_b'])


def _rwkv_sample_pre_kernel(zs_ref, prev_ref, mu_ref, w0_ref, w2a_ref, a0_ref, g2_ref, kk_ref, ka_ref,
                            r_ref, kh_ref, v_ref, g_ref, rt_ref, wt_ref, kht_ref, vt_ref, nkkt_ref, bt_ref):
    zs = zs_ref[...]
    z = zs + (prev_ref[...] - zs) * mu_ref[...]
    r, lw, k_h, v, kk, b, g = _rwkv_pre(z, w0_ref[...], w2a_ref[...], a0_ref[...], g2_ref[...], kk_ref[...],
                                        ka_ref[...])
    r_ref[...] = r
    kh_ref[...] = k_h
    v_ref[...] = v
    g_ref[...] = g
    rt_ref[...] = r.T
    wt_ref[...] = jnp.exp(lw).T
    kht_ref[...] = k_h.T
    vt_ref[...] = v.T
    nkkt_ref[...] = (-kk).T
    bt_ref[...] = b.T


def _rwkv_sample_pre(zr, prev, p):
    row = lambda n: pl.BlockSpec((1, n), lambda i: (0, 0))
    full = lambda a, b: pl.BlockSpec((a, b), lambda i: (0, 0))
    vec = jax.ShapeDtypeStruct((N_SAMPLE, RW_WIDTH), F32)
    vec_t = jax.ShapeDtypeStruct((RW_WIDTH, N_SAMPLE), F32)
    return pl.pallas_call(
        _rwkv_sample_pre_kernel,
        grid=(1,),
        in_specs=[pl.BlockSpec((N_SAMPLE, W_SHIFT), lambda i: (ROW_SAMPLE // N_SAMPLE, 0)),
                  full(N_SAMPLE, W_SHIFT),
                  row(W_SHIFT), row(RW_WIDTH), full(LANES, 2 * RW_WIDTH), row(RW_WIDTH), full(LANES, RW_WIDTH),
                  row(RW_WIDTH), row(RW_WIDTH)],
        out_specs=[full(N_SAMPLE, RW_WIDTH)] * 4 + [full(RW_WIDTH, N_SAMPLE)] * 6,
        out_shape=[vec] * 4 + [vec_t] * 6,
        compiler_params=_cparams(),
        name="rwkv_sample_pre",
    )(zr, prev, p['mu'], p['w0'], p['w2a'], p['a0'], p['g2'], p['k_k'], p['k_a'])


def _rwkv_sample_step_kernel(s_ref, r_ref, w_ref, kh_ref, v_ref, nkk_ref, b_ref, so_ref, y_ref):
    s = s_ref[0]
    sa = jnp.sum(s * nkk_ref[0][None], axis=1, keepdims=True)
    s_new = s * w_ref[0][None] + sa * b_ref[0][None] + v_ref[0] * kh_ref[0][None]
    so_ref[0] = s_new
    y_ref[0] = jnp.sum(s_new * r_ref[0][None], axis=1, keepdims=True)


def _rwkv_sample_step(state_t, r_t, w_t, kh_t, v_t, nkk_t, b_t):
    kvec = lambda t: t.reshape(RW_HEADS, RW_HEAD, N_SAMPLE)
    kspec = pl.BlockSpec((1, RW_HEAD, N_SAMPLE), lambda i: (i, 0, 0))
    vspec = pl.BlockSpec((1, RW_HEAD, 1, N_SAMPLE), lambda i: (i, 0, 0, 0))
    sspec = pl.BlockSpec((1, RW_HEAD, RW_HEAD, N_SAMPLE), lambda i: (i, 0, 0, 0))
    s_new, y = pl.pallas_call(
        _rwkv_sample_step_kernel,
        grid=(RW_HEADS,),
        in_specs=[sspec, kspec, kspec, kspec, vspec, kspec, kspec],
        out_specs=[sspec, vspec],
        out_shape=[jax.ShapeDtypeStruct(state_t.shape, F32),
                   jax.ShapeDtypeStruct((RW_HEADS, RW_HEAD, 1, N_SAMPLE), F32)],
        compiler_params=_cparams(),
        name="rwkv_sample_step",
    )(state_t, kvec(r_t), kvec(w_t), kvec(kh_t), v_t.reshape(RW_HEADS, RW_HEAD, 1, N_SAMPLE), kvec(nkk_t), kvec(b_t))
    return s_new, y.reshape(RW_WIDTH, N_SAMPLE)


def _rwkv_sample_post_kernel(yt_ref, r_ref, kh_ref, v_ref, g_ref, rk_ref, lw_ref, lb_ref, o_ref):
    o = _rwkv_post(yt_ref[...].T, r_ref[...], kh_ref[...], v_ref[...], g_ref[...], rk_ref[...], lw_ref[...],
                   lb_ref[...])
    o_ref[0:N_SAMPLE, :] = o.astype(o_ref.dtype)
    o_ref[N_SAMPLE:T_EXT, :] = jnp.zeros((T_EXT - N_SAMPLE, RW_WIDTH), o_ref.dtype)


def _rwkv_sample_post(y_t, r, kh, v, g, p):
    row = lambda n: pl.BlockSpec((1, n), lambda i: (0, 0))
    full = lambda a, b: pl.BlockSpec((a, b), lambda i: (0, 0))
    return pl.pallas_call(
        _rwkv_sample_post_kernel,
        grid=(1,),
        in_specs=[full(RW_WIDTH, N_SAMPLE)] + [full(N_SAMPLE, RW_WIDTH)] * 4 + [row(RW_WIDTH)] * 3,
        out_specs=full(T_EXT, RW_WIDTH),
        out_shape=jax.ShapeDtypeStruct((T_EXT, RW_WIDTH), BF16),
        compiler_params=_cparams(),
        name="rwkv_sample_post",
    )(y_t, r, kh, v, g, p['r_k'], p['lnx_w'], p['lnx_b'])


def _gla_logg(zg, gk_up, gk_b):
    gkd = zg[:, 2 * GLA_KDIM + GLA_WIDTH:2 * GLA_KDIM + GLA_WIDTH + LANES]
    x = _bdot(gkd, gk_up) + gk_b
    return (jnp.minimum(x, 0.0) - jnp.log(1.0 + jnp.exp(-jnp.abs(x)))) * (1.0 / GLA_NORMALIZER)


def _gla_post(o, og_act, norm_w):
    outs = []
    for h in range(GLA_HEADS):
        oh = o[:, h * GLA_DV:(h + 1) * GLA_DV]
        ms = jnp.mean(oh * oh, axis=-1, keepdims=True)
        outs.append(oh * lax.rsqrt(ms + RMS_EPS) * norm_w)
    return jnp.concatenate(outs, axis=1) * og_act


def _gla_chunk_kernel(zg_ref, og_ref, gkup_ref, gkb_ref, nw_ref, o_ref, sout_ref, s_sc, smeta_sc, *, n_chunks):
    i = pl.program_id(0)
    is_meta = i == 0
    c = lax.rem(jnp.maximum(i - 1, 0), n_chunks)
    first = jnp.logical_and(i >= 1, c == 0)

    @pl.when(is_meta)
    def _():
        s_sc[...] = jnp.zeros_like(s_sc)

    @pl.when(first)
    def _():
        s_sc[...] = smeta_sc[...]

    zg = zg_ref[...]
    rowi = _iota((CHUNK, 1), 0)
    logg = _gla_logg(zg, gkup_ref[...], gkb_ref[...])
    logg = jnp.where(jnp.logical_and(is_meta, rowi < META_PAD), 0.0, logg)
    bcum = _split_dot(_tri_incl(CHUNK), logg)
    b_end = bcum[CHUNK - 1:CHUNK, :]
    qt = zg[:, 0:GLA_KDIM] * (GLA_DK ** -0.5) * jnp.exp(bcum)
    kt = zg[:, GLA_KDIM:2 * GLA_KDIM] * jnp.exp(-bcum)
    ke = zg[:, GLA_KDIM:2 * GLA_KDIM] * jnp.exp(b_end - bcum)
    e_end = jnp.exp(b_end)
    causal = _iota((CHUNK, CHUNK), 0) >= _iota((CHUNK, CHUNK), 1)
    heads = range(GLA_HEADS)
    ks = [slice(h * GLA_DK, (h + 1) * GLA_DK) for h in heads]
    vh = [zg[:, 2 * GLA_KDIM + h * GLA_DV:2 * GLA_KDIM + (h + 1) * GLA_DV].astype(BF16) for h in heads]
    a = [jnp.where(causal, _bdot_nt(qt[:, ks[h]], kt[:, ks[h]]), 0.0) for h in heads]
    s0 = [s_sc[h] for h in heads]
    o_inter = [_bdot(qt[:, ks[h]], s0[h]) for h in heads]
    s_add = [_bdot_tn(ke[:, ks[h]], vh[h]) for h in heads]
    outs = [_bdot(a[h], vh[h]) + o_inter[h] for h in heads]
    s_new = [_col_of_row(e_end[:, ks[h]]) * s0[h] + s_add[h] for h in heads]
    for h in heads:
        s_sc[h] = s_new[h]

    @pl.when(jnp.logical_and(i >= 1, c == n_chunks - 1))
    def _():
        for h in heads:
            sout_ref[0, h] = s_new[h]

    o = _gla_post(jnp.concatenate(outs, axis=1), og_ref[...].astype(F32), nw_ref[...])
    o_ref[...] = o.astype(o_ref.dtype)

    @pl.when(is_meta)
    def _():
        smeta_sc[...] = s_sc[...]


def _gla_prompt(zg, zb, p, n_batch=N_BATCH, n_chunks=N_CHUNKS, meta_block=ROW_META // CHUNK, og_block=OG_BLOCK):
    full = lambda a, b: pl.BlockSpec((a, b), lambda i: (0, 0))
    blk = lambda i: jnp.where(i == 0, meta_block, i - 1)
    return pl.pallas_call(
        functools.partial(_gla_chunk_kernel, n_chunks=n_chunks),
        grid=(1 + n_batch * n_chunks,),
        in_specs=[pl.BlockSpec((CHUNK, W_GLA_PAD), lambda i: (blk(i), 0)),
                  pl.BlockSpec((CHUNK, GLA_WIDTH), lambda i: (blk(i), og_block)),
                  full(LANES, GLA_KDIM), full(1, GLA_KDIM), full(1, GLA_DV)],
        out_specs=[pl.BlockSpec((CHUNK, GLA_WIDTH), lambda i: (jnp.maximum(i - 1, 0), 0)),
                   pl.BlockSpec((1, GLA_HEADS, GLA_DK, GLA_DV),
                                lambda i: (jnp.maximum(i - 1, 0) // n_chunks, 0, 0, 0))],
        out_shape=[jax.ShapeDtypeStruct((n_batch * n_chunks * CHUNK, GLA_WIDTH), BF16),
                   jax.ShapeDtypeStruct((n_batch, GLA_HEADS, GLA_DK, GLA_DV), F32)],
        scratch_shapes=[pltpu.VMEM((GLA_HEADS, GLA_DK, GLA_DV), F32),
                        pltpu.VMEM((GLA_HEADS, GLA_DK, GLA_DV), F32)],
        compiler_params=_cparams(),
        name="gla_prompt",
    )(zg, zb, p['gk_up'], p['gk_b'], p['norm_w'])


def _gla_sample_pre_kernel(zg_ref, gkup_ref, gkb_ref, q_ref, k_ref, eg_ref):
    zg = zg_ref[...]
    logg = _gla_logg(zg, gkup_ref[...], gkb_ref[...])
    q_ref[...] = zg[:, 0:GLA_KDIM] * (GLA_DK ** -0.5)
    k_ref[...] = zg[:, GLA_KDIM:2 * GLA_KDIM]
    eg_ref[...] = jnp.exp(logg)


def _gla_sample_pre(zg, p):
    full = lambda a, b: pl.BlockSpec((a, b), lambda i: (0, 0))
    vec = jax.ShapeDtypeStruct((N_SAMPLE, GLA_KDIM), F32)
    return pl.pallas_call(
        _gla_sample_pre_kernel,
        grid=(1,),
        in_specs=[pl.BlockSpec((N_SAMPLE, W_GLA_PAD), lambda i: (ROW_SAMPLE // N_SAMPLE, 0)),
                  full(LANES, GLA_KDIM), full(1, GLA_KDIM)],
        out_specs=[full(N_SAMPLE, GLA_KDIM)] * 3,
        out_shape=[vec] * 3,
        compiler_params=_cparams(),
        name="gla_sample_pre",
    )(zg, p['gk_up'], p['gk_b'])


GLA_STEP_BATCH = 8


def _gla_sample_step_kernel(s_ref, q_ref, k_ref, eg_ref, v_ref, so_ref, o_ref):
    eye = _iota((GLA_DK, GLA_DK), 0) == _iota((GLA_DK, GLA_DK), 1)
    for b in range(GLA_STEP_BATCH):
        s = s_ref[b]
        col = lambda ref: jnp.sum(jnp.where(eye, ref[b], 0.0), axis=-1, keepdims=True)
        s_new = col(eg_ref) * s + col(k_ref) * v_ref[b]
        so_ref[b] = s_new
        o_ref[b] = jnp.sum(col(q_ref) * s_new, axis=-2, keepdims=True)


def _gla_sample_step(state, q, k, eg, v):
    kv = lambda t: t.reshape(N_SAMPLE, GLA_HEADS, 1, GLA_DK)
    kspec = pl.BlockSpec((GLA_STEP_BATCH, GLA_HEADS, 1, GLA_DK), lambda i: (i, 0, 0, 0))
    vspec = pl.BlockSpec((GLA_STEP_BATCH, GLA_HEADS, 1, GLA_DV), lambda i: (i, 0, 0, 0))
    sspec = pl.BlockSpec((GLA_STEP_BATCH, GLA_HEADS, GLA_DK, GLA_DV), lambda i: (i, 0, 0, 0))
    s_new, o = pl.pallas_call(
        _gla_sample_step_kernel,
        grid=(N_SAMPLE // GLA_STEP_BATCH,),
        in_specs=[sspec, kspec, kspec, kspec, vspec],
        out_specs=[sspec, vspec],
        out_shape=[jax.ShapeDtypeStruct(state.shape, F32),
                   jax.ShapeDtypeStruct((N_SAMPLE, GLA_HEADS, 1, GLA_DV), F32)],
        compiler_params=_cparams(),
        name="gla_sample_step",
    )(state, kv(q), kv(k), kv(eg), v.reshape(N_SAMPLE, GLA_HEADS, 1, GLA_DV))
    return s_new, o.reshape(N_SAMPLE, GLA_WIDTH)


def _gla_sample_post_kernel(o_ref, og_ref, nw_ref, out_ref):
    o = _gla_post(o_ref[...], og_ref[0:N_SAMPLE, :].astype(F32), nw_ref[...])
    out_ref[0:N_SAMPLE, :] = o.astype(out_ref.dtype)
    out_ref[N_SAMPLE:T_EXT, :] = jnp.zeros((T_EXT - N_SAMPLE, GLA_WIDTH), out_ref.dtype)


def _gla_sample_post(o, zb, p, og_block=OG_BLOCK):
    full = lambda a, b: pl.BlockSpec((a, b), lambda i: (0, 0))
    return pl.pallas_call(
        _gla_sample_post_kernel,
        grid=(1,),
        in_specs=[full(N_SAMPLE, GLA_WIDTH),
                  pl.BlockSpec((T_EXT, GLA_WIDTH), lambda i: (T_MAIN // T_EXT, og_block)),
                  full(1, GLA_DV)],
        out_specs=full(T_EXT, GLA_WIDTH),
        out_shape=jax.ShapeDtypeStruct((T_EXT, GLA_WIDTH), BF16),
        compiler_params=_cparams(),
        name="gla_sample_post",
    )(o, zb, p['norm_w'])


def _mix_route_kernel(or_ref, ore_ref, og_ref, oge_ref, gates_ref, xp_ref, xe_ref, pr_ref, pg_ref, wo_ref, n2_ref,
                      wrh_ref, wrl_ref, br_ref, h1_ref, xn_ref, info_ref, cnt_ref, carry_sc):
    i = pl.program_id(0)
    is_main = i < T_MAIN // TM_DENSE

    @pl.when(i == 0)
    def _():
        carry_sc[...] = jnp.zeros_like(carry_sc)

    sig_r = gates_ref[:, 0:D_MODEL].astype(F32)
    sig_g = gates_ref[:, D_MODEL:2 * D_MODEL].astype(F32)
    o_r = jnp.where(is_main, or_ref[...], ore_ref[...])
    o_g = jnp.where(is_main, og_ref[...], oge_ref[...])
    m = (sig_r * jnp.dot(o_r, pr_ref[...], preferred_element_type=F32)
         + sig_g * jnp.dot(o_g, pg_ref[...], preferred_element_type=F32))
    h = jnp.where(is_main, xp_ref[...], xe_ref[...])
    h1 = h + jnp.dot(m.astype(BF16), wo_ref[...], preferred_element_type=F32)
    h1_ref[...] = h1
    ms = jnp.mean(h1 * h1, axis=-1, keepdims=True)
    xn = h1 * lax.rsqrt(ms + RMS_EPS) * n2_ref[...]
    xn_ref[...] = xn
    xh = xn.astype(BF16)
    xl = (xn - xh.astype(F32)).astype(BF16)
    lg = (jnp.dot(xh, wrh_ref[...], preferred_element_type=F32)
          + jnp.dot(xh, wrl_ref[...], preferred_element_type=F32)
          + jnp.dot(xl, wrh_ref[...], preferred_element_type=F32)) + br_ref[...]

    lane = _iota(lg.shape, 1)
    lanef = lane.astype(F32)
    is_g = jnp.logical_and(lane >= N_EXPERTS, lane < N_EXPERTS + N_GROUPS)
    gl = jnp.where(is_g, lg, NEG_BIG)
    gmax = jnp.max(gl, axis=-1, keepdims=True)
    gsel = jnp.min(jnp.where(jnp.logical_and(is_g, gl == gmax), lanef, 1e9), axis=-1, keepdims=True) - N_EXPERTS
    p_g = 1.0 / jnp.sum(jnp.exp(gl - gmax), axis=-1, keepdims=True)
    grp = (lane // EXPERTS_PER_GROUP).astype(F32)
    in_grp = jnp.logical_and(lane < N_EXPERTS, grp == gsel)
    el = jnp.where(in_grp, lg, NEG_BIG)
    m1 = jnp.max(el, axis=-1, keepdims=True)
    i1 = jnp.min(jnp.where(jnp.logical_and(in_grp, el == m1), lanef, 1e9), axis=-1, keepdims=True)
    in2 = jnp.logical_and(in_grp, lanef != i1)
    el2 = jnp.where(in2, lg, NEG_BIG)
    m2 = jnp.max(el2, axis=-1, keepdims=True)
    i2 = jnp.min(jnp.where(jnp.logical_and(in2, el2 == m2), lanef, 1e9), axis=-1, keepdims=True)
    e2 = jnp.exp(m2 - m1)
    w1 = p_g / (1.0 + e2)
    w2 = p_g * e2 / (1.0 + e2)

    oh1 = lanef == i1
    oh2 = lanef == i2
    cnt = jnp.where(jnp.logical_or(oh1, oh2), 1.0, 0.0)
    tm = cnt.shape[0]
    lstrict = (_iota((tm, tm), 0) > _iota((tm, tm), 1)).astype(BF16)
    before = jnp.dot(lstrict, cnt.astype(BF16), preferred_element_type=F32) + carry_sc[...]
    rank1 = jnp.sum(jnp.where(oh1, before, 0.0), axis=-1, keepdims=True)
    rank2 = jnp.sum(jnp.where(oh2, before, 0.0), axis=-1, keepdims=True)
    carry_sc[...] = carry_sc[...] + jnp.sum(cnt, axis=0, keepdims=True)
    cnt_ref[...] = carry_sc[...]
    info = jnp.where(lane == 0, i1, jnp.where(lane == 1, i2, jnp.where(lane == 2, w1, jnp.where(
        lane == 3, w2, jnp.where(lane == 4, rank1, jnp.where(lane == 5, rank2, 0.0))))))
    info_ref[...] = info


def _mix_route(o_r, o_r_ext, o_g, o_g_ext, zb, xp, xe, p):
    nmain = T_MAIN // TM_DENSE
    tile = lambda n: pl.BlockSpec((TM_DENSE, n), lambda i: (i, 0))
    main = lambda n: pl.BlockSpec((TM_DENSE, n), lambda i: (jnp.minimum(i, nmain - 1), 0))
    ext = lambda n: pl.BlockSpec((T_EXT, n), lambda i: (0, 0))
    const = lambda a, b: pl.BlockSpec((a, b), lambda i: (0, 0), pipeline_mode=pl.Buffered(1))
    return pl.pallas_call(
        _mix_route_kernel,
        grid=(T_ALL // TM_DENSE,),
        in_specs=[main(RW_WIDTH), ext(RW_WIDTH), main(GLA_WIDTH), ext(GLA_WIDTH), tile(2 * D_MODEL),
                  main(D_MODEL), ext(D_MODEL),
                  const(RW_WIDTH, D_MODEL), const(GLA_WIDTH, D_MODEL), const(D_MODEL, D_MODEL),
                  const(1, D_MODEL), const(D_MODEL, LANES), const(D_MODEL, LANES), const(1, LANES)],
        out_specs=[tile(D_MODEL), tile(D_MODEL), tile(LANES), pl.BlockSpec((1, LANES), lambda i: (0, 0))],
        out_shape=[jax.ShapeDtypeStruct((T_ALL, D_MODEL), F32),
                   jax.ShapeDtypeStruct((T_ALL, D_MODEL), F32),
                   jax.ShapeDtypeStruct((T_ALL, LANES), F32),
                   jax.ShapeDtypeStruct((1, LANES), F32)],
        scratch_shapes=[pltpu.VMEM((1, LANES), F32)],
        compiler_params=_cparams(),
        name="mix_route",
    )(o_r, o_r_ext, o_g, o_g_ext, zb, xp, xe, p['p_rwkv'], p['p_gla'], p['w_out'], p['norm2_w'], p['wr_hi'],
      p['wr_lo'], p['b_route'])


def _dispatch_kernel(info_ref, cnt_ref, xq_ref, xs_hbm, d_ref, te_ref, nt_ref,
                     d_vm, d_sm, tab_vm, tab_sm, zbuf, sem_t, sem_x):
    i = pl.program_id(0)
    lane1 = _iota((1, LANES), 1)
    cnt = jnp.where(lane1 < N_EXPERTS, cnt_ref[...], 0.0)
    tiles = jnp.floor((cnt + (TM_MOE - 1)) * (1.0 / TM_MOE))
    upper = (_iota((LANES, LANES), 0) < _iota((LANES, LANES), 1)).astype(BF16)
    tile_start = jnp.dot(jnp.broadcast_to(tiles, (8, LANES)).astype(BF16), upper,
                         preferred_element_type=F32)[0:1, :]
    base = tile_start * TM_MOE

    info = info_ref[...]
    lanef = _iota(info.shape, 1).astype(F32)
    pick = lambda col: jnp.sum(jnp.where(lanef == info[:, col:col + 1], base, 0.0), axis=-1, keepdims=True)
    d1 = pick(0) + info[:, 4:5]
    d2 = pick(1) + info[:, 5:6]
    tm = info.shape[0]
    eye = _iota((tm, tm), 0) == _iota((tm, tm), 1)
    to_row = lambda col: jnp.sum(jnp.where(eye, col, 0.0), axis=0, keepdims=True)
    d_row = jnp.concatenate([to_row(d1), to_row(d2)], axis=1).astype(jnp.int32)
    d_ref[0] = d_row
    d_vm[...] = d_row

    @pl.when(i == 0)
    def _():
        tile_end = tile_start + tiles
        n_tiles = jnp.sum(tiles, axis=-1, keepdims=True)
        eye_l = _iota((LANES, LANES), 0) == _iota((LANES, LANES), 1)
        end_col = jnp.sum(jnp.where(eye_l, tile_end, 0.0), axis=-1, keepdims=True)
        rowl = _iota((LANES, LANES), 0)
        tile_f = _iota((LANES, LANES), 1).astype(F32)
        te = jnp.sum(jnp.where(jnp.logical_and(rowl < N_EXPERTS, end_col <= tile_f), 1.0, 0.0), axis=0,
                     keepdims=True)
        last_e = jnp.max(jnp.where(tiles > 0.0, lane1.astype(F32), 0.0), axis=-1, keepdims=True)
        te = jnp.where(lane1.astype(F32) < n_tiles, jnp.minimum(te, N_EXPERTS - 1.0), last_e)
        te_ref[...] = te.astype(jnp.int32)
        nt_ref[...] = jnp.broadcast_to(n_tiles, (1, LANES)).astype(jnp.int32)
        zbuf[...] = jnp.zeros_like(zbuf)
        pad_lo = (base + cnt).astype(jnp.int32)
        pad_hi = (base + tiles * TM_MOE).astype(jnp.int32)
        nt_row = jnp.broadcast_to(n_tiles, (1, LANES)).astype(jnp.int32)
        tab_vm[...] = jnp.concatenate([pad_lo, pad_hi, nt_row, jnp.zeros((5, LANES), jnp.int32)], axis=0)
        tab_copy = pltpu.make_async_copy(tab_vm, tab_sm, sem_t)
        tab_copy.start()
        tab_copy.wait()

        def zero_row(r):
            return pltpu.make_async_copy(zbuf.at[pl.ds(0, 1), :], xs_hbm.at[pl.ds(r, 1), :], sem_x)

        def zero_tile(t):
            return pltpu.make_async_copy(zbuf, xs_hbm.at[pl.ds(t * TM_MOE, TM_MOE), :], sem_x)

        def each(fn):
            def body(r, carry):
                fn(r)
                return carry
            return body

        for e in range(N_EXPERTS):
            lax.fori_loop(tab_sm[0, e], tab_sm[1, e], each(lambda r: zero_row(r).start()), 0)
        lax.fori_loop(tab_sm[2, 0], NT_MOE, each(lambda t: zero_tile(t).start()), 0)
        for e in range(N_EXPERTS):
            lax.fori_loop(tab_sm[0, e], tab_sm[1, e], each(lambda r: zero_row(r).wait()), 0)
        lax.fori_loop(tab_sm[2, 0], NT_MOE, each(lambda t: zero_tile(t).wait()), 0)

    d_copy = pltpu.make_async_copy(d_vm, d_sm, sem_t)
    d_copy.start()
    d_copy.wait()

    for t in range(tm):
        for half in range(2):
            pltpu.make_async_copy(xq_ref.at[pl.ds(t, 1), :], xs_hbm.at[pl.ds(d_sm[0, half * tm + t], 1), :],
                                  sem_x).start(priority=half)
    for half in range(2):
        pltpu.make_async_copy(xq_ref, xs_hbm.at[pl.ds(0, tm), :], sem_x).wait()


def _dispatch(info, counts, xq):
    nt = T_ALL // TM_DENSE
    return pl.pallas_call(
        _dispatch_kernel,
        grid=(nt,),
        in_specs=[pl.BlockSpec((TM_DENSE, LANES), lambda i: (i, 0)),
                  pl.BlockSpec((1, LANES), lambda i: (0, 0)),
                  pl.BlockSpec((TM_DENSE, D_MODEL), lambda i: (i, 0))],
        out_specs=[pl.BlockSpec(memory_space=pl.ANY),
                   pl.BlockSpec((1, 1, 2 * TM_DENSE), lambda i: (i, 0, 0)),
                   pl.BlockSpec((1, LANES), lambda i: (0, 0)),
                   pl.BlockSpec((1, LANES), lambda i: (0, 0))],
        out_shape=[jax.ShapeDtypeStruct((NT_MOE * TM_MOE, D_MODEL), F32),
                   jax.ShapeDtypeStruct((nt, 1, 2 * TM_DENSE), jnp.int32),
                   jax.ShapeDtypeStruct((1, LANES), jnp.int32),
                   jax.ShapeDtypeStruct((1, LANES), jnp.int32)],
        scratch_shapes=[pltpu.VMEM((1, 2 * TM_DENSE), jnp.int32),
                        pltpu.SMEM((1, 2 * TM_DENSE), jnp.int32),
                        pltpu.VMEM((8, LANES), jnp.int32),
                        pltpu.SMEM((8, LANES), jnp.int32),
                        pltpu.VMEM((TM_MOE, D_MODEL), F32),
                        pltpu.SemaphoreType.DMA(()),
                        pltpu.SemaphoreType.DMA(())],
        compiler_params=_cparams(),
        name="moe_dispatch",
    )(info, counts, xq)


def _moe_kernel(te_ref, nt_ref, xs_ref, w1_hbm, w3_hbm, w2_hbm, o_ref, w1r, w3r, w2r, w1b, w3b, w2b, sem):
    i = pl.program_id(0)
    n_tiles = nt_ref[0]

    def weight_copies(e):
        return [pltpu.make_async_copy(w1_hbm.at[e], w1r, sem), pltpu.make_async_copy(w3_hbm.at[e], w3r, sem),
                pltpu.make_async_copy(w2_hbm.at[e], w2r, sem)]

    @pl.when(i < n_tiles)
    def _():
        e = te_ref[i]

        @pl.when(i == 0)
        def _():
            for c in weight_copies(e):
                c.start()

        @pl.when(jnp.logical_or(i == 0, e != te_ref[jnp.maximum(i - 1, 0)]))
        def _():
            for c in weight_copies(e):
                c.wait()
            w1b[...] = w1r[...].astype(BF16)
            w3b[...] = w3r[...].astype(BF16)
            w2b[...] = w2r[...].astype(BF16)
            nxt = lax.while_loop(
                lambda j: jnp.logical_and(j < n_tiles, te_ref[jnp.minimum(j, NT_MOE - 1)] == e),
                lambda j: j + 1, i + 1)

            @pl.when(nxt < n_tiles)
            def _():
                for c in weight_copies(te_ref[jnp.minimum(nxt, NT_MOE - 1)]):
                    c.start()

        xb = xs_ref[...].astype(BF16)
        h1 = jnp.dot(xb, w1b[...], preferred_element_type=F32)
        h3 = jnp.dot(xb, w3b[...], preferred_element_type=F32)
        hh = (h1 * jax.nn.sigmoid(h1) * h3).astype(BF16)
        o_ref[...] = jnp.dot(hh, w2b[...], preferred_element_type=F32)

    @pl.when(i >= nt_ref[0])
    def _():
        o_ref[...] = jnp.zeros_like(o_ref)


def _moe(tile_expert, n_tiles, xs, w1, w3, w2):
    grid_spec = pltpu.PrefetchScalarGridSpec(
        num_scalar_prefetch=2,
        grid=(NT_MOE,),
        in_specs=[pl.BlockSpec((TM_MOE, D_MODEL), lambda i, te, nt: (jnp.minimum(i, nt[0] - 1), 0)),
                  pl.BlockSpec(memory_space=pl.ANY),
                  pl.BlockSpec(memory_space=pl.ANY),
                  pl.BlockSpec(memory_space=pl.ANY)],
        out_specs=pl.BlockSpec((TM_MOE, D_MODEL), lambda i, te, nt: (i, 0)),
        scratch_shapes=[pltpu.VMEM((D_MODEL, D_EXPERT), F32),
                        pltpu.VMEM((D_MODEL, D_EXPERT), F32),
                        pltpu.VMEM((D_EXPERT, D_MODEL), F32),
                        pltpu.VMEM((D_MODEL, D_EXPERT), BF16),
                        pltpu.VMEM((D_MODEL, D_EXPERT), BF16),
                        pltpu.VMEM((D_EXPERT, D_MODEL), BF16),
                        pltpu.SemaphoreType.DMA(())],
    )
    return pl.pallas_call(
        _moe_kernel,
        grid_spec=grid_spec,
        out_shape=jax.ShapeDtypeStruct((NT_MOE * TM_MOE, D_MODEL), F32),
        compiler_params=_cparams(),
        name="moe_experts",
    )(tile_expert, n_tiles, xs, w1, w3, w2)


def _combine_kernel(dcur_ref, dnext_ref, h1_ref, info_ref, fw_ref, eo_hbm, yp_ref, ye_ref, buf, sem):
    i = pl.program_id(0)
    nmain = T_MAIN // TM_DENSE
    slot = lax.rem(i, 2)

    def row_copy(d_ref, s, t):
        return pltpu.make_async_copy(eo_hbm.at[pl.ds(d_ref[0, 0, t], 1), :], buf.at[s, pl.ds(t, 1), :], sem.at[s])

    def issue_all(d_ref, s):
        for t in range(2 * TM_DENSE):
            row_copy(d_ref, s, t).start()

    @pl.when(i == 0)
    def _():
        issue_all(dcur_ref, slot)

    @pl.when(i + 1 < pl.num_programs(0))
    def _():
        issue_all(dnext_ref, 1 - slot)

    pltpu.make_async_copy(eo_hbm.at[pl.ds(0, 2 * TM_DENSE), :], buf.at[slot], sem.at[slot]).wait()

    info = info_ref[...]
    y = h1_ref[...] + info[:, 2:3] * buf[slot, 0:TM_DENSE, :] + info[:, 3:4] * buf[slot, TM_DENSE:2 * TM_DENSE, :]
    ms = jnp.mean(y * y, axis=-1, keepdims=True)
    out = y * lax.rsqrt(ms + RMS_EPS) * fw_ref[...]

    @pl.when(i < nmain)
    def _():
        yp_ref[...] = out

    @pl.when(i == nmain)
    def _():
        ye_ref[...] = out


def _combine(dst, h1, info, fw, eo):
    nmain = T_MAIN // TM_DENSE
    return pl.pallas_call(
        _combine_kernel,
        grid=(T_ALL // TM_DENSE,),
        in_specs=[pl.BlockSpec((1, 1, 2 * TM_DENSE), lambda i: (i, 0, 0), memory_space=pltpu.SMEM),
                  pl.BlockSpec((1, 1, 2 * TM_DENSE), lambda i: (jnp.minimum(i + 1, nmain), 0, 0),
                               memory_space=pltpu.SMEM),
                  pl.BlockSpec((TM_DENSE, D_MODEL), lambda i: (i, 0)),
                  pl.BlockSpec((TM_DENSE, LANES), lambda i: (i, 0)),
                  pl.BlockSpec((1, D_MODEL), lambda i: (0, 0)),
                  pl.BlockSpec(memory_space=pl.ANY)],
        out_specs=[pl.BlockSpec((TM_DENSE, D_MODEL), lambda i: (jnp.minimum(i, nmain - 1), 0)),
                   pl.BlockSpec((T_EXT, D_MODEL), lambda i: (0, 0))],
        out_shape=[jax.ShapeDtypeStruct((T_MAIN, D_MODEL), F32),
                   jax.ShapeDtypeStruct((T_EXT, D_MODEL), F32)],
        scratch_shapes=[pltpu.VMEM((2, 2 * TM_DENSE, D_MODEL), F32), pltpu.SemaphoreType.DMA((2,))],
        compiler_params=_cparams(),
        name="moe_combine",
    )(dst, dst, h1, info, fw, eo)


def kernel(x_prompt, x_sample, state_rwkv, state_shift, state_gla, meta_tokens, norm1_w, w_in, mu_shift, rw_w0,
           rw_w2, rw_a0, rw_a2, rw_g2, rw_k_k, rw_k_a, rw_r_k, rw_lnx_w, rw_lnx_b, gla_gk_up, gla_gk_b, gla_norm_w,
           p_rwkv, p_gla, w_out, norm2_w, moe_w_group, moe_b_group, moe_w_router, moe_b_router, moe_w1, moe_w3,
           moe_w2, final_norm_w):
    w_t = w_in[0].T
    row = lambda t: t.reshape(1, -1)
    w2a = jnp.zeros((LANES, 2 * RW_WIDTH), F32)
    w2a = w2a.at[0:64, 0:RW_WIDTH].set(rw_w2[0]).at[64:128, RW_WIDTH:].set(rw_a2[0])
    rw = dict(mu=row(mu_shift[0][_SHIFT_PERM]), w0=row(rw_w0[0]), w2a=w2a, a0=row(rw_a0[0]), g2=rw_g2[0],
              k_k=row(rw_k_k[0]), k_a=row(rw_k_a[0]), r_k=row(rw_r_k[0]), lnx_w=row(rw_lnx_w[0]),
              lnx_b=row(rw_lnx_b[0]))
    gk_up = jnp.zeros((LANES, GLA_KDIM), F32).at[0:GLA_LORA].set(gla_gk_up[0])
    gl = dict(gk_up=gk_up, gk_b=row(gla_gk_b[0]), norm_w=row(gla_norm_w[0]))
    w_route = jnp.zeros((D_MODEL, LANES), F32)
    w_route = w_route.at[:, 0:N_EXPERTS].set(moe_w_router[0]).at[:, N_EXPERTS:N_EXPERTS + N_GROUPS].set(moe_w_group[0])
    wr_hi = w_route.astype(BF16)
    wr_lo = (w_route - wr_hi.astype(F32)).astype(BF16)
    b_route = jnp.zeros((1, LANES), F32)
    b_route = b_route.at[0, 0:N_EXPERTS].set(moe_b_router[0]).at[0, N_EXPERTS:N_EXPERTS + N_GROUPS].set(moe_b_group[0])
    mix = dict(p_rwkv=p_rwkv[0].astype(BF16), p_gla=p_gla[0].astype(BF16), w_out=w_out[0].astype(BF16),
               norm2_w=row(norm2_w[0]), wr_hi=wr_hi, wr_lo=wr_lo, b_route=b_route)

    xp = x_prompt.reshape(T_MAIN, D_MODEL)
    xe = jnp.concatenate([x_sample[:, 0, :], jnp.zeros((META_PAD, D_MODEL), F32), meta_tokens,
                          jnp.zeros((T_ALL - ROW_META - CHUNK, D_MODEL), F32)], axis=0)

    xn = _norm1(xp, xe, row(norm1_w[0]))
    zr = _proj(xn, w_t, _RWKV_TILES, 1664, F32, name="proj_rwkv")
    zg = _proj(xn, w_t, _GLA_TILES, 1152, F32, name="proj_gla")
    zb = _proj(xn, w_t, _GATE_TILES, 1024, BF16, silu_from=4, name="proj_gates")

    o_r, h_fin = _rwkv_prompt(zr, rw)
    o_g, s_fin = _gla_prompt(zg, zb, gl)

    r_s, kh_s, v_s, g_s, r_t, w_t, kh_t, v_t, nkk_t, b_t = _rwkv_sample_pre(zr, state_shift[0][:, _SHIFT_PERM], rw)
    rw_new_t, y_t = _rwkv_sample_step(state_rwkv[0].transpose(1, 2, 3, 0), r_t, w_t, kh_t, v_t, nkk_t, b_t)
    rw_new = rw_new_t.transpose(3, 0, 1, 2)
    o_r_ext = _rwkv_sample_post(y_t, r_s, kh_s, v_s, g_s, rw)
    q_s, k_s, eg_s = _gla_sample_pre(zg, gl)
    v_gs = zg[ROW_SAMPLE:ROW_SAMPLE + N_SAMPLE, 2 * GLA_KDIM:2 * GLA_KDIM + GLA_WIDTH]
    gla_new, og_s = _gla_sample_step(state_gla[0], q_s, k_s, eg_s, v_gs)
    o_g_ext = _gla_sample_post(og_s, zb, gl)

    h1, xn2, info, counts = _mix_route(o_r, o_r_ext, o_g, o_g_ext, zb, xp, xe, mix)
    xs, dst, tile_expert, n_tiles = _dispatch(info, counts, xn2)
    eo = _moe(tile_expert[0, :NT_MOE], n_tiles[0, :1], xs, moe_w1[0], moe_w3[0], moe_w2[0])
    y_p, y_e = _combine(dst, h1, info, row(final_norm_w), eo)

    y_prompt = y_p.reshape(N_BATCH, SEQ, D_MODEL)
    y_sample = y_e[0:N_SAMPLE].reshape(N_SAMPLE, 1, D_MODEL)
    new_rwkv_prompt = h_fin.reshape(N_BATCH, RW_HEAD, RW_HEADS, RW_HEAD).transpose(0, 2, 3, 1)[None]
    shift_rows = jnp.concatenate([zr[(b + 1) * SEQ - 1:(b + 1) * SEQ] for b in range(N_BATCH)]
                                 + [zr[ROW_SAMPLE:ROW_SAMPLE + N_SAMPLE]], axis=0)
    shift_rows = shift_rows[:, _SHIFT_INV]
    return (y_prompt, y_sample, new_rwkv_prompt, shift_rows[None, 0:N_BATCH], s_fin[None],
            rw_new[None], shift_rows[None, N_BATCH:], gla_new[None])
```

```python
import functools
import math

import numpy as np
import jax
import jax.numpy as jnp
from jax import lax
from jax.experimental import pallas as pl
from jax.experimental.pallas import tpu as pltpu

F32 = jnp.float32
BF16 = jnp.bfloat16

D_MODEL = 2048
N_BATCH = 4
SEQ = 2048
N_SAMPLE = 128
N_META = 16
RMS_EPS = 1e-6

RW_WIDTH = 1024
RW_HEAD = 64
RW_HEADS = 16
RW_GN_EPS = RW_HEAD * 1e-5
W_SHIFT = 3328
GLA_HEADS = 4
GLA_DK = 128
GLA_DV = 256
GLA_KDIM = 512
GLA_WIDTH = 1024
GLA_LORA = 16
GLA_NORMALIZER = 16.0
W_GLA_PAD = 2304
OG_BLOCK = 2 * D_MODEL // GLA_WIDTH
N_GROUPS = 4
EXPERTS_PER_GROUP = 8
N_EXPERTS = 32
D_EXPERT = 512

CHUNK = 64
N_CHUNKS = SEQ // CHUNK
META_PAD = CHUNK - N_META
T_MAIN = N_BATCH * SEQ
ROW_SAMPLE = T_MAIN
ROW_META = T_MAIN + N_SAMPLE
T_EXT = 256
T_ALL = T_MAIN + T_EXT
LANES = 128

TM_DENSE = 256
TM_MM = 1056
TM_MOE = 256
NT_MOE = (2 * T_ALL) // TM_MOE + N_EXPERTS
NEG_BIG = -1e30

VMEM_LIMIT = 56 * 1024 * 1024

_SHIFT_PERM = np.concatenate([np.arange(0, 1024), np.arange(1088, 2112), np.arange(2112, 3136),
                              np.arange(1024, 1088), np.arange(3136, 3328)])
_SHIFT_INV = np.argsort(_SHIFT_PERM)


def _cparams(n_axes=1):
    return pltpu.CompilerParams(dimension_semantics=("arbitrary",) * n_axes, vmem_limit_bytes=VMEM_LIMIT)


def _bdot(a, b):
    return jnp.dot(a.astype(BF16), b.astype(BF16), preferred_element_type=F32)


def _bdot_nt(a, b):
    return lax.dot_general(a.astype(BF16), b.astype(BF16), (((1,), (1,)), ((), ())), preferred_element_type=F32)


def _bdot_tn(a, b):
    return lax.dot_general(a.astype(BF16), b.astype(BF16), (((0,), (0,)), ((), ())), preferred_element_type=F32)


def _split_dot(m_bf16, x):
    hi = x.astype(BF16)
    lo = (x - hi.astype(F32)).astype(BF16)
    return (jnp.dot(m_bf16, hi, preferred_element_type=F32) + jnp.dot(m_bf16, lo, preferred_element_type=F32))


def _iota(shape, dim):
    return lax.broadcasted_iota(jnp.int32, shape, dim)


def _tri_incl(n):
    return (_iota((n, n), 0) >= _iota((n, n), 1)).astype(BF16)


def _seg_sum(x, width):
    m, n = x.shape
    nb = n // LANES
    bd = ((_iota((LANES, LANES), 0) // width) == (_iota((LANES, LANES), 1) // width)).astype(BF16)
    xs = jnp.concatenate([x[:, j * LANES:(j + 1) * LANES] for j in range(nb)], axis=0)
    hi = xs.astype(BF16)
    lo = (xs - hi.astype(F32)).astype(BF16)
    s = jnp.dot(hi, bd, preferred_element_type=F32) + jnp.dot(lo, bd, preferred_element_type=F32)
    return jnp.concatenate([s[j * m:(j + 1) * m] for j in range(nb)], axis=1)


def _col_of_row(row):
    n = row.shape[-1]
    eye = _iota((n, n), 0) == _iota((n, n), 1)
    return jnp.sum(jnp.where(eye, jnp.broadcast_to(row, (n, n)), 0.0), axis=-1, keepdims=True)


def _norm1_kernel(xp_ref, xe_ref, w_ref, o_ref):
    i = pl.program_id(0)

    def f(x):
        ms = jnp.mean(x * x, axis=-1, keepdims=True)
        return (x * lax.rsqrt(ms + RMS_EPS) * w_ref[...]).astype(BF16)

    @pl.when(i < T_MAIN // TM_DENSE)
    def _():
        o_ref[...] = f(xp_ref[...])

    @pl.when(i == T_MAIN // TM_DENSE)
    def _():
        o_ref[...] = f(xe_ref[...])


def _norm1(xp, xe, w):
    nmain = T_MAIN // TM_DENSE
    return pl.pallas_call(
        _norm1_kernel,
        grid=(T_ALL // TM_DENSE,),
        in_specs=[pl.BlockSpec((TM_DENSE, D_MODEL), lambda i: (jnp.minimum(i, nmain - 1), 0)),
                  pl.BlockSpec((T_EXT, D_MODEL), lambda i: (0, 0)),
                  pl.BlockSpec((1, D_MODEL), lambda i: (0, 0))],
        out_specs=pl.BlockSpec((TM_DENSE, D_MODEL), lambda i: (i, 0)),
        out_shape=jax.ShapeDtypeStruct((T_ALL, D_MODEL), BF16),
        compiler_params=_cparams(),
        name="norm1",
    )(xp, xe, w)


def _proj_kernel(x_ref, wt_hbm, o_ref, wraw, wbf, sem, *, tiles, silu_from):
    j = pl.program_id(0)
    m = pl.program_id(1)
    tn = wbf.shape[0]

    def copies(jj):
        return [pltpu.make_async_copy(wt_hbm.at[pl.ds(src, n), :], wraw.at[pl.ds(dst, n), :], sem)
                for (src, n, dst) in tiles[jj]]

    @pl.when(jnp.logical_and(j == 0, m == 0))
    def _():
        for c in copies(0):
            c.start()

    for jj in range(len(tiles)):
        @pl.when(jnp.logical_and(j == jj, m == 0))
        def _(jj=jj):
            for c in copies(jj):
                c.wait()
            edge = 0
            for lo, hi in sorted((dst, dst + n) for (_, n, dst) in tiles[jj]) + [(tn, tn)]:
                if lo > edge:
                    wbf[edge:lo, :] = jnp.zeros((lo - edge, wbf.shape[1]), BF16)
                if hi > lo:
                    wbf[lo:hi, :] = wraw[lo:hi, :].astype(BF16)
                edge = hi
            if jj + 1 < len(tiles):
                for c in copies(jj + 1):
                    c.start()

    z = lax.dot_general(x_ref[...], wbf[...], (((1,), (1,)), ((), ())), preferred_element_type=F32)
    if silu_from is None:
        o_ref[...] = z.astype(o_ref.dtype)
    else:
        s = jax.nn.sigmoid(z)
        o_ref[...] = jnp.where(j >= silu_from, z * s, s).astype(o_ref.dtype)


def _proj(x, w_t, tiles, tn, out_dtype, silu_from=None, name="proj"):
    t, k = x.shape
    return pl.pallas_call(
        functools.partial(_proj_kernel, tiles=tiles, silu_from=silu_from),
        grid=(len(tiles), t // TM_MM),
        in_specs=[pl.BlockSpec((TM_MM, k), lambda j, m: (m, 0)),
                  pl.BlockSpec(memory_space=pl.ANY)],
        out_specs=pl.BlockSpec((TM_MM, tn), lambda j, m: (m, j)),
        out_shape=jax.ShapeDtypeStruct((t, tn * len(tiles)), out_dtype),
        scratch_shapes=[pltpu.VMEM((tn, k), F32), pltpu.VMEM((tn, k), BF16), pltpu.SemaphoreType.DMA(())],
        compiler_params=_cparams(2),
        name=name,
    )(x, w_t)


_RWKV_TILES = (((0, 1024, 0), (1088, 640, 1024)),
               ((1728, 1408, 0), (1024, 64, 1408), (3136, 192, 1472)))
_GLA_TILES = (((3328, 1152, 0),), ((4480, 912, 0),))
_GATE_TILES = tuple(((6416 + 1024 * j, 1024, 0),) for j in range(4)) + (((5392, 1024, 0),),)


def _rwkv_pre(z, w0, w2a, a0, g2, k_k, k_a):
    r = z[:, 0:1024]
    k = z[:, 1024:2048]
    v = z[:, 2048:3072]
    wa = z[:, 3072:3200]
    gd = z[:, 3200:3328]
    lane = _iota(wa.shape, 1)
    wa = jnp.where(lane < 64, jnp.tanh(wa), wa)
    up = _bdot(wa, w2a)
    lw = -math.exp(-0.5) * jax.nn.sigmoid(w0 + up[:, :1024])
    a = jax.nn.sigmoid(a0 + up[:, 1024:])
    g = _bdot(jax.nn.sigmoid(gd), g2)
    kk = k * k_k
    kk = kk * lax.rsqrt(jnp.maximum(_seg_sum(kk * kk, RW_HEAD), 1e-24))
    k_h = k * (1.0 + (a - 1.0) * k_a)
    return r, lw, k_h, v, kk, kk * a, g


def _rwkv_post(y, r, k_h, v, g, r_k, lnx_w, lnx_b):
    mean = _seg_sum(y, RW_HEAD) * (1.0 / RW_HEAD)
    d = y - mean
    var = _seg_sum(d * d, RW_HEAD) * (1.0 / RW_HEAD)
    yn = d * lax.rsqrt(var + RW_GN_EPS) * lnx_w + lnx_b
    bonus = _seg_sum(r * k_h * r_k, RW_HEAD) * v
    return (yn + bonus) * g


def _stack2(x):
    lane = _iota(x.shape, 1)
    return jnp.concatenate([jnp.where(lane < RW_HEAD, x, 0.0), jnp.where(lane >= RW_HEAD, x, 0.0)], axis=0)


def _unstack2(x):
    c = x.shape[0] // 2
    return x[:c] + x[c:]


def _prompt_chunk_kernel(zs_ref, mu_ref, w0_ref, w2a_ref, a0_ref, g2_ref, kk_ref, ka_ref, rk_ref, lw_ref, lb_ref,
                         zg_ref, og_ref, gkup_ref, gkb_ref, nw_ref,
                         o_ref, hout_ref, go_ref, sout_ref,
                         prev_sc, h_sc, hmeta_sc, prevmeta_sc, y_sc, s_sc, smeta_sc, *, n_chunks):
    i = pl.program_id(0)
    is_meta = i == 0
    c = lax.rem(jnp.maximum(i - 1, 0), n_chunks)
    first = jnp.logical_and(i >= 1, c == 0)
    last = jnp.logical_and(i >= 1, c == n_chunks - 1)

    @pl.when(is_meta)
    def _():
        h_sc[...] = jnp.zeros_like(h_sc)
        prev_sc[...] = jnp.zeros_like(prev_sc)
        s_sc[...] = jnp.zeros_like(s_sc)

    @pl.when(first)
    def _():
        h_sc[...] = hmeta_sc[...]
        prev_sc[...] = prevmeta_sc[...]
        s_sc[...] = smeta_sc[...]

    rowi = _iota((CHUNK, 1), 0)
    pad_row = jnp.logical_and(is_meta, rowi < META_PAD)
    heads = range(GLA_HEADS)
    ks = [slice(h * GLA_DK, (h + 1) * GLA_DK) for h in heads]
    gla = {}

    def gla_prep():
        zg = zg_ref[...]
        logg = jnp.where(pad_row, 0.0, _gla_logg(zg, gkup_ref[...], gkb_ref[...]))
        bcum = _split_dot(_tri_incl(CHUNK), logg)
        b_end = bcum[CHUNK - 1:CHUNK, :]
        gla['qt'] = zg[:, 0:GLA_KDIM] * (GLA_DK ** -0.5) * jnp.exp(bcum)
        gla['kt'] = zg[:, GLA_KDIM:2 * GLA_KDIM] * jnp.exp(-bcum)
        gla['ke'] = zg[:, GLA_KDIM:2 * GLA_KDIM] * jnp.exp(b_end - bcum)
        gla['e_end'] = jnp.exp(b_end)
        gla['vh'] = [zg[:, 2 * GLA_KDIM + h * GLA_DV:2 * GLA_KDIM + (h + 1) * GLA_DV].astype(BF16) for h in heads]

    def gla_scores():
        causal = _iota((CHUNK, CHUNK), 0) >= _iota((CHUNK, CHUNK), 1)
        gla['a'] = [jnp.where(causal, _bdot_nt(gla['qt'][:, ks[h]], gla['kt'][:, ks[h]]), 0.0) for h in heads]
        gla['s0'] = [s_sc[h] for h in heads]

    def gla_state():
        gla['o_inter'] = [_bdot(gla['qt'][:, ks[h]], gla['s0'][h]) for h in heads]
        gla['s_add'] = [_bdot_tn(gla['ke'][:, ks[h]], gla['vh'][h]) for h in heads]

    def gla_finish():
        outs = [_bdot(gla['a'][h], gla['vh'][h]) + gla['o_inter'][h] for h in heads]
        gla['s_new'] = [_col_of_row(gla['e_end'][:, ks[h]]) * gla['s0'][h] + gla['s_add'][h] for h in heads]
        for h in heads:
            s_sc[h] = gla['s_new'][h]
        go = _gla_post(jnp.concatenate(outs, axis=1), og_ref[...].astype(F32), nw_ref[...])
        go_ref[...] = go.astype(go_ref.dtype)

    zs = zs_ref[...]
    sh = pltpu.roll(zs, 1, 0)
    sh = jnp.where(rowi == 0, prev_sc[...], sh)
    prev_sc[...] = zs[CHUNK - 1:CHUNK, :]
    z = zs + (sh - zs) * mu_ref[...]
    r, lw, k_h, v, kk, b, g = _rwkv_pre(z, w0_ref[...], w2a_ref[...], a0_ref[...], g2_ref[...], kk_ref[...],
                                        ka_ref[...])
    lw = jnp.where(pad_row, 0.0, lw)
    gla_prep()

    cl = _split_dot(_tri_incl(CHUNK), lw)
    cl_end = cl[CHUNK - 1:CHUNK, :]
    e_neg = jnp.exp(-cl)
    e_end = jnp.exp(cl_end - cl)
    kkt = kk * jnp.exp(cl - lw)
    rt = r * jnp.exp(cl)
    bt = b * e_neg
    kt = k_h * e_neg
    bh = b * e_end
    kh = k_h * e_end
    e_c = jnp.exp(cl_end)

    n2 = 2 * CHUNK
    tok_r = jnp.bitwise_and(_iota((n2, n2), 0), CHUNK - 1)
    tok_c = jnp.bitwise_and(_iota((n2, n2), 1), CHUNK - 1)
    strict = tok_r > tok_c
    incl = tok_r >= tok_c

    pairs = range(RW_HEADS // 2)
    sls = [slice(j * LANES, (j + 1) * LANES) for j in pairs]
    s_kkt = [_stack2(kkt[:, sl]) for sl in sls]
    s_rt = [_stack2(rt[:, sl]) for sl in sls]
    s_v = [_stack2(v[:, sl]).astype(BF16) for sl in sls]
    rb = [jnp.concatenate([_stack2(bt[:, sl]), _stack2(kt[:, sl])], axis=0).astype(BF16) for sl in sls]
    aa = [_bdot_nt(s_kkt[j], rb[j]) for j in pairs]
    mm = [_bdot_nt(s_rt[j], rb[j]) for j in pairs]
    h_kv = [_bdot_tn(_stack2(kh[:, sls[j]]), s_v[j]) for j in pairs]
    gla_scores()
    a_b =[jnp.where(strict, aa[j][:, :n2], 0.0).astype(BF16) for j in pairs]
    a_k = [jnp.where(strict, aa[j][:, n2:], 0.0) for j in pairs]
    m_rb = [jnp.where(incl, mm[j][:, :n2], 0.0).astype(BF16) for j in pairs]
    m_rk = [jnp.where(incl, mm[j][:, n2:], 0.0) for j in pairs]
    akv = [_bdot(a_k[j], s_v[j]) for j in pairs]
    y_kv = [_bdot(m_rk[j], s_v[j]) for j in pairs]
    x = [jnp.concatenate([s_kkt[j], akv[j]], axis=1) for j in pairs]
    p = a_b
    x = [x[j] - _bdot(p[j], x[j]) for j in pairs]
    for step in range(5):
        p = [_bdot(p[j], p[j]).astype(BF16) for j in pairs]
        x = [x[j] + _bdot(p[j], x[j]) for j in pairs]
        if step == 1:
            gla_state()
        if step == 3:
            gla_finish()
    zq = [_bdot(m_rb[j], x[j]) for j in pairs]
    zb = [_bdot_tn(_stack2(bh[:, sls[j]]), x[j]) for j in pairs]
    h0 = [h_sc[j] for j in pairs]
    for j in pairs:
        q_eff = _unstack2(s_rt[j] - zq[j][:, :LANES])
        y_in = _unstack2(y_kv[j] - zq[j][:, LANES:])
        y_sc[:, sls[j]] = _bdot(q_eff, h0[j]) + y_in
    h_new = [_col_of_row(e_c[:, sls[j]]) * h0[j] - _bdot(zb[j][:, :LANES], h0[j]) + (h_kv[j] - zb[j][:, LANES:])
             for j in pairs]
    for j in pairs:
        h_sc[j] = h_new[j]

    o = _rwkv_post(y_sc[...], r, k_h, v, g, rk_ref[...], lw_ref[...], lb_ref[...])
    o_ref[...] = o.astype(o_ref.dtype)

    @pl.when(last)
    def _():
        for j in pairs:
            hout_ref[0, :, sls[j]] = _unstack2(h_new[j])
        for h in heads:
            sout_ref[0, h] = gla['s_new'][h]

    @pl.when(is_meta)
    def _():
        hmeta_sc[...] = h_sc[...]
        prevmeta_sc[...] = prev_sc[...]
        smeta_sc[...] = s_sc[...]


def _prompt_recurrences(zr, zg, zb, p, q, n_batch=N_BATCH, n_chunks=N_CHUNKS, meta_block=ROW_META // CHUNK,
                        og_block=OG_BLOCK):
    row = lambda n: pl.BlockSpec((1, n), lambda i: (0, 0))
    full = lambda a, b: pl.BlockSpec((a, b), lambda i: (0, 0))
    blk = lambda i: jnp.where(i == 0, meta_block, i - 1)
    out_blk = lambda i: jnp.maximum(i - 1, 0)
    seq = lambda i: jnp.maximum(i - 1, 0) // n_chunks
    n_rows = n_batch * n_chunks * CHUNK
    return pl.pallas_call(
        functools.partial(_prompt_chunk_kernel, n_chunks=n_chunks),
        grid=(1 + n_batch * n_chunks,),
        in_specs=[pl.BlockSpec((CHUNK, W_SHIFT), lambda i: (blk(i), 0)),
                  row(W_SHIFT), row(RW_WIDTH), full(LANES, 2 * RW_WIDTH), row(RW_WIDTH), full(LANES, RW_WIDTH),
                  row(RW_WIDTH), row(RW_WIDTH), row(RW_WIDTH), row(RW_WIDTH), row(RW_WIDTH),
                  pl.BlockSpec((CHUNK, W_GLA_PAD), lambda i: (blk(i), 0)),
                  pl.BlockSpec((CHUNK, GLA_WIDTH), lambda i: (blk(i), og_block)),
                  full(LANES, GLA_KDIM), full(1, GLA_KDIM), full(1, GLA_DV)],
        out_specs=[pl.BlockSpec((CHUNK, RW_WIDTH), lambda i: (out_blk(i), 0)),
                   pl.BlockSpec((1, RW_HEAD, RW_WIDTH), lambda i: (seq(i), 0, 0)),
                   pl.BlockSpec((CHUNK, GLA_WIDTH), lambda i: (out_blk(i), 0)),
                   pl.BlockSpec((1, GLA_HEADS, GLA_DK, GLA_DV), lambda i: (seq(i), 0, 0, 0))],
        out_shape=[jax.ShapeDtypeStruct((n_rows, RW_WIDTH), BF16),
                   jax.ShapeDtypeStruct((n_batch, RW_HEAD, RW_WIDTH), F32),
                   jax.ShapeDtypeStruct((n_rows, GLA_WIDTH), BF16),
                   jax.ShapeDtypeStruct((n_batch, GLA_HEADS, GLA_DK, GLA_DV), F32)],
        scratch_shapes=[pltpu.VMEM((1, W_SHIFT), F32),
                        pltpu.VMEM((RW_HEADS // 2, LANES, LANES), F32),
                        pltpu.VMEM((RW_HEADS // 2, LANES, LANES), F32),
                        pltpu.VMEM((1, W_SHIFT), F32),
                        pltpu.VMEM((CHUNK, RW_WIDTH), F32),
                        pltpu.VMEM((GLA_HEADS, GLA_DK, GLA_DV), F32),
                        pltpu.VMEM((GLA_HEADS, GLA_DK, GLA_DV), F32)],
        compiler_params=_cparams(),
        name="prompt_recurrences",
    )(zr, p['mu'], p['w0'], p['w2a'], p['a0'], p['g2'], p['k_k'], p['k_a'], p['r_k'], p['lnx_w'], p['lnx_b'],
      zg, zb, q['gk_up'], q['gk_b'], q['norm_w'])


def _rwkv_sample_pre_kernel(zs_ref, prev_ref, mu_ref, w0_ref, w2a_ref, a0_ref, g2_ref, kk_ref, ka_ref,
                            r_ref, kh_ref, v_ref, g_ref, rt_ref, wt_ref, kht_ref, vt_ref, nkkt_ref, bt_ref):
    zs = zs_ref[...]
    z = zs + (prev_ref[...] - zs) * mu_ref[...]
    r, lw, k_h, v, kk, b, g = _rwkv_pre(z, w0_ref[...], w2a_ref[...], a0_ref[...], g2_ref[...], kk_ref[...],
                                        ka_ref[...])
    r_ref[...] = r
    kh_ref[...] = k_h
    v_ref[...] = v
    g_ref[...] = g
    rt_ref[...] = r.T
    wt_ref[...] = jnp.exp(lw).T
    kht_ref[...] = k_h.T
    vt_ref[...] = v.T
    nkkt_ref[...] = (-kk).T
    bt_ref[...] = b.T


def _rwkv_sample_pre(zr, prev, p):
    row = lambda n: pl.BlockSpec((1, n), lambda i: (0, 0))
    full = lambda a, b: pl.BlockSpec((a, b), lambda i: (0, 0))
    vec = jax.ShapeDtypeStruct((N_SAMPLE, RW_WIDTH), F32)
    vec_t = jax.ShapeDtypeStruct((RW_WIDTH, N_SAMPLE), F32)
    return pl.pallas_call(
        _rwkv_sample_pre_kernel,
        grid=(1,),
        in_specs=[pl.BlockSpec((N_SAMPLE, W_SHIFT), lambda i: (ROW_SAMPLE // N_SAMPLE, 0)),
                  full(N_SAMPLE, W_SHIFT),
                  row(W_SHIFT), row(RW_WIDTH), full(LANES, 2 * RW_WIDTH), row(RW_WIDTH), full(LANES, RW_WIDTH),
                  row(RW_WIDTH), row(RW_WIDTH)],
        out_specs=[full(N_SAMPLE, RW_WIDTH)] * 4 + [full(RW_WIDTH, N_SAMPLE)] * 6,
        out_shape=[vec] * 4 + [vec_t] * 6,
        compiler_params=_cparams(),
        name="rwkv_sample_pre",
    )(zr, prev, p['mu'], p['w0'], p['w2a'], p['a0'], p['g2'], p['k_k'], p['k_a'])


def _rwkv_sample_step_kernel(s_ref, r_ref, w_ref, kh_ref, v_ref, nkk_ref, b_ref, so_ref, y_ref):
    s = s_ref[0]
    sa = jnp.sum(s * nkk_ref[0][None], axis=1, keepdims=True)
    s_new = s * w_ref[0][None] + sa * b_ref[0][None] + v_ref[0] * kh_ref[0][None]
    so_ref[0] = s_new
    y_ref[0] = jnp.sum(s_new * r_ref[0][None], axis=1, keepdims=True)


def _rwkv_sample_step(state_t, r_t, w_t, kh_t, v_t, nkk_t, b_t):
    kvec = lambda t: t.reshape(RW_HEADS, RW_HEAD, N_SAMPLE)
    kspec = pl.BlockSpec((1, RW_HEAD, N_SAMPLE), lambda i: (i, 0, 0))
    vspec = pl.BlockSpec((1, RW_HEAD, 1, N_SAMPLE), lambda i: (i, 0, 0, 0))
    sspec = pl.BlockSpec((1, RW_HEAD, RW_HEAD, N_SAMPLE), lambda i: (i, 0, 0, 0))
    s_new, y = pl.pallas_call(
        _rwkv_sample_step_kernel,
        grid=(RW_HEADS,),
        in_specs=[sspec, kspec, kspec, kspec, vspec, kspec, kspec],
        out_specs=[sspec, vspec],
        out_shape=[jax.ShapeDtypeStruct(state_t.shape, F32),
                   jax.ShapeDtypeStruct((RW_HEADS, RW_HEAD, 1, N_SAMPLE), F32)],
        compiler_params=_cparams(),
        name="rwkv_sample_step",
    )(state_t, kvec(r_t), kvec(w_t), kvec(kh_t), v_t.reshape(RW_HEADS, RW_HEAD, 1, N_SAMPLE), kvec(nkk_t), kvec(b_t))
    return s_new, y.reshape(RW_WIDTH, N_SAMPLE)


def _rwkv_sample_post_kernel(yt_ref, r_ref, kh_ref, v_ref, g_ref, rk_ref, lw_ref, lb_ref, o_ref):
    o = _rwkv_post(yt_ref[...].T, r_ref[...], kh_ref[...], v_ref[...], g_ref[...], rk_ref[...], lw_ref[...],
                   lb_ref[...])
    o_ref[0:N_SAMPLE, :] = o.astype(o_ref.dtype)
    o_ref[N_SAMPLE:T_EXT, :] = jnp.zeros((T_EXT - N_SAMPLE, RW_WIDTH), o_ref.dtype)


def _rwkv_sample_post(y_t, r, kh, v, g, p):
    row = lambda n: pl.BlockSpec((1, n), lambda i: (0, 0))
    full = lambda a, b: pl.BlockSpec((a, b), lambda i: (0, 0))
    return pl.pallas_call(
        _rwkv_sample_post_kernel,
        grid=(1,),
        in_specs=[full(RW_WIDTH, N_SAMPLE)] + [full(N_SAMPLE, RW_WIDTH)] * 4 + [row(RW_WIDTH)] * 3,
        out_specs=full(T_EXT, RW_WIDTH),
        out_shape=jax.ShapeDtypeStruct((T_EXT, RW_WIDTH), BF16),
        compiler_params=_cparams(),
        name="rwkv_sample_post",
    )(y_t, r, kh, v, g, p['r_k'], p['lnx_w'], p['lnx_b'])


def _gla_logg(zg, gk_up, gk_b):
    gkd = zg[:, 2 * GLA_KDIM + GLA_WIDTH:2 * GLA_KDIM + GLA_WIDTH + LANES]
    x = _bdot(gkd, gk_up) + gk_b
    return (jnp.minimum(x, 0.0) - jnp.log(1.0 + jnp.exp(-jnp.abs(x)))) * (1.0 / GLA_NORMALIZER)


def _gla_post(o, og_act, norm_w):
    outs = []
    for h in range(GLA_HEADS):
        oh = o[:, h * GLA_DV:(h + 1) * GLA_DV]
        ms = jnp.mean(oh * oh, axis=-1, keepdims=True)
        outs.append(oh * lax.rsqrt(ms + RMS_EPS) * norm_w)
    return jnp.concatenate(outs, axis=1) * og_act


def _gla_sample_pre_kernel(zg_ref, gkup_ref, gkb_ref, q_ref, k_ref, eg_ref):
    zg = zg_ref[...]
    logg = _gla_logg(zg, gkup_ref[...], gkb_ref[...])
    q_ref[...] = zg[:, 0:GLA_KDIM] * (GLA_DK ** -0.5)
    k_ref[...] = zg[:, GLA_KDIM:2 * GLA_KDIM]
    eg_ref[...] = jnp.exp(logg)


def _gla_sample_pre(zg, p):
    full = lambda a, b: pl.BlockSpec((a, b), lambda i: (0, 0))
    vec = jax.ShapeDtypeStruct((N_SAMPLE, GLA_KDIM), F32)
    return pl.pallas_call(
        _gla_sample_pre_kernel,
        grid=(1,),
        in_specs=[pl.BlockSpec((N_SAMPLE, W_GLA_PAD), lambda i: (ROW_SAMPLE // N_SAMPLE, 0)),
                  full(LANES, GLA_KDIM), full(1, GLA_KDIM)],
        out_specs=[full(N_SAMPLE, GLA_KDIM)] * 3,
        out_shape=[vec] * 3,
        compiler_params=_cparams(),
        name="gla_sample_pre",
    )(zg, p['gk_up'], p['gk_b'])


GLA_STEP_BATCH = 8


def _gla_sample_step_kernel(s_ref, q_ref, k_ref, eg_ref, v_ref, so_ref, o_ref):
    eye = _iota((GLA_DK, GLA_DK), 0) == _iota((GLA_DK, GLA_DK), 1)
    for b in range(GLA_STEP_BATCH):
        s = s_ref[b]
        col = lambda ref: jnp.sum(jnp.where(eye, ref[b], 0.0), axis=-1, keepdims=True)
        s_new = col(eg_ref) * s + col(k_ref) * v_ref[b]
        so_ref[b] = s_new
        o_ref[b] = jnp.sum(col(q_ref) * s_new, axis=-2, keepdims=True)


def _gla_sample_step(state, q, k, eg, v):
    kv = lambda t: t.reshape(N_SAMPLE, GLA_HEADS, 1, GLA_DK)
    kspec = pl.BlockSpec((GLA_STEP_BATCH, GLA_HEADS, 1, GLA_DK), lambda i: (i, 0, 0, 0))
    vspec = pl.BlockSpec((GLA_STEP_BATCH, GLA_HEADS, 1, GLA_DV), lambda i: (i, 0, 0, 0))
    sspec = pl.BlockSpec((GLA_STEP_BATCH, GLA_HEADS, GLA_DK, GLA_DV), lambda i: (i, 0, 0, 0))
    s_new, o = pl.pallas_call(
        _gla_sample_step_kernel,
        grid=(N_SAMPLE // GLA_STEP_BATCH,),
        in_specs=[sspec, kspec, kspec, kspec, vspec],
        out_specs=[sspec, vspec],
        out_shape=[jax.ShapeDtypeStruct(state.shape, F32),
                   jax.ShapeDtypeStruct((N_SAMPLE, GLA_HEADS, 1, GLA_DV), F32)],
        compiler_params=_cparams(),
        name="gla_sample_step",
    )(state, kv(q), kv(k), kv(eg), v.reshape(N_SAMPLE, GLA_HEADS, 1, GLA_DV))
    return s_new, o.reshape(N_SAMPLE, GLA_WIDTH)


def _gla_sample_post_kernel(o_ref, og_ref, nw_ref, out_ref):
    o = _gla_post(o_ref[...], og_ref[0:N_SAMPLE, :].astype(F32), nw_ref[...])
    out_ref[0:N_SAMPLE, :] = o.astype(out_ref.dtype)
    out_ref[N_SAMPLE:T_EXT, :] = jnp.zeros((T_EXT - N_SAMPLE, GLA_WIDTH), out_ref.dtype)


def _gla_sample_post(o, zb, p, og_block=OG_BLOCK):
    full = lambda a, b: pl.BlockSpec((a, b), lambda i: (0, 0))
    return pl.pallas_call(
        _gla_sample_post_kernel,
        grid=(1,),
        in_specs=[full(N_SAMPLE, GLA_WIDTH),
                  pl.BlockSpec((T_EXT, GLA_WIDTH), lambda i: (T_MAIN // T_EXT, og_block)),
                  full(1, GLA_DV)],
        out_specs=full(T_EXT, GLA_WIDTH),
        out_shape=jax.ShapeDtypeStruct((T_EXT, GLA_WIDTH), BF16),
        compiler_params=_cparams(),
        name="gla_sample_post",
    )(o, zb, p['norm_w'])


def _mix_route_kernel(or_ref, ore_ref, og_ref, oge_ref, gates_ref, xp_ref, xe_ref, pr_ref, pg_ref, wo_ref, n2_ref,
                      wrh_ref, wrl_ref, br_ref, h1_ref, xn_ref, info_ref, cnt_ref, carry_sc):
    i = pl.program_id(0)
    is_main = i < T_MAIN // TM_DENSE

    @pl.when(i == 0)
    def _():
        carry_sc[...] = jnp.zeros_like(carry_sc)

    sig_r = gates_ref[:, 0:D_MODEL].astype(F32)
    sig_g = gates_ref[:, D_MODEL:2 * D_MODEL].astype(F32)
    o_r = jnp.where(is_main, or_ref[...], ore_ref[...])
    o_g = jnp.where(is_main, og_ref[...], oge_ref[...])
    m = (sig_r * jnp.dot(o_r, pr_ref[...], preferred_element_type=F32)
         + sig_g * jnp.dot(o_g, pg_ref[...], preferred_element_type=F32))
    h = jnp.where(is_main, xp_ref[...], xe_ref[...])
    h1 = h + jnp.dot(m.astype(BF16), wo_ref[...], preferred_element_type=F32)
    h1_ref[...] = h1
    ms = jnp.mean(h1 * h1, axis=-1, keepdims=True)
    xn = h1 * lax.rsqrt(ms + RMS_EPS) * n2_ref[...]
    xn_ref[...] = xn
    xh = xn.astype(BF16)
    xl = (xn - xh.astype(F32)).astype(BF16)
    lg = (jnp.dot(xh, wrh_ref[...], preferred_element_type=F32)
          + jnp.dot(xh, wrl_ref[...], preferred_element_type=F32)
          + jnp.dot(xl, wrh_ref[...], preferred_element_type=F32)) + br_ref[...]

    lane = _iota(lg.shape, 1)
    lanef = lane.astype(F32)
    is_g = jnp.logical_and(lane >= N_EXPERTS, lane < N_EXPERTS + N_GROUPS)
    gl = jnp.where(is_g, lg, NEG_BIG)
    gmax = jnp.max(gl, axis=-1, keepdims=True)
    gsel = jnp.min(jnp.where(jnp.logical_and(is_g, gl == gmax), lanef, 1e9), axis=-1, keepdims=True) - N_EXPERTS
    p_g = 1.0 / jnp.sum(jnp.exp(gl - gmax), axis=-1, keepdims=True)
    grp = (lane // EXPERTS_PER_GROUP).astype(F32)
    in_grp = jnp.logical_and(lane < N_EXPERTS, grp == gsel)
    el = jnp.where(in_grp, lg, NEG_BIG)
    m1 = jnp.max(el, axis=-1, keepdims=True)
    i1 = jnp.min(jnp.where(jnp.logical_and(in_grp, el == m1), lanef, 1e9), axis=-1, keepdims=True)
    in2 = jnp.logical_and(in_grp, lanef != i1)
    el2 = jnp.where(in2, lg, NEG_BIG)
    m2 = jnp.max(el2, axis=-1, keepdims=True)
    i2 = jnp.min(jnp.where(jnp.logical_and(in2, el2 == m2), lanef, 1e9), axis=-1, keepdims=True)
    e2 = jnp.exp(m2 - m1)
    w1 = p_g / (1.0 + e2)
    w2 = p_g * e2 / (1.0 + e2)

    oh1 = lanef == i1
    oh2 = lanef == i2
    cnt = jnp.where(jnp.logical_or(oh1, oh2), 1.0, 0.0)
    tm = cnt.shape[0]
    lstrict = (_iota((tm, tm), 0) > _iota((tm, tm), 1)).astype(BF16)
    before = jnp.dot(lstrict, cnt.astype(BF16), preferred_element_type=F32) + carry_sc[...]
    rank1 = jnp.sum(jnp.where(oh1, before, 0.0), axis=-1, keepdims=True)
    rank2 = jnp.sum(jnp.where(oh2, before, 0.0), axis=-1, keepdims=True)
    carry_sc[...] = carry_sc[...] + jnp.sum(cnt, axis=0, keepdims=True)
    cnt_ref[...] = carry_sc[...]
    info = jnp.where(lane == 0, i1, jnp.where(lane == 1, i2, jnp.where(lane == 2, w1, jnp.where(
        lane == 3, w2, jnp.where(lane == 4, rank1, jnp.where(lane == 5, rank2, 0.0))))))
    info_ref[...] = info


def _mix_route(o_r, o_r_ext, o_g, o_g_ext, zb, xp, xe, p):
    nmain = T_MAIN // TM_DENSE
    tile = lambda n: pl.BlockSpec((TM_DENSE, n), lambda i: (i, 0))
    main = lambda n: pl.BlockSpec((TM_DENSE, n), lambda i: (jnp.minimum(i, nmain - 1), 0))
    ext = lambda n: pl.BlockSpec((T_EXT, n), lambda i: (0, 0))
    const = lambda a, b: pl.BlockSpec((a, b), lambda i: (0, 0), pipeline_mode=pl.Buffered(1))
    return pl.pallas_call(
        _mix_route_kernel,
        grid=(T_ALL // TM_DENSE,),
        in_specs=[main(RW_WIDTH), ext(RW_WIDTH), main(GLA_WIDTH), ext(GLA_WIDTH), tile(2 * D_MODEL),
                  main(D_MODEL), ext(D_MODEL),
                  const(RW_WIDTH, D_MODEL), const(GLA_WIDTH, D_MODEL), const(D_MODEL, D_MODEL),
                  const(1, D_MODEL), const(D_MODEL, LANES), const(D_MODEL, LANES), const(1, LANES)],
        out_specs=[tile(D_MODEL), tile(D_MODEL), tile(LANES), pl.BlockSpec((1, LANES), lambda i: (0, 0))],
        out_shape=[jax.ShapeDtypeStruct((T_ALL, D_MODEL), F32),
                   jax.ShapeDtypeStruct((T_ALL, D_MODEL), F32),
                   jax.ShapeDtypeStruct((T_ALL, LANES), F32),
                   jax.ShapeDtypeStruct((1, LANES), F32)],
        scratch_shapes=[pltpu.VMEM((1, LANES), F32)],
        compiler_params=_cparams(),
        name="mix_route",
    )(o_r, o_r_ext, o_g, o_g_ext, zb, xp, xe, p['p_rwkv'], p['p_gla'], p['w_out'], p['norm2_w'], p['wr_hi'],
      p['wr_lo'], p['b_route'])


def _dispatch_kernel(info_ref, cnt_ref, xq_ref, xs_hbm, d_ref, te_ref, nt_ref,
                     d_vm, d_sm, tab_vm, tab_sm, zbuf, sem_t, sem_x):
    i = pl.program_id(0)
    lane1 = _iota((1, LANES), 1)
    cnt = jnp.where(lane1 < N_EXPERTS, cnt_ref[...], 0.0)
    tiles = jnp.floor((cnt + (TM_MOE - 1)) * (1.0 / TM_MOE))
    upper = (_iota((LANES, LANES), 0) < _iota((LANES, LANES), 1)).astype(BF16)
    tile_start = jnp.dot(jnp.broadcast_to(tiles, (8, LANES)).astype(BF16), upper,
                         preferred_element_type=F32)[0:1, :]
    base = tile_start * TM_MOE

    info = info_ref[...]
    lanef = _iota(info.shape, 1).astype(F32)
    pick = lambda col: jnp.sum(jnp.where(lanef == info[:, col:col + 1], base, 0.0), axis=-1, keepdims=True)
    d1 = pick(0) + info[:, 4:5]
    d2 = pick(1) + info[:, 5:6]
    tm = info.shape[0]
    eye = _iota((tm, tm), 0) == _iota((tm, tm), 1)
    to_row = lambda col: jnp.sum(jnp.where(eye, col, 0.0), axis=0, keepdims=True)
    d_row = jnp.concatenate([to_row(d1), to_row(d2)], axis=1).astype(jnp.int32)
    d_ref[0] = d_row
    d_vm[...] = d_row

    @pl.when(i == 0)
    def _():
        tile_end = tile_start + tiles
        n_tiles = jnp.sum(tiles, axis=-1, keepdims=True)
        eye_l = _iota((LANES, LANES), 0) == _iota((LANES, LANES), 1)
        end_col = jnp.sum(jnp.where(eye_l, tile_end, 0.0), axis=-1, keepdims=True)
        rowl = _iota((LANES, LANES), 0)
        tile_f = _iota((LANES, LANES), 1).astype(F32)
        te = jnp.sum(jnp.where(jnp.logical_and(rowl < N_EXPERTS, end_col <= tile_f), 1.0, 0.0), axis=0,
                     keepdims=True)
        last_e = jnp.max(jnp.where(tiles > 0.0, lane1.astype(F32), 0.0), axis=-1, keepdims=True)
        te = jnp.where(lane1.astype(F32) < n_tiles, jnp.minimum(te, N_EXPERTS - 1.0), last_e)
        te_ref[...] = te.astype(jnp.int32)
        nt_ref[...] = jnp.broadcast_to(n_tiles, (1, LANES)).astype(jnp.int32)
        zbuf[...] = jnp.zeros_like(zbuf)
        pad_lo = (base + cnt).astype(jnp.int32)
        pad_hi = (base + tiles * TM_MOE).astype(jnp.int32)
        nt_row = jnp.broadcast_to(n_tiles, (1, LANES)).astype(jnp.int32)
        tab_vm[...] = jnp.concatenate([pad_lo, pad_hi, nt_row, jnp.zeros((5, LANES), jnp.int32)], axis=0)
        tab_copy = pltpu.make_async_copy(tab_vm, tab_sm, sem_t)
        tab_copy.start()
        tab_copy.wait()

        def zero_row(r):
            return pltpu.make_async_copy(zbuf.at[pl.ds(0, 1), :], xs_hbm.at[pl.ds(r, 1), :], sem_x)

        def zero_tile(t):
            return pltpu.make_async_copy(zbuf, xs_hbm.at[pl.ds(t * TM_MOE, TM_MOE), :], sem_x)

        def each(fn):
            def body(r, carry):
                fn(r)
                return carry
            return body

        for e in range(N_EXPERTS):
            lax.fori_loop(tab_sm[0, e], tab_sm[1, e], each(lambda r: zero_row(r).start()), 0)
        lax.fori_loop(tab_sm[2, 0], NT_MOE, each(lambda t: zero_tile(t).start()), 0)
        for e in range(N_EXPERTS):
            lax.fori_loop(tab_sm[0, e], tab_sm[1, e], each(lambda r: zero_row(r).wait()), 0)
        lax.fori_loop(tab_sm[2, 0], NT_MOE, each(lambda t: zero_tile(t).wait()), 0)

    d_copy = pltpu.make_async_copy(d_vm, d_sm, sem_t)
    d_copy.start()
    d_copy.wait()

    for t in range(tm):
        for half in range(2):
            pltpu.make_async_copy(xq_ref.at[pl.ds(t, 1), :], xs_hbm.at[pl.ds(d_sm[0, half * tm + t], 1), :],
                                  sem_x).start(priority=half)
    for half in range(2):
        pltpu.make_async_copy(xq_ref, xs_hbm.at[pl.ds(0, tm), :], sem_x).wait()


def _dispatch(info, counts, xq):
    nt = T_ALL // TM_DENSE
    return pl.pallas_call(
        _dispatch_kernel,
        grid=(nt,),
        in_specs=[pl.BlockSpec((TM_DENSE, LANES), lambda i: (i, 0)),
                  pl.BlockSpec((1, LANES), lambda i: (0, 0)),
                  pl.BlockSpec((TM_DENSE, D_MODEL), lambda i: (i, 0))],
        out_specs=[pl.BlockSpec(memory_space=pl.ANY),
                   pl.BlockSpec((1, 1, 2 * TM_DENSE), lambda i: (i, 0, 0)),
                   pl.BlockSpec((1, LANES), lambda i: (0, 0)),
                   pl.BlockSpec((1, LANES), lambda i: (0, 0))],
        out_shape=[jax.ShapeDtypeStruct((NT_MOE * TM_MOE, D_MODEL), F32),
                   jax.ShapeDtypeStruct((nt, 1, 2 * TM_DENSE), jnp.int32),
                   jax.ShapeDtypeStruct((1, LANES), jnp.int32),
                   jax.ShapeDtypeStruct((1, LANES), jnp.int32)],
        scratch_shapes=[pltpu.VMEM((1, 2 * TM_DENSE), jnp.int32),
                        pltpu.SMEM((1, 2 * TM_DENSE), jnp.int32),
                        pltpu.VMEM((8, LANES), jnp.int32),
                        pltpu.SMEM((8, LANES), jnp.int32),
                        pltpu.VMEM((TM_MOE, D_MODEL), F32),
                        pltpu.SemaphoreType.DMA(()),
                        pltpu.SemaphoreType.DMA(())],
        compiler_params=_cparams(),
        name="moe_dispatch",
    )(info, counts, xq)


def _moe_kernel(te_ref, nt_ref, xs_ref, w1_hbm, w3_hbm, w2_hbm, o_ref, w1r, w3r, w2r, w1b, w3b, w2b, sem):
    i = pl.program_id(0)
    n_tiles = nt_ref[0]

    def weight_copies(e):
        return [pltpu.make_async_copy(w1_hbm.at[e], w1r, sem), pltpu.make_async_copy(w3_hbm.at[e], w3r, sem),
                pltpu.make_async_copy(w2_hbm.at[e], w2r, sem)]

    @pl.when(i < n_tiles)
    def _():
        e = te_ref[i]

        @pl.when(i == 0)
        def _():
            for c in weight_copies(e):
                c.start()

        @pl.when(jnp.logical_or(i == 0, e != te_ref[jnp.maximum(i - 1, 0)]))
        def _():
            for c in weight_copies(e):
                c.wait()
            w1b[...] = w1r[...].astype(BF16)
            w3b[...] = w3r[...].astype(BF16)
            w2b[...] = w2r[...].astype(BF16)
            nxt = lax.while_loop(
                lambda j: jnp.logical_and(j < n_tiles, te_ref[jnp.minimum(j, NT_MOE - 1)] == e),
                lambda j: j + 1, i + 1)

            @pl.when(nxt < n_tiles)
            def _():
                for c in weight_copies(te_ref[jnp.minimum(nxt, NT_MOE - 1)]):
                    c.start()

        xb = xs_ref[...].astype(BF16)
        h1 = jnp.dot(xb, w1b[...], preferred_element_type=F32)
        h3 = jnp.dot(xb, w3b[...], preferred_element_type=F32)
        hh = (h1 * jax.nn.sigmoid(h1) * h3).astype(BF16)
        o_ref[...] = jnp.dot(hh, w2b[...], preferred_element_type=F32)

    @pl.when(i >= nt_ref[0])
    def _():
        o_ref[...] = jnp.zeros_like(o_ref)


def _moe(tile_expert, n_tiles, xs, w1, w3, w2):
    grid_spec = pltpu.PrefetchScalarGridSpec(
        num_scalar_prefetch=2,
        grid=(NT_MOE,),
        in_specs=[pl.BlockSpec((TM_MOE, D_MODEL), lambda i, te, nt: (jnp.minimum(i, nt[0] - 1), 0)),
                  pl.BlockSpec(memory_space=pl.ANY),
                  pl.BlockSpec(memory_space=pl.ANY),
                  pl.BlockSpec(memory_space=pl.ANY)],
        out_specs=pl.BlockSpec((TM_MOE, D_MODEL), lambda i, te, nt: (i, 0)),
        scratch_shapes=[pltpu.VMEM((D_MODEL, D_EXPERT), F32),
                        pltpu.VMEM((D_MODEL, D_EXPERT), F32),
                        pltpu.VMEM((D_EXPERT, D_MODEL), F32),
                        pltpu.VMEM((D_MODEL, D_EXPERT), BF16),
                        pltpu.VMEM((D_MODEL, D_EXPERT), BF16),
                        pltpu.VMEM((D_EXPERT, D_MODEL), BF16),
                        pltpu.SemaphoreType.DMA(())],
    )
    return pl.pallas_call(
        _moe_kernel,
        grid_spec=grid_spec,
        out_shape=jax.ShapeDtypeStruct((NT_MOE * TM_MOE, D_MODEL), F32),
        compiler_params=_cparams(),
        name="moe_experts",
    )(tile_expert, n_tiles, xs, w1, w3, w2)


def _combine_kernel(dcur_ref, dnext_ref, h1_ref, info_ref, fw_ref, eo_hbm, yp_ref, ye_ref, buf, sem):
    i = pl.program_id(0)
    nmain = T_MAIN // TM_DENSE
    slot = lax.rem(i, 2)

    def row_copy(d_ref, s, t):
        return pltpu.make_async_copy(eo_hbm.at[pl.ds(d_ref[0, 0, t], 1), :], buf.at[s, pl.ds(t, 1), :], sem.at[s])

    def issue_all(d_ref, s):
        for t in range(2 * TM_DENSE):
            row_copy(d_ref, s, t).start()

    @pl.when(i == 0)
    def _():
        issue_all(dcur_ref, slot)

    @pl.when(i + 1 < pl.num_programs(0))
    def _():
        issue_all(dnext_ref, 1 - slot)

    pltpu.make_async_copy(eo_hbm.at[pl.ds(0, 2 * TM_DENSE), :], buf.at[slot], sem.at[slot]).wait()

    info = info_ref[...]
    y = h1_ref[...] + info[:, 2:3] * buf[slot, 0:TM_DENSE, :] + info[:, 3:4] * buf[slot, TM_DENSE:2 * TM_DENSE, :]
    ms = jnp.mean(y * y, axis=-1, keepdims=True)
    out = y * lax.rsqrt(ms + RMS_EPS) * fw_ref[...]

    @pl.when(i < nmain)
    def _():
        yp_ref[...] = out

    @pl.when(i == nmain)
    def _():
        ye_ref[...] = out


def _combine(dst, h1, info, fw, eo):
    nmain = T_MAIN // TM_DENSE
    return pl.pallas_call(
        _combine_kernel,
        grid=(T_ALL // TM_DENSE,),
        in_specs=[pl.BlockSpec((1, 1, 2 * TM_DENSE), lambda i: (i, 0, 0), memory_space=pltpu.SMEM),
                  pl.BlockSpec((1, 1, 2 * TM_DENSE), lambda i: (jnp.minimum(i + 1, nmain), 0, 0),
                               memory_space=pltpu.SMEM),
                  pl.BlockSpec((TM_DENSE, D_MODEL), lambda i: (i, 0)),
                  pl.BlockSpec((TM_DENSE, LANES), lambda i: (i, 0)),
                  pl.BlockSpec((1, D_MODEL), lambda i: (0, 0)),
                  pl.BlockSpec(memory_space=pl.ANY)],
        out_specs=[pl.BlockSpec((TM_DENSE, D_MODEL), lambda i: (jnp.minimum(i, nmain - 1), 0)),
                   pl.BlockSpec((T_EXT, D_MODEL), lambda i: (0, 0))],
        out_shape=[jax.ShapeDtypeStruct((T_MAIN, D_MODEL), F32),
                   jax.ShapeDtypeStruct((T_EXT, D_MODEL), F32)],
        scratch_shapes=[pltpu.VMEM((2, 2 * TM_DENSE, D_MODEL), F32), pltpu.SemaphoreType.DMA((2,))],
        compiler_params=_cparams(),
        name="moe_combine",
    )(dst, dst, h1, info, fw, eo)


def kernel(x_prompt, x_sample, state_rwkv, state_shift, state_gla, meta_tokens, norm1_w, w_in, mu_shift, rw_w0,
           rw_w2, rw_a0, rw_a2, rw_g2, rw_k_k, rw_k_a, rw_r_k, rw_lnx_w, rw_lnx_b, gla_gk_up, gla_gk_b, gla_norm_w,
           p_rwkv, p_gla, w_out, norm2_w, moe_w_group, moe_b_group, moe_w_router, moe_b_router, moe_w1, moe_w3,
           moe_w2, final_norm_w):
    w_t = w_in[0].T
    row = lambda t: t.reshape(1, -1)
    w2a = jnp.zeros((LANES, 2 * RW_WIDTH), F32)
    w2a = w2a.at[0:64, 0:RW_WIDTH].set(rw_w2[0]).at[64:128, RW_WIDTH:].set(rw_a2[0])
    rw = dict(mu=row(mu_shift[0][_SHIFT_PERM]), w0=row(rw_w0[0]), w2a=w2a, a0=row(rw_a0[0]), g2=rw_g2[0],
              k_k=row(rw_k_k[0]), k_a=row(rw_k_a[0]), r_k=row(rw_r_k[0]), lnx_w=row(rw_lnx_w[0]),
              lnx_b=row(rw_lnx_b[0]))
    gk_up = jnp.zeros((LANES, GLA_KDIM), F32).at[0:GLA_LORA].set(gla_gk_up[0])
    gl = dict(gk_up=gk_up, gk_b=row(gla_gk_b[0]), norm_w=row(gla_norm_w[0]))
    w_route = jnp.zeros((D_MODEL, LANES), F32)
    w_route = w_route.at[:, 0:N_EXPERTS].set(moe_w_router[0]).at[:, N_EXPERTS:N_EXPERTS + N_GROUPS].set(moe_w_group[0])
    wr_hi = w_route.astype(BF16)
    wr_lo = (w_route - wr_hi.astype(F32)).astype(BF16)
    b_route = jnp.zeros((1, LANES), F32)
    b_route = b_route.at[0, 0:N_EXPERTS].set(moe_b_router[0]).at[0, N_EXPERTS:N_EXPERTS + N_GROUPS].set(moe_b_group[0])
    mix = dict(p_rwkv=p_rwkv[0].astype(BF16), p_gla=p_gla[0].astype(BF16), w_out=w_out[0].astype(BF16),
               norm2_w=row(norm2_w[0]), wr_hi=wr_hi, wr_lo=wr_lo, b_route=b_route)

    xp = x_prompt.reshape(T_MAIN, D_MODEL)
    xe = jnp.concatenate([x_sample[:, 0, :], jnp.zeros((META_PAD, D_MODEL), F32), meta_tokens,
                          jnp.zeros((T_ALL - ROW_META - CHUNK, D_MODEL), F32)], axis=0)

    xn = _norm1(xp, xe, row(norm1_w[0]))
    zr = _proj(xn, w_t, _RWKV_TILES, 1664, F32, name="proj_rwkv")
    zg = _proj(xn, w_t, _GLA_TILES, 1152, F32, name="proj_gla")
    zb = _proj(xn, w_t, _GATE_TILES, 1024, BF16, silu_from=4, name="proj_gates")

    o_r, h_fin, o_g, s_fin = _prompt_recurrences(zr, zg, zb, rw, gl)

    r_s, kh_s, v_s, g_s, r_t, w_t, kh_t, v_t, nkk_t, b_t = _rwkv_sample_pre(zr, state_shift[0][:, _SHIFT_PERM], rw)
    rw_new_t, y_t = _rwkv_sample_step(state_rwkv[0].transpose(1, 2, 3, 0), r_t, w_t, kh_t, v_t, nkk_t, b_t)
    rw_new = rw_new_t.transpose(3, 0, 1, 2)
    o_r_ext = _rwkv_sample_post(y_t, r_s, kh_s, v_s, g_s, rw)
    q_s, k_s, eg_s = _gla_sample_pre(zg, gl)
    v_gs = zg[ROW_SAMPLE:ROW_SAMPLE + N_SAMPLE, 2 * GLA_KDIM:2 * GLA_KDIM + GLA_WIDTH]
    gla_new, og_s = _gla_sample_step(state_gla[0], q_s, k_s, eg_s, v_gs)
    o_g_ext = _gla_sample_post(og_s, zb, gl)

    h1, xn2, info, counts = _mix_route(o_r, o_r_ext, o_g, o_g_ext, zb, xp, xe, mix)
    xs, dst, tile_expert, n_tiles = _dispatch(info, counts, xn2)
    eo = _moe(tile_expert[0, :NT_MOE], n_tiles[0, :1], xs, moe_w1[0], moe_w3[0], moe_w2[0])
    y_p, y_e = _combine(dst, h1, info, row(final_norm_w), eo)

    y_prompt = y_p.reshape(N_BATCH, SEQ, D_MODEL)
    y_sample = y_e[0:N_SAMPLE].reshape(N_SAMPLE, 1, D_MODEL)
    new_rwkv_prompt = h_fin.reshape(N_BATCH, RW_HEAD, RW_HEADS, RW_HEAD).transpose(0, 2, 3, 1)[None]
    shift_rows = jnp.concatenate([zr[(b + 1) * SEQ - 1:(b + 1) * SEQ] for b in range(N_BATCH)]
                                 + [zr[ROW_SAMPLE:ROW_SAMPLE + N_SAMPLE]], axis=0)
    shift_rows = shift_rows[:, _SHIFT_INV]
    return (y_prompt, y_sample, new_rwkv_prompt, shift_rows[None, 0:N_BATCH], s_fin[None],
            rw_new[None], shift_rows[None, N_BATCH:], gla_new[None])
```

```python
import functools
import math

import numpy as np
import jax
import jax.numpy as jnp
from jax import lax
from jax.experimental import pallas as pl
from jax.experimental.pallas import tpu as pltpu

F32 = jnp.float32
BF16 = jnp.bfloat16

D_MODEL = 2048
N_BATCH = 4
SEQ = 2048
N_SAMPLE = 128
N_META = 16
RMS_EPS = 1e-6

RW_WIDTH = 1024
RW_HEAD = 64
RW_HEADS = 16
RW_GN_EPS = RW_HEAD * 1e-5
W_SHIFT = 3328
GLA_HEADS = 4
GLA_DK = 128
GLA_DV = 256
GLA_KDIM = 512
GLA_WIDTH = 1024
GLA_LORA = 16
GLA_NORMALIZER = 16.0
W_GLA_PAD = 2304
OG_BLOCK = 2 * D_MODEL // GLA_WIDTH
N_GROUPS = 4
EXPERTS_PER_GROUP = 8
N_EXPERTS = 32
D_EXPERT = 512

CHUNK = 64
N_CHUNKS = SEQ // CHUNK
META_PAD = CHUNK - N_META
T_MAIN = N_BATCH * SEQ
ROW_SAMPLE = T_MAIN
ROW_META = T_MAIN + N_SAMPLE
T_EXT = 256
T_ALL = T_MAIN + T_EXT
LANES = 128

TM_DENSE = 256
TM_MM = 1056
TM_MOE = 256
NT_MOE = (2 * T_ALL) // TM_MOE + N_EXPERTS
NEG_BIG = -1e30

VMEM_LIMIT = 56 * 1024 * 1024

_SHIFT_PERM = np.concatenate([np.arange(0, 1024), np.arange(1088, 2112), np.arange(2112, 3136),
                              np.arange(1024, 1088), np.arange(3136, 3328)])
_SHIFT_INV = np.argsort(_SHIFT_PERM)


def _cparams(n_axes=1):
    return pltpu.CompilerParams(dimension_semantics=("arbitrary",) * n_axes, vmem_limit_bytes=VMEM_LIMIT)


def _bdot(a, b):
    return jnp.dot(a.astype(BF16), b.astype(BF16), preferred_element_type=F32)


def _bdot_nt(a, b):
    return lax.dot_general(a.astype(BF16), b.astype(BF16), (((1,), (1,)), ((), ())), preferred_element_type=F32)


def _bdot_tn(a, b):
    return lax.dot_general(a.astype(BF16), b.astype(BF16), (((0,), (0,)), ((), ())), preferred_element_type=F32)


def _split_dot(m_bf16, x):
    hi = x.astype(BF16)
    lo = (x - hi.astype(F32)).astype(BF16)
    return (jnp.dot(m_bf16, hi, preferred_element_type=F32) + jnp.dot(m_bf16, lo, preferred_element_type=F32))


def _iota(shape, dim):
    return lax.broadcasted_iota(jnp.int32, shape, dim)


def _tri_incl(n):
    return (_iota((n, n), 0) >= _iota((n, n), 1)).astype(BF16)


def _seg_sum(x, width):
    m, n = x.shape
    nb = n // LANES
    bd = ((_iota((LANES, LANES), 0) // width) == (_iota((LANES, LANES), 1) // width)).astype(BF16)
    xs = jnp.concatenate([x[:, j * LANES:(j + 1) * LANES] for j in range(nb)], axis=0)
    hi = xs.astype(BF16)
    lo = (xs - hi.astype(F32)).astype(BF16)
    s = jnp.dot(hi, bd, preferred_element_type=F32) + jnp.dot(lo, bd, preferred_element_type=F32)
    return jnp.concatenate([s[j * m:(j + 1) * m] for j in range(nb)], axis=1)


def _col_of_row(row):
    n = row.shape[-1]
    eye = _iota((n, n), 0) == _iota((n, n), 1)
    return jnp.sum(jnp.where(eye, jnp.broadcast_to(row, (n, n)), 0.0), axis=-1, keepdims=True)


def _norm1_kernel(xp_ref, xe_ref, w_ref, o_ref):
    i = pl.program_id(0)

    def f(x):
        ms = jnp.mean(x * x, axis=-1, keepdims=True)
        return (x * lax.rsqrt(ms + RMS_EPS) * w_ref[...]).astype(BF16)

    @pl.when(i < T_MAIN // TM_DENSE)
    def _():
        o_ref[...] = f(xp_ref[...])

    @pl.when(i == T_MAIN // TM_DENSE)
    def _():
        o_ref[...] = f(xe_ref[...])


def _norm1(xp, xe, w):
    nmain = T_MAIN // TM_DENSE
    return pl.pallas_call(
        _norm1_kernel,
        grid=(T_ALL // TM_DENSE,),
        in_specs=[pl.BlockSpec((TM_DENSE, D_MODEL), lambda i: (jnp.minimum(i, nmain - 1), 0)),
                  pl.BlockSpec((T_EXT, D_MODEL), lambda i: (0, 0)),
                  pl.BlockSpec((1, D_MODEL), lambda i: (0, 0))],
        out_specs=pl.BlockSpec((TM_DENSE, D_MODEL), lambda i: (i, 0)),
        out_shape=jax.ShapeDtypeStruct((T_ALL, D_MODEL), BF16),
        compiler_params=_cparams(),
        name="norm1",
    )(xp, xe, w)


def _proj_kernel(x_ref, wt_hbm, o_ref, wraw, wbf, sem, *, tiles, silu_from):
    j = pl.program_id(0)
    m = pl.program_id(1)
    tn = wbf.shape[0]

    def copies(jj):
        return [pltpu.make_async_copy(wt_hbm.at[pl.ds(src, n), :], wraw.at[pl.ds(dst, n), :], sem)
                for (src, n, dst) in tiles[jj]]

    @pl.when(jnp.logical_and(j == 0, m == 0))
    def _():
        for c in copies(0):
            c.start()

    for jj in range(len(tiles)):
        @pl.when(jnp.logical_and(j == jj, m == 0))
        def _(jj=jj):
            for c in copies(jj):
                c.wait()
            edge = 0
            for lo, hi in sorted((dst, dst + n) for (_, n, dst) in tiles[jj]) + [(tn, tn)]:
                if lo > edge:
                    wbf[edge:lo, :] = jnp.zeros((lo - edge, wbf.shape[1]), BF16)
                if hi > lo:
                    wbf[lo:hi, :] = wraw[lo:hi, :].astype(BF16)
                edge = hi
            if jj + 1 < len(tiles):
                for c in copies(jj + 1):
                    c.start()

    z = lax.dot_general(x_ref[...], wbf[...], (((1,), (1,)), ((), ())), preferred_element_type=F32)
    if silu_from is None:
        o_ref[...] = z.astype(o_ref.dtype)
    else:
        s = jax.nn.sigmoid(z)
        o_ref[...] = jnp.where(j >= silu_from, z * s, s).astype(o_ref.dtype)


def _proj(x, w_t, tiles, tn, out_dtype, tm, silu_from=None, name="proj"):
    t, k = x.shape
    return pl.pallas_call(
        functools.partial(_proj_kernel, tiles=tiles, silu_from=silu_from),
        grid=(len(tiles), t // tm),
        in_specs=[pl.BlockSpec((tm, k), lambda j, m: (m, 0)),
                  pl.BlockSpec(memory_space=pl.ANY)],
        out_specs=pl.BlockSpec((tm, tn), lambda j, m: (m, j)),
        out_shape=jax.ShapeDtypeStruct((t, tn * len(tiles)), out_dtype),
        scratch_shapes=[pltpu.VMEM((tn, k), F32), pltpu.VMEM((tn, k), BF16), pltpu.SemaphoreType.DMA(())],
        compiler_params=_cparams(2),
        name=name,
    )(x, w_t)


_RWKV_TILES = (((0, 1024, 0), (1088, 640, 1024)),
               ((1728, 1408, 0), (1024, 64, 1408), (3136, 192, 1472)))
_GLA_TILES = (((3328, 1152, 0),), ((4480, 912, 0),))
_GATE_TILES = tuple(((6416 + 1024 * j, 1024, 0),) for j in range(4)) + (((5392, 1024, 0),),)


def _rwkv_pre(z, w0, w2a, a0, g2, k_k, k_a):
    r = z[:, 0:1024]
    k = z[:, 1024:2048]
    v = z[:, 2048:3072]
    wa = z[:, 3072:3200]
    gd = z[:, 3200:3328]
    lane = _iota(wa.shape, 1)
    wa = jnp.where(lane < 64, jnp.tanh(wa), wa)
    up = _bdot(wa, w2a)
    lw = -math.exp(-0.5) * jax.nn.sigmoid(w0 + up[:, :1024])
    a = jax.nn.sigmoid(a0 + up[:, 1024:])
    g = _bdot(jax.nn.sigmoid(gd), g2)
    kk = k * k_k
    kk = kk * lax.rsqrt(jnp.maximum(_seg_sum(kk * kk, RW_HEAD), 1e-24))
    k_h = k * (1.0 + (a - 1.0) * k_a)
    return r, lw, k_h, v, kk, kk * a, g


def _rwkv_post(y, r, k_h, v, g, r_k, lnx_w, lnx_b):
    mean = _seg_sum(y, RW_HEAD) * (1.0 / RW_HEAD)
    d = y - mean
    var = _seg_sum(d * d, RW_HEAD) * (1.0 / RW_HEAD)
    yn = d * lax.rsqrt(var + RW_GN_EPS) * lnx_w + lnx_b
    bonus = _seg_sum(r * k_h * r_k, RW_HEAD) * v
    return (yn + bonus) * g


def _stack2(x):
    lane = _iota(x.shape, 1)
    return jnp.concatenate([jnp.where(lane < RW_HEAD, x, 0.0), jnp.where(lane >= RW_HEAD, x, 0.0)], axis=0)


def _unstack2(x):
    c = x.shape[0] // 2
    return x[:c] + x[c:]


def _prompt_chunk_kernel(zs_ref, mu_ref, w0_ref, w2a_ref, a0_ref, g2_ref, kk_ref, ka_ref, rk_ref, lw_ref, lb_ref,
                         zg_ref, og_ref, gkup_ref, gkb_ref, nw_ref,
                         o_ref, hout_ref, go_ref, sout_ref,
                         prev_sc, h_sc, hmeta_sc, prevmeta_sc, y_sc, s_sc, smeta_sc, *, n_chunks):
    i = pl.program_id(0)
    is_meta = i == 0
    c = lax.rem(jnp.maximum(i - 1, 0), n_chunks)
    first = jnp.logical_and(i >= 1, c == 0)
    last = jnp.logical_and(i >= 1, c == n_chunks - 1)

    @pl.when(is_meta)
    def _():
        h_sc[...] = jnp.zeros_like(h_sc)
        prev_sc[...] = jnp.zeros_like(prev_sc)
        s_sc[...] = jnp.zeros_like(s_sc)

    @pl.when(first)
    def _():
        h_sc[...] = hmeta_sc[...]
        prev_sc[...] = prevmeta_sc[...]
        s_sc[...] = smeta_sc[...]

    rowi = _iota((CHUNK, 1), 0)
    pad_row = jnp.logical_and(is_meta, rowi < META_PAD)
    heads = range(GLA_HEADS)
    ks = [slice(h * GLA_DK, (h + 1) * GLA_DK) for h in heads]
    gla = {}

    def gla_prep():
        zg = zg_ref[...]
        logg = jnp.where(pad_row, 0.0, _gla_logg(zg, gkup_ref[...], gkb_ref[...]))
        bcum = _split_dot(_tri_incl(CHUNK), logg)
        b_end = bcum[CHUNK - 1:CHUNK, :]
        gla['qt'] = zg[:, 0:GLA_KDIM] * (GLA_DK ** -0.5) * jnp.exp(bcum)
        gla['kt'] = zg[:, GLA_KDIM:2 * GLA_KDIM] * jnp.exp(-bcum)
        gla['ke'] = zg[:, GLA_KDIM:2 * GLA_KDIM] * jnp.exp(b_end - bcum)
        gla['e_end'] = jnp.exp(b_end)
        gla['vh'] = [zg[:, 2 * GLA_KDIM + h * GLA_DV:2 * GLA_KDIM + (h + 1) * GLA_DV].astype(BF16) for h in heads]

    def gla_scores():
        causal = _iota((CHUNK, CHUNK), 0) >= _iota((CHUNK, CHUNK), 1)
        gla['a'] = [jnp.where(causal, _bdot_nt(gla['qt'][:, ks[h]], gla['kt'][:, ks[h]]), 0.0) for h in heads]
        gla['s0'] = [s_sc[h] for h in heads]

    def gla_state():
        gla['o_inter'] = [_bdot(gla['qt'][:, ks[h]], gla['s0'][h]) for h in heads]
        gla['s_add'] = [_bdot_tn(gla['ke'][:, ks[h]], gla['vh'][h]) for h in heads]

    def gla_finish():
        outs = [_bdot(gla['a'][h], gla['vh'][h]) + gla['o_inter'][h] for h in heads]
        gla['s_new'] = [_col_of_row(gla['e_end'][:, ks[h]]) * gla['s0'][h] + gla['s_add'][h] for h in heads]
        for h in heads:
            s_sc[h] = gla['s_new'][h]
        go = _gla_post(jnp.concatenate(outs, axis=1), og_ref[...].astype(F32), nw_ref[...])
        go_ref[...] = go.astype(go_ref.dtype)

    zs = zs_ref[...]
    sh = pltpu.roll(zs, 1, 0)
    sh = jnp.where(rowi == 0, prev_sc[...], sh)
    prev_sc[...] = zs[CHUNK - 1:CHUNK, :]
    z = zs + (sh - zs) * mu_ref[...]
    r, lw, k_h, v, kk, b, g = _rwkv_pre(z, w0_ref[...], w2a_ref[...], a0_ref[...], g2_ref[...], kk_ref[...],
                                        ka_ref[...])
    lw = jnp.where(pad_row, 0.0, lw)
    gla_prep()

    cl = _split_dot(_tri_incl(CHUNK), lw)
    cl_end = cl[CHUNK - 1:CHUNK, :]
    e_neg = jnp.exp(-cl)
    e_end = jnp.exp(cl_end - cl)
    kkt = kk * jnp.exp(cl - lw)
    rt = r * jnp.exp(cl)
    bt = b * e_neg
    kt = k_h * e_neg
    bh = b * e_end
    kh = k_h * e_end
    e_c = jnp.exp(cl_end)

    n2 = 2 * CHUNK
    tok_r = jnp.bitwise_and(_iota((n2, n2), 0), CHUNK - 1)
    tok_c = jnp.bitwise_and(_iota((n2, n2), 1), CHUNK - 1)
    strict = tok_r > tok_c
    incl = tok_r >= tok_c

    pairs = range(RW_HEADS // 2)
    sls = [slice(j * LANES, (j + 1) * LANES) for j in pairs]
    s_kkt = [_stack2(kkt[:, sl]) for sl in sls]
    s_rt = [_stack2(rt[:, sl]) for sl in sls]
    s_v = [_stack2(v[:, sl]).astype(BF16) for sl in sls]
    rb = [jnp.concatenate([_stack2(bt[:, sl]), _stack2(kt[:, sl])], axis=0).astype(BF16) for sl in sls]
    aa = [_bdot_nt(s_kkt[j], rb[j]) for j in pairs]
    mm = [_bdot_nt(s_rt[j], rb[j]) for j in pairs]
    h_kv = [_bdot_tn(_stack2(kh[:, sls[j]]), s_v[j]) for j in pairs]
    gla_scores()
    a_b =[jnp.where(strict, aa[j][:, :n2], 0.0).astype(BF16) for j in pairs]
    a_k = [jnp.where(strict, aa[j][:, n2:], 0.0) for j in pairs]
    m_rb = [jnp.where(incl, mm[j][:, :n2], 0.0).astype(BF16) for j in pairs]
    m_rk = [jnp.where(incl, mm[j][:, n2:], 0.0) for j in pairs]
    akv = [_bdot(a_k[j], s_v[j]) for j in pairs]
    y_kv = [_bdot(m_rk[j], s_v[j]) for j in pairs]
    x = [jnp.concatenate([s_kkt[j], akv[j]], axis=1) for j in pairs]
    p = a_b
    x = [x[j] - _bdot(p[j], x[j]) for j in pairs]
    for step in range(5):
        p = [_bdot(p[j], p[j]).astype(BF16) for j in pairs]
        x = [x[j] + _bdot(p[j], x[j]) for j in pairs]
        if step == 1:
            gla_state()
        if step == 3:
            gla_finish()
    zq = [_bdot(m_rb[j], x[j]) for j in pairs]
    zb = [_bdot_tn(_stack2(bh[:, sls[j]]), x[j]) for j in pairs]
    h0 = [h_sc[j] for j in pairs]
    for j in pairs:
        q_eff = _unstack2(s_rt[j] - zq[j][:, :LANES])
        y_in = _unstack2(y_kv[j] - zq[j][:, LANES:])
        y_sc[:, sls[j]] = _bdot(q_eff, h0[j]) + y_in
    h_new = [_col_of_row(e_c[:, sls[j]]) * h0[j] - _bdot(zb[j][:, :LANES], h0[j]) + (h_kv[j] - zb[j][:, LANES:])
             for j in pairs]
    for j in pairs:
        h_sc[j] = h_new[j]

    o = _rwkv_post(y_sc[...], r, k_h, v, g, rk_ref[...], lw_ref[...], lb_ref[...])
    o_ref[...] = o.astype(o_ref.dtype)

    @pl.when(last)
    def _():
        for j in pairs:
            hout_ref[0, :, sls[j]] = _unstack2(h_new[j])
        for h in heads:
            sout_ref[0, h] = gla['s_new'][h]

    @pl.when(is_meta)
    def _():
        hmeta_sc[...] = h_sc[...]
        prevmeta_sc[...] = prev_sc[...]
        smeta_sc[...] = s_sc[...]


def _prompt_recurrences(zr, zg, zb, p, q, n_batch=N_BATCH, n_chunks=N_CHUNKS, meta_block=ROW_META // CHUNK,
                        og_block=OG_BLOCK):
    row = lambda n: pl.BlockSpec((1, n), lambda i: (0, 0))
    full = lambda a, b: pl.BlockSpec((a, b), lambda i: (0, 0))
    blk = lambda i: jnp.where(i == 0, meta_block, i - 1)
    out_blk = lambda i: jnp.maximum(i - 1, 0)
    seq = lambda i: jnp.maximum(i - 1, 0) // n_chunks
    n_rows = n_batch * n_chunks * CHUNK
    return pl.pallas_call(
        functools.partial(_prompt_chunk_kernel, n_chunks=n_chunks),
        grid=(1 + n_batch * n_chunks,),
        in_specs=[pl.BlockSpec((CHUNK, W_SHIFT), lambda i: (blk(i), 0)),
                  row(W_SHIFT), row(RW_WIDTH), full(LANES, 2 * RW_WIDTH), row(RW_WIDTH), full(LANES, RW_WIDTH),
                  row(RW_WIDTH), row(RW_WIDTH), row(RW_WIDTH), row(RW_WIDTH), row(RW_WIDTH),
                  pl.BlockSpec((CHUNK, W_GLA_PAD), lambda i: (blk(i), 0)),
                  pl.BlockSpec((CHUNK, GLA_WIDTH), lambda i: (blk(i), og_block)),
                  full(LANES, GLA_KDIM), full(1, GLA_KDIM), full(1, GLA_DV)],
        out_specs=[pl.BlockSpec((CHUNK, RW_WIDTH), lambda i: (out_blk(i), 0)),
                   pl.BlockSpec((1, RW_HEAD, RW_WIDTH), lambda i: (seq(i), 0, 0)),
                   pl.BlockSpec((CHUNK, GLA_WIDTH), lambda i: (out_blk(i), 0)),
                   pl.BlockSpec((1, GLA_HEADS, GLA_DK, GLA_DV), lambda i: (seq(i), 0, 0, 0))],
        out_shape=[jax.ShapeDtypeStruct((n_rows, RW_WIDTH), BF16),
                   jax.ShapeDtypeStruct((n_batch, RW_HEAD, RW_WIDTH), F32),
                   jax.ShapeDtypeStruct((n_rows, GLA_WIDTH), BF16),
                   jax.ShapeDtypeStruct((n_batch, GLA_HEADS, GLA_DK, GLA_DV), F32)],
        scratch_shapes=[pltpu.VMEM((1, W_SHIFT), F32),
                        pltpu.VMEM((RW_HEADS // 2, LANES, LANES), F32),
                        pltpu.VMEM((RW_HEADS // 2, LANES, LANES), F32),
                        pltpu.VMEM((1, W_SHIFT), F32),
                        pltpu.VMEM((CHUNK, RW_WIDTH), F32),
                        pltpu.VMEM((GLA_HEADS, GLA_DK, GLA_DV), F32),
                        pltpu.VMEM((GLA_HEADS, GLA_DK, GLA_DV), F32)],
        compiler_params=_cparams(),
        name="prompt_recurrences",
    )(zr, p['mu'], p['w0'], p['w2a'], p['a0'], p['g2'], p['k_k'], p['k_a'], p['r_k'], p['lnx_w'], p['lnx_b'],
      zg, zb, q['gk_up'], q['gk_b'], q['norm_w'])


def _rwkv_sample_pre_kernel(zs_ref, prev_ref, mu_ref, w0_ref, w2a_ref, a0_ref, g2_ref, kk_ref, ka_ref,
                            r_ref, kh_ref, v_ref, g_ref, rt_ref, wt_ref, kht_ref, vt_ref, nkkt_ref, bt_ref):
    zs = zs_ref[...]
    z = zs + (prev_ref[...] - zs) * mu_ref[...]
    r, lw, k_h, v, kk, b, g = _rwkv_pre(z, w0_ref[...], w2a_ref[...], a0_ref[...], g2_ref[...], kk_ref[...],
                                        ka_ref[...])
    r_ref[...] = r
    kh_ref[...] = k_h
    v_ref[...] = v
    g_ref[...] = g
    rt_ref[...] = r.T
    wt_ref[...] = jnp.exp(lw).T
    kht_ref[...] = k_h.T
    vt_ref[...] = v.T
    nkkt_ref[...] = (-kk).T
    bt_ref[...] = b.T


def _rwkv_sample_pre(zr, prev, p):
    row = lambda n: pl.BlockSpec((1, n), lambda i: (0, 0))
    full = lambda a, b: pl.BlockSpec((a, b), lambda i: (0, 0))
    vec = jax.ShapeDtypeStruct((N_SAMPLE, RW_WIDTH), F32)
    vec_t = jax.ShapeDtypeStruct((RW_WIDTH, N_SAMPLE), F32)
    return pl.pallas_call(
        _rwkv_sample_pre_kernel,
        grid=(1,),
        in_specs=[pl.BlockSpec((N_SAMPLE, W_SHIFT), lambda i: (ROW_SAMPLE // N_SAMPLE, 0)),
                  full(N_SAMPLE, W_SHIFT),
                  row(W_SHIFT), row(RW_WIDTH), full(LANES, 2 * RW_WIDTH), row(RW_WIDTH), full(LANES, RW_WIDTH),
                  row(RW_WIDTH), row(RW_WIDTH)],
        out_specs=[full(N_SAMPLE, RW_WIDTH)] * 4 + [full(RW_WIDTH, N_SAMPLE)] * 6,
        out_shape=[vec] * 4 + [vec_t] * 6,
        compiler_params=_cparams(),
        name="rwkv_sample_pre",
    )(zr, prev, p['mu'], p['w0'], p['w2a'], p['a0'], p['g2'], p['k_k'], p['k_a'])


def _rwkv_sample_step_kernel(s_ref, r_ref, w_ref, kh_ref, v_ref, nkk_ref, b_ref, so_ref, y_ref):
    s = s_ref[0]
    sa = jnp.sum(s * nkk_ref[0][None], axis=1, keepdims=True)
    s_new = s * w_ref[0][None] + sa * b_ref[0][None] + v_ref[0] * kh_ref[0][None]
    so_ref[0] = s_new
    y_ref[0] = jnp.sum(s_new * r_ref[0][None], axis=1, keepdims=True)


def _rwkv_sample_step(state_t, r_t, w_t, kh_t, v_t, nkk_t, b_t):
    kvec = lambda t: t.reshape(RW_HEADS, RW_HEAD, N_SAMPLE)
    kspec = pl.BlockSpec((1, RW_HEAD, N_SAMPLE), lambda i: (i, 0, 0))
    vspec = pl.BlockSpec((1, RW_HEAD, 1, N_SAMPLE), lambda i: (i, 0, 0, 0))
    sspec = pl.BlockSpec((1, RW_HEAD, RW_HEAD, N_SAMPLE), lambda i: (i, 0, 0, 0))
    s_new, y = pl.pallas_call(
        _rwkv_sample_step_kernel,
        grid=(RW_HEADS,),
        in_specs=[sspec, kspec, kspec, kspec, vspec, kspec, kspec],
        out_specs=[sspec, vspec],
        out_shape=[jax.ShapeDtypeStruct(state_t.shape, F32),
                   jax.ShapeDtypeStruct((RW_HEADS, RW_HEAD, 1, N_SAMPLE), F32)],
        compiler_params=_cparams(),
        name="rwkv_sample_step",
    )(state_t, kvec(r_t), kvec(w_t), kvec(kh_t), v_t.reshape(RW_HEADS, RW_HEAD, 1, N_SAMPLE), kvec(nkk_t), kvec(b_t))
    return s_new, y.reshape(RW_WIDTH, N_SAMPLE)


def _rwkv_sample_post_kernel(yt_ref, r_ref, kh_ref, v_ref, g_ref, rk_ref, lw_ref, lb_ref, o_ref):
    o = _rwkv_post(yt_ref[...].T, r_ref[...], kh_ref[...], v_ref[...], g_ref[...], rk_ref[...], lw_ref[...],
                   lb_ref[...])
    o_ref[0:N_SAMPLE, :] = o.astype(o_ref.dtype)
    o_ref[N_SAMPLE:T_EXT, :] = jnp.zeros((T_EXT - N_SAMPLE, RW_WIDTH), o_ref.dtype)


def _rwkv_sample_post(y_t, r, kh, v, g, p):
    row = lambda n: pl.BlockSpec((1, n), lambda i: (0, 0))
    full = lambda a, b: pl.BlockSpec((a, b), lambda i: (0, 0))
    return pl.pallas_call(
        _rwkv_sample_post_kernel,
        grid=(1,),
        in_specs=[full(RW_WIDTH, N_SAMPLE)] + [full(N_SAMPLE, RW_WIDTH)] * 4 + [row(RW_WIDTH)] * 3,
        out_specs=full(T_EXT, RW_WIDTH),
        out_shape=jax.ShapeDtypeStruct((T_EXT, RW_WIDTH), BF16),
        compiler_params=_cparams(),
        name="rwkv_sample_post",
    )(y_t, r, kh, v, g, p['r_k'], p['lnx_w'], p['lnx_b'])


def _gla_logg(zg, gk_up, gk_b):
    gkd = zg[:, 2 * GLA_KDIM + GLA_WIDTH:2 * GLA_KDIM + GLA_WIDTH + LANES]
    x = _bdot(gkd, gk_up) + gk_b
    return (jnp.minimum(x, 0.0) - jnp.log(1.0 + jnp.exp(-jnp.abs(x)))) * (1.0 / GLA_NORMALIZER)


def _gla_post(o, og_act, norm_w):
    outs = []
    for h in range(GLA_HEADS):
        oh = o[:, h * GLA_DV:(h + 1) * GLA_DV]
        ms = jnp.mean(oh * oh, axis=-1, keepdims=True)
        outs.append(oh * lax.rsqrt(ms + RMS_EPS) * norm_w)
    return jnp.concatenate(outs, axis=1) * og_act


def _gla_sample_pre_kernel(zg_ref, gkup_ref, gkb_ref, q_ref, k_ref, eg_ref):
    zg = zg_ref[...]
    logg = _gla_logg(zg, gkup_ref[...], gkb_ref[...])
    q_ref[...] = zg[:, 0:GLA_KDIM] * (GLA_DK ** -0.5)
    k_ref[...] = zg[:, GLA_KDIM:2 * GLA_KDIM]
    eg_ref[...] = jnp.exp(logg)


def _gla_sample_pre(zg, p):
    full = lambda a, b: pl.BlockSpec((a, b), lambda i: (0, 0))
    vec = jax.ShapeDtypeStruct((N_SAMPLE, GLA_KDIM), F32)
    return pl.pallas_call(
        _gla_sample_pre_kernel,
        grid=(1,),
        in_specs=[pl.BlockSpec((N_SAMPLE, W_GLA_PAD), lambda i: (ROW_SAMPLE // N_SAMPLE, 0)),
                  full(LANES, GLA_KDIM), full(1, GLA_KDIM)],
        out_specs=[full(N_SAMPLE, GLA_KDIM)] * 3,
        out_shape=[vec] * 3,
        compiler_params=_cparams(),
        name="gla_sample_pre",
    )(zg, p['gk_up'], p['gk_b'])


GLA_STEP_BATCH = 8


def _gla_sample_step_kernel(s_ref, q_ref, k_ref, eg_ref, v_ref, so_ref, o_ref):
    eye = _iota((GLA_DK, GLA_DK), 0) == _iota((GLA_DK, GLA_DK), 1)
    for b in range(GLA_STEP_BATCH):
        s = s_ref[b]
        col = lambda ref: jnp.sum(jnp.where(eye, ref[b], 0.0), axis=-1, keepdims=True)
        s_new = col(eg_ref) * s + col(k_ref) * v_ref[b]
        so_ref[b] = s_new
        o_ref[b] = jnp.sum(col(q_ref) * s_new, axis=-2, keepdims=True)


def _gla_sample_step(state, q, k, eg, v):
    kv = lambda t: t.reshape(N_SAMPLE, GLA_HEADS, 1, GLA_DK)
    kspec = pl.BlockSpec((GLA_STEP_BATCH, GLA_HEADS, 1, GLA_DK), lambda i: (i, 0, 0, 0))
    vspec = pl.BlockSpec((GLA_STEP_BATCH, GLA_HEADS, 1, GLA_DV), lambda i: (i, 0, 0, 0))
    sspec = pl.BlockSpec((GLA_STEP_BATCH, GLA_HEADS, GLA_DK, GLA_DV), lambda i: (i, 0, 0, 0))
    s_new, o = pl.pallas_call(
        _gla_sample_step_kernel,
        grid=(N_SAMPLE // GLA_STEP_BATCH,),
        in_specs=[sspec, kspec, kspec, kspec, vspec],
        out_specs=[sspec, vspec],
        out_shape=[jax.ShapeDtypeStruct(state.shape, F32),
                   jax.ShapeDtypeStruct((N_SAMPLE, GLA_HEADS, 1, GLA_DV), F32)],
        compiler_params=_cparams(),
        name="gla_sample_step",
    )(state, kv(q), kv(k), kv(eg), v.reshape(N_SAMPLE, GLA_HEADS, 1, GLA_DV))
    return s_new, o.reshape(N_SAMPLE, GLA_WIDTH)


def _gla_sample_post_kernel(o_ref, og_ref, nw_ref, out_ref):
    o = _gla_post(o_ref[...], og_ref[0:N_SAMPLE, :].astype(F32), nw_ref[...])
    out_ref[0:N_SAMPLE, :] = o.astype(out_ref.dtype)
    out_ref[N_SAMPLE:T_EXT, :] = jnp.zeros((T_EXT - N_SAMPLE, GLA_WIDTH), out_ref.dtype)


def _gla_sample_post(o, zb, p, og_block=OG_BLOCK):
    full = lambda a, b: pl.BlockSpec((a, b), lambda i: (0, 0))
    return pl.pallas_call(
        _gla_sample_post_kernel,
        grid=(1,),
        in_specs=[full(N_SAMPLE, GLA_WIDTH),
                  pl.BlockSpec((T_EXT, GLA_WIDTH), lambda i: (T_MAIN // T_EXT, og_block)),
                  full(1, GLA_DV)],
        out_specs=full(T_EXT, GLA_WIDTH),
        out_shape=jax.ShapeDtypeStruct((T_EXT, GLA_WIDTH), BF16),
        compiler_params=_cparams(),
        name="gla_sample_post",
    )(o, zb, p['norm_w'])


def _mix_route_kernel(or_ref, ore_ref, og_ref, oge_ref, gates_ref, xp_ref, xe_ref, pr_ref, pg_ref, wo_ref, n2_ref,
                      wrh_ref, wrl_ref, br_ref, h1_ref, xn_ref, info_ref, cnt_ref, carry_sc):
    i = pl.program_id(0)
    is_main = i < T_MAIN // TM_DENSE

    @pl.when(i == 0)
    def _():
        carry_sc[...] = jnp.zeros_like(carry_sc)

    sig_r = gates_ref[:, 0:D_MODEL].astype(F32)
    sig_g = gates_ref[:, D_MODEL:2 * D_MODEL].astype(F32)
    o_r = jnp.where(is_main, or_ref[...], ore_ref[...])
    o_g = jnp.where(is_main, og_ref[...], oge_ref[...])
    m = (sig_r * jnp.dot(o_r, pr_ref[...], preferred_element_type=F32)
         + sig_g * jnp.dot(o_g, pg_ref[...], preferred_element_type=F32))
    h = jnp.where(is_main, xp_ref[...], xe_ref[...])
    h1 = h + jnp.dot(m.astype(BF16), wo_ref[...], preferred_element_type=F32)
    h1_ref[...] = h1
    ms = jnp.mean(h1 * h1, axis=-1, keepdims=True)
    xn = h1 * lax.rsqrt(ms + RMS_EPS) * n2_ref[...]
    xn_ref[...] = xn
    xh = xn.astype(BF16)
    xl = (xn - xh.astype(F32)).astype(BF16)
    lg = (jnp.dot(xh, wrh_ref[...], preferred_element_type=F32)
          + jnp.dot(xh, wrl_ref[...], preferred_element_type=F32)
          + jnp.dot(xl, wrh_ref[...], preferred_element_type=F32)) + br_ref[...]

    lane = _iota(lg.shape, 1)
    lanef = lane.astype(F32)
    is_g = jnp.logical_and(lane >= N_EXPERTS, lane < N_EXPERTS + N_GROUPS)
    gl = jnp.where(is_g, lg, NEG_BIG)
    gmax = jnp.max(gl, axis=-1, keepdims=True)
    gsel = jnp.min(jnp.where(jnp.logical_and(is_g, gl == gmax), lanef, 1e9), axis=-1, keepdims=True) - N_EXPERTS
    p_g = 1.0 / jnp.sum(jnp.exp(gl - gmax), axis=-1, keepdims=True)
    grp = (lane // EXPERTS_PER_GROUP).astype(F32)
    in_grp = jnp.logical_and(lane < N_EXPERTS, grp == gsel)
    el = jnp.where(in_grp, lg, NEG_BIG)
    m1 = jnp.max(el, axis=-1, keepdims=True)
    i1 = jnp.min(jnp.where(jnp.logical_and(in_grp, el == m1), lanef, 1e9), axis=-1, keepdims=True)
    in2 = jnp.logical_and(in_grp, lanef != i1)
    el2 = jnp.where(in2, lg, NEG_BIG)
    m2 = jnp.max(el2, axis=-1, keepdims=True)
    i2 = jnp.min(jnp.where(jnp.logical_and(in2, el2 == m2), lanef, 1e9), axis=-1, keepdims=True)
    e2 = jnp.exp(m2 - m1)
    w1 = p_g / (1.0 + e2)
    w2 = p_g * e2 / (1.0 + e2)

    oh1 = lanef == i1
    oh2 = lanef == i2
    cnt = jnp.where(jnp.logical_or(oh1, oh2), 1.0, 0.0)
    tm = cnt.shape[0]
    lstrict = (_iota((tm, tm), 0) > _iota((tm, tm), 1)).astype(BF16)
    before = jnp.dot(lstrict, cnt.astype(BF16), preferred_element_type=F32) + carry_sc[...]
    rank1 = jnp.sum(jnp.where(oh1, before, 0.0), axis=-1, keepdims=True)
    rank2 = jnp.sum(jnp.where(oh2, before, 0.0), axis=-1, keepdims=True)
    carry_sc[...] = carry_sc[...] + jnp.sum(cnt, axis=0, keepdims=True)
    cnt_ref[...] = carry_sc[...]
    info = jnp.where(lane == 0, i1, jnp.where(lane == 1, i2, jnp.where(lane == 2, w1, jnp.where(
        lane == 3, w2, jnp.where(lane == 4, rank1, jnp.where(lane == 5, rank2, 0.0))))))
    info_ref[...] = info


def _mix_route(o_r, o_r_ext, o_g, o_g_ext, zb, xp, xe, p):
    nmain = T_MAIN // TM_DENSE
    tile = lambda n: pl.BlockSpec((TM_DENSE, n), lambda i: (i, 0))
    main = lambda n: pl.BlockSpec((TM_DENSE, n), lambda i: (jnp.minimum(i, nmain - 1), 0))
    ext = lambda n: pl.BlockSpec((T_EXT, n), lambda i: (0, 0))
    const = lambda a, b: pl.BlockSpec((a, b), lambda i: (0, 0), pipeline_mode=pl.Buffered(1))
    return pl.pallas_call(
        _mix_route_kernel,
        grid=(T_ALL // TM_DENSE,),
        in_specs=[main(RW_WIDTH), ext(RW_WIDTH), main(GLA_WIDTH), ext(GLA_WIDTH), tile(2 * D_MODEL),
                  main(D_MODEL), ext(D_MODEL),
                  const(RW_WIDTH, D_MODEL), const(GLA_WIDTH, D_MODEL), const(D_MODEL, D_MODEL),
                  const(1, D_MODEL), const(D_MODEL, LANES), const(D_MODEL, LANES), const(1, LANES)],
        out_specs=[tile(D_MODEL), tile(D_MODEL), tile(LANES), pl.BlockSpec((1, LANES), lambda i: (0, 0))],
        out_shape=[jax.ShapeDtypeStruct((T_ALL, D_MODEL), F32),
                   jax.ShapeDtypeStruct((T_ALL, D_MODEL), F32),
                   jax.ShapeDtypeStruct((T_ALL, LANES), F32),
                   jax.ShapeDtypeStruct((1, LANES), F32)],
        scratch_shapes=[pltpu.VMEM((1, LANES), F32)],
        compiler_params=_cparams(),
        name="mix_route",
    )(o_r, o_r_ext, o_g, o_g_ext, zb, xp, xe, p['p_rwkv'], p['p_gla'], p['w_out'], p['norm2_w'], p['wr_hi'],
      p['wr_lo'], p['b_route'])


def _dispatch_kernel(info_ref, cnt_ref, xq_ref, xs_hbm, d_ref, te_ref, nt_ref,
                     d_vm, d_sm, tab_vm, tab_sm, zbuf, sem_t, sem_x):
    i = pl.program_id(0)
    lane1 = _iota((1, LANES), 1)
    cnt = jnp.where(lane1 < N_EXPERTS, cnt_ref[...], 0.0)
    tiles = jnp.floor((cnt + (TM_MOE - 1)) * (1.0 / TM_MOE))
    upper = (_iota((LANES, LANES), 0) < _iota((LANES, LANES), 1)).astype(BF16)
    tile_start = jnp.dot(jnp.broadcast_to(tiles, (8, LANES)).astype(BF16), upper,
                         preferred_element_type=F32)[0:1, :]
    base = tile_start * TM_MOE

    info = info_ref[...]
    lanef = _iota(info.shape, 1).astype(F32)
    pick = lambda col: jnp.sum(jnp.where(lanef == info[:, col:col + 1], base, 0.0), axis=-1, keepdims=True)
    d1 = pick(0) + info[:, 4:5]
    d2 = pick(1) + info[:, 5:6]
    tm = info.shape[0]
    eye = _iota((tm, tm), 0) == _iota((tm, tm), 1)
    to_row = lambda col: jnp.sum(jnp.where(eye, col, 0.0), axis=0, keepdims=True)
    d_row = jnp.concatenate([to_row(d1), to_row(d2)], axis=1).astype(jnp.int32)
    d_ref[0] = d_row
    d_vm[...] = d_row

    @pl.when(i == 0)
    def _():
        tile_end = tile_start + tiles
        n_tiles = jnp.sum(tiles, axis=-1, keepdims=True)
        eye_l = _iota((LANES, LANES), 0) == _iota((LANES, LANES), 1)
        end_col = jnp.sum(jnp.where(eye_l, tile_end, 0.0), axis=-1, keepdims=True)
        rowl = _iota((LANES, LANES), 0)
        tile_f = _iota((LANES, LANES), 1).astype(F32)
        te = jnp.sum(jnp.where(jnp.logical_and(rowl < N_EXPERTS, end_col <= tile_f), 1.0, 0.0), axis=0,
                     keepdims=True)
        last_e = jnp.max(jnp.where(tiles > 0.0, lane1.astype(F32), 0.0), axis=-1, keepdims=True)
        te = jnp.where(lane1.astype(F32) < n_tiles, jnp.minimum(te, N_EXPERTS - 1.0), last_e)
        te_ref[...] = te.astype(jnp.int32)
        nt_ref[...] = jnp.broadcast_to(n_tiles, (1, LANES)).astype(jnp.int32)
        zbuf[...] = jnp.zeros_like(zbuf)
        pad_lo = (base + cnt).astype(jnp.int32)
        pad_hi = (base + tiles * TM_MOE).astype(jnp.int32)
        nt_row = jnp.broadcast_to(n_tiles, (1, LANES)).astype(jnp.int32)
        tab_vm[...] = jnp.concatenate([pad_lo, pad_hi, nt_row, jnp.zeros((5, LANES), jnp.int32)], axis=0)
        tab_copy = pltpu.make_async_copy(tab_vm, tab_sm, sem_t)
        tab_copy.start()
        tab_copy.wait()

        def zero_row(r):
            return pltpu.make_async_copy(zbuf.at[pl.ds(0, 1), :], xs_hbm.at[pl.ds(r, 1), :], sem_x)

        def zero_tile(t):
            return pltpu.make_async_copy(zbuf, xs_hbm.at[pl.ds(t * TM_MOE, TM_MOE), :], sem_x)

        def each(fn):
            def body(r, carry):
                fn(r)
                return carry
            return body

        for e in range(N_EXPERTS):
            lax.fori_loop(tab_sm[0, e], tab_sm[1, e], each(lambda r: zero_row(r).start()), 0)
        lax.fori_loop(tab_sm[2, 0], NT_MOE, each(lambda t: zero_tile(t).start()), 0)
        for e in range(N_EXPERTS):
            lax.fori_loop(tab_sm[0, e], tab_sm[1, e], each(lambda r: zero_row(r).wait()), 0)
        lax.fori_loop(tab_sm[2, 0], NT_MOE, each(lambda t: zero_tile(t).wait()), 0)

    d_copy = pltpu.make_async_copy(d_vm, d_sm, sem_t)
    d_copy.start()
    d_copy.wait()

    for t in range(tm):
        for half in range(2):
            pltpu.make_async_copy(xq_ref.at[pl.ds(t, 1), :], xs_hbm.at[pl.ds(d_sm[0, half * tm + t], 1), :],
                                  sem_x).start(priority=half)
    for half in range(2):
        pltpu.make_async_copy(xq_ref, xs_hbm.at[pl.ds(0, tm), :], sem_x).wait()


def _dispatch(info, counts, xq):
    nt = T_ALL // TM_DENSE
    return pl.pallas_call(
        _dispatch_kernel,
        grid=(nt,),
        in_specs=[pl.BlockSpec((TM_DENSE, LANES), lambda i: (i, 0)),
                  pl.BlockSpec((1, LANES), lambda i: (0, 0)),
                  pl.BlockSpec((TM_DENSE, D_MODEL), lambda i: (i, 0))],
        out_specs=[pl.BlockSpec(memory_space=pl.ANY),
                   pl.BlockSpec((1, 1, 2 * TM_DENSE), lambda i: (i, 0, 0)),
                   pl.BlockSpec((1, LANES), lambda i: (0, 0)),
                   pl.BlockSpec((1, LANES), lambda i: (0, 0))],
        out_shape=[jax.ShapeDtypeStruct((NT_MOE * TM_MOE, D_MODEL), F32),
                   jax.ShapeDtypeStruct((nt, 1, 2 * TM_DENSE), jnp.int32),
                   jax.ShapeDtypeStruct((1, LANES), jnp.int32),
                   jax.ShapeDtypeStruct((1, LANES), jnp.int32)],
        scratch_shapes=[pltpu.VMEM((1, 2 * TM_DENSE), jnp.int32),
                        pltpu.SMEM((1, 2 * TM_DENSE), jnp.int32),
                        pltpu.VMEM((8, LANES), jnp.int32),
                        pltpu.SMEM((8, LANES), jnp.int32),
                        pltpu.VMEM((TM_MOE, D_MODEL), F32),
                        pltpu.SemaphoreType.DMA(()),
                        pltpu.SemaphoreType.DMA(())],
        compiler_params=_cparams(),
        name="moe_dispatch",
    )(info, counts, xq)


def _moe_kernel(te_ref, nt_ref, xs_ref, w1_hbm, w3_hbm, w2_hbm, o_ref, w1r, w3r, w2r, w1b, w3b, w2b, sem):
    i = pl.program_id(0)
    n_tiles = nt_ref[0]

    def weight_copies(e):
        return [pltpu.make_async_copy(w1_hbm.at[e], w1r, sem), pltpu.make_async_copy(w3_hbm.at[e], w3r, sem),
                pltpu.make_async_copy(w2_hbm.at[e], w2r, sem)]

    @pl.when(i < n_tiles)
    def _():
        e = te_ref[i]

        @pl.when(i == 0)
        def _():
            for c in weight_copies(e):
                c.start()

        @pl.when(jnp.logical_or(i == 0, e != te_ref[jnp.maximum(i - 1, 0)]))
        def _():
            for c in weight_copies(e):
                c.wait()
            w1b[...] = w1r[...].astype(BF16)
            w3b[...] = w3r[...].astype(BF16)
            w2b[...] = w2r[...].astype(BF16)
            nxt = lax.while_loop(
                lambda j: jnp.logical_and(j < n_tiles, te_ref[jnp.minimum(j, NT_MOE - 1)] == e),
                lambda j: j + 1, i + 1)

            @pl.when(nxt < n_tiles)
            def _():
                for c in weight_copies(te_ref[jnp.minimum(nxt, NT_MOE - 1)]):
                    c.start()

        xb = xs_ref[...].astype(BF16)
        h1 = jnp.dot(xb, w1b[...], preferred_element_type=F32)
        h3 = jnp.dot(xb, w3b[...], preferred_element_type=F32)
        hh = (h1 * jax.nn.sigmoid(h1) * h3).astype(BF16)
        o_ref[...] = jnp.dot(hh, w2b[...], preferred_element_type=F32)

    @pl.when(i >= nt_ref[0])
    def _():
        o_ref[...] = jnp.zeros_like(o_ref)


def _moe(tile_expert, n_tiles, xs, w1, w3, w2):
    grid_spec = pltpu.PrefetchScalarGridSpec(
        num_scalar_prefetch=2,
        grid=(NT_MOE,),
        in_specs=[pl.BlockSpec((TM_MOE, D_MODEL), lambda i, te, nt: (jnp.minimum(i, nt[0] - 1), 0)),
                  pl.BlockSpec(memory_space=pl.ANY),
                  pl.BlockSpec(memory_space=pl.ANY),
                  pl.BlockSpec(memory_space=pl.ANY)],
        out_specs=pl.BlockSpec((TM_MOE, D_MODEL), lambda i, te, nt: (i, 0)),
        scratch_shapes=[pltpu.VMEM((D_MODEL, D_EXPERT), F32),
                        pltpu.VMEM((D_MODEL, D_EXPERT), F32),
                        pltpu.VMEM((D_EXPERT, D_MODEL), F32),
                        pltpu.VMEM((D_MODEL, D_EXPERT), BF16),
                        pltpu.VMEM((D_MODEL, D_EXPERT), BF16),
                        pltpu.VMEM((D_EXPERT, D_MODEL), BF16),
                        pltpu.SemaphoreType.DMA(())],
    )
    return pl.pallas_call(
        _moe_kernel,
        grid_spec=grid_spec,
        out_shape=jax.ShapeDtypeStruct((NT_MOE * TM_MOE, D_MODEL), F32),
        compiler_params=_cparams(),
        name="moe_experts",
    )(tile_expert, n_tiles, xs, w1, w3, w2)


def _combine_kernel(dcur_ref, dnext_ref, h1_ref, info_ref, fw_ref, eo_hbm, yp_ref, ye_ref, buf, sem):
    i = pl.program_id(0)
    nmain = T_MAIN // TM_DENSE
    slot = lax.rem(i, 2)

    def row_copy(d_ref, s, t):
        return pltpu.make_async_copy(eo_hbm.at[pl.ds(d_ref[0, 0, t], 1), :], buf.at[s, pl.ds(t, 1), :], sem.at[s])

    def issue_all(d_ref, s):
        for t in range(2 * TM_DENSE):
            row_copy(d_ref, s, t).start()

    @pl.when(i == 0)
    def _():
        issue_all(dcur_ref, slot)

    @pl.when(i + 1 < pl.num_programs(0))
    def _():
        issue_all(dnext_ref, 1 - slot)

    pltpu.make_async_copy(eo_hbm.at[pl.ds(0, 2 * TM_DENSE), :], buf.at[slot], sem.at[slot]).wait()

    info = info_ref[...]
    y = h1_ref[...] + info[:, 2:3] * buf[slot, 0:TM_DENSE, :] + info[:, 3:4] * buf[slot, TM_DENSE:2 * TM_DENSE, :]
    ms = jnp.mean(y * y, axis=-1, keepdims=True)
    out = y * lax.rsqrt(ms + RMS_EPS) * fw_ref[...]

    @pl.when(i < nmain)
    def _():
        yp_ref[...] = out

    @pl.when(i == nmain)
    def _():
        ye_ref[...] = out


def _combine(dst, h1, info, fw, eo):
    nmain = T_MAIN // TM_DENSE
    return pl.pallas_call(
        _combine_kernel,
        grid=(T_ALL // TM_DENSE,),
        in_specs=[pl.BlockSpec((1, 1, 2 * TM_DENSE), lambda i: (i, 0, 0), memory_space=pltpu.SMEM),
                  pl.BlockSpec((1, 1, 2 * TM_DENSE), lambda i: (jnp.minimum(i + 1, nmain), 0, 0),
                               memory_space=pltpu.SMEM),
                  pl.BlockSpec((TM_DENSE, D_MODEL), lambda i: (i, 0)),
                  pl.BlockSpec((TM_DENSE, LANES), lambda i: (i, 0)),
                  pl.BlockSpec((1, D_MODEL), lambda i: (0, 0)),
                  pl.BlockSpec(memory_space=pl.ANY)],
        out_specs=[pl.BlockSpec((TM_DENSE, D_MODEL), lambda i: (jnp.minimum(i, nmain - 1), 0)),
                   pl.BlockSpec((T_EXT, D_MODEL), lambda i: (0, 0))],
        out_shape=[jax.ShapeDtypeStruct((T_MAIN, D_MODEL), F32),
                   jax.ShapeDtypeStruct((T_EXT, D_MODEL), F32)],
        scratch_shapes=[pltpu.VMEM((2, 2 * TM_DENSE, D_MODEL), F32), pltpu.SemaphoreType.DMA((2,))],
        compiler_params=_cparams(),
        name="moe_combine",
    )(dst, dst, h1, info, fw, eo)


def kernel(x_prompt, x_sample, state_rwkv, state_shift, state_gla, meta_tokens, norm1_w, w_in, mu_shift, rw_w0,
           rw_w2, rw_a0, rw_a2, rw_g2, rw_k_k, rw_k_a, rw_r_k, rw_lnx_w, rw_lnx_b, gla_gk_up, gla_gk_b, gla_norm_w,
           p_rwkv, p_gla, w_out, norm2_w, moe_w_group, moe_b_group, moe_w_router, moe_b_router, moe_w1, moe_w3,
           moe_w2, final_norm_w):
    w_t = w_in[0].T
    row = lambda t: t.reshape(1, -1)
    w2a = jnp.zeros((LANES, 2 * RW_WIDTH), F32)
    w2a = w2a.at[0:64, 0:RW_WIDTH].set(rw_w2[0]).at[64:128, RW_WIDTH:].set(rw_a2[0])
    rw = dict(mu=row(mu_shift[0][_SHIFT_PERM]), w0=row(rw_w0[0]), w2a=w2a, a0=row(rw_a0[0]), g2=rw_g2[0],
              k_k=row(rw_k_k[0]), k_a=row(rw_k_a[0]), r_k=row(rw_r_k[0]), lnx_w=row(rw_lnx_w[0]),
              lnx_b=row(rw_lnx_b[0]))
    gk_up = jnp.zeros((LANES, GLA_KDIM), F32).at[0:GLA_LORA].set(gla_gk_up[0])
    gl = dict(gk_up=gk_up, gk_b=row(gla_gk_b[0]), norm_w=row(gla_norm_w[0]))
    w_route = jnp.zeros((D_MODEL, LANES), F32)
    w_route = w_route.at[:, 0:N_EXPERTS].set(moe_w_router[0]).at[:, N_EXPERTS:N_EXPERTS + N_GROUPS].set(moe_w_group[0])
    wr_hi = w_route.astype(BF16)
    wr_lo = (w_route - wr_hi.astype(F32)).astype(BF16)
    b_route = jnp.zeros((1, LANES), F32)
    b_route = b_route.at[0, 0:N_EXPERTS].set(moe_b_router[0]).at[0, N_EXPERTS:N_EXPERTS + N_GROUPS].set(moe_b_group[0])
    mix = dict(p_rwkv=p_rwkv[0].astype(BF16), p_gla=p_gla[0].astype(BF16), w_out=w_out[0].astype(BF16),
               norm2_w=row(norm2_w[0]), wr_hi=wr_hi, wr_lo=wr_lo, b_route=b_route)

    xp = x_prompt.reshape(T_MAIN, D_MODEL)
    xe = jnp.concatenate([x_sample[:, 0, :], jnp.zeros((META_PAD, D_MODEL), F32), meta_tokens,
                          jnp.zeros((T_ALL - ROW_META - CHUNK, D_MODEL), F32)], axis=0)

    xn = _norm1(xp, xe, row(norm1_w[0]))
    zr = _proj(xn, w_t, _RWKV_TILES, 1664, F32, TM_MM, name="proj_rwkv")
    zg = _proj(xn, w_t, _GLA_TILES, 1152, F32, TM_MM, name="proj_gla")
    zb = _proj(xn, w_t, _GATE_TILES, 1024, BF16, 2 * TM_MM, silu_from=4, name="proj_gates")

    o_r, h_fin, o_g, s_fin = _prompt_recurrences(zr, zg, zb, rw, gl)

    r_s, kh_s, v_s, g_s, r_t, w_t, kh_t, v_t, nkk_t, b_t = _rwkv_sample_pre(zr, state_shift[0][:, _SHIFT_PERM], rw)
    rw_new_t, y_t = _rwkv_sample_step(state_rwkv[0].transpose(1, 2, 3, 0), r_t, w_t, kh_t, v_t, nkk_t, b_t)
    rw_new = rw_new_t.transpose(3, 0, 1, 2)
    o_r_ext = _rwkv_sample_post(y_t, r_s, kh_s, v_s, g_s, rw)
    q_s, k_s, eg_s = _gla_sample_pre(zg, gl)
    v_gs = zg[ROW_SAMPLE:ROW_SAMPLE + N_SAMPLE, 2 * GLA_KDIM:2 * GLA_KDIM + GLA_WIDTH]
    gla_new, og_s = _gla_sample_step(state_gla[0], q_s, k_s, eg_s, v_gs)
    o_g_ext = _gla_sample_post(og_s, zb, gl)

    h1, xn2, info, counts = _mix_route(o_r, o_r_ext, o_g, o_g_ext, zb, xp, xe, mix)
    xs, dst, tile_expert, n_tiles = _dispatch(info, counts, xn2)
    eo = _moe(tile_expert[0, :NT_MOE], n_tiles[0, :1], xs, moe_w1[0], moe_w3[0], moe_w2[0])
    y_p, y_e = _combine(dst, h1, info, row(final_norm_w), eo)

    y_prompt = y_p.reshape(N_BATCH, SEQ, D_MODEL)
    y_sample = y_e[0:N_SAMPLE].reshape(N_SAMPLE, 1, D_MODEL)
    new_rwkv_prompt = h_fin.reshape(N_BATCH, RW_HEAD, RW_HEADS, RW_HEAD).transpose(0, 2, 3, 1)[None]
    shift_rows = jnp.concatenate([zr[(b + 1) * SEQ - 1:(b + 1) * SEQ] for b in range(N_BATCH)]
                                 + [zr[ROW_SAMPLE:ROW_SAMPLE + N_SAMPLE]], axis=0)
    shift_rows = shift_rows[:, _SHIFT_INV]
    return (y_prompt, y_sample, new_rwkv_prompt, shift_rows[None, 0:N_BATCH], s_fin[None],
            rw_new[None], shift_rows[None, N_BATCH:], gla_new[None])
```

```python
import functools
import math

import numpy as np
import jax
import jax.numpy as jnp
from jax import lax
from jax.experimental import pallas as pl
from jax.experimental.pallas import tpu as pltpu

F32 = jnp.float32
BF16 = jnp.bfloat16

D_MODEL = 2048
N_BATCH = 4
SEQ = 2048
N_SAMPLE = 128
N_META = 16
RMS_EPS = 1e-6

RW_WIDTH = 1024
RW_HEAD = 64
RW_HEADS = 16
RW_GN_EPS = RW_HEAD * 1e-5
W_SHIFT = 3328
GLA_HEADS = 4
GLA_DK = 128
GLA_DV = 256
GLA_KDIM = 512
GLA_WIDTH = 1024
GLA_LORA = 16
GLA_NORMALIZER = 16.0
W_GLA_PAD = 2304
OG_BLOCK = 2 * D_MODEL // GLA_WIDTH
N_GROUPS = 4
EXPERTS_PER_GROUP = 8
N_EXPERTS = 32
D_EXPERT = 512

CHUNK = 64
N_CHUNKS = SEQ // CHUNK
META_PAD = CHUNK - N_META
T_MAIN = N_BATCH * SEQ
ROW_SAMPLE = T_MAIN
ROW_META = T_MAIN + N_SAMPLE
T_EXT = 256
T_ALL = T_MAIN + T_EXT
LANES = 128

TM_DENSE = 256
TM_MM = 1056
TM_MOE = 256
NT_MOE = (2 * T_ALL) // TM_MOE + N_EXPERTS
NEG_BIG = -1e30

VMEM_LIMIT = 56 * 1024 * 1024

_SHIFT_PERM = np.concatenate([np.arange(0, 1024), np.arange(1088, 2112), np.arange(2112, 3136),
                              np.arange(1024, 1088), np.arange(3136, 3328)])
_SHIFT_INV = np.argsort(_SHIFT_PERM)


def _cparams(n_axes=1):
    return pltpu.CompilerParams(dimension_semantics=("arbitrary",) * n_axes, vmem_limit_bytes=VMEM_LIMIT)


def _bdot(a, b):
    return jnp.dot(a.astype(BF16), b.astype(BF16), preferred_element_type=F32)


def _bdot_nt(a, b):
    return lax.dot_general(a.astype(BF16), b.astype(BF16), (((1,), (1,)), ((), ())), preferred_element_type=F32)


def _bdot_tn(a, b):
    return lax.dot_general(a.astype(BF16), b.astype(BF16), (((0,), (0,)), ((), ())), preferred_element_type=F32)


def _split_dot(m_bf16, x):
    hi = x.astype(BF16)
    lo = (x - hi.astype(F32)).astype(BF16)
    return (jnp.dot(m_bf16, hi, preferred_element_type=F32) + jnp.dot(m_bf16, lo, preferred_element_type=F32))


def _iota(shape, dim):
    return lax.broadcasted_iota(jnp.int32, shape, dim)


def _tri_incl(n):
    return (_iota((n, n), 0) >= _iota((n, n), 1)).astype(BF16)


def _seg_sum(x, width):
    m, n = x.shape
    nb = n // LANES
    bd = ((_iota((LANES, LANES), 0) // width) == (_iota((LANES, LANES), 1) // width)).astype(BF16)
    xs = jnp.concatenate([x[:, j * LANES:(j + 1) * LANES] for j in range(nb)], axis=0)
    s = jnp.dot(xs.astype(BF16), bd, preferred_element_type=F32)
    return jnp.concatenate([s[j * m:(j + 1) * m] for j in range(nb)], axis=1)


def _col_of_row(row):
    n = row.shape[-1]
    eye = _iota((n, n), 0) == _iota((n, n), 1)
    return jnp.sum(jnp.where(eye, jnp.broadcast_to(row, (n, n)), 0.0), axis=-1, keepdims=True)


def _norm1_kernel(xp_ref, xe_ref, w_ref, o_ref):
    i = pl.program_id(0)

    def f(x):
        ms = jnp.mean(x * x, axis=-1, keepdims=True)
        return (x * lax.rsqrt(ms + RMS_EPS) * w_ref[...]).astype(BF16)

    @pl.when(i < T_MAIN // TM_DENSE)
    def _():
        o_ref[...] = f(xp_ref[...])

    @pl.when(i == T_MAIN // TM_DENSE)
    def _():
        o_ref[...] = f(xe_ref[...])


def _norm1(xp, xe, w):
    nmain = T_MAIN // TM_DENSE
    return pl.pallas_call(
        _norm1_kernel,
        grid=(T_ALL // TM_DENSE,),
        in_specs=[pl.BlockSpec((TM_DENSE, D_MODEL), lambda i: (jnp.minimum(i, nmain - 1), 0)),
                  pl.BlockSpec((T_EXT, D_MODEL), lambda i: (0, 0)),
                  pl.BlockSpec((1, D_MODEL), lambda i: (0, 0))],
        out_specs=pl.BlockSpec((TM_DENSE, D_MODEL), lambda i: (i, 0)),
        out_shape=jax.ShapeDtypeStruct((T_ALL, D_MODEL), BF16),
        compiler_params=_cparams(),
        name="norm1",
    )(xp, xe, w)


def _proj_kernel(x_ref, wt_hbm, o_ref, wraw, wbf, sem, *, tiles, silu_from):
    j = pl.program_id(0)
    m = pl.program_id(1)
    tn = wbf.shape[0]

    def copies(jj):
        return [pltpu.make_async_copy(wt_hbm.at[pl.ds(src, n), :], wraw.at[pl.ds(dst, n), :], sem)
                for (src, n, dst) in tiles[jj]]

    @pl.when(jnp.logical_and(j == 0, m == 0))
    def _():
        for c in copies(0):
            c.start()

    for jj in range(len(tiles)):
        @pl.when(jnp.logical_and(j == jj, m == 0))
        def _(jj=jj):
            for c in copies(jj):
                c.wait()
            edge = 0
            for lo, hi in sorted((dst, dst + n) for (_, n, dst) in tiles[jj]) + [(tn, tn)]:
                if lo > edge:
                    wbf[edge:lo, :] = jnp.zeros((lo - edge, wbf.shape[1]), BF16)
                if hi > lo:
                    wbf[lo:hi, :] = wraw[lo:hi, :].astype(BF16)
                edge = hi
            if jj + 1 < len(tiles):
                for c in copies(jj + 1):
                    c.start()

    z = lax.dot_general(x_ref[...], wbf[...], (((1,), (1,)), ((), ())), preferred_element_type=F32)
    if silu_from is None:
        o_ref[...] = z.astype(o_ref.dtype)
    else:
        s = jax.nn.sigmoid(z)
        o_ref[...] = jnp.where(j >= silu_from, z * s, s).astype(o_ref.dtype)


def _proj(x, w_t, tiles, tn, out_dtype, silu_from=None, name="proj"):
    t, k = x.shape
    return pl.pallas_call(
        functools.partial(_proj_kernel, tiles=tiles, silu_from=silu_from),
        grid=(len(tiles), t // TM_MM),
        in_specs=[pl.BlockSpec((TM_MM, k), lambda j, m: (m, 0)),
                  pl.BlockSpec(memory_space=pl.ANY)],
        out_specs=pl.BlockSpec((TM_MM, tn), lambda j, m: (m, j)),
        out_shape=jax.ShapeDtypeStruct((t, tn * len(tiles)), out_dtype),
        scratch_shapes=[pltpu.VMEM((tn, k), F32), pltpu.VMEM((tn, k), BF16), pltpu.SemaphoreType.DMA(())],
        compiler_params=_cparams(2),
        name=name,
    )(x, w_t)


_RWKV_TILES = (((0, 1024, 0), (1088, 640, 1024)),
               ((1728, 1408, 0), (1024, 64, 1408), (3136, 192, 1472)))
_GLA_TILES = (((3328, 1152, 0),), ((4480, 912, 0),))
_GATE_TILES = tuple(((6416 + 1024 * j, 1024, 0),) for j in range(4)) + (((5392, 1024, 0),),)


def _rwkv_pre(z, w0, w2a, a0, g2, k_k, k_a):
    r = z[:, 0:1024]
    k = z[:, 1024:2048]
    v = z[:, 2048:3072]
    wa = z[:, 3072:3200]
    gd = z[:, 3200:3328]
    lane = _iota(wa.shape, 1)
    wa = jnp.where(lane < 64, jnp.tanh(wa), wa)
    up = _bdot(wa, w2a)
    lw = -math.exp(-0.5) * jax.nn.sigmoid(w0 + up[:, :1024])
    a = jax.nn.sigmoid(a0 + up[:, 1024:])
    g = _bdot(jax.nn.sigmoid(gd), g2)
    kk = k * k_k
    kk = kk * lax.rsqrt(jnp.maximum(_seg_sum(kk * kk, RW_HEAD), 1e-24))
    k_h = k * (1.0 + (a - 1.0) * k_a)
    return r, lw, k_h, v, kk, kk * a, g


def _rwkv_post(y, r, k_h, v, g, r_k, lnx_w, lnx_b):
    mean = _seg_sum(y, RW_HEAD) * (1.0 / RW_HEAD)
    d = y - mean
    var = _seg_sum(d * d, RW_HEAD) * (1.0 / RW_HEAD)
    yn = d * lax.rsqrt(var + RW_GN_EPS) * lnx_w + lnx_b
    bonus = _seg_sum(r * k_h * r_k, RW_HEAD) * v
    return (yn + bonus) * g


def _stack2(x):
    lane = _iota(x.shape, 1)
    return jnp.concatenate([jnp.where(lane < RW_HEAD, x, 0.0), jnp.where(lane >= RW_HEAD, x, 0.0)], axis=0)


def _unstack2(x):
    c = x.shape[0] // 2
    return x[:c] + x[c:]


def _prompt_chunk_kernel(zs_ref, mu_ref, w0_ref, w2a_ref, a0_ref, g2_ref, kk_ref, ka_ref, rk_ref, lw_ref, lb_ref,
                         zg_ref, og_ref, gkup_ref, gkb_ref, nw_ref,
                         o_ref, hout_ref, go_ref, sout_ref,
                         prev_sc, h_sc, hmeta_sc, prevmeta_sc, y_sc, s_sc, smeta_sc, *, n_chunks):
    i = pl.program_id(0)
    is_meta = i == 0
    c = lax.rem(jnp.maximum(i - 1, 0), n_chunks)
    first = jnp.logical_and(i >= 1, c == 0)
    last = jnp.logical_and(i >= 1, c == n_chunks - 1)

    @pl.when(is_meta)
    def _():
        h_sc[...] = jnp.zeros_like(h_sc)
        prev_sc[...] = jnp.zeros_like(prev_sc)
        s_sc[...] = jnp.zeros_like(s_sc)

    @pl.when(first)
    def _():
        h_sc[...] = hmeta_sc[...]
        prev_sc[...] = prevmeta_sc[...]
        s_sc[...] = smeta_sc[...]

    rowi = _iota((CHUNK, 1), 0)
    pad_row = jnp.logical_and(is_meta, rowi < META_PAD)
    heads = range(GLA_HEADS)
    ks = [slice(h * GLA_DK, (h + 1) * GLA_DK) for h in heads]
    gla = {}

    def gla_prep():
        zg = zg_ref[...]
        logg = jnp.where(pad_row, 0.0, _gla_logg(zg, gkup_ref[...], gkb_ref[...]))
        bcum = _split_dot(_tri_incl(CHUNK), logg)
        b_end = bcum[CHUNK - 1:CHUNK, :]
        gla['qt'] = zg[:, 0:GLA_KDIM] * (GLA_DK ** -0.5) * jnp.exp(bcum)
        gla['kt'] = zg[:, GLA_KDIM:2 * GLA_KDIM] * jnp.exp(-bcum)
        gla['ke'] = zg[:, GLA_KDIM:2 * GLA_KDIM] * jnp.exp(b_end - bcum)
        gla['e_end'] = jnp.exp(b_end)
        gla['vh'] = [zg[:, 2 * GLA_KDIM + h * GLA_DV:2 * GLA_KDIM + (h + 1) * GLA_DV].astype(BF16) for h in heads]

    def gla_scores():
        causal = _iota((CHUNK, CHUNK), 0) >= _iota((CHUNK, CHUNK), 1)
        gla['a'] = [jnp.where(causal, _bdot_nt(gla['qt'][:, ks[h]], gla['kt'][:, ks[h]]), 0.0) for h in heads]
        gla['s0'] = [s_sc[h] for h in heads]

    def gla_state():
        gla['o_inter'] = [_bdot(gla['qt'][:, ks[h]], gla['s0'][h]) for h in heads]
        gla['s_add'] = [_bdot_tn(gla['ke'][:, ks[h]], gla['vh'][h]) for h in heads]

    def gla_finish():
        outs = [_bdot(gla['a'][h], gla['vh'][h]) + gla['o_inter'][h] for h in heads]
        gla['s_new'] = [_col_of_row(gla['e_end'][:, ks[h]]) * gla['s0'][h] + gla['s_add'][h] for h in heads]
        for h in heads:
            s_sc[h] = gla['s_new'][h]
        go = _gla_post(jnp.concatenate(outs, axis=1), og_ref[...].astype(F32), nw_ref[...])
        go_ref[...] = go.astype(go_ref.dtype)

    zs = zs_ref[...]
    sh = pltpu.roll(zs, 1, 0)
    sh = jnp.where(rowi == 0, prev_sc[...], sh)
    prev_sc[...] = zs[CHUNK - 1:CHUNK, :]
    z = zs + (sh - zs) * mu_ref[...]
    r, lw, k_h, v, kk, b, g = _rwkv_pre(z, w0_ref[...], w2a_ref[...], a0_ref[...], g2_ref[...], kk_ref[...],
                                        ka_ref[...])
    lw = jnp.where(pad_row, 0.0, lw)
    gla_prep()

    cl = _split_dot(_tri_incl(CHUNK), lw)
    cl_end = cl[CHUNK - 1:CHUNK, :]
    e_neg = jnp.exp(-cl)
    e_end = jnp.exp(cl_end - cl)
    kkt = kk * jnp.exp(cl - lw)
    rt = r * jnp.exp(cl)
    bt = b * e_neg
    kt = k_h * e_neg
    bh = b * e_end
    kh = k_h * e_end
    e_c = jnp.exp(cl_end)

    n2 = 2 * CHUNK
    tok_r = jnp.bitwise_and(_iota((n2, n2), 0), CHUNK - 1)
    tok_c = jnp.bitwise_and(_iota((n2, n2), 1), CHUNK - 1)
    strict = tok_r > tok_c
    incl = tok_r >= tok_c

    pairs = range(RW_HEADS // 2)
    sls = [slice(j * LANES, (j + 1) * LANES) for j in pairs]
    s_kkt = [_stack2(kkt[:, sl]) for sl in sls]
    s_rt = [_stack2(rt[:, sl]) for sl in sls]
    s_v = [_stack2(v[:, sl]).astype(BF16) for sl in sls]
    rb = [jnp.concatenate([_stack2(bt[:, sl]), _stack2(kt[:, sl])], axis=0).astype(BF16) for sl in sls]
    aa = [_bdot_nt(s_kkt[j], rb[j]) for j in pairs]
    mm = [_bdot_nt(s_rt[j], rb[j]) for j in pairs]
    h_kv = [_bdot_tn(_stack2(kh[:, sls[j]]), s_v[j]) for j in pairs]
    gla_scores()
    a_b =[jnp.where(strict, aa[j][:, :n2], 0.0).astype(BF16) for j in pairs]
    a_k = [jnp.where(strict, aa[j][:, n2:], 0.0) for j in pairs]
    m_rb = [jnp.where(incl, mm[j][:, :n2], 0.0).astype(BF16) for j in pairs]
    m_rk = [jnp.where(incl, mm[j][:, n2:], 0.0) for j in pairs]
    akv = [_bdot(a_k[j], s_v[j]) for j in pairs]
    y_kv = [_bdot(m_rk[j], s_v[j]) for j in pairs]
    x = [jnp.concatenate([s_kkt[j], akv[j]], axis=1) for j in pairs]
    p = a_b
    x = [x[j] - _bdot(p[j], x[j]) for j in pairs]
    for step in range(5):
        p = [_bdot(p[j], p[j]).astype(BF16) for j in pairs]
        x = [x[j] + _bdot(p[j], x[j]) for j in pairs]
        if step == 1:
            gla_state()
        if step == 3:
            gla_finish()
    zq = [_bdot(m_rb[j], x[j]) for j in pairs]
    zb = [_bdot_tn(_stack2(bh[:, sls[j]]), x[j]) for j in pairs]
    h0 = [h_sc[j] for j in pairs]
    for j in pairs:
        q_eff = _unstack2(s_rt[j] - zq[j][:, :LANES])
        y_in = _unstack2(y_kv[j] - zq[j][:, LANES:])
        y_sc[:, sls[j]] = _bdot(q_eff, h0[j]) + y_in
    h_new = [_col_of_row(e_c[:, sls[j]]) * h0[j] - _bdot(zb[j][:, :LANES], h0[j]) + (h_kv[j] - zb[j][:, LANES:])
             for j in pairs]
    for j in pairs:
        h_sc[j] = h_new[j]

    o = _rwkv_post(y_sc[...], r, k_h, v, g, rk_ref[...], lw_ref[...], lb_ref[...])
    o_ref[...] = o.astype(o_ref.dtype)

    @pl.when(last)
    def _():
        for j in pairs:
            hout_ref[0, :, sls[j]] = _unstack2(h_new[j])
        for h in heads:
            sout_ref[0, h] = gla['s_new'][h]

    @pl.when(is_meta)
    def _():
        hmeta_sc[...] = h_sc[...]
        prevmeta_sc[...] = prev_sc[...]
        smeta_sc[...] = s_sc[...]


def _prompt_recurrences(zr, zg, zb, p, q, n_batch=N_BATCH, n_chunks=N_CHUNKS, meta_block=ROW_META // CHUNK,
                        og_block=OG_BLOCK):
    row = lambda n: pl.BlockSpec((1, n), lambda i: (0, 0))
    full = lambda a, b: pl.BlockSpec((a, b), lambda i: (0, 0))
    blk = lambda i: jnp.where(i == 0, meta_block, i - 1)
    out_blk = lambda i: jnp.maximum(i - 1, 0)
    seq = lambda i: jnp.maximum(i - 1, 0) // n_chunks
    n_rows = n_batch * n_chunks * CHUNK
    return pl.pallas_call(
        functools.partial(_prompt_chunk_kernel, n_chunks=n_chunks),
        grid=(1 + n_batch * n_chunks,),
        in_specs=[pl.BlockSpec((CHUNK, W_SHIFT), lambda i: (blk(i), 0)),
                  row(W_SHIFT), row(RW_WIDTH), full(LANES, 2 * RW_WIDTH), row(RW_WIDTH), full(LANES, RW_WIDTH),
                  row(RW_WIDTH), row(RW_WIDTH), row(RW_WIDTH), row(RW_WIDTH), row(RW_WIDTH),
                  pl.BlockSpec((CHUNK, W_GLA_PAD), lambda i: (blk(i), 0)),
                  pl.BlockSpec((CHUNK, GLA_WIDTH), lambda i: (blk(i), og_block)),
                  full(LANES, GLA_KDIM), full(1, GLA_KDIM), full(1, GLA_DV)],
        out_specs=[pl.BlockSpec((CHUNK, RW_WIDTH), lambda i: (out_blk(i), 0)),
                   pl.BlockSpec((1, RW_HEAD, RW_WIDTH), lambda i: (seq(i), 0, 0)),
                   pl.BlockSpec((CHUNK, GLA_WIDTH), lambda i: (out_blk(i), 0)),
                   pl.BlockSpec((1, GLA_HEADS, GLA_DK, GLA_DV), lambda i: (seq(i), 0, 0, 0))],
        out_shape=[jax.ShapeDtypeStruct((n_rows, RW_WIDTH), BF16),
                   jax.ShapeDtypeStruct((n_batch, RW_HEAD, RW_WIDTH), F32),
                   jax.ShapeDtypeStruct((n_rows, GLA_WIDTH), BF16),
                   jax.ShapeDtypeStruct((n_batch, GLA_HEADS, GLA_DK, GLA_DV), F32)],
        scratch_shapes=[pltpu.VMEM((1, W_SHIFT), F32),
                        pltpu.VMEM((RW_HEADS // 2, LANES, LANES), F32),
                        pltpu.VMEM((RW_HEADS // 2, LANES, LANES), F32),
                        pltpu.VMEM((1, W_SHIFT), F32),
                        pltpu.VMEM((CHUNK, RW_WIDTH), F32),
                        pltpu.VMEM((GLA_HEADS, GLA_DK, GLA_DV), F32),
                        pltpu.VMEM((GLA_HEADS, GLA_DK, GLA_DV), F32)],
        compiler_params=_cparams(),
        name="prompt_recurrences",
    )(zr, p['mu'], p['w0'], p['w2a'], p['a0'], p['g2'], p['k_k'], p['k_a'], p['r_k'], p['lnx_w'], p['lnx_b'],
      zg, zb, q['gk_up'], q['gk_b'], q['norm_w'])


def _rwkv_sample_pre_kernel(zs_ref, prev_ref, mu_ref, w0_ref, w2a_ref, a0_ref, g2_ref, kk_ref, ka_ref,
                            r_ref, kh_ref, v_ref, g_ref, rt_ref, wt_ref, kht_ref, vt_ref, nkkt_ref, bt_ref):
    zs = zs_ref[...]
    z = zs + (prev_ref[...] - zs) * mu_ref[...]
    r, lw, k_h, v, kk, b, g = _rwkv_pre(z, w0_ref[...], w2a_ref[...], a0_ref[...], g2_ref[...], kk_ref[...],
                                        ka_ref[...])
    r_ref[...] = r
    kh_ref[...] = k_h
    v_ref[...] = v
    g_ref[...] = g
    rt_ref[...] = r.T
    wt_ref[...] = jnp.exp(lw).T
    kht_ref[...] = k_h.T
    vt_ref[...] = v.T
    nkkt_ref[...] = (-kk).T
    bt_ref[...] = b.T


def _rwkv_sample_pre(zr, prev, p):
    row = lambda n: pl.BlockSpec((1, n), lambda i: (0, 0))
    full = lambda a, b: pl.BlockSpec((a, b), lambda i: (0, 0))
    vec = jax.ShapeDtypeStruct((N_SAMPLE, RW_WIDTH), F32)
    vec_t = jax.ShapeDtypeStruct((RW_WIDTH, N_SAMPLE), F32)
    return pl.pallas_call(
        _rwkv_sample_pre_kernel,
        grid=(1,),
        in_specs=[pl.BlockSpec((N_SAMPLE, W_SHIFT), lambda i: (ROW_SAMPLE // N_SAMPLE, 0)),
                  full(N_SAMPLE, W_SHIFT),
                  row(W_SHIFT), row(RW_WIDTH), full(LANES, 2 * RW_WIDTH), row(RW_WIDTH), full(LANES, RW_WIDTH),
                  row(RW_WIDTH), row(RW_WIDTH)],
        out_specs=[full(N_SAMPLE, RW_WIDTH)] * 4 + [full(RW_WIDTH, N_SAMPLE)] * 6,
        out_shape=[vec] * 4 + [vec_t] * 6,
        compiler_params=_cparams(),
        name="rwkv_sample_pre",
    )(zr, prev, p['mu'], p['w0'], p['w2a'], p['a0'], p['g2'], p['k_k'], p['k_a'])


def _rwkv_sample_step_kernel(s_ref, r_ref, w_ref, kh_ref, v_ref, nkk_ref, b_ref, so_ref, y_ref):
    s = s_ref[0]
    sa = jnp.sum(s * nkk_ref[0][None], axis=1, keepdims=True)
    s_new = s * w_ref[0][None] + sa * b_ref[0][None] + v_ref[0] * kh_ref[0][None]
    so_ref[0] = s_new
    y_ref[0] = jnp.sum(s_new * r_ref[0][None], axis=1, keepdims=True)


def _rwkv_sample_step(state_t, r_t, w_t, kh_t, v_t, nkk_t, b_t):
    kvec = lambda t: t.reshape(RW_HEADS, RW_HEAD, N_SAMPLE)
    kspec = pl.BlockSpec((1, RW_HEAD, N_SAMPLE), lambda i: (i, 0, 0))
    vspec = pl.BlockSpec((1, RW_HEAD, 1, N_SAMPLE), lambda i: (i, 0, 0, 0))
    sspec = pl.BlockSpec((1, RW_HEAD, RW_HEAD, N_SAMPLE), lambda i: (i, 0, 0, 0))
    s_new, y = pl.pallas_call(
        _rwkv_sample_step_kernel,
        grid=(RW_HEADS,),
        in_specs=[sspec, kspec, kspec, kspec, vspec, kspec, kspec],
        out_specs=[sspec, vspec],
        out_shape=[jax.ShapeDtypeStruct(state_t.shape, F32),
                   jax.ShapeDtypeStruct((RW_HEADS, RW_HEAD, 1, N_SAMPLE), F32)],
        compiler_params=_cparams(),
        name="rwkv_sample_step",
    )(state_t, kvec(r_t), kvec(w_t), kvec(kh_t), v_t.reshape(RW_HEADS, RW_HEAD, 1, N_SAMPLE), kvec(nkk_t), kvec(b_t))
    return s_new, y.reshape(RW_WIDTH, N_SAMPLE)


def _rwkv_sample_post_kernel(yt_ref, r_ref, kh_ref, v_ref, g_ref, rk_ref, lw_ref, lb_ref, o_ref):
    o = _rwkv_post(yt_ref[...].T, r_ref[...], kh_ref[...], v_ref[...], g_ref[...], rk_ref[...], lw_ref[...],
                   lb_ref[...])
    o_ref[0:N_SAMPLE, :] = o.astype(o_ref.dtype)
    o_ref[N_SAMPLE:T_EXT, :] = jnp.zeros((T_EXT - N_SAMPLE, RW_WIDTH), o_ref.dtype)


def _rwkv_sample_post(y_t, r, kh, v, g, p):
    row = lambda n: pl.BlockSpec((1, n), lambda i: (0, 0))
    full = lambda a, b: pl.BlockSpec((a, b), lambda i: (0, 0))
    return pl.pallas_call(
        _rwkv_sample_post_kernel,
        grid=(1,),
        in_specs=[full(RW_WIDTH, N_SAMPLE)] + [full(N_SAMPLE, RW_WIDTH)] * 4 + [row(RW_WIDTH)] * 3,
        out_specs=full(T_EXT, RW_WIDTH),
        out_shape=jax.ShapeDtypeStruct((T_EXT, RW_WIDTH), BF16),
        compiler_params=_cparams(),
        name="rwkv_sample_post",
    )(y_t, r, kh, v, g, p['r_k'], p['lnx_w'], p['lnx_b'])


def _gla_logg(zg, gk_up, gk_b):
    gkd = zg[:, 2 * GLA_KDIM + GLA_WIDTH:2 * GLA_KDIM + GLA_WIDTH + LANES]
    x = _bdot(gkd, gk_up) + gk_b
    return (jnp.minimum(x, 0.0) - jnp.log(1.0 + jnp.exp(-jnp.abs(x)))) * (1.0 / GLA_NORMALIZER)


def _gla_post(o, og_act, norm_w):
    outs = []
    for h in range(GLA_HEADS):
        oh = o[:, h * GLA_DV:(h + 1) * GLA_DV]
        ms = jnp.mean(oh * oh, axis=-1, keepdims=True)
        outs.append(oh * lax.rsqrt(ms + RMS_EPS) * norm_w)
    return jnp.concatenate(outs, axis=1) * og_act


def _gla_sample_pre_kernel(zg_ref, gkup_ref, gkb_ref, q_ref, k_ref, eg_ref):
    zg = zg_ref[...]
    logg = _gla_logg(zg, gkup_ref[...], gkb_ref[...])
    q_ref[...] = zg[:, 0:GLA_KDIM] * (GLA_DK ** -0.5)
    k_ref[...] = zg[:, GLA_KDIM:2 * GLA_KDIM]
    eg_ref[...] = jnp.exp(logg)


def _gla_sample_pre(zg, p):
    full = lambda a, b: pl.BlockSpec((a, b), lambda i: (0, 0))
    vec = jax.ShapeDtypeStruct((N_SAMPLE, GLA_KDIM), F32)
    return pl.pallas_call(
        _gla_sample_pre_kernel,
        grid=(1,),
        in_specs=[pl.BlockSpec((N_SAMPLE, W_GLA_PAD), lambda i: (ROW_SAMPLE // N_SAMPLE, 0)),
                  full(LANES, GLA_KDIM), full(1, GLA_KDIM)],
        out_specs=[full(N_SAMPLE, GLA_KDIM)] * 3,
        out_shape=[vec] * 3,
        compiler_params=_cparams(),
        name="gla_sample_pre",
    )(zg, p['gk_up'], p['gk_b'])


GLA_STEP_BATCH = 8


def _gla_sample_step_kernel(s_ref, q_ref, k_ref, eg_ref, v_ref, so_ref, o_ref):
    eye = _iota((GLA_DK, GLA_DK), 0) == _iota((GLA_DK, GLA_DK), 1)
    for b in range(GLA_STEP_BATCH):
        s = s_ref[b]
        col = lambda ref: jnp.sum(jnp.where(eye, ref[b], 0.0), axis=-1, keepdims=True)
        s_new = col(eg_ref) * s + col(k_ref) * v_ref[b]
        so_ref[b] = s_new
        o_ref[b] = jnp.sum(col(q_ref) * s_new, axis=-2, keepdims=True)


def _gla_sample_step(state, q, k, eg, v):
    kv = lambda t: t.reshape(N_SAMPLE, GLA_HEADS, 1, GLA_DK)
    kspec = pl.BlockSpec((GLA_STEP_BATCH, GLA_HEADS, 1, GLA_DK), lambda i: (i, 0, 0, 0))
    vspec = pl.BlockSpec((GLA_STEP_BATCH, GLA_HEADS, 1, GLA_DV), lambda i: (i, 0, 0, 0))
    sspec = pl.BlockSpec((GLA_STEP_BATCH, GLA_HEADS, GLA_DK, GLA_DV), lambda i: (i, 0, 0, 0))
    s_new, o = pl.pallas_call(
        _gla_sample_step_kernel,
        grid=(N_SAMPLE // GLA_STEP_BATCH,),
        in_specs=[sspec, kspec, kspec, kspec, vspec],
        out_specs=[sspec, vspec],
        out_shape=[jax.ShapeDtypeStruct(state.shape, F32),
                   jax.ShapeDtypeStruct((N_SAMPLE, GLA_HEADS, 1, GLA_DV), F32)],
        compiler_params=_cparams(),
        name="gla_sample_step",
    )(state, kv(q), kv(k), kv(eg), v.reshape(N_SAMPLE, GLA_HEADS, 1, GLA_DV))
    return s_new, o.reshape(N_SAMPLE, GLA_WIDTH)


def _gla_sample_post_kernel(o_ref, og_ref, nw_ref, out_ref):
    o = _gla_post(o_ref[...], og_ref[0:N_SAMPLE, :].astype(F32), nw_ref[...])
    out_ref[0:N_SAMPLE, :] = o.astype(out_ref.dtype)
    out_ref[N_SAMPLE:T_EXT, :] = jnp.zeros((T_EXT - N_SAMPLE, GLA_WIDTH), out_ref.dtype)


def _gla_sample_post(o, zb, p, og_block=OG_BLOCK):
    full = lambda a, b: pl.BlockSpec((a, b), lambda i: (0, 0))
    return pl.pallas_call(
        _gla_sample_post_kernel,
        grid=(1,),
        in_specs=[full(N_SAMPLE, GLA_WIDTH),
                  pl.BlockSpec((T_EXT, GLA_WIDTH), lambda i: (T_MAIN // T_EXT, og_block)),
                  full(1, GLA_DV)],
        out_specs=full(T_EXT, GLA_WIDTH),
        out_shape=jax.ShapeDtypeStruct((T_EXT, GLA_WIDTH), BF16),
        compiler_params=_cparams(),
        name="gla_sample_post",
    )(o, zb, p['norm_w'])


def _mix_route_kernel(or_ref, ore_ref, og_ref, oge_ref, gates_ref, xp_ref, xe_ref, pr_ref, pg_ref, wo_ref, n2_ref,
                      wrh_ref, wrl_ref, br_ref, h1_ref, xn_ref, info_ref, cnt_ref, carry_sc):
    i = pl.program_id(0)
    is_main = i < T_MAIN // TM_DENSE

    @pl.when(i == 0)
    def _():
        carry_sc[...] = jnp.zeros_like(carry_sc)

    sig_r = gates_ref[:, 0:D_MODEL].astype(F32)
    sig_g = gates_ref[:, D_MODEL:2 * D_MODEL].astype(F32)
    o_r = jnp.where(is_main, or_ref[...], ore_ref[...])
    o_g = jnp.where(is_main, og_ref[...], oge_ref[...])
    m = (sig_r * jnp.dot(o_r, pr_ref[...], preferred_element_type=F32)
         + sig_g * jnp.dot(o_g, pg_ref[...], preferred_element_type=F32))
    h = jnp.where(is_main, xp_ref[...], xe_ref[...])
    h1 = h + jnp.dot(m.astype(BF16), wo_ref[...], preferred_element_type=F32)
    h1_ref[...] = h1
    ms = jnp.mean(h1 * h1, axis=-1, keepdims=True)
    xn = h1 * lax.rsqrt(ms + RMS_EPS) * n2_ref[...]
    xn_ref[...] = xn
    xh = xn.astype(BF16)
    xl = (xn - xh.astype(F32)).astype(BF16)
    lg = (jnp.dot(xh, wrh_ref[...], preferred_element_type=F32)
          + jnp.dot(xh, wrl_ref[...], preferred_element_type=F32)
          + jnp.dot(xl, wrh_ref[...], preferred_element_type=F32)) + br_ref[...]

    lane = _iota(lg.shape, 1)
    lanef = lane.astype(F32)
    is_g = jnp.logical_and(lane >= N_EXPERTS, lane < N_EXPERTS + N_GROUPS)
    gl = jnp.where(is_g, lg, NEG_BIG)
    gmax = jnp.max(gl, axis=-1, keepdims=True)
    gsel = jnp.min(jnp.where(jnp.logical_and(is_g, gl == gmax), lanef, 1e9), axis=-1, keepdims=True) - N_EXPERTS
    p_g = 1.0 / jnp.sum(jnp.exp(gl - gmax), axis=-1, keepdims=True)
    grp = (lane // EXPERTS_PER_GROUP).astype(F32)
    in_grp = jnp.logical_and(lane < N_EXPERTS, grp == gsel)
    el = jnp.where(in_grp, lg, NEG_BIG)
    m1 = jnp.max(el, axis=-1, keepdims=True)
    i1 = jnp.min(jnp.where(jnp.logical_and(in_grp, el == m1), lanef, 1e9), axis=-1, keepdims=True)
    in2 = jnp.logical_and(in_grp, lanef != i1)
    el2 = jnp.where(in2, lg, NEG_BIG)
    m2 = jnp.max(el2, axis=-1, keepdims=True)
    i2 = jnp.min(jnp.where(jnp.logical_and(in2, el2 == m2), lanef, 1e9), axis=-1, keepdims=True)
    e2 = jnp.exp(m2 - m1)
    w1 = p_g / (1.0 + e2)
    w2 = p_g * e2 / (1.0 + e2)

    oh1 = lanef == i1
    oh2 = lanef == i2
    cnt = jnp.where(jnp.logical_or(oh1, oh2), 1.0, 0.0)
    tm = cnt.shape[0]
    lstrict = (_iota((tm, tm), 0) > _iota((tm, tm), 1)).astype(BF16)
    before = jnp.dot(lstrict, cnt.astype(BF16), preferred_element_type=F32) + carry_sc[...]
    rank1 = jnp.sum(jnp.where(oh1, before, 0.0), axis=-1, keepdims=True)
    rank2 = jnp.sum(jnp.where(oh2, before, 0.0), axis=-1, keepdims=True)
    carry_sc[...] = carry_sc[...] + jnp.sum(cnt, axis=0, keepdims=True)
    cnt_ref[...] = carry_sc[...]
    info = jnp.where(lane == 0, i1, jnp.where(lane == 1, i2, jnp.where(lane == 2, w1, jnp.where(
        lane == 3, w2, jnp.where(lane == 4, rank1, jnp.where(lane == 5, rank2, 0.0))))))
    info_ref[...] = info


def _mix_route(o_r, o_r_ext, o_g, o_g_ext, zb, xp, xe, p):
    nmain = T_MAIN // TM_DENSE
    tile = lambda n: pl.BlockSpec((TM_DENSE, n), lambda i: (i, 0))
    main = lambda n: pl.BlockSpec((TM_DENSE, n), lambda i: (jnp.minimum(i, nmain - 1), 0))
    ext = lambda n: pl.BlockSpec((T_EXT, n), lambda i: (0, 0))
    const = lambda a, b: pl.BlockSpec((a, b), lambda i: (0, 0), pipeline_mode=pl.Buffered(1))
    return pl.pallas_call(
        _mix_route_kernel,
        grid=(T_ALL // TM_DENSE,),
        in_specs=[main(RW_WIDTH), ext(RW_WIDTH), main(GLA_WIDTH), ext(GLA_WIDTH), tile(2 * D_MODEL),
                  main(D_MODEL), ext(D_MODEL),
                  const(RW_WIDTH, D_MODEL), const(GLA_WIDTH, D_MODEL), const(D_MODEL, D_MODEL),
                  const(1, D_MODEL), const(D_MODEL, LANES), const(D_MODEL, LANES), const(1, LANES)],
        out_specs=[tile(D_MODEL), tile(D_MODEL), tile(LANES), pl.BlockSpec((1, LANES), lambda i: (0, 0))],
        out_shape=[jax.ShapeDtypeStruct((T_ALL, D_MODEL), F32),
                   jax.ShapeDtypeStruct((T_ALL, D_MODEL), F32),
                   jax.ShapeDtypeStruct((T_ALL, LANES), F32),
                   jax.ShapeDtypeStruct((1, LANES), F32)],
        scratch_shapes=[pltpu.VMEM((1, LANES), F32)],
        compiler_params=_cparams(),
        name="mix_route",
    )(o_r, o_r_ext, o_g, o_g_ext, zb, xp, xe, p['p_rwkv'], p['p_gla'], p['w_out'], p['norm2_w'], p['wr_hi'],
      p['wr_lo'], p['b_route'])


def _dispatch_kernel(info_ref, cnt_ref, xq_ref, xs_hbm, d_ref, te_ref, nt_ref,
                     d_vm, d_sm, tab_vm, tab_sm, zbuf, sem_t, sem_x):
    i = pl.program_id(0)
    lane1 = _iota((1, LANES), 1)
    cnt = jnp.where(lane1 < N_EXPERTS, cnt_ref[...], 0.0)
    tiles = jnp.floor((cnt + (TM_MOE - 1)) * (1.0 / TM_MOE))
    upper = (_iota((LANES, LANES), 0) < _iota((LANES, LANES), 1)).astype(BF16)
    tile_start = jnp.dot(jnp.broadcast_to(tiles, (8, LANES)).astype(BF16), upper,
                         preferred_element_type=F32)[0:1, :]
    base = tile_start * TM_MOE

    info = info_ref[...]
    lanef = _iota(info.shape, 1).astype(F32)
    pick = lambda col: jnp.sum(jnp.where(lanef == info[:, col:col + 1], base, 0.0), axis=-1, keepdims=True)
    d1 = pick(0) + info[:, 4:5]
    d2 = pick(1) + info[:, 5:6]
    tm = info.shape[0]
    eye = _iota((tm, tm), 0) == _iota((tm, tm), 1)
    to_row = lambda col: jnp.sum(jnp.where(eye, col, 0.0), axis=0, keepdims=True)
    d_row = jnp.concatenate([to_row(d1), to_row(d2)], axis=1).astype(jnp.int32)
    d_ref[0] = d_row
    d_vm[...] = d_row

    @pl.when(i == 0)
    def _():
        tile_end = tile_start + tiles
        n_tiles = jnp.sum(tiles, axis=-1, keepdims=True)
        eye_l = _iota((LANES, LANES), 0) == _iota((LANES, LANES), 1)
        end_col = jnp.sum(jnp.where(eye_l, tile_end, 0.0), axis=-1, keepdims=True)
        rowl = _iota((LANES, LANES), 0)
        tile_f = _iota((LANES, LANES), 1).astype(F32)
        te = jnp.sum(jnp.where(jnp.logical_and(rowl < N_EXPERTS, end_col <= tile_f), 1.0, 0.0), axis=0,
                     keepdims=True)
        last_e = jnp.max(jnp.where(tiles > 0.0, lane1.astype(F32), 0.0), axis=-1, keepdims=True)
        te = jnp.where(lane1.astype(F32) < n_tiles, jnp.minimum(te, N_EXPERTS - 1.0), last_e)
        te_ref[...] = te.astype(jnp.int32)
        nt_ref[...] = jnp.broadcast_to(n_tiles, (1, LANES)).astype(jnp.int32)
        zbuf[...] = jnp.zeros_like(zbuf)
        pad_lo = (base + cnt).astype(jnp.int32)
        pad_hi = (base + tiles * TM_MOE).astype(jnp.int32)
        nt_row = jnp.broadcast_to(n_tiles, (1, LANES)).astype(jnp.int32)
        tab_vm[...] = jnp.concatenate([pad_lo, pad_hi, nt_row, jnp.zeros((5, LANES), jnp.int32)], axis=0)
        tab_copy = pltpu.make_async_copy(tab_vm, tab_sm, sem_t)
        tab_copy.start()
        tab_copy.wait()

        def zero_row(r):
            return pltpu.make_async_copy(zbuf.at[pl.ds(0, 1), :], xs_hbm.at[pl.ds(r, 1), :], sem_x)

        def zero_tile(t):
            return pltpu.make_async_copy(zbuf, xs_hbm.at[pl.ds(t * TM_MOE, TM_MOE), :], sem_x)

        def each(fn):
            def body(r, carry):
                fn(r)
                return carry
            return body

        for e in range(N_EXPERTS):
            lax.fori_loop(tab_sm[0, e], tab_sm[1, e], each(lambda r: zero_row(r).start()), 0)
        lax.fori_loop(tab_sm[2, 0], NT_MOE, each(lambda t: zero_tile(t).start()), 0)
        for e in range(N_EXPERTS):
            lax.fori_loop(tab_sm[0, e], tab_sm[1, e], each(lambda r: zero_row(r).wait()), 0)
        lax.fori_loop(tab_sm[2, 0], NT_MOE, each(lambda t: zero_tile(t).wait()), 0)

    d_copy = pltpu.make_async_copy(d_vm, d_sm, sem_t)
    d_copy.start()
    d_copy.wait()

    for t in range(tm):
        for half in range(2):
            pltpu.make_async_copy(xq_ref.at[pl.ds(t, 1), :], xs_hbm.at[pl.ds(d_sm[0, half * tm + t], 1), :],
                                  sem_x).start(priority=half)
    for half in range(2):
        pltpu.make_async_copy(xq_ref, xs_hbm.at[pl.ds(0, tm), :], sem_x).wait()


def _dispatch(info, counts, xq):
    nt = T_ALL // TM_DENSE
    return pl.pallas_call(
        _dispatch_kernel,
        grid=(nt,),
        in_specs=[pl.BlockSpec((TM_DENSE, LANES), lambda i: (i, 0)),
                  pl.BlockSpec((1, LANES), lambda i: (0, 0)),
                  pl.BlockSpec((TM_DENSE, D_MODEL), lambda i: (i, 0))],
        out_specs=[pl.BlockSpec(memory_space=pl.ANY),
                   pl.BlockSpec((1, 1, 2 * TM_DENSE), lambda i: (i, 0, 0)),
                   pl.BlockSpec((1, LANES), lambda i: (0, 0)),
                   pl.BlockSpec((1, LANES), lambda i: (0, 0))],
        out_shape=[jax.ShapeDtypeStruct((NT_MOE * TM_MOE, D_MODEL), F32),
                   jax.ShapeDtypeStruct((nt, 1, 2 * TM_DENSE), jnp.int32),
                   jax.ShapeDtypeStruct((1, LANES), jnp.int32),
                   jax.ShapeDtypeStruct((1, LANES), jnp.int32)],
        scratch_shapes=[pltpu.VMEM((1, 2 * TM_DENSE), jnp.int32),
                        pltpu.SMEM((1, 2 * TM_DENSE), jnp.int32),
                        pltpu.VMEM((8, LANES), jnp.int32),
                        pltpu.SMEM((8, LANES), jnp.int32),
                        pltpu.VMEM((TM_MOE, D_MODEL), F32),
                        pltpu.SemaphoreType.DMA(()),
                        pltpu.SemaphoreType.DMA(())],
        compiler_params=_cparams(),
        name="moe_dispatch",
    )(info, counts, xq)


def _moe_kernel(te_ref, nt_ref, xs_ref, w1_hbm, w3_hbm, w2_hbm, o_ref, w1r, w3r, w2r, w1b, w3b, w2b, sem):
    i = pl.program_id(0)
    n_tiles = nt_ref[0]

    def weight_copies(e):
        return [pltpu.make_async_copy(w1_hbm.at[e], w1r, sem), pltpu.make_async_copy(w3_hbm.at[e], w3r, sem),
                pltpu.make_async_copy(w2_hbm.at[e], w2r, sem)]

    @pl.when(i < n_tiles)
    def _():
        e = te_ref[i]

        @pl.when(i == 0)
        def _():
            for c in weight_copies(e):
                c.start()

        @pl.when(jnp.logical_or(i == 0, e != te_ref[jnp.maximum(i - 1, 0)]))
        def _():
            for c in weight_copies(e):
                c.wait()
            w1b[...] = w1r[...].astype(BF16)
            w3b[...] = w3r[...].astype(BF16)
            w2b[...] = w2r[...].astype(BF16)
            nxt = lax.while_loop(
                lambda j: jnp.logical_and(j < n_tiles, te_ref[jnp.minimum(j, NT_MOE - 1)] == e),
                lambda j: j + 1, i + 1)

            @pl.when(nxt < n_tiles)
            def _():
                for c in weight_copies(te_ref[jnp.minimum(nxt, NT_MOE - 1)]):
                    c.start()

        xb = xs_ref[...].astype(BF16)
        h1 = jnp.dot(xb, w1b[...], preferred_element_type=F32)
        h3 = jnp.dot(xb, w3b[...], preferred_element_type=F32)
        hh = (h1 * jax.nn.sigmoid(h1) * h3).astype(BF16)
        o_ref[...] = jnp.dot(hh, w2b[...], preferred_element_type=F32)

    @pl.when(i >= nt_ref[0])
    def _():
        o_ref[...] = jnp.zeros_like(o_ref)


def _moe(tile_expert, n_tiles, xs, w1, w3, w2):
    grid_spec = pltpu.PrefetchScalarGridSpec(
        num_scalar_prefetch=2,
        grid=(NT_MOE,),
        in_specs=[pl.BlockSpec((TM_MOE, D_MODEL), lambda i, te, nt: (jnp.minimum(i, nt[0] - 1), 0)),
                  pl.BlockSpec(memory_space=pl.ANY),
                  pl.BlockSpec(memory_space=pl.ANY),
                  pl.BlockSpec(memory_space=pl.ANY)],
        out_specs=pl.BlockSpec((TM_MOE, D_MODEL), lambda i, te, nt: (i, 0)),
        scratch_shapes=[pltpu.VMEM((D_MODEL, D_EXPERT), F32),
                        pltpu.VMEM((D_MODEL, D_EXPERT), F32),
                        pltpu.VMEM((D_EXPERT, D_MODEL), F32),
                        pltpu.VMEM((D_MODEL, D_EXPERT), BF16),
                        pltpu.VMEM((D_MODEL, D_EXPERT), BF16),
                        pltpu.VMEM((D_EXPERT, D_MODEL), BF16),
                        pltpu.SemaphoreType.DMA(())],
    )
    return pl.pallas_call(
        _moe_kernel,
        grid_spec=grid_spec,
        out_shape=jax.ShapeDtypeStruct((NT_MOE * TM_MOE, D_MODEL), F32),
        compiler_params=_cparams(),
        name="moe_experts",
    )(tile_expert, n_tiles, xs, w1, w3, w2)


def _combine_kernel(dcur_ref, dnext_ref, h1_ref, info_ref, fw_ref, eo_hbm, yp_ref, ye_ref, buf, sem):
    i = pl.program_id(0)
    nmain = T_MAIN // TM_DENSE
    slot = lax.rem(i, 2)

    def row_copy(d_ref, s, t):
        return pltpu.make_async_copy(eo_hbm.at[pl.ds(d_ref[0, 0, t], 1), :], buf.at[s, pl.ds(t, 1), :], sem.at[s])

    def issue_all(d_ref, s):
        for t in range(2 * TM_DENSE):
            row_copy(d_ref, s, t).start(priority=t % 2)

    @pl.when(i == 0)
    def _():
        issue_all(dcur_ref, slot)

    @pl.when(i + 1 < pl.num_programs(0))
    def _():
        issue_all(dnext_ref, 1 - slot)

    pltpu.make_async_copy(eo_hbm.at[pl.ds(0, 2 * TM_DENSE), :], buf.at[slot], sem.at[slot]).wait()

    info = info_ref[...]
    y = h1_ref[...] + info[:, 2:3] * buf[slot, 0:TM_DENSE, :] + info[:, 3:4] * buf[slot, TM_DENSE:2 * TM_DENSE, :]
    ms = jnp.mean(y * y, axis=-1, keepdims=True)
    out = y * lax.rsqrt(ms + RMS_EPS) * fw_ref[...]

    @pl.when(i < nmain)
    def _():
        yp_ref[...] = out

    @pl.when(i == nmain)
    def _():
        ye_ref[...] = out


def _combine(dst, h1, info, fw, eo):
    nmain = T_MAIN // TM_DENSE
    return pl.pallas_call(
        _combine_kernel,
        grid=(T_ALL // TM_DENSE,),
        in_specs=[pl.BlockSpec((1, 1, 2 * TM_DENSE), lambda i: (i, 0, 0), memory_space=pltpu.SMEM),
                  pl.BlockSpec((1, 1, 2 * TM_DENSE), lambda i: (jnp.minimum(i + 1, nmain), 0, 0),
                               memory_space=pltpu.SMEM),
                  pl.BlockSpec((TM_DENSE, D_MODEL), lambda i: (i, 0)),
                  pl.BlockSpec((TM_DENSE, LANES), lambda i: (i, 0)),
                  pl.BlockSpec((1, D_MODEL), lambda i: (0, 0)),
                  pl.BlockSpec(memory_space=pl.ANY)],
        out_specs=[pl.BlockSpec((TM_DENSE, D_MODEL), lambda i: (jnp.minimum(i, nmain - 1), 0)),
                   pl.BlockSpec((T_EXT, D_MODEL), lambda i: (0, 0))],
        out_shape=[jax.ShapeDtypeStruct((T_MAIN, D_MODEL), F32),
                   jax.ShapeDtypeStruct((T_EXT, D_MODEL), F32)],
        scratch_shapes=[pltpu.VMEM((2, 2 * TM_DENSE, D_MODEL), F32), pltpu.SemaphoreType.DMA((2,))],
        compiler_params=_cparams(),
        name="moe_combine",
    )(dst, dst, h1, info, fw, eo)


def kernel(x_prompt, x_sample, state_rwkv, state_shift, state_gla, meta_tokens, norm1_w, w_in, mu_shift, rw_w0,
           rw_w2, rw_a0, rw_a2, rw_g2, rw_k_k, rw_k_a, rw_r_k, rw_lnx_w, rw_lnx_b, gla_gk_up, gla_gk_b, gla_norm_w,
           p_rwkv, p_gla, w_out, norm2_w, moe_w_group, moe_b_group, moe_w_router, moe_b_router, moe_w1, moe_w3,
           moe_w2, final_norm_w):
    w_t = w_in[0].T
    row = lambda t: t.reshape(1, -1)
    w2a = jnp.zeros((LANES, 2 * RW_WIDTH), F32)
    w2a = w2a.at[0:64, 0:RW_WIDTH].set(rw_w2[0]).at[64:128, RW_WIDTH:].set(rw_a2[0])
    rw = dict(mu=row(mu_shift[0][_SHIFT_PERM]), w0=row(rw_w0[0]), w2a=w2a, a0=row(rw_a0[0]), g2=rw_g2[0],
              k_k=row(rw_k_k[0]), k_a=row(rw_k_a[0]), r_k=row(rw_r_k[0]), lnx_w=row(rw_lnx_w[0]),
              lnx_b=row(rw_lnx_b[0]))
    gk_up = jnp.zeros((LANES, GLA_KDIM), F32).at[0:GLA_LORA].set(gla_gk_up[0])
    gl = dict(gk_up=gk_up, gk_b=row(gla_gk_b[0]), norm_w=row(gla_norm_w[0]))
    w_route = jnp.zeros((D_MODEL, LANES), F32)
    w_route = w_route.at[:, 0:N_EXPERTS].set(moe_w_router[0]).at[:, N_EXPERTS:N_EXPERTS + N_GROUPS].set(moe_w_group[0])
    wr_hi = w_route.astype(BF16)
    wr_lo = (w_route - wr_hi.astype(F32)).astype(BF16)
    b_route = jnp.zeros((1, LANES), F32)
    b_route = b_route.at[0, 0:N_EXPERTS].set(moe_b_router[0]).at[0, N_EXPERTS:N_EXPERTS + N_GROUPS].set(moe_b_group[0])
    mix = dict(p_rwkv=p_rwkv[0].astype(BF16), p_gla=p_gla[0].astype(BF16), w_out=w_out[0].astype(BF16),
               norm2_w=row(norm2_w[0]), wr_hi=wr_hi, wr_lo=wr_lo, b_route=b_route)

    xp = x_prompt.reshape(T_MAIN, D_MODEL)
    xe = jnp.concatenate([x_sample[:, 0, :], jnp.zeros((META_PAD, D_MODEL), F32), meta_tokens,
                          jnp.zeros((T_ALL - ROW_META - CHUNK, D_MODEL), F32)], axis=0)

    xn = _norm1(xp, xe, row(norm1_w[0]))
    zr = _proj(xn, w_t, _RWKV_TILES, 1664, F32, name="proj_rwkv")
    zg = _proj(xn, w_t, _GLA_TILES, 1152, F32, name="proj_gla")
    zb = _proj(xn, w_t, _GATE_TILES, 1024, BF16, silu_from=4, name="proj_gates")

    o_r, h_fin, o_g, s_fin = _prompt_recurrences(zr, zg, zb, rw, gl)

    r_s, kh_s, v_s, g_s, r_t, w_t, kh_t, v_t, nkk_t, b_t = _rwkv_sample_pre(zr, state_shift[0][:, _SHIFT_PERM], rw)
    rw_new_t, y_t = _rwkv_sample_step(state_rwkv[0].transpose(1, 2, 3, 0), r_t, w_t, kh_t, v_t, nkk_t, b_t)
    rw_new = rw_new_t.transpose(3, 0, 1, 2)
    o_r_ext = _rwkv_sample_post(y_t, r_s, kh_s, v_s, g_s, rw)
    q_s, k_s, eg_s = _gla_sample_pre(zg, gl)
    v_gs = zg[ROW_SAMPLE:ROW_SAMPLE + N_SAMPLE, 2 * GLA_KDIM:2 * GLA_KDIM + GLA_WIDTH]
    gla_new, og_s = _gla_sample_step(state_gla[0], q_s, k_s, eg_s, v_gs)
    o_g_ext = _gla_sample_post(og_s, zb, gl)

    h1, xn2, info, counts = _mix_route(o_r, o_r_ext, o_g, o_g_ext, zb, xp, xe, mix)
    xs, dst, tile_expert, n_tiles = _dispatch(info, counts, xn2)
    eo = _moe(tile_expert[0, :NT_MOE], n_tiles[0, :1], xs, moe_w1[0], moe_w3[0], moe_w2[0])
    y_p, y_e = _combine(dst, h1, info, row(final_norm_w), eo)

    y_prompt = y_p.reshape(N_BATCH, SEQ, D_MODEL)
    y_sample = y_e[0:N_SAMPLE].reshape(N_SAMPLE, 1, D_MODEL)
    new_rwkv_prompt = h_fin.reshape(N_BATCH, RW_HEAD, RW_HEADS, RW_HEAD).transpose(0, 2, 3, 1)[None]
    shift_rows = jnp.concatenate([zr[(b + 1) * SEQ - 1:(b + 1) * SEQ] for b in range(N_BATCH)]
                                 + [zr[ROW_SAMPLE:ROW_SAMPLE + N_SAMPLE]], axis=0)
    shift_rows = shift_rows[:, _SHIFT_INV]
    return (y_prompt, y_sample, new_rwkv_prompt, shift_rows[None, 0:N_BATCH], s_fin[None],
            rw_new[None], shift_rows[None, N_BATCH:], gla_new[None])
```

```python
import functools
import math

import numpy as np
import jax
import jax.numpy as jnp
from jax import lax
from jax.experimental import pallas as pl
from jax.experimental.pallas import tpu as pltpu

F32 = jnp.float32
BF16 = jnp.bfloat16

D_MODEL = 2048
N_BATCH = 4
SEQ = 2048
N_SAMPLE = 128
N_META = 16
RMS_EPS = 1e-6

RW_WIDTH = 1024
RW_HEAD = 64
RW_HEADS = 16
RW_GN_EPS = RW_HEAD * 1e-5
W_SHIFT = 3328
GLA_HEADS = 4
GLA_DK = 128
GLA_DV = 256
GLA_KDIM = 512
GLA_WIDTH = 1024
GLA_LORA = 16
GLA_NORMALIZER = 16.0
W_GLA_PAD = 2304
OG_BLOCK = 2 * D_MODEL // GLA_WIDTH
N_GROUPS = 4
EXPERTS_PER_GROUP = 8
N_EXPERTS = 32
D_EXPERT = 512

CHUNK = 64
N_CHUNKS = SEQ // CHUNK
META_PAD = CHUNK - N_META
T_MAIN = N_BATCH * SEQ
ROW_SAMPLE = T_MAIN
ROW_META = T_MAIN + N_SAMPLE
T_EXT = 256
T_ALL = T_MAIN + T_EXT
LANES = 128

TM_DENSE = 256
TM_MM = T_ALL // 8
TM_MOE = 256
NT_MOE = (2 * T_ALL) // TM_MOE + N_EXPERTS
NEG_BIG = -1e30

VMEM_LIMIT = 56 * 1024 * 1024

_SHIFT_PERM = np.concatenate([np.arange(0, 1024), np.arange(1088, 2112), np.arange(2112, 3136),
                              np.arange(1024, 1088), np.arange(3136, 3328)])
_SHIFT_INV = np.argsort(_SHIFT_PERM)


def _cparams(n_axes=1):
    return pltpu.CompilerParams(dimension_semantics=("arbitrary",) * n_axes, vmem_limit_bytes=VMEM_LIMIT)


def _bdot(a, b):
    return jnp.dot(a.astype(BF16), b.astype(BF16), preferred_element_type=F32)


def _bdot_nt(a, b):
    return lax.dot_general(a.astype(BF16), b.astype(BF16), (((1,), (1,)), ((), ())), preferred_element_type=F32)


def _bdot_tn(a, b):
    return lax.dot_general(a.astype(BF16), b.astype(BF16), (((0,), (0,)), ((), ())), preferred_element_type=F32)


def _split_dot(m_bf16, x):
    hi = x.astype(BF16)
    lo = (x - hi.astype(F32)).astype(BF16)
    return (jnp.dot(m_bf16, hi, preferred_element_type=F32) + jnp.dot(m_bf16, lo, preferred_element_type=F32))


def _iota(shape, dim):
    return lax.broadcasted_iota(jnp.int32, shape, dim)


def _tri_incl(n):
    return (_iota((n, n), 0) >= _iota((n, n), 1)).astype(BF16)


def _seg_sum(x, width):
    m, n = x.shape
    nb = n // LANES
    bd = ((_iota((LANES, LANES), 0) // width) == (_iota((LANES, LANES), 1) // width)).astype(BF16)
    xs = jnp.concatenate([x[:, j * LANES:(j + 1) * LANES] for j in range(nb)], axis=0)
    s = jnp.dot(xs.astype(BF16), bd, preferred_element_type=F32)
    return jnp.concatenate([s[j * m:(j + 1) * m] for j in range(nb)], axis=1)


def _col_of_row(row):
    n = row.shape[-1]
    eye = _iota((n, n), 0) == _iota((n, n), 1)
    return jnp.sum(jnp.where(eye, jnp.broadcast_to(row, (n, n)), 0.0), axis=-1, keepdims=True)


def _norm1_kernel(xp_ref, xe_ref, w_ref, o_ref):
    i = pl.program_id(0)

    def f(x):
        ms = jnp.mean(x * x, axis=-1, keepdims=True)
        return (x * lax.rsqrt(ms + RMS_EPS) * w_ref[...]).astype(BF16)

    @pl.when(i < T_MAIN // TM_DENSE)
    def _():
        o_ref[...] = f(xp_ref[...])

    @pl.when(i == T_MAIN // TM_DENSE)
    def _():
        o_ref[...] = f(xe_ref[...])


def _norm1(xp, xe, w):
    nmain = T_MAIN // TM_DENSE
    return pl.pallas_call(
        _norm1_kernel,
        grid=(T_ALL // TM_DENSE,),
        in_specs=[pl.BlockSpec((TM_DENSE, D_MODEL), lambda i: (jnp.minimum(i, nmain - 1), 0)),
                  pl.BlockSpec((T_EXT, D_MODEL), lambda i: (0, 0)),
                  pl.BlockSpec((1, D_MODEL), lambda i: (0, 0))],
        out_specs=pl.BlockSpec((TM_DENSE, D_MODEL), lambda i: (i, 0)),
        out_shape=jax.ShapeDtypeStruct((T_ALL, D_MODEL), BF16),
        compiler_params=_cparams(),
        name="norm1",
    )(xp, xe, w)


def _proj_kernel(x_ref, wt_hbm, o_ref, wraw, wbf, sem, *, tiles, silu_from):
    j = pl.program_id(0)
    m = pl.program_id(1)
    tn = wbf.shape[0]

    def copies(jj):
        return [pltpu.make_async_copy(wt_hbm.at[pl.ds(src, n), :], wraw.at[pl.ds(dst, n), :], sem)
                for (src, n, dst) in tiles[jj]]

    @pl.when(jnp.logical_and(j == 0, m == 0))
    def _():
        for c in copies(0):
            c.start()

    for jj in range(len(tiles)):
        @pl.when(jnp.logical_and(j == jj, m == 0))
        def _(jj=jj):
            for c in copies(jj):
                c.wait()
            edge = 0
            for lo, hi in sorted((dst, dst + n) for (_, n, dst) in tiles[jj]) + [(tn, tn)]:
                if lo > edge:
                    wbf[edge:lo, :] = jnp.zeros((lo - edge, wbf.shape[1]), BF16)
                if hi > lo:
                    wbf[lo:hi, :] = wraw[lo:hi, :].astype(BF16)
                edge = hi
            if jj + 1 < len(tiles):
                for c in copies(jj + 1):
                    c.start()

    z = lax.dot_general(x_ref[...], wbf[...], (((1,), (1,)), ((), ())), preferred_element_type=F32)
    if silu_from is None:
        o_ref[...] = z.astype(o_ref.dtype)
    else:
        s = jax.nn.sigmoid(z)
        o_ref[...] = jnp.where(j >= silu_from, z * s, s).astype(o_ref.dtype)


def _proj(x, w_t, tiles, tn, out_dtype, silu_from=None, name="proj"):
    t, k = x.shape
    return pl.pallas_call(
        functools.partial(_proj_kernel, tiles=tiles, silu_from=silu_from),
        grid=(len(tiles), t // TM_MM),
        in_specs=[pl.BlockSpec((TM_MM, k), lambda j, m: (m, 0)),
                  pl.BlockSpec(memory_space=pl.ANY)],
        out_specs=pl.BlockSpec((TM_MM, tn), lambda j, m: (m, j)),
        out_shape=jax.ShapeDtypeStruct((t, tn * len(tiles)), out_dtype),
        scratch_shapes=[pltpu.VMEM((tn, k), F32), pltpu.VMEM((tn, k), BF16), pltpu.SemaphoreType.DMA(())],
        compiler_params=_cparams(2),
        name=name,
    )(x, w_t)


_RWKV_TILES = (((0, 1024, 0), (1088, 640, 1024)),
               ((1728, 1408, 0), (1024, 64, 1408), (3136, 192, 1472)))
_GLA_TILES = (((3328, 1152, 0),), ((4480, 912, 0),))
_GATE_TILES = tuple(((6416 + 1024 * j, 1024, 0),) for j in range(4)) + (((5392, 1024, 0),),)


def _rwkv_pre(z, w0, w2a, a0, g2, k_k, k_a):
    r = z[:, 0:1024]
    k = z[:, 1024:2048]
    v = z[:, 2048:3072]
    wa = z[:, 3072:3200]
    gd = z[:, 3200:3328]
    lane = _iota(wa.shape, 1)
    wa = jnp.where(lane < 64, jnp.tanh(wa), wa)
    up = _bdot(wa, w2a)
    lw = -math.exp(-0.5) * jax.nn.sigmoid(w0 + up[:, :1024])
    a = jax.nn.sigmoid(a0 + up[:, 1024:])
    g = _bdot(jax.nn.sigmoid(gd), g2)
    kk = k * k_k
    kk = kk * lax.rsqrt(jnp.maximum(_seg_sum(kk * kk, RW_HEAD), 1e-24))
    k_h = k * (1.0 + (a - 1.0) * k_a)
    return r, lw, k_h, v, kk, kk * a, g


def _rwkv_post(y, r, k_h, v, g, r_k, lnx_w, lnx_b):
    mean = _seg_sum(y, RW_HEAD) * (1.0 / RW_HEAD)
    d = y - mean
    var = _seg_sum(d * d, RW_HEAD) * (1.0 / RW_HEAD)
    yn = d * lax.rsqrt(var + RW_GN_EPS) * lnx_w + lnx_b
    bonus = _seg_sum(r * k_h * r_k, RW_HEAD) * v
    return (yn + bonus) * g


def _stack2(x):
    lane = _iota(x.shape, 1)
    return jnp.concatenate([jnp.where(lane < RW_HEAD, x, 0.0), jnp.where(lane >= RW_HEAD, x, 0.0)], axis=0)


def _unstack2(x):
    c = x.shape[0] // 2
    return x[:c] + x[c:]


def _prompt_chunk_kernel(zs_ref, mu_ref, w0_ref, w2a_ref, a0_ref, g2_ref, kk_ref, ka_ref, rk_ref, lw_ref, lb_ref,
                         zg_ref, og_ref, gkup_ref, gkb_ref, nw_ref,
                         o_ref, hout_ref, go_ref, sout_ref,
                         prev_sc, h_sc, hmeta_sc, prevmeta_sc, y_sc, s_sc, smeta_sc, *, n_chunks):
    i = pl.program_id(0)
    is_meta = i == 0
    c = lax.rem(jnp.maximum(i - 1, 0), n_chunks)
    first = jnp.logical_and(i >= 1, c == 0)
    last = jnp.logical_and(i >= 1, c == n_chunks - 1)

    @pl.when(is_meta)
    def _():
        h_sc[...] = jnp.zeros_like(h_sc)
        prev_sc[...] = jnp.zeros_like(prev_sc)
        s_sc[...] = jnp.zeros_like(s_sc)

    @pl.when(first)
    def _():
        h_sc[...] = hmeta_sc[...]
        prev_sc[...] = prevmeta_sc[...]
        s_sc[...] = smeta_sc[...]

    rowi = _iota((CHUNK, 1), 0)
    pad_row = jnp.logical_and(is_meta, rowi < META_PAD)
    heads = range(GLA_HEADS)
    ks = [slice(h * GLA_DK, (h + 1) * GLA_DK) for h in heads]
    gla = {}

    def gla_prep():
        zg = zg_ref[...]
        logg = jnp.where(pad_row, 0.0, _gla_logg(zg, gkup_ref[...], gkb_ref[...]))
        bcum = _split_dot(_tri_incl(CHUNK), logg)
        b_end = bcum[CHUNK - 1:CHUNK, :]
        gla['qt'] = zg[:, 0:GLA_KDIM] * (GLA_DK ** -0.5) * jnp.exp(bcum)
        gla['kt'] = zg[:, GLA_KDIM:2 * GLA_KDIM] * jnp.exp(-bcum)
        gla['ke'] = zg[:, GLA_KDIM:2 * GLA_KDIM] * jnp.exp(b_end - bcum)
        gla['e_end'] = jnp.exp(b_end)
        gla['vh'] = [zg[:, 2 * GLA_KDIM + h * GLA_DV:2 * GLA_KDIM + (h + 1) * GLA_DV].astype(BF16) for h in heads]

    def gla_scores():
        causal = _iota((CHUNK, CHUNK), 0) >= _iota((CHUNK, CHUNK), 1)
        gla['a'] = [jnp.where(causal, _bdot_nt(gla['qt'][:, ks[h]], gla['kt'][:, ks[h]]), 0.0) for h in heads]
        gla['s0'] = [s_sc[h] for h in heads]

    def gla_state():
        gla['o_inter'] = [_bdot(gla['qt'][:, ks[h]], gla['s0'][h]) for h in heads]
        gla['s_add'] = [_bdot_tn(gla['ke'][:, ks[h]], gla['vh'][h]) for h in heads]

    def gla_finish():
        outs = [_bdot(gla['a'][h], gla['vh'][h]) + gla['o_inter'][h] for h in heads]
        gla['s_new'] = [_col_of_row(gla['e_end'][:, ks[h]]) * gla['s0'][h] + gla['s_add'][h] for h in heads]
        for h in heads:
            s_sc[h] = gla['s_new'][h]
        go = _gla_post(jnp.concatenate(outs, axis=1), og_ref[...].astype(F32), nw_ref[...])
        go_ref[...] = go.astype(go_ref.dtype)

    zs = zs_ref[...]
    sh = pltpu.roll(zs, 1, 0)
    sh = jnp.where(rowi == 0, prev_sc[...], sh)
    prev_sc[...] = zs[CHUNK - 1:CHUNK, :]
    z = zs + (sh - zs) * mu_ref[...]
    r, lw, k_h, v, kk, b, g = _rwkv_pre(z, w0_ref[...], w2a_ref[...], a0_ref[...], g2_ref[...], kk_ref[...],
                                        ka_ref[...])
    lw = jnp.where(pad_row, 0.0, lw)
    gla_prep()

    cl = _split_dot(_tri_incl(CHUNK), lw)
    cl_end = cl[CHUNK - 1:CHUNK, :]
    e_neg = jnp.exp(-cl)
    e_end = jnp.exp(cl_end - cl)
    kkt = kk * jnp.exp(cl - lw)
    rt = r * jnp.exp(cl)
    bt = b * e_neg
    kt = k_h * e_neg
    bh = b * e_end
    kh = k_h * e_end
    e_c = jnp.exp(cl_end)

    n2 = 2 * CHUNK
    tok_r = jnp.bitwise_and(_iota((n2, n2), 0), CHUNK - 1)
    tok_c = jnp.bitwise_and(_iota((n2, n2), 1), CHUNK - 1)
    strict = tok_r > tok_c
    incl = tok_r >= tok_c

    pairs = range(RW_HEADS // 2)
    sls = [slice(j * LANES, (j + 1) * LANES) for j in pairs]
    s_kkt = [_stack2(kkt[:, sl]) for sl in sls]
    s_rt = [_stack2(rt[:, sl]) for sl in sls]
    s_v = [_stack2(v[:, sl]).astype(BF16) for sl in sls]
    rb = [jnp.concatenate([_stack2(bt[:, sl]), _stack2(kt[:, sl])], axis=0).astype(BF16) for sl in sls]
    aa = [_bdot_nt(s_kkt[j], rb[j]) for j in pairs]
    mm = [_bdot_nt(s_rt[j], rb[j]) for j in pairs]
    h_kv = [_bdot_tn(_stack2(kh[:, sls[j]]), s_v[j]) for j in pairs]
    gla_scores()
    a_b =[jnp.where(strict, aa[j][:, :n2], 0.0).astype(BF16) for j in pairs]
    a_k = [jnp.where(strict, aa[j][:, n2:], 0.0) for j in pairs]
    m_rb = [jnp.where(incl, mm[j][:, :n2], 0.0).astype(BF16) for j in pairs]
    m_rk = [jnp.where(incl, mm[j][:, n2:], 0.0) for j in pairs]
    h0 = [h_sc[j] for j in pairs]
    akv = [_bdot(a_k[j], s_v[j]) for j in pairs]
    y_kv = [_bdot(m_rk[j], s_v[j]) for j in pairs]
    y_h = [_bdot(rt[:, sls[j]], h0[j]) for j in pairs]
    x = [_bdot(s_kkt[j], h0[j]) + akv[j] for j in pairs]
    p = a_b
    x = [x[j] - _bdot(p[j], x[j]) for j in pairs]
    for step in range(5):
        p = [_bdot(p[j], p[j]).astype(BF16) for j in pairs]
        x = [x[j] + _bdot(p[j], x[j]) for j in pairs]
        if step == 1:
            gla_state()
        if step == 3:
            gla_finish()
    zq = [_bdot(m_rb[j], x[j]) for j in pairs]
    zb = [_bdot_tn(_stack2(bh[:, sls[j]]), x[j]) for j in pairs]
    for j in pairs:
        y_sc[:, sls[j]] = y_h[j] + _unstack2(y_kv[j] - zq[j])
    h_new = [_col_of_row(e_c[:, sls[j]]) * h0[j] + (h_kv[j] - zb[j]) for j in pairs]
    for j in pairs:
        h_sc[j] = h_new[j]

    o = _rwkv_post(y_sc[...], r, k_h, v, g, rk_ref[...], lw_ref[...], lb_ref[...])
    o_ref[...] = o.astype(o_ref.dtype)

    @pl.when(last)
    def _():
        for j in pairs:
            hout_ref[0, :, sls[j]] = _unstack2(h_new[j])
        for h in heads:
            sout_ref[0, h] = gla['s_new'][h]

    @pl.when(is_meta)
    def _():
        hmeta_sc[...] = h_sc[...]
        prevmeta_sc[...] = prev_sc[...]
        smeta_sc[...] = s_sc[...]


def _prompt_recurrences(zr, zg, zb, p, q, n_batch=N_BATCH, n_chunks=N_CHUNKS, meta_block=ROW_META // CHUNK,
                        og_block=OG_BLOCK):
    row = lambda n: pl.BlockSpec((1, n), lambda i: (0, 0))
    full = lambda a, b: pl.BlockSpec((a, b), lambda i: (0, 0))
    blk = lambda i: jnp.where(i == 0, meta_block, i - 1)
    out_blk = lambda i: jnp.maximum(i - 1, 0)
    seq = lambda i: jnp.maximum(i - 1, 0) // n_chunks
    n_rows = n_batch * n_chunks * CHUNK
    return pl.pallas_call(
        functools.partial(_prompt_chunk_kernel, n_chunks=n_chunks),
        grid=(1 + n_batch * n_chunks,),
        in_specs=[pl.BlockSpec((CHUNK, W_SHIFT), lambda i: (blk(i), 0)),
                  row(W_SHIFT), row(RW_WIDTH), full(LANES, 2 * RW_WIDTH), row(RW_WIDTH), full(LANES, RW_WIDTH),
                  row(RW_WIDTH), row(RW_WIDTH), row(RW_WIDTH), row(RW_WIDTH), row(RW_WIDTH),
                  pl.BlockSpec((CHUNK, W_GLA_PAD), lambda i: (blk(i), 0)),
                  pl.BlockSpec((CHUNK, GLA_WIDTH), lambda i: (blk(i), og_block)),
                  full(LANES, GLA_KDIM), full(1, GLA_KDIM), full(1, GLA_DV)],
        out_specs=[pl.BlockSpec((CHUNK, RW_WIDTH), lambda i: (out_blk(i), 0)),
                   pl.BlockSpec((1, RW_HEAD, RW_WIDTH), lambda i: (seq(i), 0, 0)),
                   pl.BlockSpec((CHUNK, GLA_WIDTH), lambda i: (out_blk(i), 0)),
                   pl.BlockSpec((1, GLA_HEADS, GLA_DK, GLA_DV), lambda i: (seq(i), 0, 0, 0))],
        out_shape=[jax.ShapeDtypeStruct((n_rows, RW_WIDTH), BF16),
                   jax.ShapeDtypeStruct((n_batch, RW_HEAD, RW_WIDTH), F32),
                   jax.ShapeDtypeStruct((n_rows, GLA_WIDTH), BF16),
                   jax.ShapeDtypeStruct((n_batch, GLA_HEADS, GLA_DK, GLA_DV), F32)],
        scratch_shapes=[pltpu.VMEM((1, W_SHIFT), F32),
                        pltpu.VMEM((RW_HEADS // 2, LANES, LANES), F32),
                        pltpu.VMEM((RW_HEADS // 2, LANES, LANES), F32),
                        pltpu.VMEM((1, W_SHIFT), F32),
                        pltpu.VMEM((CHUNK, RW_WIDTH), F32),
                        pltpu.VMEM((GLA_HEADS, GLA_DK, GLA_DV), F32),
                        pltpu.VMEM((GLA_HEADS, GLA_DK, GLA_DV), F32)],
        compiler_params=_cparams(),
        name="prompt_recurrences",
    )(zr, p['mu'], p['w0'], p['w2a'], p['a0'], p['g2'], p['k_k'], p['k_a'], p['r_k'], p['lnx_w'], p['lnx_b'],
      zg, zb, q['gk_up'], q['gk_b'], q['norm_w'])


def _rwkv_sample_pre_kernel(zs_ref, prev_ref, mu_ref, w0_ref, w2a_ref, a0_ref, g2_ref, kk_ref, ka_ref,
                            r_ref, kh_ref, v_ref, g_ref, rt_ref, wt_ref, kht_ref, vt_ref, nkkt_ref, bt_ref):
    zs = zs_ref[...]
    z = zs + (prev_ref[...] - zs) * mu_ref[...]
    r, lw, k_h, v, kk, b, g = _rwkv_pre(z, w0_ref[...], w2a_ref[...], a0_ref[...], g2_ref[...], kk_ref[...],
                                        ka_ref[...])
    r_ref[...] = r
    kh_ref[...] = k_h
    v_ref[...] = v
    g_ref[...] = g
    rt_ref[...] = r.T
    wt_ref[...] = jnp.exp(lw).T
    kht_ref[...] = k_h.T
    vt_ref[...] = v.T
    nkkt_ref[...] = (-kk).T
    bt_ref[...] = b.T


def _rwkv_sample_pre(zr, prev, p):
    row = lambda n: pl.BlockSpec((1, n), lambda i: (0, 0))
    full = lambda a, b: pl.BlockSpec((a, b), lambda i: (0, 0))
    vec = jax.ShapeDtypeStruct((N_SAMPLE, RW_WIDTH), F32)
    vec_t = jax.ShapeDtypeStruct((RW_WIDTH, N_SAMPLE), F32)
    return pl.pallas_call(
        _rwkv_sample_pre_kernel,
        grid=(1,),
        in_specs=[pl.BlockSpec((N_SAMPLE, W_SHIFT), lambda i: (ROW_SAMPLE // N_SAMPLE, 0)),
                  full(N_SAMPLE, W_SHIFT),
                  row(W_SHIFT), row(RW_WIDTH), full(LANES, 2 * RW_WIDTH), row(RW_WIDTH), full(LANES, RW_WIDTH),
                  row(RW_WIDTH), row(RW_WIDTH)],
        out_specs=[full(N_SAMPLE, RW_WIDTH)] * 4 + [full(RW_WIDTH, N_SAMPLE)] * 6,
        out_shape=[vec] * 4 + [vec_t] * 6,
        compiler_params=_cparams(),
        name="rwkv_sample_pre",
    )(zr, prev, p['mu'], p['w0'], p['w2a'], p['a0'], p['g2'], p['k_k'], p['k_a'])


def _rwkv_sample_step_kernel(s_ref, r_ref, w_ref, kh_ref, v_ref, nkk_ref, b_ref, so_ref, y_ref):
    s = s_ref[0]
    sa = jnp.sum(s * nkk_ref[0][None], axis=1, keepdims=True)
    s_new = s * w_ref[0][None] + sa * b_ref[0][None] + v_ref[0] * kh_ref[0][None]
    so_ref[0] = s_new
    y_ref[0] = jnp.sum(s_new * r_ref[0][None], axis=1, keepdims=True)


def _rwkv_sample_step(state_t, r_t, w_t, kh_t, v_t, nkk_t, b_t):
    kvec = lambda t: t.reshape(RW_HEADS, RW_HEAD, N_SAMPLE)
    kspec = pl.BlockSpec((1, RW_HEAD, N_SAMPLE), lambda i: (i, 0, 0))
    vspec = pl.BlockSpec((1, RW_HEAD, 1, N_SAMPLE), lambda i: (i, 0, 0, 0))
    sspec = pl.BlockSpec((1, RW_HEAD, RW_HEAD, N_SAMPLE), lambda i: (i, 0, 0, 0))
    s_new, y = pl.pallas_call(
        _rwkv_sample_step_kernel,
        grid=(RW_HEADS,),
        in_specs=[sspec, kspec, kspec, kspec, vspec, kspec, kspec],
        out_specs=[sspec, vspec],
        out_shape=[jax.ShapeDtypeStruct(state_t.shape, F32),
                   jax.ShapeDtypeStruct((RW_HEADS, RW_HEAD, 1, N_SAMPLE), F32)],
        compiler_params=_cparams(),
        name="rwkv_sample_step",
    )(state_t, kvec(r_t), kvec(w_t), kvec(kh_t), v_t.reshape(RW_HEADS, RW_HEAD, 1, N_SAMPLE), kvec(nkk_t), kvec(b_t))
    return s_new, y.reshape(RW_WIDTH, N_SAMPLE)


def _rwkv_sample_post_kernel(yt_ref, r_ref, kh_ref, v_ref, g_ref, rk_ref, lw_ref, lb_ref, o_ref):
    o = _rwkv_post(yt_ref[...].T, r_ref[...], kh_ref[...], v_ref[...], g_ref[...], rk_ref[...], lw_ref[...],
                   lb_ref[...])
    o_ref[0:N_SAMPLE, :] = o.astype(o_ref.dtype)
    o_ref[N_SAMPLE:T_EXT, :] = jnp.zeros((T_EXT - N_SAMPLE, RW_WIDTH), o_ref.dtype)


def _rwkv_sample_post(y_t, r, kh, v, g, p):
    row = lambda n: pl.BlockSpec((1, n), lambda i: (0, 0))
    full = lambda a, b: pl.BlockSpec((a, b), lambda i: (0, 0))
    return pl.pallas_call(
        _rwkv_sample_post_kernel,
        grid=(1,),
        in_specs=[full(RW_WIDTH, N_SAMPLE)] + [full(N_SAMPLE, RW_WIDTH)] * 4 + [row(RW_WIDTH)] * 3,
        out_specs=full(T_EXT, RW_WIDTH),
        out_shape=jax.ShapeDtypeStruct((T_EXT, RW_WIDTH), BF16),
        compiler_params=_cparams(),
        name="rwkv_sample_post",
    )(y_t, r, kh, v, g, p['r_k'], p['lnx_w'], p['lnx_b'])


def _gla_logg(zg, gk_up, gk_b):
    gkd = zg[:, 2 * GLA_KDIM + GLA_WIDTH:2 * GLA_KDIM + GLA_WIDTH + LANES]
    x = _bdot(gkd, gk_up) + gk_b
    return (jnp.minimum(x, 0.0) - jnp.log(1.0 + jnp.exp(-jnp.abs(x)))) * (1.0 / GLA_NORMALIZER)


def _gla_post(o, og_act, norm_w):
    outs = []
    for h in range(GLA_HEADS):
        oh = o[:, h * GLA_DV:(h + 1) * GLA_DV]
        ms = jnp.mean(oh * oh, axis=-1, keepdims=True)
        outs.append(oh * lax.rsqrt(ms + RMS_EPS) * norm_w)
    return jnp.concatenate(outs, axis=1) * og_act


def _gla_sample_pre_kernel(zg_ref, gkup_ref, gkb_ref, q_ref, k_ref, eg_ref):
    zg = zg_ref[...]
    logg = _gla_logg(zg, gkup_ref[...], gkb_ref[...])
    q_ref[...] = zg[:, 0:GLA_KDIM] * (GLA_DK ** -0.5)
    k_ref[...] = zg[:, GLA_KDIM:2 * GLA_KDIM]
    eg_ref[...] = jnp.exp(logg)


def _gla_sample_pre(zg, p):
    full = lambda a, b: pl.BlockSpec((a, b), lambda i: (0, 0))
    vec = jax.ShapeDtypeStruct((N_SAMPLE, GLA_KDIM), F32)
    return pl.pallas_call(
        _gla_sample_pre_kernel,
        grid=(1,),
        in_specs=[pl.BlockSpec((N_SAMPLE, W_GLA_PAD), lambda i: (ROW_SAMPLE // N_SAMPLE, 0)),
                  full(LANES, GLA_KDIM), full(1, GLA_KDIM)],
        out_specs=[full(N_SAMPLE, GLA_KDIM)] * 3,
        out_shape=[vec] * 3,
        compiler_params=_cparams(),
        name="gla_sample_pre",
    )(zg, p['gk_up'], p['gk_b'])


GLA_STEP_BATCH = 8


def _gla_sample_step_kernel(s_ref, q_ref, k_ref, eg_ref, v_ref, so_ref, o_ref):
    eye = _iota((GLA_DK, GLA_DK), 0) == _iota((GLA_DK, GLA_DK), 1)
    for b in range(GLA_STEP_BATCH):
        s = s_ref[b]
        col = lambda ref: jnp.sum(jnp.where(eye, ref[b], 0.0), axis=-1, keepdims=True)
        s_new = col(eg_ref) * s + col(k_ref) * v_ref[b]
        so_ref[b] = s_new
        o_ref[b] = jnp.sum(col(q_ref) * s_new, axis=-2, keepdims=True)


def _gla_sample_step(state, q, k, eg, v):
    kv = lambda t: t.reshape(N_SAMPLE, GLA_HEADS, 1, GLA_DK)
    kspec = pl.BlockSpec((GLA_STEP_BATCH, GLA_HEADS, 1, GLA_DK), lambda i: (i, 0, 0, 0))
    vspec = pl.BlockSpec((GLA_STEP_BATCH, GLA_HEADS, 1, GLA_DV), lambda i: (i, 0, 0, 0))
    sspec = pl.BlockSpec((GLA_STEP_BATCH, GLA_HEADS, GLA_DK, GLA_DV), lambda i: (i, 0, 0, 0))
    s_new, o = pl.pallas_call(
        _gla_sample_step_kernel,
        grid=(N_SAMPLE // GLA_STEP_BATCH,),
        in_specs=[sspec, kspec, kspec, kspec, vspec],
        out_specs=[sspec, vspec],
        out_shape=[jax.ShapeDtypeStruct(state.shape, F32),
                   jax.ShapeDtypeStruct((N_SAMPLE, GLA_HEADS, 1, GLA_DV), F32)],
        compiler_params=_cparams(),
        name="gla_sample_step",
    )(state, kv(q), kv(k), kv(eg), v.reshape(N_SAMPLE, GLA_HEADS, 1, GLA_DV))
    return s_new, o.reshape(N_SAMPLE, GLA_WIDTH)


def _gla_sample_post_kernel(o_ref, og_ref, nw_ref, out_ref):
    o = _gla_post(o_ref[...], og_ref[0:N_SAMPLE, :].astype(F32), nw_ref[...])
    out_ref[0:N_SAMPLE, :] = o.astype(out_ref.dtype)
    out_ref[N_SAMPLE:T_EXT, :] = jnp.zeros((T_EXT - N_SAMPLE, GLA_WIDTH), out_ref.dtype)


def _gla_sample_post(o, zb, p, og_block=OG_BLOCK):
    full = lambda a, b: pl.BlockSpec((a, b), lambda i: (0, 0))
    return pl.pallas_call(
        _gla_sample_post_kernel,
        grid=(1,),
        in_specs=[full(N_SAMPLE, GLA_WIDTH),
                  pl.BlockSpec((T_EXT, GLA_WIDTH), lambda i: (T_MAIN // T_EXT, og_block)),
                  full(1, GLA_DV)],
        out_specs=full(T_EXT, GLA_WIDTH),
        out_shape=jax.ShapeDtypeStruct((T_EXT, GLA_WIDTH), BF16),
        compiler_params=_cparams(),
        name="gla_sample_post",
    )(o, zb, p['norm_w'])


def _mix_route_kernel(or_ref, ore_ref, og_ref, oge_ref, gates_ref, xp_ref, xe_ref, pr_ref, pg_ref, wo_ref, n2_ref,
                      wrh_ref, wrl_ref, br_ref, h1_ref, xn_ref, info_ref, cnt_ref, carry_sc):
    i = pl.program_id(0)
    is_main = i < T_MAIN // TM_DENSE

    @pl.when(i == 0)
    def _():
        carry_sc[...] = jnp.zeros_like(carry_sc)

    sig_r = gates_ref[:, 0:D_MODEL].astype(F32)
    sig_g = gates_ref[:, D_MODEL:2 * D_MODEL].astype(F32)
    o_r = jnp.where(is_main, or_ref[...], ore_ref[...])
    o_g = jnp.where(is_main, og_ref[...], oge_ref[...])
    m = (sig_r * jnp.dot(o_r, pr_ref[...], preferred_element_type=F32)
         + sig_g * jnp.dot(o_g, pg_ref[...], preferred_element_type=F32))
    h = jnp.where(is_main, xp_ref[...], xe_ref[...])
    h1 = h + jnp.dot(m.astype(BF16), wo_ref[...], preferred_element_type=F32)
    h1_ref[...] = h1
    ms = jnp.mean(h1 * h1, axis=-1, keepdims=True)
    xn = h1 * lax.rsqrt(ms + RMS_EPS) * n2_ref[...]
    xn_ref[...] = xn
    xh = xn.astype(BF16)
    xl = (xn - xh.astype(F32)).astype(BF16)
    lg = (jnp.dot(xh, wrh_ref[...], preferred_element_type=F32)
          + jnp.dot(xh, wrl_ref[...], preferred_element_type=F32)
          + jnp.dot(xl, wrh_ref[...], preferred_element_type=F32)) + br_ref[...]

    lane = _iota(lg.shape, 1)
    lanef = lane.astype(F32)
    is_g = jnp.logical_and(lane >= N_EXPERTS, lane < N_EXPERTS + N_GROUPS)
    gl = jnp.where(is_g, lg, NEG_BIG)
    gmax = jnp.max(gl, axis=-1, keepdims=True)
    gsel = jnp.min(jnp.where(jnp.logical_and(is_g, gl == gmax), lanef, 1e9), axis=-1, keepdims=True) - N_EXPERTS
    p_g = 1.0 / jnp.sum(jnp.exp(gl - gmax), axis=-1, keepdims=True)
    grp = (lane // EXPERTS_PER_GROUP).astype(F32)
    in_grp = jnp.logical_and(lane < N_EXPERTS, grp == gsel)
    el = jnp.where(in_grp, lg, NEG_BIG)
    m1 = jnp.max(el, axis=-1, keepdims=True)
    i1 = jnp.min(jnp.where(jnp.logical_and(in_grp, el == m1), lanef, 1e9), axis=-1, keepdims=True)
    in2 = jnp.logical_and(in_grp, lanef != i1)
    el2 = jnp.where(in2, lg, NEG_BIG)
    m2 = jnp.max(el2, axis=-1, keepdims=True)
    i2 = jnp.min(jnp.where(jnp.logical_and(in2, el2 == m2), lanef, 1e9), axis=-1, keepdims=True)
    e2 = jnp.exp(m2 - m1)
    w1 = p_g / (1.0 + e2)
    w2 = p_g * e2 / (1.0 + e2)

    oh1 = lanef == i1
    oh2 = lanef == i2
    cnt = jnp.where(jnp.logical_or(oh1, oh2), 1.0, 0.0)
    tm = cnt.shape[0]
    lstrict = (_iota((tm, tm), 0) > _iota((tm, tm), 1)).astype(BF16)
    before = jnp.dot(lstrict, cnt.astype(BF16), preferred_element_type=F32) + carry_sc[...]
    rank1 = jnp.sum(jnp.where(oh1, before, 0.0), axis=-1, keepdims=True)
    rank2 = jnp.sum(jnp.where(oh2, before, 0.0), axis=-1, keepdims=True)
    carry_sc[...] = carry_sc[...] + jnp.sum(cnt, axis=0, keepdims=True)
    cnt_ref[...] = carry_sc[...]
    info = jnp.where(lane == 0, i1, jnp.where(lane == 1, i2, jnp.where(lane == 2, w1, jnp.where(
        lane == 3, w2, jnp.where(lane == 4, rank1, jnp.where(lane == 5, rank2, 0.0))))))
    info_ref[...] = info


def _mix_route(o_r, o_r_ext, o_g, o_g_ext, zb, xp, xe, p):
    nmain = T_MAIN // TM_DENSE
    tile = lambda n: pl.BlockSpec((TM_DENSE, n), lambda i: (i, 0))
    main = lambda n: pl.BlockSpec((TM_DENSE, n), lambda i: (jnp.minimum(i, nmain - 1), 0))
    ext = lambda n: pl.BlockSpec((T_EXT, n), lambda i: (0, 0))
    const = lambda a, b: pl.BlockSpec((a, b), lambda i: (0, 0), pipeline_mode=pl.Buffered(1))
    return pl.pallas_call(
        _mix_route_kernel,
        grid=(T_ALL // TM_DENSE,),
        in_specs=[main(RW_WIDTH), ext(RW_WIDTH), main(GLA_WIDTH), ext(GLA_WIDTH), tile(2 * D_MODEL),
                  main(D_MODEL), ext(D_MODEL),
                  const(RW_WIDTH, D_MODEL), const(GLA_WIDTH, D_MODEL), const(D_MODEL, D_MODEL),
                  const(1, D_MODEL), const(D_MODEL, LANES), const(D_MODEL, LANES), const(1, LANES)],
        out_specs=[tile(D_MODEL), tile(D_MODEL), tile(LANES), pl.BlockSpec((1, LANES), lambda i: (0, 0))],
        out_shape=[jax.ShapeDtypeStruct((T_ALL, D_MODEL), F32),
                   jax.ShapeDtypeStruct((T_ALL, D_MODEL), F32),
                   jax.ShapeDtypeStruct((T_ALL, LANES), F32),
                   jax.ShapeDtypeStruct((1, LANES), F32)],
        scratch_shapes=[pltpu.VMEM((1, LANES), F32)],
        compiler_params=_cparams(),
        name="mix_route",
    )(o_r, o_r_ext, o_g, o_g_ext, zb, xp, xe, p['p_rwkv'], p['p_gla'], p['w_out'], p['norm2_w'], p['wr_hi'],
      p['wr_lo'], p['b_route'])


def _dispatch_kernel(info_ref, cnt_ref, xq_ref, xs_hbm, d_ref, te_ref, nt_ref,
                     d_vm, d_sm, tab_vm, tab_sm, zbuf, sem_t, sem_x):
    i = pl.program_id(0)
    lane1 = _iota((1, LANES), 1)
    cnt = jnp.where(lane1 < N_EXPERTS, cnt_ref[...], 0.0)
    tiles = jnp.floor((cnt + (TM_MOE - 1)) * (1.0 / TM_MOE))
    upper = (_iota((LANES, LANES), 0) < _iota((LANES, LANES), 1)).astype(BF16)
    tile_start = jnp.dot(jnp.broadcast_to(tiles, (8, LANES)).astype(BF16), upper,
                         preferred_element_type=F32)[0:1, :]
    base = tile_start * TM_MOE

    info = info_ref[...]
    lanef = _iota(info.shape, 1).astype(F32)
    pick = lambda col: jnp.sum(jnp.where(lanef == info[:, col:col + 1], base, 0.0), axis=-1, keepdims=True)
    d1 = pick(0) + info[:, 4:5]
    d2 = pick(1) + info[:, 5:6]
    tm = info.shape[0]
    eye = _iota((tm, tm), 0) == _iota((tm, tm), 1)
    to_row = lambda col: jnp.sum(jnp.where(eye, col, 0.0), axis=0, keepdims=True)
    d_row = jnp.concatenate([to_row(d1), to_row(d2)], axis=1).astype(jnp.int32)
    d_ref[0] = d_row
    d_vm[...] = d_row

    @pl.when(i == 0)
    def _():
        tile_end = tile_start + tiles
        n_tiles = jnp.sum(tiles, axis=-1, keepdims=True)
        eye_l = _iota((LANES, LANES), 0) == _iota((LANES, LANES), 1)
        end_col = jnp.sum(jnp.where(eye_l, tile_end, 0.0), axis=-1, keepdims=True)
        rowl = _iota((LANES, LANES), 0)
        tile_f = _iota((LANES, LANES), 1).astype(F32)
        te = jnp.sum(jnp.where(jnp.logical_and(rowl < N_EXPERTS, end_col <= tile_f), 1.0, 0.0), axis=0,
                     keepdims=True)
        last_e = jnp.max(jnp.where(tiles > 0.0, lane1.astype(F32), 0.0), axis=-1, keepdims=True)
        te = jnp.where(lane1.astype(F32) < n_tiles, jnp.minimum(te, N_EXPERTS - 1.0), last_e)
        te_ref[...] = te.astype(jnp.int32)
        nt_ref[...] = jnp.broadcast_to(n_tiles, (1, LANES)).astype(jnp.int32)
        zbuf[...] = jnp.zeros_like(zbuf)
        pad_lo = (base + cnt).astype(jnp.int32)
        pad_hi = (base + tiles * TM_MOE).astype(jnp.int32)
        nt_row = jnp.broadcast_to(n_tiles, (1, LANES)).astype(jnp.int32)
        tab_vm[...] = jnp.concatenate([pad_lo, pad_hi, nt_row, jnp.zeros((5, LANES), jnp.int32)], axis=0)
        tab_copy = pltpu.make_async_copy(tab_vm, tab_sm, sem_t)
        tab_copy.start()
        tab_copy.wait()

        def zero_row(r):
            return pltpu.make_async_copy(zbuf.at[pl.ds(0, 1), :], xs_hbm.at[pl.ds(r, 1), :], sem_x)

        def zero_tile(t):
            return pltpu.make_async_copy(zbuf, xs_hbm.at[pl.ds(t * TM_MOE, TM_MOE), :], sem_x)

        def each(fn):
            def body(r, carry):
                fn(r)
                return carry
            return body

        for e in range(N_EXPERTS):
            lax.fori_loop(tab_sm[0, e], tab_sm[1, e], each(lambda r: zero_row(r).start()), 0)
        lax.fori_loop(tab_sm[2, 0], NT_MOE, each(lambda t: zero_tile(t).start()), 0)
        for e in range(N_EXPERTS):
            lax.fori_loop(tab_sm[0, e], tab_sm[1, e], each(lambda r: zero_row(r).wait()), 0)
        lax.fori_loop(tab_sm[2, 0], NT_MOE, each(lambda t: zero_tile(t).wait()), 0)

    d_copy = pltpu.make_async_copy(d_vm, d_sm, sem_t)
    d_copy.start()
    d_copy.wait()

    for t in range(tm):
        for half in range(2):
            pltpu.make_async_copy(xq_ref.at[pl.ds(t, 1), :], xs_hbm.at[pl.ds(d_sm[0, half * tm + t], 1), :],
                                  sem_x).start(priority=half)
    for half in range(2):
        pltpu.make_async_copy(xq_ref, xs_hbm.at[pl.ds(0, tm), :], sem_x).wait()


def _dispatch(info, counts, xq):
    nt = T_ALL // TM_DENSE
    return pl.pallas_call(
        _dispatch_kernel,
        grid=(nt,),
        in_specs=[pl.BlockSpec((TM_DENSE, LANES), lambda i: (i, 0)),
                  pl.BlockSpec((1, LANES), lambda i: (0, 0)),
                  pl.BlockSpec((TM_DENSE, D_MODEL), lambda i: (i, 0))],
        out_specs=[pl.BlockSpec(memory_space=pl.ANY),
                   pl.BlockSpec((1, 1, 2 * TM_DENSE), lambda i: (i, 0, 0)),
                   pl.BlockSpec((1, LANES), lambda i: (0, 0)),
                   pl.BlockSpec((1, LANES), lambda i: (0, 0))],
        out_shape=[jax.ShapeDtypeStruct((NT_MOE * TM_MOE, D_MODEL), F32),
                   jax.ShapeDtypeStruct((nt, 1, 2 * TM_DENSE), jnp.int32),
                   jax.ShapeDtypeStruct((1, LANES), jnp.int32),
                   jax.ShapeDtypeStruct((1, LANES), jnp.int32)],
        scratch_shapes=[pltpu.VMEM((1, 2 * TM_DENSE), jnp.int32),
                        pltpu.SMEM((1, 2 * TM_DENSE), jnp.int32),
                        pltpu.VMEM((8, LANES), jnp.int32),
                        pltpu.SMEM((8, LANES), jnp.int32),
                        pltpu.VMEM((TM_MOE, D_MODEL), F32),
                        pltpu.SemaphoreType.DMA(()),
                        pltpu.SemaphoreType.DMA(())],
        compiler_params=_cparams(),
        name="moe_dispatch",
    )(info, counts, xq)


def _moe_kernel(te_ref, nt_ref, xs_ref, w1_hbm, w3_hbm, w2_hbm, o_ref, w1r, w3r, w2r, w1b, w3b, w2b, sem):
    i = pl.program_id(0)
    n_tiles = nt_ref[0]

    def weight_copies(e):
        return [pltpu.make_async_copy(w1_hbm.at[e], w1r, sem), pltpu.make_async_copy(w3_hbm.at[e], w3r, sem),
                pltpu.make_async_copy(w2_hbm.at[e], w2r, sem)]

    @pl.when(i < n_tiles)
    def _():
        e = te_ref[i]

        @pl.when(i == 0)
        def _():
            for c in weight_copies(e):
                c.start()

        @pl.when(jnp.logical_or(i == 0, e != te_ref[jnp.maximum(i - 1, 0)]))
        def _():
            for c in weight_copies(e):
                c.wait()
            w1b[...] = w1r[...].astype(BF16)
            w3b[...] = w3r[...].astype(BF16)
            w2b[...] = w2r[...].astype(BF16)
            nxt = lax.while_loop(
                lambda j: jnp.logical_and(j < n_tiles, te_ref[jnp.minimum(j, NT_MOE - 1)] == e),
                lambda j: j + 1, i + 1)

            @pl.when(nxt < n_tiles)
            def _():
                for c in weight_copies(te_ref[jnp.minimum(nxt, NT_MOE - 1)]):
                    c.start()

        xb = xs_ref[...].astype(BF16)
        h1 = jnp.dot(xb, w1b[...], preferred_element_type=F32)
        h3 = jnp.dot(xb, w3b[...], preferred_element_type=F32)
        hh = (h1 * jax.nn.sigmoid(h1) * h3).astype(BF16)
        o_ref[...] = jnp.dot(hh, w2b[...], preferred_element_type=F32)

    @pl.when(i >= nt_ref[0])
    def _():
        o_ref[...] = jnp.zeros_like(o_ref)


def _moe(tile_expert, n_tiles, xs, w1, w3, w2):
    grid_spec = pltpu.PrefetchScalarGridSpec(
        num_scalar_prefetch=2,
        grid=(NT_MOE,),
        in_specs=[pl.BlockSpec((TM_MOE, D_MODEL), lambda i, te, nt: (jnp.minimum(i, nt[0] - 1), 0)),
                  pl.BlockSpec(memory_space=pl.ANY),
                  pl.BlockSpec(memory_space=pl.ANY),
                  pl.BlockSpec(memory_space=pl.ANY)],
        out_specs=pl.BlockSpec((TM_MOE, D_MODEL), lambda i, te, nt: (i, 0)),
        scratch_shapes=[pltpu.VMEM((D_MODEL, D_EXPERT), F32),
                        pltpu.VMEM((D_MODEL, D_EXPERT), F32),
                        pltpu.VMEM((D_EXPERT, D_MODEL), F32),
                        pltpu.VMEM((D_MODEL, D_EXPERT), BF16),
                        pltpu.VMEM((D_MODEL, D_EXPERT), BF16),
                        pltpu.VMEM((D_EXPERT, D_MODEL), BF16),
                        pltpu.SemaphoreType.DMA(())],
    )
    return pl.pallas_call(
        _moe_kernel,
        grid_spec=grid_spec,
        out_shape=jax.ShapeDtypeStruct((NT_MOE * TM_MOE, D_MODEL), F32),
        compiler_params=_cparams(),
        name="moe_experts",
    )(tile_expert, n_tiles, xs, w1, w3, w2)


def _combine_kernel(dcur_ref, dnext_ref, h1_ref, info_ref, fw_ref, eo_hbm, yp_ref, ye_ref, buf, sem):
    i = pl.program_id(0)
    nmain = T_MAIN // TM_DENSE
    slot = lax.rem(i, 2)

    def row_copy(d_ref, s, t):
        return pltpu.make_async_copy(eo_hbm.at[pl.ds(d_ref[0, 0, t], 1), :], buf.at[s, pl.ds(t, 1), :], sem.at[s])

    def issue_all(d_ref, s):
        for t in range(2 * TM_DENSE):
            row_copy(d_ref, s, t).start(priority=t % 2)

    @pl.when(i == 0)
    def _():
        issue_all(dcur_ref, slot)

    @pl.when(i + 1 < pl.num_programs(0))
    def _():
        issue_all(dnext_ref, 1 - slot)

    pltpu.make_async_copy(eo_hbm.at[pl.ds(0, 2 * TM_DENSE), :], buf.at[slot], sem.at[slot]).wait()

    info = info_ref[...]
    y = h1_ref[...] + info[:, 2:3] * buf[slot, 0:TM_DENSE, :] + info[:, 3:4] * buf[slot, TM_DENSE:2 * TM_DENSE, :]
    ms = jnp.mean(y * y, axis=-1, keepdims=True)
    out = y * lax.rsqrt(ms + RMS_EPS) * fw_ref[...]

    @pl.when(i < nmain)
    def _():
        yp_ref[...] = out

    @pl.when(i == nmain)
    def _():
        ye_ref[...] = out


def _combine(dst, h1, info, fw, eo):
    nmain = T_MAIN // TM_DENSE
    return pl.pallas_call(
        _combine_kernel,
        grid=(T_ALL // TM_DENSE,),
        in_specs=[pl.BlockSpec((1, 1, 2 * TM_DENSE), lambda i: (i, 0, 0), memory_space=pltpu.SMEM),
                  pl.BlockSpec((1, 1, 2 * TM_DENSE), lambda i: (jnp.minimum(i + 1, nmain), 0, 0),
                               memory_space=pltpu.SMEM),
                  pl.BlockSpec((TM_DENSE, D_MODEL), lambda i: (i, 0)),
                  pl.BlockSpec((TM_DENSE, LANES), lambda i: (i, 0)),
                  pl.BlockSpec((1, D_MODEL), lambda i: (0, 0)),
                  pl.BlockSpec(memory_space=pl.ANY)],
        out_specs=[pl.BlockSpec((TM_DENSE, D_MODEL), lambda i: (jnp.minimum(i, nmain - 1), 0)),
                   pl.BlockSpec((T_EXT, D_MODEL), lambda i: (0, 0))],
        out_shape=[jax.ShapeDtypeStruct((T_MAIN, D_MODEL), F32),
                   jax.ShapeDtypeStruct((T_EXT, D_MODEL), F32)],
        scratch_shapes=[pltpu.VMEM((2, 2 * TM_DENSE, D_MODEL), F32), pltpu.SemaphoreType.DMA((2,))],
        compiler_params=_cparams(),
        name="moe_combine",
    )(dst, dst, h1, info, fw, eo)


def kernel(x_prompt, x_sample, state_rwkv, state_shift, state_gla, meta_tokens, norm1_w, w_in, mu_shift, rw_w0,
           rw_w2, rw_a0, rw_a2, rw_g2, rw_k_k, rw_k_a, rw_r_k, rw_lnx_w, rw_lnx_b, gla_gk_up, gla_gk_b, gla_norm_w,
           p_rwkv, p_gla, w_out, norm2_w, moe_w_group, moe_b_group, moe_w_router, moe_b_router, moe_w1, moe_w3,
           moe_w2, final_norm_w):
    w_t = w_in[0].T
    row = lambda t: t.reshape(1, -1)
    w2a = jnp.zeros((LANES, 2 * RW_WIDTH), F32)
    w2a = w2a.at[0:64, 0:RW_WIDTH].set(rw_w2[0]).at[64:128, RW_WIDTH:].set(rw_a2[0])
    rw = dict(mu=row(mu_shift[0][_SHIFT_PERM]), w0=row(rw_w0[0]), w2a=w2a, a0=row(rw_a0[0]), g2=rw_g2[0],
              k_k=row(rw_k_k[0]), k_a=row(rw_k_a[0]), r_k=row(rw_r_k[0]), lnx_w=row(rw_lnx_w[0]),
              lnx_b=row(rw_lnx_b[0]))
    gk_up = jnp.zeros((LANES, GLA_KDIM), F32).at[0:GLA_LORA].set(gla_gk_up[0])
    gl = dict(gk_up=gk_up, gk_b=row(gla_gk_b[0]), norm_w=row(gla_norm_w[0]))
    w_route = jnp.zeros((D_MODEL, LANES), F32)
    w_route = w_route.at[:, 0:N_EXPERTS].set(moe_w_router[0]).at[:, N_EXPERTS:N_EXPERTS + N_GROUPS].set(moe_w_group[0])
    wr_hi = w_route.astype(BF16)
    wr_lo = (w_route - wr_hi.astype(F32)).astype(BF16)
    b_route = jnp.zeros((1, LANES), F32)
    b_route = b_route.at[0, 0:N_EXPERTS].set(moe_b_router[0]).at[0, N_EXPERTS:N_EXPERTS + N_GROUPS].set(moe_b_group[0])
    mix = dict(p_rwkv=p_rwkv[0].astype(BF16), p_gla=p_gla[0].astype(BF16), w_out=w_out[0].astype(BF16),
               norm2_w=row(norm2_w[0]), wr_hi=wr_hi, wr_lo=wr_lo, b_route=b_route)

    xp = x_prompt.reshape(T_MAIN, D_MODEL)
    xe = jnp.concatenate([x_sample[:, 0, :], jnp.zeros((META_PAD, D_MODEL), F32), meta_tokens,
                          jnp.zeros((T_ALL - ROW_META - CHUNK, D_MODEL), F32)], axis=0)

    xn = _norm1(xp, xe, row(norm1_w[0]))
    zr = _proj(xn, w_t, _RWKV_TILES, W_SHIFT // 2, F32, name="proj_rwkv")
    zg = _proj(xn, w_t, _GLA_TILES, W_GLA_PAD // 2, F32, name="proj_gla")
    zb = _proj(xn, w_t, _GATE_TILES, GLA_WIDTH, BF16, silu_from=2 * D_MODEL // GLA_WIDTH, name="proj_gates")

    o_r, h_fin, o_g, s_fin = _prompt_recurrences(zr, zg, zb, rw, gl)

    r_s, kh_s, v_s, g_s, r_t, w_t, kh_t, v_t, nkk_t, b_t = _rwkv_sample_pre(zr, state_shift[0][:, _SHIFT_PERM], rw)
    rw_new_t, y_t = _rwkv_sample_step(state_rwkv[0].transpose(1, 2, 3, 0), r_t, w_t, kh_t, v_t, nkk_t, b_t)
    rw_new = rw_new_t.transpose(3, 0, 1, 2)
    o_r_ext = _rwkv_sample_post(y_t, r_s, kh_s, v_s, g_s, rw)
    q_s, k_s, eg_s = _gla_sample_pre(zg, gl)
    v_gs = zg[ROW_SAMPLE:ROW_SAMPLE + N_SAMPLE, 2 * GLA_KDIM:2 * GLA_KDIM + GLA_WIDTH]
    gla_new, og_s = _gla_sample_step(state_gla[0], q_s, k_s, eg_s, v_gs)
    o_g_ext = _gla_sample_post(og_s, zb, gl)

    h1, xn2, info, counts = _mix_route(o_r, o_r_ext, o_g, o_g_ext, zb, xp, xe, mix)
    xs, dst, tile_expert, n_tiles = _dispatch(info, counts, xn2)
    eo = _moe(tile_expert[0, :NT_MOE], n_tiles[0, :1], xs, moe_w1[0], moe_w3[0], moe_w2[0])
    y_p, y_e = _combine(dst, h1, info, row(final_norm_w), eo)

    y_prompt = y_p.reshape(N_BATCH, SEQ, D_MODEL)
    y_sample = y_e[0:N_SAMPLE].reshape(N_SAMPLE, 1, D_MODEL)
    new_rwkv_prompt = h_fin.reshape(N_BATCH, RW_HEAD, RW_HEADS, RW_HEAD).transpose(0, 2, 3, 1)[None]
    shift_rows = jnp.concatenate([zr[(b + 1) * SEQ - 1:(b + 1) * SEQ] for b in range(N_BATCH)]
                                 + [zr[ROW_SAMPLE:ROW_SAMPLE + N_SAMPLE]], axis=0)
    shift_rows = shift_rows[:, _SHIFT_INV]
    return (y_prompt, y_sample, new_rwkv_prompt, shift_rows[None, 0:N_BATCH], s_fin[None],
            rw_new[None], shift_rows[None, N_BATCH:], gla_new[None])
```

```python
import functools
import math

import numpy as np
import jax
import jax.numpy as jnp
from jax import lax
from jax.experimental import pallas as pl
from jax.experimental.pallas import tpu as pltpu

F32 = jnp.float32
BF16 = jnp.bfloat16

D_MODEL = 2048
N_BATCH = 4
SEQ = 2048
N_SAMPLE = 128
N_META = 16
RMS_EPS = 1e-6

RW_WIDTH = 1024
RW_HEAD = 64
RW_HEADS = 16
RW_GN_EPS = RW_HEAD * 1e-5
W_SHIFT = 3328
GLA_HEADS = 4
GLA_DK = 128
GLA_DV = 256
GLA_KDIM = 512
GLA_WIDTH = 1024
GLA_LORA = 16
GLA_NORMALIZER = 16.0
W_GLA_PAD = 2304
OG_BLOCK = 2 * D_MODEL // GLA_WIDTH
N_GROUPS = 4
EXPERTS_PER_GROUP = 8
N_EXPERTS = 32
D_EXPERT = 512

CHUNK = 64
N_CHUNKS = SEQ // CHUNK
META_PAD = CHUNK - N_META
T_MAIN = N_BATCH * SEQ
ROW_SAMPLE = T_MAIN
ROW_META = T_MAIN + N_SAMPLE
T_EXT = 256
T_ALL = T_MAIN + T_EXT
LANES = 128

TM_DENSE = 256
TM_MM = T_ALL // 8
TM_MOE = 256
NT_MOE = (2 * T_ALL) // TM_MOE + N_EXPERTS
NEG_BIG = -1e30

VMEM_LIMIT = 56 * 1024 * 1024

_SHIFT_PERM = np.concatenate([np.arange(0, 1024), np.arange(1088, 2112), np.arange(2112, 3136),
                              np.arange(1024, 1088), np.arange(3136, 3328)])
_SHIFT_INV = np.argsort(_SHIFT_PERM)


def _cparams(n_axes=1):
    return pltpu.CompilerParams(dimension_semantics=("arbitrary",) * n_axes, vmem_limit_bytes=VMEM_LIMIT)


def _bdot(a, b):
    return jnp.dot(a.astype(BF16), b.astype(BF16), preferred_element_type=F32)


def _bdot_nt(a, b):
    return lax.dot_general(a.astype(BF16), b.astype(BF16), (((1,), (1,)), ((), ())), preferred_element_type=F32)


def _bdot_tn(a, b):
    return lax.dot_general(a.astype(BF16), b.astype(BF16), (((0,), (0,)), ((), ())), preferred_element_type=F32)


def _split_dot(m_bf16, x):
    hi = x.astype(BF16)
    lo = (x - hi.astype(F32)).astype(BF16)
    return (jnp.dot(m_bf16, hi, preferred_element_type=F32) + jnp.dot(m_bf16, lo, preferred_element_type=F32))


def _iota(shape, dim):
    return lax.broadcasted_iota(jnp.int32, shape, dim)


def _tri_incl(n):
    return (_iota((n, n), 0) >= _iota((n, n), 1)).astype(BF16)


def _head_ones():
    return ((_iota((LANES, LANES), 0) // RW_HEAD) == (_iota((LANES, LANES), 1) // RW_HEAD)).astype(BF16)


def _seg_sum(x, bd):
    m, n = x.shape
    nb = n // LANES
    xs = jnp.concatenate([x[:, j * LANES:(j + 1) * LANES] for j in range(nb)], axis=0)
    s = jnp.dot(xs.astype(BF16), bd, preferred_element_type=F32)
    return jnp.concatenate([s[j * m:(j + 1) * m] for j in range(nb)], axis=1)


def _col_of_row(row):
    n = row.shape[-1]
    eye = _iota((n, n), 0) == _iota((n, n), 1)
    return jnp.sum(jnp.where(eye, jnp.broadcast_to(row, (n, n)), 0.0), axis=-1, keepdims=True)


def _norm1_kernel(xp_ref, xe_ref, w_ref, o_ref):
    i = pl.program_id(0)

    def f(x):
        ms = jnp.mean(x * x, axis=-1, keepdims=True)
        return (x * lax.rsqrt(ms + RMS_EPS) * w_ref[...]).astype(BF16)

    @pl.when(i < T_MAIN // TM_DENSE)
    def _():
        o_ref[...] = f(xp_ref[...])

    @pl.when(i == T_MAIN // TM_DENSE)
    def _():
        o_ref[...] = f(xe_ref[...])


def _norm1(xp, xe, w):
    nmain = T_MAIN // TM_DENSE
    return pl.pallas_call(
        _norm1_kernel,
        grid=(T_ALL // TM_DENSE,),
        in_specs=[pl.BlockSpec((TM_DENSE, D_MODEL), lambda i: (jnp.minimum(i, nmain - 1), 0)),
                  pl.BlockSpec((T_EXT, D_MODEL), lambda i: (0, 0)),
                  pl.BlockSpec((1, D_MODEL), lambda i: (0, 0))],
        out_specs=pl.BlockSpec((TM_DENSE, D_MODEL), lambda i: (i, 0)),
        out_shape=jax.ShapeDtypeStruct((T_ALL, D_MODEL), BF16),
        compiler_params=_cparams(),
        name="norm1",
    )(xp, xe, w)


def _proj_kernel(x_ref, wt_hbm, o_ref, wraw, wbf, sem, *, tiles, silu_from):
    j = pl.program_id(0)
    m = pl.program_id(1)
    tn = wbf.shape[0]

    def copies(jj):
        return [pltpu.make_async_copy(wt_hbm.at[pl.ds(src, n), :], wraw.at[pl.ds(dst, n), :], sem)
                for (src, n, dst) in tiles[jj]]

    @pl.when(jnp.logical_and(j == 0, m == 0))
    def _():
        for c in copies(0):
            c.start()

    for jj in range(len(tiles)):
        @pl.when(jnp.logical_and(j == jj, m == 0))
        def _(jj=jj):
            for c in copies(jj):
                c.wait()
            edge = 0
            for lo, hi in sorted((dst, dst + n) for (_, n, dst) in tiles[jj]) + [(tn, tn)]:
                if lo > edge:
                    wbf[edge:lo, :] = jnp.zeros((lo - edge, wbf.shape[1]), BF16)
                if hi > lo:
                    wbf[lo:hi, :] = wraw[lo:hi, :].astype(BF16)
                edge = hi
            if jj + 1 < len(tiles):
                for c in copies(jj + 1):
                    c.start()

    z = lax.dot_general(x_ref[...], wbf[...], (((1,), (1,)), ((), ())), preferred_element_type=F32)
    if silu_from is None:
        o_ref[...] = z.astype(o_ref.dtype)
    else:
        s = jax.nn.sigmoid(z)
        o_ref[...] = jnp.where(j >= silu_from, z * s, s).astype(o_ref.dtype)


def _proj(x, w_t, tiles, tn, out_dtype, silu_from=None, name="proj"):
    t, k = x.shape
    return pl.pallas_call(
        functools.partial(_proj_kernel, tiles=tiles, silu_from=silu_from),
        grid=(len(tiles), t // TM_MM),
        in_specs=[pl.BlockSpec((TM_MM, k), lambda j, m: (m, 0)),
                  pl.BlockSpec(memory_space=pl.ANY)],
        out_specs=pl.BlockSpec((TM_MM, tn), lambda j, m: (m, j)),
        out_shape=jax.ShapeDtypeStruct((t, tn * len(tiles)), out_dtype),
        scratch_shapes=[pltpu.VMEM((tn, k), F32), pltpu.VMEM((tn, k), BF16), pltpu.SemaphoreType.DMA(())],
        compiler_params=_cparams(2),
        name=name,
    )(x, w_t)


_RWKV_TILES = (((0, 1024, 0), (1088, 640, 1024)),
               ((1728, 1408, 0), (1024, 64, 1408), (3136, 192, 1472)))
_GLA_TILES = (((3328, 1152, 0),), ((4480, 912, 0),))
_GATE_TILES = tuple(((6416 + 1024 * j, 1024, 0),) for j in range(4)) + (((5392, 1024, 0),),)


def _rwkv_pre(z, w0, w2a, a0, g2, k_k, k_a, bd):
    r = z[:, 0:1024]
    k = z[:, 1024:2048]
    v = z[:, 2048:3072]
    wa = z[:, 3072:3200]
    gd = z[:, 3200:3328]
    lane = _iota(wa.shape, 1)
    wa = jnp.where(lane < 64, jnp.tanh(wa), wa)
    up = _bdot(wa, w2a)
    lw = -math.exp(-0.5) * jax.nn.sigmoid(w0 + up[:, :1024])
    a = jax.nn.sigmoid(a0 + up[:, 1024:])
    g = _bdot(jax.nn.sigmoid(gd), g2)
    kk = k * k_k
    kk = kk * lax.rsqrt(jnp.maximum(_seg_sum(kk * kk, bd), 1e-24))
    k_h = k * (1.0 + (a - 1.0) * k_a)
    return r, lw, k_h, v, kk, kk * a, g


def _rwkv_post(y, r, k_h, v, g, r_k, lnx_w, lnx_b, bd):
    mean = _seg_sum(y, bd) * (1.0 / RW_HEAD)
    d = y - mean
    var = _seg_sum(d * d, bd) * (1.0 / RW_HEAD)
    yn = d * lax.rsqrt(var + RW_GN_EPS) * lnx_w + lnx_b
    bonus = _seg_sum(r * k_h * r_k, bd) * v
    return (yn + bonus) * g


def _stack2(x):
    lane = _iota(x.shape, 1)
    return jnp.concatenate([jnp.where(lane < RW_HEAD, x, 0.0), jnp.where(lane >= RW_HEAD, x, 0.0)], axis=0)


def _unstack2(x):
    c = x.shape[0] // 2
    return x[:c] + x[c:]


def _prompt_chunk_kernel(zs_ref, mu_ref, w0_ref, w2a_ref, a0_ref, g2_ref, kk_ref, ka_ref, rk_ref, lw_ref, lb_ref,
                         zg_ref, og_ref, gkup_ref, gkb_ref, nw_ref,
                         o_ref, hout_ref, go_ref, sout_ref,
                         prev_sc, h_sc, hmeta_sc, prevmeta_sc, y_sc, s_sc, smeta_sc, *, n_chunks):
    i = pl.program_id(0)
    is_meta = i == 0
    c = lax.rem(jnp.maximum(i - 1, 0), n_chunks)
    first = jnp.logical_and(i >= 1, c == 0)
    last = jnp.logical_and(i >= 1, c == n_chunks - 1)

    @pl.when(is_meta)
    def _():
        h_sc[...] = jnp.zeros_like(h_sc)
        prev_sc[...] = jnp.zeros_like(prev_sc)
        s_sc[...] = jnp.zeros_like(s_sc)

    @pl.when(first)
    def _():
        h_sc[...] = hmeta_sc[...]
        prev_sc[...] = prevmeta_sc[...]
        s_sc[...] = smeta_sc[...]

    rowi = _iota((CHUNK, 1), 0)
    pad_row = jnp.logical_and(is_meta, rowi < META_PAD)
    heads = range(GLA_HEADS)
    ks = [slice(h * GLA_DK, (h + 1) * GLA_DK) for h in heads]
    gla = {}

    def gla_prep():
        zg = zg_ref[...]
        logg = jnp.where(pad_row, 0.0, _gla_logg(zg, gkup_ref[...], gkb_ref[...]))
        bcum = _split_dot(_tri_incl(CHUNK), logg)
        b_end = bcum[CHUNK - 1:CHUNK, :]
        gla['qt'] = zg[:, 0:GLA_KDIM] * (GLA_DK ** -0.5) * jnp.exp(bcum)
        gla['kt'] = zg[:, GLA_KDIM:2 * GLA_KDIM] * jnp.exp(-bcum)
        gla['ke'] = zg[:, GLA_KDIM:2 * GLA_KDIM] * jnp.exp(b_end - bcum)
        gla['e_end'] = jnp.exp(b_end)
        gla['vh'] = [zg[:, 2 * GLA_KDIM + h * GLA_DV:2 * GLA_KDIM + (h + 1) * GLA_DV].astype(BF16) for h in heads]

    def gla_scores():
        causal = _iota((CHUNK, CHUNK), 0) >= _iota((CHUNK, CHUNK), 1)
        gla['a'] = [jnp.where(causal, _bdot_nt(gla['qt'][:, ks[h]], gla['kt'][:, ks[h]]), 0.0) for h in heads]
        gla['s0'] = [s_sc[h] for h in heads]

    def gla_state():
        gla['o_inter'] = [_bdot(gla['qt'][:, ks[h]], gla['s0'][h]) for h in heads]
        gla['s_add'] = [_bdot_tn(gla['ke'][:, ks[h]], gla['vh'][h]) for h in heads]

    def gla_finish():
        outs = [_bdot(gla['a'][h], gla['vh'][h]) + gla['o_inter'][h] for h in heads]
        gla['s_new'] = [_col_of_row(gla['e_end'][:, ks[h]]) * gla['s0'][h] + gla['s_add'][h] for h in heads]
        for h in heads:
            s_sc[h] = gla['s_new'][h]
        go = _gla_post(jnp.concatenate(outs, axis=1), og_ref[...].astype(F32), nw_ref[...])
        go_ref[...] = go.astype(go_ref.dtype)

    zs = zs_ref[...]
    sh = pltpu.roll(zs, 1, 0)
    sh = jnp.where(rowi == 0, prev_sc[...], sh)
    prev_sc[...] = zs[CHUNK - 1:CHUNK, :]
    z = zs + (sh - zs) * mu_ref[...]
    head_ones = _head_ones()
    r, lw, k_h, v, kk, b, g = _rwkv_pre(z, w0_ref[...], w2a_ref[...], a0_ref[...], g2_ref[...], kk_ref[...],
                                        ka_ref[...], head_ones)
    lw = jnp.where(pad_row, 0.0, lw)
    gla_prep()

    cl = _split_dot(_tri_incl(CHUNK), lw)
    cl_end = cl[CHUNK - 1:CHUNK, :]
    e_neg = jnp.exp(-cl)
    e_c = jnp.exp(cl_end)
    kkt = kk * jnp.exp(cl - lw)
    rt = r * jnp.exp(cl)
    bt = b * e_neg
    kt = k_h * e_neg
    bh = bt * e_c
    kh = kt * e_c

    n2 = 2 * CHUNK
    tok_r = jnp.bitwise_and(_iota((n2, n2), 0), CHUNK - 1)
    tok_c = jnp.bitwise_and(_iota((n2, n2), 1), CHUNK - 1)
    strict = tok_r > tok_c
    incl = tok_r >= tok_c

    pairs = range(RW_HEADS // 2)
    sls = [slice(j * LANES, (j + 1) * LANES) for j in pairs]
    s_kkt = [_stack2(kkt[:, sl]) for sl in sls]
    s_rt = [_stack2(rt[:, sl]) for sl in sls]
    s_v = [_stack2(v[:, sl]).astype(BF16) for sl in sls]
    rb = [jnp.concatenate([_stack2(bt[:, sl]), _stack2(kt[:, sl])], axis=0).astype(BF16) for sl in sls]
    aa = [_bdot_nt(s_kkt[j], rb[j]) for j in pairs]
    mm = [_bdot_nt(s_rt[j], rb[j]) for j in pairs]
    h_kv = [_bdot_tn(_stack2(kh[:, sls[j]]), s_v[j]) for j in pairs]
    gla_scores()
    a_b =[jnp.where(strict, aa[j][:, :n2], 0.0).astype(BF16) for j in pairs]
    a_k = [jnp.where(strict, aa[j][:, n2:], 0.0) for j in pairs]
    m_rb = [jnp.where(incl, mm[j][:, :n2], 0.0).astype(BF16) for j in pairs]
    m_rk = [jnp.where(incl, mm[j][:, n2:], 0.0) for j in pairs]
    h0 = [h_sc[j] for j in pairs]
    akv = [_bdot(a_k[j], s_v[j]) for j in pairs]
    y_kv = [_bdot(m_rk[j], s_v[j]) for j in pairs]
    y_h = [_bdot(rt[:, sls[j]], h0[j]) for j in pairs]
    x = [_bdot(s_kkt[j], h0[j]) + akv[j] for j in pairs]
    p = a_b
    x = [x[j] - _bdot(p[j], x[j]) for j in pairs]
    for step in range(5):
        p = [_bdot(p[j], p[j]).astype(BF16) for j in pairs]
        x = [x[j] + _bdot(p[j], x[j]) for j in pairs]
        if step == 1:
            gla_state()
        if step == 3:
            gla_finish()
    zq = [_bdot(m_rb[j], x[j]) for j in pairs]
    zb = [_bdot_tn(_stack2(bh[:, sls[j]]), x[j]) for j in pairs]
    for j in pairs:
        y_sc[:, sls[j]] = y_h[j] + _unstack2(y_kv[j] - zq[j])
    h_new = [_col_of_row(e_c[:, sls[j]]) * h0[j] + (h_kv[j] - zb[j]) for j in pairs]
    for j in pairs:
        h_sc[j] = h_new[j]

    o = _rwkv_post(y_sc[...], r, k_h, v, g, rk_ref[...], lw_ref[...], lb_ref[...], head_ones)
    o_ref[...] = o.astype(o_ref.dtype)

    @pl.when(last)
    def _():
        for j in pairs:
            hout_ref[0, :, sls[j]] = _unstack2(h_new[j])
        for h in heads:
            sout_ref[0, h] = gla['s_new'][h]

    @pl.when(is_meta)
    def _():
        hmeta_sc[...] = h_sc[...]
        prevmeta_sc[...] = prev_sc[...]
        smeta_sc[...] = s_sc[...]


def _prompt_recurrences(zr, zg, zb, p, q, n_batch=N_BATCH, n_chunks=N_CHUNKS, meta_block=ROW_META // CHUNK,
                        og_block=OG_BLOCK):
    row = lambda n: pl.BlockSpec((1, n), lambda i: (0, 0))
    full = lambda a, b: pl.BlockSpec((a, b), lambda i: (0, 0))
    blk = lambda i: jnp.where(i == 0, meta_block, i - 1)
    out_blk = lambda i: jnp.maximum(i - 1, 0)
    seq = lambda i: jnp.maximum(i - 1, 0) // n_chunks
    n_rows = n_batch * n_chunks * CHUNK
    return pl.pallas_call(
        functools.partial(_prompt_chunk_kernel, n_chunks=n_chunks),
        grid=(1 + n_batch * n_chunks,),
        in_specs=[pl.BlockSpec((CHUNK, W_SHIFT), lambda i: (blk(i), 0)),
                  row(W_SHIFT), row(RW_WIDTH), full(LANES, 2 * RW_WIDTH), row(RW_WIDTH), full(LANES, RW_WIDTH),
                  row(RW_WIDTH), row(RW_WIDTH), row(RW_WIDTH), row(RW_WIDTH), row(RW_WIDTH),
                  pl.BlockSpec((CHUNK, W_GLA_PAD), lambda i: (blk(i), 0)),
                  pl.BlockSpec((CHUNK, GLA_WIDTH), lambda i: (blk(i), og_block)),
                  full(LANES, GLA_KDIM), full(1, GLA_KDIM), full(1, GLA_DV)],
        out_specs=[pl.BlockSpec((CHUNK, RW_WIDTH), lambda i: (out_blk(i), 0)),
                   pl.BlockSpec((1, RW_HEAD, RW_WIDTH), lambda i: (seq(i), 0, 0)),
                   pl.BlockSpec((CHUNK, GLA_WIDTH), lambda i: (out_blk(i), 0)),
                   pl.BlockSpec((1, GLA_HEADS, GLA_DK, GLA_DV), lambda i: (seq(i), 0, 0, 0))],
        out_shape=[jax.ShapeDtypeStruct((n_rows, RW_WIDTH), BF16),
                   jax.ShapeDtypeStruct((n_batch, RW_HEAD, RW_WIDTH), F32),
                   jax.ShapeDtypeStruct((n_rows, GLA_WIDTH), BF16),
                   jax.ShapeDtypeStruct((n_batch, GLA_HEADS, GLA_DK, GLA_DV), F32)],
        scratch_shapes=[pltpu.VMEM((1, W_SHIFT), F32),
                        pltpu.VMEM((RW_HEADS // 2, LANES, LANES), F32),
                        pltpu.VMEM((RW_HEADS // 2, LANES, LANES), F32),
                        pltpu.VMEM((1, W_SHIFT), F32),
                        pltpu.VMEM((CHUNK, RW_WIDTH), F32),
                        pltpu.VMEM((GLA_HEADS, GLA_DK, GLA_DV), F32),
                        pltpu.VMEM((GLA_HEADS, GLA_DK, GLA_DV), F32)],
        compiler_params=_cparams(),
        name="prompt_recurrences",
    )(zr, p['mu'], p['w0'], p['w2a'], p['a0'], p['g2'], p['k_k'], p['k_a'], p['r_k'], p['lnx_w'], p['lnx_b'],
      zg, zb, q['gk_up'], q['gk_b'], q['norm_w'])


def _rwkv_sample_pre_kernel(zs_ref, prev_ref, mu_ref, w0_ref, w2a_ref, a0_ref, g2_ref, kk_ref, ka_ref,
                            r_ref, kh_ref, v_ref, g_ref, rt_ref, wt_ref, kht_ref, vt_ref, nkkt_ref, bt_ref):
    zs = zs_ref[...]
    z = zs + (prev_ref[...] - zs) * mu_ref[...]
    r, lw, k_h, v, kk, b, g = _rwkv_pre(z, w0_ref[...], w2a_ref[...], a0_ref[...], g2_ref[...], kk_ref[...],
                                        ka_ref[...], _head_ones())
    r_ref[...] = r
    kh_ref[...] = k_h
    v_ref[...] = v
    g_ref[...] = g
    rt_ref[...] = r.T
    wt_ref[...] = jnp.exp(lw).T
    kht_ref[...] = k_h.T
    vt_ref[...] = v.T
    nkkt_ref[...] = (-kk).T
    bt_ref[...] = b.T


def _rwkv_sample_pre(zr, prev, p):
    row = lambda n: pl.BlockSpec((1, n), lambda i: (0, 0))
    full = lambda a, b: pl.BlockSpec((a, b), lambda i: (0, 0))
    vec = jax.ShapeDtypeStruct((N_SAMPLE, RW_WIDTH), F32)
    vec_t = jax.ShapeDtypeStruct((RW_WIDTH, N_SAMPLE), F32)
    return pl.pallas_call(
        _rwkv_sample_pre_kernel,
        grid=(1,),
        in_specs=[pl.BlockSpec((N_SAMPLE, W_SHIFT), lambda i: (ROW_SAMPLE // N_SAMPLE, 0)),
                  full(N_SAMPLE, W_SHIFT),
                  row(W_SHIFT), row(RW_WIDTH), full(LANES, 2 * RW_WIDTH), row(RW_WIDTH), full(LANES, RW_WIDTH),
                  row(RW_WIDTH), row(RW_WIDTH)],
        out_specs=[full(N_SAMPLE, RW_WIDTH)] * 4 + [full(RW_WIDTH, N_SAMPLE)] * 6,
        out_shape=[vec] * 4 + [vec_t] * 6,
        compiler_params=_cparams(),
        name="rwkv_sample_pre",
    )(zr, prev, p['mu'], p['w0'], p['w2a'], p['a0'], p['g2'], p['k_k'], p['k_a'])


def _rwkv_sample_step_kernel(s_ref, r_ref, w_ref, kh_ref, v_ref, nkk_ref, b_ref, so_ref, y_ref):
    s = s_ref[0]
    sa = jnp.sum(s * nkk_ref[0][None], axis=1, keepdims=True)
    s_new = s * w_ref[0][None] + sa * b_ref[0][None] + v_ref[0] * kh_ref[0][None]
    so_ref[0] = s_new
    y_ref[0] = jnp.sum(s_new * r_ref[0][None], axis=1, keepdims=True)


def _rwkv_sample_step(state_t, r_t, w_t, kh_t, v_t, nkk_t, b_t):
    kvec = lambda t: t.reshape(RW_HEADS, RW_HEAD, N_SAMPLE)
    kspec = pl.BlockSpec((1, RW_HEAD, N_SAMPLE), lambda i: (i, 0, 0))
    vspec = pl.BlockSpec((1, RW_HEAD, 1, N_SAMPLE), lambda i: (i, 0, 0, 0))
    sspec = pl.BlockSpec((1, RW_HEAD, RW_HEAD, N_SAMPLE), lambda i: (i, 0, 0, 0))
    s_new, y = pl.pallas_call(
        _rwkv_sample_step_kernel,
        grid=(RW_HEADS,),
        in_specs=[sspec, kspec, kspec, kspec, vspec, kspec, kspec],
        out_specs=[sspec, vspec],
        out_shape=[jax.ShapeDtypeStruct(state_t.shape, F32),
                   jax.ShapeDtypeStruct((RW_HEADS, RW_HEAD, 1, N_SAMPLE), F32)],
        compiler_params=_cparams(),
        name="rwkv_sample_step",
    )(state_t, kvec(r_t), kvec(w_t), kvec(kh_t), v_t.reshape(RW_HEADS, RW_HEAD, 1, N_SAMPLE), kvec(nkk_t), kvec(b_t))
    return s_new, y.reshape(RW_WIDTH, N_SAMPLE)


def _rwkv_sample_post_kernel(yt_ref, r_ref, kh_ref, v_ref, g_ref, rk_ref, lw_ref, lb_ref, o_ref):
    o = _rwkv_post(yt_ref[...].T, r_ref[...], kh_ref[...], v_ref[...], g_ref[...], rk_ref[...], lw_ref[...],
                   lb_ref[...], _head_ones())
    o_ref[0:N_SAMPLE, :] = o.astype(o_ref.dtype)
    o_ref[N_SAMPLE:T_EXT, :] = jnp.zeros((T_EXT - N_SAMPLE, RW_WIDTH), o_ref.dtype)


def _rwkv_sample_post(y_t, r, kh, v, g, p):
    row = lambda n: pl.BlockSpec((1, n), lambda i: (0, 0))
    full = lambda a, b: pl.BlockSpec((a, b), lambda i: (0, 0))
    return pl.pallas_call(
        _rwkv_sample_post_kernel,
        grid=(1,),
        in_specs=[full(RW_WIDTH, N_SAMPLE)] + [full(N_SAMPLE, RW_WIDTH)] * 4 + [row(RW_WIDTH)] * 3,
        out_specs=full(T_EXT, RW_WIDTH),
        out_shape=jax.ShapeDtypeStruct((T_EXT, RW_WIDTH), BF16),
        compiler_params=_cparams(),
        name="rwkv_sample_post",
    )(y_t, r, kh, v, g, p['r_k'], p['lnx_w'], p['lnx_b'])


def _gla_logg(zg, gk_up, gk_b):
    gkd = zg[:, 2 * GLA_KDIM + GLA_WIDTH:2 * GLA_KDIM + GLA_WIDTH + LANES]
    x = _bdot(gkd, gk_up) + gk_b
    return (jnp.minimum(x, 0.0) - jnp.log(1.0 + jnp.exp(-jnp.abs(x)))) * (1.0 / GLA_NORMALIZER)


def _gla_post(o, og_act, norm_w):
    outs = []
    for h in range(GLA_HEADS):
        oh = o[:, h * GLA_DV:(h + 1) * GLA_DV]
        ms = jnp.mean(oh * oh, axis=-1, keepdims=True)
        outs.append(oh * lax.rsqrt(ms + RMS_EPS) * norm_w)
    return jnp.concatenate(outs, axis=1) * og_act


def _gla_sample_pre_kernel(zg_ref, gkup_ref, gkb_ref, q_ref, k_ref, eg_ref):
    zg = zg_ref[...]
    logg = _gla_logg(zg, gkup_ref[...], gkb_ref[...])
    q_ref[...] = zg[:, 0:GLA_KDIM] * (GLA_DK ** -0.5)
    k_ref[...] = zg[:, GLA_KDIM:2 * GLA_KDIM]
    eg_ref[...] = jnp.exp(logg)


def _gla_sample_pre(zg, p):
    full = lambda a, b: pl.BlockSpec((a, b), lambda i: (0, 0))
    vec = jax.ShapeDtypeStruct((N_SAMPLE, GLA_KDIM), F32)
    return pl.pallas_call(
        _gla_sample_pre_kernel,
        grid=(1,),
        in_specs=[pl.BlockSpec((N_SAMPLE, W_GLA_PAD), lambda i: (ROW_SAMPLE // N_SAMPLE, 0)),
                  full(LANES, GLA_KDIM), full(1, GLA_KDIM)],
        out_specs=[full(N_SAMPLE, GLA_KDIM)] * 3,
        out_shape=[vec] * 3,
        compiler_params=_cparams(),
        name="gla_sample_pre",
    )(zg, p['gk_up'], p['gk_b'])


GLA_STEP_BATCH = 8


def _gla_sample_step_kernel(s_ref, q_ref, k_ref, eg_ref, v_ref, so_ref, o_ref):
    eye = _iota((GLA_DK, GLA_DK), 0) == _iota((GLA_DK, GLA_DK), 1)
    for b in range(GLA_STEP_BATCH):
        s = s_ref[b]
        col = lambda ref: jnp.sum(jnp.where(eye, ref[b], 0.0), axis=-1, keepdims=True)
        s_new = col(eg_ref) * s + col(k_ref) * v_ref[b]
        so_ref[b] = s_new
        o_ref[b] = jnp.sum(col(q_ref) * s_new, axis=-2, keepdims=True)


def _gla_sample_step(state, q, k, eg, v):
    kv = lambda t: t.reshape(N_SAMPLE, GLA_HEADS, 1, GLA_DK)
    kspec = pl.BlockSpec((GLA_STEP_BATCH, GLA_HEADS, 1, GLA_DK), lambda i: (i, 0, 0, 0))
    vspec = pl.BlockSpec((GLA_STEP_BATCH, GLA_HEADS, 1, GLA_DV), lambda i: (i, 0, 0, 0))
    sspec = pl.BlockSpec((GLA_STEP_BATCH, GLA_HEADS, GLA_DK, GLA_DV), lambda i: (i, 0, 0, 0))
    s_new, o = pl.pallas_call(
        _gla_sample_step_kernel,
        grid=(N_SAMPLE // GLA_STEP_BATCH,),
        in_specs=[sspec, kspec, kspec, kspec, vspec],
        out_specs=[sspec, vspec],
        out_shape=[jax.ShapeDtypeStruct(state.shape, F32),
                   jax.ShapeDtypeStruct((N_SAMPLE, GLA_HEADS, 1, GLA_DV), F32)],
        compiler_params=_cparams(),
        name="gla_sample_step",
    )(state, kv(q), kv(k), kv(eg), v.reshape(N_SAMPLE, GLA_HEADS, 1, GLA_DV))
    return s_new, o.reshape(N_SAMPLE, GLA_WIDTH)


def _gla_sample_post_kernel(o_ref, og_ref, nw_ref, out_ref):
    o = _gla_post(o_ref[...], og_ref[0:N_SAMPLE, :].astype(F32), nw_ref[...])
    out_ref[0:N_SAMPLE, :] = o.astype(out_ref.dtype)
    out_ref[N_SAMPLE:T_EXT, :] = jnp.zeros((T_EXT - N_SAMPLE, GLA_WIDTH), out_ref.dtype)


def _gla_sample_post(o, zb, p, og_block=OG_BLOCK):
    full = lambda a, b: pl.BlockSpec((a, b), lambda i: (0, 0))
    return pl.pallas_call(
        _gla_sample_post_kernel,
        grid=(1,),
        in_specs=[full(N_SAMPLE, GLA_WIDTH),
                  pl.BlockSpec((T_EXT, GLA_WIDTH), lambda i: (T_MAIN // T_EXT, og_block)),
                  full(1, GLA_DV)],
        out_specs=full(T_EXT, GLA_WIDTH),
        out_shape=jax.ShapeDtypeStruct((T_EXT, GLA_WIDTH), BF16),
        compiler_params=_cparams(),
        name="gla_sample_post",
    )(o, zb, p['norm_w'])


def _mix_route_kernel(or_ref, ore_ref, og_ref, oge_ref, gates_ref, xp_ref, xe_ref, pr_ref, pg_ref, wo_ref, n2_ref,
                      wrh_ref, wrl_ref, br_ref, h1_ref, xn_ref, info_ref, cnt_ref, carry_sc):
    i = pl.program_id(0)
    is_main = i < T_MAIN // TM_DENSE

    @pl.when(i == 0)
    def _():
        carry_sc[...] = jnp.zeros_like(carry_sc)

    sig_r = gates_ref[:, 0:D_MODEL].astype(F32)
    sig_g = gates_ref[:, D_MODEL:2 * D_MODEL].astype(F32)
    o_r = jnp.where(is_main, or_ref[...], ore_ref[...])
    o_g = jnp.where(is_main, og_ref[...], oge_ref[...])
    m = (sig_r * jnp.dot(o_r, pr_ref[...], preferred_element_type=F32)
         + sig_g * jnp.dot(o_g, pg_ref[...], preferred_element_type=F32))
    h = jnp.where(is_main, xp_ref[...], xe_ref[...])
    h1 = h + jnp.dot(m.astype(BF16), wo_ref[...], preferred_element_type=F32)
    h1_ref[...] = h1
    ms = jnp.mean(h1 * h1, axis=-1, keepdims=True)
    xn = h1 * lax.rsqrt(ms + RMS_EPS) * n2_ref[...]
    xn_ref[...] = xn
    xh = xn.astype(BF16)
    xl = (xn - xh.astype(F32)).astype(BF16)
    lg = (jnp.dot(xh, wrh_ref[...], preferred_element_type=F32)
          + jnp.dot(xh, wrl_ref[...], preferred_element_type=F32)
          + jnp.dot(xl, wrh_ref[...], preferred_element_type=F32)) + br_ref[...]

    lane = _iota(lg.shape, 1)
    lanef = lane.astype(F32)
    is_g = jnp.logical_and(lane >= N_EXPERTS, lane < N_EXPERTS + N_GROUPS)
    gl = jnp.where(is_g, lg, NEG_BIG)
    gmax = jnp.max(gl, axis=-1, keepdims=True)
    gsel = jnp.min(jnp.where(jnp.logical_and(is_g, gl == gmax), lanef, 1e9), axis=-1, keepdims=True) - N_EXPERTS
    p_g = 1.0 / jnp.sum(jnp.exp(gl - gmax), axis=-1, keepdims=True)
    grp = (lane // EXPERTS_PER_GROUP).astype(F32)
    in_grp = jnp.logical_and(lane < N_EXPERTS, grp == gsel)
    el = jnp.where(in_grp, lg, NEG_BIG)
    m1 = jnp.max(el, axis=-1, keepdims=True)
    i1 = jnp.min(jnp.where(jnp.logical_and(in_grp, el == m1), lanef, 1e9), axis=-1, keepdims=True)
    in2 = jnp.logical_and(in_grp, lanef != i1)
    el2 = jnp.where(in2, lg, NEG_BIG)
    m2 = jnp.max(el2, axis=-1, keepdims=True)
    i2 = jnp.min(jnp.where(jnp.logical_and(in2, el2 == m2), lanef, 1e9), axis=-1, keepdims=True)
    e2 = jnp.exp(m2 - m1)
    w1 = p_g / (1.0 + e2)
    w2 = p_g * e2 / (1.0 + e2)

    oh1 = lanef == i1
    oh2 = lanef == i2
    cnt = jnp.where(jnp.logical_or(oh1, oh2), 1.0, 0.0)
    tm = cnt.shape[0]
    lstrict = (_iota((tm, tm), 0) > _iota((tm, tm), 1)).astype(BF16)
    before = jnp.dot(lstrict, cnt.astype(BF16), preferred_element_type=F32) + carry_sc[...]
    rank1 = jnp.sum(jnp.where(oh1, before, 0.0), axis=-1, keepdims=True)
    rank2 = jnp.sum(jnp.where(oh2, before, 0.0), axis=-1, keepdims=True)
    carry_sc[...] = carry_sc[...] + jnp.sum(cnt, axis=0, keepdims=True)
    cnt_ref[...] = carry_sc[...]
    info = jnp.where(lane == 0, i1, jnp.where(lane == 1, i2, jnp.where(lane == 2, w1, jnp.where(
        lane == 3, w2, jnp.where(lane == 4, rank1, jnp.where(lane == 5, rank2, 0.0))))))
    info_ref[...] = info


def _mix_route(o_r, o_r_ext, o_g, o_g_ext, zb, xp, xe, p):
    nmain = T_MAIN // TM_DENSE
    tile = lambda n: pl.BlockSpec((TM_DENSE, n), lambda i: (i, 0))
    main = lambda n: pl.BlockSpec((TM_DENSE, n), lambda i: (jnp.minimum(i, nmain - 1), 0))
    ext = lambda n: pl.BlockSpec((T_EXT, n), lambda i: (0, 0))
    const = lambda a, b: pl.BlockSpec((a, b), lambda i: (0, 0), pipeline_mode=pl.Buffered(1))
    return pl.pallas_call(
        _mix_route_kernel,
        grid=(T_ALL // TM_DENSE,),
        in_specs=[main(RW_WIDTH), ext(RW_WIDTH), main(GLA_WIDTH), ext(GLA_WIDTH), tile(2 * D_MODEL),
                  main(D_MODEL), ext(D_MODEL),
                  const(RW_WIDTH, D_MODEL), const(GLA_WIDTH, D_MODEL), const(D_MODEL, D_MODEL),
                  const(1, D_MODEL), const(D_MODEL, LANES), const(D_MODEL, LANES), const(1, LANES)],
        out_specs=[tile(D_MODEL), tile(D_MODEL), tile(LANES), pl.BlockSpec((1, LANES), lambda i: (0, 0))],
        out_shape=[jax.ShapeDtypeStruct((T_ALL, D_MODEL), F32),
                   jax.ShapeDtypeStruct((T_ALL, D_MODEL), F32),
                   jax.ShapeDtypeStruct((T_ALL, LANES), F32),
                   jax.ShapeDtypeStruct((1, LANES), F32)],
        scratch_shapes=[pltpu.VMEM((1, LANES), F32)],
        compiler_params=_cparams(),
        name="mix_route",
    )(o_r, o_r_ext, o_g, o_g_ext, zb, xp, xe, p['p_rwkv'], p['p_gla'], p['w_out'], p['norm2_w'], p['wr_hi'],
      p['wr_lo'], p['b_route'])


def _dispatch_kernel(info_ref, cnt_ref, xq_ref, xs_hbm, d_ref, te_ref, nt_ref,
                     d_vm, d_sm, tab_vm, tab_sm, zbuf, sem_t, sem_x):
    i = pl.program_id(0)
    lane1 = _iota((1, LANES), 1)
    cnt = jnp.where(lane1 < N_EXPERTS, cnt_ref[...], 0.0)
    tiles = jnp.floor((cnt + (TM_MOE - 1)) * (1.0 / TM_MOE))
    upper = (_iota((LANES, LANES), 0) < _iota((LANES, LANES), 1)).astype(BF16)
    tile_start = jnp.dot(jnp.broadcast_to(tiles, (8, LANES)).astype(BF16), upper,
                         preferred_element_type=F32)[0:1, :]
    base = tile_start * TM_MOE

    info = info_ref[...]
    lanef = _iota(info.shape, 1).astype(F32)
    pick = lambda col: jnp.sum(jnp.where(lanef == info[:, col:col + 1], base, 0.0), axis=-1, keepdims=True)
    d1 = pick(0) + info[:, 4:5]
    d2 = pick(1) + info[:, 5:6]
    tm = info.shape[0]
    eye = _iota((tm, tm), 0) == _iota((tm, tm), 1)
    to_row = lambda col: jnp.sum(jnp.where(eye, col, 0.0), axis=0, keepdims=True)
    d_row = jnp.concatenate([to_row(d1), to_row(d2)], axis=1).astype(jnp.int32)
    d_ref[0] = d_row
    d_vm[...] = d_row

    @pl.when(i == 0)
    def _():
        tile_end = tile_start + tiles
        n_tiles = jnp.sum(tiles, axis=-1, keepdims=True)
        eye_l = _iota((LANES, LANES), 0) == _iota((LANES, LANES), 1)
        end_col = jnp.sum(jnp.where(eye_l, tile_end, 0.0), axis=-1, keepdims=True)
        rowl = _iota((LANES, LANES), 0)
        tile_f = _iota((LANES, LANES), 1).astype(F32)
        te = jnp.sum(jnp.where(jnp.logical_and(rowl < N_EXPERTS, end_col <= tile_f), 1.0, 0.0), axis=0,
                     keepdims=True)
        last_e = jnp.max(jnp.where(tiles > 0.0, lane1.astype(F32), 0.0), axis=-1, keepdims=True)
        te = jnp.where(lane1.astype(F32) < n_tiles, jnp.minimum(te, N_EXPERTS - 1.0), last_e)
        te_ref[...] = te.astype(jnp.int32)
        nt_ref[...] = jnp.broadcast_to(n_tiles, (1, LANES)).astype(jnp.int32)
        zbuf[...] = jnp.zeros_like(zbuf)
        pad_lo = (base + cnt).astype(jnp.int32)
        pad_hi = (base + tiles * TM_MOE).astype(jnp.int32)
        nt_row = jnp.broadcast_to(n_tiles, (1, LANES)).astype(jnp.int32)
        tab_vm[...] = jnp.concatenate([pad_lo, pad_hi, nt_row, jnp.zeros((5, LANES), jnp.int32)], axis=0)
        tab_copy = pltpu.make_async_copy(tab_vm, tab_sm, sem_t)
        tab_copy.start()
        tab_copy.wait()

        def zero_row(r):
            return pltpu.make_async_copy(zbuf.at[pl.ds(0, 1), :], xs_hbm.at[pl.ds(r, 1), :], sem_x)

        def zero_tile(t):
            return pltpu.make_async_copy(zbuf, xs_hbm.at[pl.ds(t * TM_MOE, TM_MOE), :], sem_x)

        def each(fn):
            def body(r, carry):
                fn(r)
                return carry
            return body

        for e in range(N_EXPERTS):
            lax.fori_loop(tab_sm[0, e], tab_sm[1, e], each(lambda r: zero_row(r).start()), 0)
        lax.fori_loop(tab_sm[2, 0], NT_MOE, each(lambda t: zero_tile(t).start()), 0)
        for e in range(N_EXPERTS):
            lax.fori_loop(tab_sm[0, e], tab_sm[1, e], each(lambda r: zero_row(r).wait()), 0)
        lax.fori_loop(tab_sm[2, 0], NT_MOE, each(lambda t: zero_tile(t).wait()), 0)

    d_copy = pltpu.make_async_copy(d_vm, d_sm, sem_t)
    d_copy.start()
    d_copy.wait()

    for t in range(tm):
        for half in range(2):
            pltpu.make_async_copy(xq_ref.at[pl.ds(t, 1), :], xs_hbm.at[pl.ds(d_sm[0, half * tm + t], 1), :],
                                  sem_x).start(priority=half)
    for half in range(2):
        pltpu.make_async_copy(xq_ref, xs_hbm.at[pl.ds(0, tm), :], sem_x).wait()


def _dispatch(info, counts, xq):
    nt = T_ALL // TM_DENSE
    return pl.pallas_call(
        _dispatch_kernel,
        grid=(nt,),
        in_specs=[pl.BlockSpec((TM_DENSE, LANES), lambda i: (i, 0)),
                  pl.BlockSpec((1, LANES), lambda i: (0, 0)),
                  pl.BlockSpec((TM_DENSE, D_MODEL), lambda i: (i, 0))],
        out_specs=[pl.BlockSpec(memory_space=pl.ANY),
                   pl.BlockSpec((1, 1, 2 * TM_DENSE), lambda i: (i, 0, 0)),
                   pl.BlockSpec((1, LANES), lambda i: (0, 0)),
                   pl.BlockSpec((1, LANES), lambda i: (0, 0))],
        out_shape=[jax.ShapeDtypeStruct((NT_MOE * TM_MOE, D_MODEL), F32),
                   jax.ShapeDtypeStruct((nt, 1, 2 * TM_DENSE), jnp.int32),
                   jax.ShapeDtypeStruct((1, LANES), jnp.int32),
                   jax.ShapeDtypeStruct((1, LANES), jnp.int32)],
        scratch_shapes=[pltpu.VMEM((1, 2 * TM_DENSE), jnp.int32),
                        pltpu.SMEM((1, 2 * TM_DENSE), jnp.int32),
                        pltpu.VMEM((8, LANES), jnp.int32),
                        pltpu.SMEM((8, LANES), jnp.int32),
                        pltpu.VMEM((TM_MOE, D_MODEL), F32),
                        pltpu.SemaphoreType.DMA(()),
                        pltpu.SemaphoreType.DMA(())],
        compiler_params=_cparams(),
        name="moe_dispatch",
    )(info, counts, xq)


def _moe_kernel(te_ref, nt_ref, xs_ref, w1_hbm, w3_hbm, w2_hbm, o_ref, w1r, w3r, w2r, w1b, w3b, w2b, sem):
    i = pl.program_id(0)
    n_tiles = nt_ref[0]

    def weight_copies(e):
        return [pltpu.make_async_copy(w1_hbm.at[e], w1r, sem), pltpu.make_async_copy(w3_hbm.at[e], w3r, sem),
                pltpu.make_async_copy(w2_hbm.at[e], w2r, sem)]

    @pl.when(i < n_tiles)
    def _():
        e = te_ref[i]

        @pl.when(i == 0)
        def _():
            for c in weight_copies(e):
                c.start()

        @pl.when(jnp.logical_or(i == 0, e != te_ref[jnp.maximum(i - 1, 0)]))
        def _():
            for c in weight_copies(e):
                c.wait()
            w1b[...] = w1r[...].astype(BF16)
            w3b[...] = w3r[...].astype(BF16)
            w2b[...] = w2r[...].astype(BF16)
            nxt = lax.while_loop(
                lambda j: jnp.logical_and(j < n_tiles, te_ref[jnp.minimum(j, NT_MOE - 1)] == e),
                lambda j: j + 1, i + 1)

            @pl.when(nxt < n_tiles)
            def _():
                for c in weight_copies(te_ref[jnp.minimum(nxt, NT_MOE - 1)]):
                    c.start()

        xb = xs_ref[...].astype(BF16)
        h1 = jnp.dot(xb, w1b[...], preferred_element_type=F32)
        h3 = jnp.dot(xb, w3b[...], preferred_element_type=F32)
        hh = (h1 * jax.nn.sigmoid(h1) * h3).astype(BF16)
        o_ref[...] = jnp.dot(hh, w2b[...], preferred_element_type=F32)

    @pl.when(i >= nt_ref[0])
    def _():
        o_ref[...] = jnp.zeros_like(o_ref)


def _moe(tile_expert, n_tiles, xs, w1, w3, w2):
    grid_spec = pltpu.PrefetchScalarGridSpec(
        num_scalar_prefetch=2,
        grid=(NT_MOE,),
        in_specs=[pl.BlockSpec((TM_MOE, D_MODEL), lambda i, te, nt: (jnp.minimum(i, nt[0] - 1), 0)),
                  pl.BlockSpec(memory_space=pl.ANY),
                  pl.BlockSpec(memory_space=pl.ANY),
                  pl.BlockSpec(memory_space=pl.ANY)],
        out_specs=pl.BlockSpec((TM_MOE, D_MODEL), lambda i, te, nt: (i, 0)),
        scratch_shapes=[pltpu.VMEM((D_MODEL, D_EXPERT), F32),
                        pltpu.VMEM((D_MODEL, D_EXPERT), F32),
                        pltpu.VMEM((D_EXPERT, D_MODEL), F32),
                        pltpu.VMEM((D_MODEL, D_EXPERT), BF16),
                        pltpu.VMEM((D_MODEL, D_EXPERT), BF16),
                        pltpu.VMEM((D_EXPERT, D_MODEL), BF16),
                        pltpu.SemaphoreType.DMA(())],
    )
    return pl.pallas_call(
        _moe_kernel,
        grid_spec=grid_spec,
        out_shape=jax.ShapeDtypeStruct((NT_MOE * TM_MOE, D_MODEL), F32),
        compiler_params=_cparams(),
        name="moe_experts",
    )(tile_expert, n_tiles, xs, w1, w3, w2)


def _combine_kernel(dcur_ref, dnext_ref, h1_ref, info_ref, fw_ref, eo_hbm, yp_ref, ye_ref, buf, sem):
    i = pl.program_id(0)
    nmain = T_MAIN // TM_DENSE
    slot = lax.rem(i, 2)

    def row_copy(d_ref, s, t):
        return pltpu.make_async_copy(eo_hbm.at[pl.ds(d_ref[0, 0, t], 1), :], buf.at[s, pl.ds(t, 1), :], sem.at[s])

    def issue_all(d_ref, s):
        for t in range(2 * TM_DENSE):
            row_copy(d_ref, s, t).start(priority=t % 2)

    @pl.when(i == 0)
    def _():
        issue_all(dcur_ref, slot)

    @pl.when(i + 1 < pl.num_programs(0))
    def _():
        issue_all(dnext_ref, 1 - slot)

    pltpu.make_async_copy(eo_hbm.at[pl.ds(0, 2 * TM_DENSE), :], buf.at[slot], sem.at[slot]).wait()

    info = info_ref[...]
    y = h1_ref[...] + info[:, 2:3] * buf[slot, 0:TM_DENSE, :] + info[:, 3:4] * buf[slot, TM_DENSE:2 * TM_DENSE, :]
    ms = jnp.mean(y * y, axis=-1, keepdims=True)
    out = y * lax.rsqrt(ms + RMS_EPS) * fw_ref[...]

    @pl.when(i < nmain)
    def _():
        yp_ref[...] = out

    @pl.when(i == nmain)
    def _():
        ye_ref[...] = out


def _combine(dst, h1, info, fw, eo):
    nmain = T_MAIN // TM_DENSE
    return pl.pallas_call(
        _combine_kernel,
        grid=(T_ALL // TM_DENSE,),
        in_specs=[pl.BlockSpec((1, 1, 2 * TM_DENSE), lambda i: (i, 0, 0), memory_space=pltpu.SMEM),
                  pl.BlockSpec((1, 1, 2 * TM_DENSE), lambda i: (jnp.minimum(i + 1, nmain), 0, 0),
                               memory_space=pltpu.SMEM),
                  pl.BlockSpec((TM_DENSE, D_MODEL), lambda i: (i, 0)),
                  pl.BlockSpec((TM_DENSE, LANES), lambda i: (i, 0)),
                  pl.BlockSpec((1, D_MODEL), lambda i: (0, 0)),
                  pl.BlockSpec(memory_space=pl.ANY)],
        out_specs=[pl.BlockSpec((TM_DENSE, D_MODEL), lambda i: (jnp.minimum(i, nmain - 1), 0)),
                   pl.BlockSpec((T_EXT, D_MODEL), lambda i: (0, 0))],
        out_shape=[jax.ShapeDtypeStruct((T_MAIN, D_MODEL), F32),
                   jax.ShapeDtypeStruct((T_EXT, D_MODEL), F32)],
        scratch_shapes=[pltpu.VMEM((2, 2 * TM_DENSE, D_MODEL), F32), pltpu.SemaphoreType.DMA((2,))],
        compiler_params=_cparams(),
        name="moe_combine",
    )(dst, dst, h1, info, fw, eo)


def kernel(x_prompt, x_sample, state_rwkv, state_shift, state_gla, meta_tokens, norm1_w, w_in, mu_shift, rw_w0,
           rw_w2, rw_a0, rw_a2, rw_g2, rw_k_k, rw_k_a, rw_r_k, rw_lnx_w, rw_lnx_b, gla_gk_up, gla_gk_b, gla_norm_w,
           p_rwkv, p_gla, w_out, norm2_w, moe_w_group, moe_b_group, moe_w_router, moe_b_router, moe_w1, moe_w3,
           moe_w2, final_norm_w):
    w_t = w_in[0].T
    row = lambda t: t.reshape(1, -1)
    w2a = jnp.zeros((LANES, 2 * RW_WIDTH), F32)
    w2a = w2a.at[0:64, 0:RW_WIDTH].set(rw_w2[0]).at[64:128, RW_WIDTH:].set(rw_a2[0])
    rw = dict(mu=row(mu_shift[0][_SHIFT_PERM]), w0=row(rw_w0[0]), w2a=w2a, a0=row(rw_a0[0]), g2=rw_g2[0],
              k_k=row(rw_k_k[0]), k_a=row(rw_k_a[0]), r_k=row(rw_r_k[0]), lnx_w=row(rw_lnx_w[0]),
              lnx_b=row(rw_lnx_b[0]))
    gk_up = jnp.zeros((LANES, GLA_KDIM), F32).at[0:GLA_LORA].set(gla_gk_up[0])
    gl = dict(gk_up=gk_up, gk_b=row(gla_gk_b[0]), norm_w=row(gla_norm_w[0]))
    w_route = jnp.zeros((D_MODEL, LANES), F32)
    w_route = w_route.at[:, 0:N_EXPERTS].set(moe_w_router[0]).at[:, N_EXPERTS:N_EXPERTS + N_GROUPS].set(moe_w_group[0])
    wr_hi = w_route.astype(BF16)
    wr_lo = (w_route - wr_hi.astype(F32)).astype(BF16)
    b_route = jnp.zeros((1, LANES), F32)
    b_route = b_route.at[0, 0:N_EXPERTS].set(moe_b_router[0]).at[0, N_EXPERTS:N_EXPERTS + N_GROUPS].set(moe_b_group[0])
    mix = dict(p_rwkv=p_rwkv[0].astype(BF16), p_gla=p_gla[0].astype(BF16), w_out=w_out[0].astype(BF16),
               norm2_w=row(norm2_w[0]), wr_hi=wr_hi, wr_lo=wr_lo, b_route=b_route)

    xp = x_prompt.reshape(T_MAIN, D_MODEL)
    xe = jnp.concatenate([x_sample[:, 0, :], jnp.zeros((META_PAD, D_MODEL), F32), meta_tokens,
                          jnp.zeros((T_ALL - ROW_META - CHUNK, D_MODEL), F32)], axis=0)

    xn = _norm1(xp, xe, row(norm1_w[0]))
    zr = _proj(xn, w_t, _RWKV_TILES, W_SHIFT // 2, F32, name="proj_rwkv")
    zg = _proj(xn, w_t, _GLA_TILES, W_GLA_PAD // 2, F32, name="proj_gla")
    zb = _proj(xn, w_t, _GATE_TILES, GLA_WIDTH, BF16, silu_from=2 * D_MODEL // GLA_WIDTH, name="proj_gates")

    o_r, h_fin, o_g, s_fin = _prompt_recurrences(zr, zg, zb, rw, gl)

    r_s, kh_s, v_s, g_s, r_t, w_t, kh_t, v_t, nkk_t, b_t = _rwkv_sample_pre(zr, state_shift[0][:, _SHIFT_PERM], rw)
    rw_new_t, y_t = _rwkv_sample_step(state_rwkv[0].transpose(1, 2, 3, 0), r_t, w_t, kh_t, v_t, nkk_t, b_t)
    rw_new = rw_new_t.transpose(3, 0, 1, 2)
    o_r_ext = _rwkv_sample_post(y_t, r_s, kh_s, v_s, g_s, rw)
    q_s, k_s, eg_s = _gla_sample_pre(zg, gl)
    v_gs = zg[ROW_SAMPLE:ROW_SAMPLE + N_SAMPLE, 2 * GLA_KDIM:2 * GLA_KDIM + GLA_WIDTH]
    gla_new, og_s = _gla_sample_step(state_gla[0], q_s, k_s, eg_s, v_gs)
    o_g_ext = _gla_sample_post(og_s, zb, gl)

    h1, xn2, info, counts = _mix_route(o_r, o_r_ext, o_g, o_g_ext, zb, xp, xe, mix)
    xs, dst, tile_expert, n_tiles = _dispatch(info, counts, xn2)
    eo = _moe(tile_expert[0, :NT_MOE], n_tiles[0, :1], xs, moe_w1[0], moe_w3[0], moe_w2[0])
    y_p, y_e = _combine(dst, h1, info, row(final_norm_w), eo)

    y_prompt = y_p.reshape(N_BATCH, SEQ, D_MODEL)
    y_sample = y_e[0:N_SAMPLE].reshape(N_SAMPLE, 1, D_MODEL)
    new_rwkv_prompt = h_fin.reshape(N_BATCH, RW_HEAD, RW_HEADS, RW_HEAD).transpose(0, 2, 3, 1)[None]
    shift_rows = jnp.concatenate([zr[(b + 1) * SEQ - 1:(b + 1) * SEQ] for b in range(N_BATCH)]
                                 + [zr[ROW_SAMPLE:ROW_SAMPLE + N_SAMPLE]], axis=0)
    shift_rows = shift_rows[:, _SHIFT_INV]
    return (y_prompt, y_sample, new_rwkv_prompt, shift_rows[None, 0:N_BATCH], s_fin[None],
            rw_new[None], shift_rows[None, N_BATCH:], gla_new[None])
```

```python
import functools
import math

import numpy as np
import jax
import jax.numpy as jnp
from jax import lax
from jax.experimental import pallas as pl
from jax.experimental.pallas import tpu as pltpu

F32 = jnp.float32
BF16 = jnp.bfloat16

D_MODEL = 2048
N_BATCH = 4
SEQ = 2048
N_SAMPLE = 128
N_META = 16
RMS_EPS = 1e-6

RW_WIDTH = 1024
RW_HEAD = 64
RW_HEADS = 16
RW_GN_EPS = RW_HEAD * 1e-5
W_SHIFT = 3328
GLA_HEADS = 4
GLA_DK = 128
GLA_DV = 256
GLA_KDIM = 512
GLA_WIDTH = 1024
GLA_LORA = 16
GLA_NORMALIZER = 16.0
W_GLA_PAD = 2304
OG_BLOCK = 2 * D_MODEL // GLA_WIDTH
N_GROUPS = 4
EXPERTS_PER_GROUP = 8
N_EXPERTS = 32
D_EXPERT = 512

CHUNK = 64
N_CHUNKS = SEQ // CHUNK
META_PAD = CHUNK - N_META
T_MAIN = N_BATCH * SEQ
ROW_SAMPLE = T_MAIN
ROW_META = T_MAIN + N_SAMPLE
T_EXT = 256
T_ALL = T_MAIN + T_EXT
LANES = 128

TM_DENSE = 256
TM_MM = T_ALL // 8
TM_MOE = 256
NT_MOE = (2 * T_ALL) // TM_MOE + N_EXPERTS
NEG_BIG = -1e30

VMEM_LIMIT = 56 * 1024 * 1024

_SHIFT_PERM = np.concatenate([np.arange(0, 1024), np.arange(1088, 2112), np.arange(2112, 3136),
                              np.arange(1024, 1088), np.arange(3136, 3328)])
_SHIFT_INV = np.argsort(_SHIFT_PERM)


def _cparams(n_axes=1):
    return pltpu.CompilerParams(dimension_semantics=("arbitrary",) * n_axes, vmem_limit_bytes=VMEM_LIMIT)


def _bdot(a, b):
    return jnp.dot(a.astype(BF16), b.astype(BF16), preferred_element_type=F32)


def _bdot_nt(a, b):
    return lax.dot_general(a.astype(BF16), b.astype(BF16), (((1,), (1,)), ((), ())), preferred_element_type=F32)


def _bdot_tn(a, b):
    return lax.dot_general(a.astype(BF16), b.astype(BF16), (((0,), (0,)), ((), ())), preferred_element_type=F32)


def _split_dot(m_bf16, x):
    hi = x.astype(BF16)
    lo = (x - hi.astype(F32)).astype(BF16)
    return (jnp.dot(m_bf16, hi, preferred_element_type=F32) + jnp.dot(m_bf16, lo, preferred_element_type=F32))


def _iota(shape, dim):
    return lax.broadcasted_iota(jnp.int32, shape, dim)


def _tri_incl(n):
    return (_iota((n, n), 0) >= _iota((n, n), 1)).astype(BF16)


def _head_ones():
    return ((_iota((LANES, LANES), 0) // RW_HEAD) == (_iota((LANES, LANES), 1) // RW_HEAD)).astype(BF16)


def _seg_sum(x, bd):
    m, n = x.shape
    nb = n // LANES
    xs = jnp.concatenate([x[:, j * LANES:(j + 1) * LANES] for j in range(nb)], axis=0)
    s = jnp.dot(xs.astype(BF16), bd, preferred_element_type=F32)
    return jnp.concatenate([s[j * m:(j + 1) * m] for j in range(nb)], axis=1)


def _col_of_row(row):
    n = row.shape[-1]
    eye = _iota((n, n), 0) == _iota((n, n), 1)
    return jnp.sum(jnp.where(eye, jnp.broadcast_to(row, (n, n)), 0.0), axis=-1, keepdims=True)


def _norm1_kernel(xp_ref, xe_ref, w_ref, o_ref):
    i = pl.program_id(0)

    def f(x):
        ms = jnp.mean(x * x, axis=-1, keepdims=True)
        return (x * lax.rsqrt(ms + RMS_EPS) * w_ref[...]).astype(BF16)

    @pl.when(i < T_MAIN // TM_DENSE)
    def _():
        o_ref[...] = f(xp_ref[...])

    @pl.when(i == T_MAIN // TM_DENSE)
    def _():
        o_ref[...] = f(xe_ref[...])


def _norm1(xp, xe, w):
    nmain = T_MAIN // TM_DENSE
    return pl.pallas_call(
        _norm1_kernel,
        grid=(T_ALL // TM_DENSE,),
        in_specs=[pl.BlockSpec((TM_DENSE, D_MODEL), lambda i: (jnp.minimum(i, nmain - 1), 0)),
                  pl.BlockSpec((T_EXT, D_MODEL), lambda i: (0, 0)),
                  pl.BlockSpec((1, D_MODEL), lambda i: (0, 0))],
        out_specs=pl.BlockSpec((TM_DENSE, D_MODEL), lambda i: (i, 0)),
        out_shape=jax.ShapeDtypeStruct((T_ALL, D_MODEL), BF16),
        compiler_params=_cparams(),
        name="norm1",
    )(xp, xe, w)


def _proj_kernel(x_ref, wt_hbm, o_ref, wraw, wbf, sem, *, tiles, silu_from):
    j = pl.program_id(0)
    m = pl.program_id(1)
    tn = wbf.shape[0]

    def copies(jj):
        return [pltpu.make_async_copy(wt_hbm.at[pl.ds(src, n), :], wraw.at[pl.ds(dst, n), :], sem)
                for (src, n, dst) in tiles[jj]]

    @pl.when(jnp.logical_and(j == 0, m == 0))
    def _():
        for c in copies(0):
            c.start()

    for jj in range(len(tiles)):
        @pl.when(jnp.logical_and(j == jj, m == 0))
        def _(jj=jj):
            for c in copies(jj):
                c.wait()
            edge = 0
            for lo, hi in sorted((dst, dst + n) for (_, n, dst) in tiles[jj]) + [(tn, tn)]:
                if lo > edge:
                    wbf[edge:lo, :] = jnp.zeros((lo - edge, wbf.shape[1]), BF16)
                if hi > lo:
                    wbf[lo:hi, :] = wraw[lo:hi, :].astype(BF16)
                edge = hi
            if jj + 1 < len(tiles):
                for c in copies(jj + 1):
                    c.start()

    z = lax.dot_general(x_ref[...], wbf[...], (((1,), (1,)), ((), ())), preferred_element_type=F32)
    if silu_from is None:
        o_ref[...] = z.astype(o_ref.dtype)
    else:
        s = jax.nn.sigmoid(z)
        o_ref[...] = jnp.where(j >= silu_from, z * s, s).astype(o_ref.dtype)


def _proj(x, w_t, tiles, tn, out_dtype, silu_from=None, name="proj"):
    t, k = x.shape
    return pl.pallas_call(
        functools.partial(_proj_kernel, tiles=tiles, silu_from=silu_from),
        grid=(len(tiles), t // TM_MM),
        in_specs=[pl.BlockSpec((TM_MM, k), lambda j, m: (m, 0)),
                  pl.BlockSpec(memory_space=pl.ANY)],
        out_specs=pl.BlockSpec((TM_MM, tn), lambda j, m: (m, j)),
        out_shape=jax.ShapeDtypeStruct((t, tn * len(tiles)), out_dtype),
        scratch_shapes=[pltpu.VMEM((tn, k), F32), pltpu.VMEM((tn, k), BF16), pltpu.SemaphoreType.DMA(())],
        compiler_params=_cparams(2),
        name=name,
    )(x, w_t)


_RWKV_TILES = (((0, 1024, 0), (1088, 640, 1024)),
               ((1728, 1408, 0), (1024, 64, 1408), (3136, 192, 1472)))
_GLA_TILES = (((3328, 1152, 0),), ((4480, 912, 0),))
_GATE_TILES = tuple(((6416 + 1024 * j, 1024, 0),) for j in range(4)) + (((5392, 1024, 0),),)


def _rwkv_pre(z, w0, w2a, a0, g2, k_k, k_a, bd):
    r = z[:, 0:1024]
    k = z[:, 1024:2048]
    v = z[:, 2048:3072]
    wa = z[:, 3072:3200]
    gd = z[:, 3200:3328]
    lane = _iota(wa.shape, 1)
    wa = jnp.where(lane < 64, jnp.tanh(wa), wa)
    up = _bdot(wa, w2a)
    lw = -math.exp(-0.5) * jax.nn.sigmoid(w0 + up[:, :1024])
    a = jax.nn.sigmoid(a0 + up[:, 1024:])
    g = _bdot(jax.nn.sigmoid(gd), g2)
    kk = k * k_k
    kk = kk * lax.rsqrt(jnp.maximum(_seg_sum(kk * kk, bd), 1e-24))
    k_h = k * (1.0 + (a - 1.0) * k_a)
    return r, lw, k_h, v, kk, kk * a, g


def _rwkv_post(y, r, k_h, v, g, r_k, lnx_w, lnx_b, bd):
    mean = _seg_sum(y, bd) * (1.0 / RW_HEAD)
    d = y - mean
    var = _seg_sum(d * d, bd) * (1.0 / RW_HEAD)
    yn = d * lax.rsqrt(var + RW_GN_EPS) * lnx_w + lnx_b
    bonus = _seg_sum(r * k_h * r_k, bd) * v
    return (yn + bonus) * g


def _stack2(x):
    lane = _iota(x.shape, 1)
    return jnp.concatenate([jnp.where(lane < RW_HEAD, x, 0.0), jnp.where(lane >= RW_HEAD, x, 0.0)], axis=0)


def _unstack2(x):
    c = x.shape[0] // 2
    return x[:c] + x[c:]


def _prompt_chunk_kernel(zs_ref, mu_ref, w0_ref, w2a_ref, a0_ref, g2_ref, kk_ref, ka_ref, rk_ref, lw_ref, lb_ref,
                         zg_ref, og_ref, gkup_ref, gkb_ref, nw_ref,
                         o_ref, hout_ref, go_ref, sout_ref,
                         prev_sc, h_sc, hmeta_sc, prevmeta_sc, y_sc, s_sc, smeta_sc, *, n_chunks):
    i = pl.program_id(0)
    is_meta = i == 0
    c = lax.rem(jnp.maximum(i - 1, 0), n_chunks)
    first = jnp.logical_and(i >= 1, c == 0)
    last = jnp.logical_and(i >= 1, c == n_chunks - 1)

    @pl.when(is_meta)
    def _():
        h_sc[...] = jnp.zeros_like(h_sc)
        prev_sc[...] = jnp.zeros_like(prev_sc)
        s_sc[...] = jnp.zeros_like(s_sc)

    @pl.when(first)
    def _():
        h_sc[...] = hmeta_sc[...]
        prev_sc[...] = prevmeta_sc[...]
        s_sc[...] = smeta_sc[...]

    rowi = _iota((CHUNK, 1), 0)
    pad_row = jnp.logical_and(is_meta, rowi < META_PAD)
    heads = range(GLA_HEADS)
    ks = [slice(h * GLA_DK, (h + 1) * GLA_DK) for h in heads]
    gla = {}

    def gla_prep():
        zg = zg_ref[...]
        logg = jnp.where(pad_row, 0.0, _gla_logg(zg, gkup_ref[...], gkb_ref[...]))
        bcum = _split_dot(_tri_incl(CHUNK), logg)
        b_end = bcum[CHUNK - 1:CHUNK, :]
        gla['qt'] = zg[:, 0:GLA_KDIM] * (GLA_DK ** -0.5) * jnp.exp(bcum)
        gla['kt'] = zg[:, GLA_KDIM:2 * GLA_KDIM] * jnp.exp(-bcum)
        gla['ke'] = zg[:, GLA_KDIM:2 * GLA_KDIM] * jnp.exp(b_end - bcum)
        gla['e_end'] = jnp.exp(b_end)
        gla['vh'] = [zg[:, 2 * GLA_KDIM + h * GLA_DV:2 * GLA_KDIM + (h + 1) * GLA_DV].astype(BF16) for h in heads]

    def gla_scores():
        causal = _iota((CHUNK, CHUNK), 0) >= _iota((CHUNK, CHUNK), 1)
        gla['a'] = [jnp.where(causal, _bdot_nt(gla['qt'][:, ks[h]], gla['kt'][:, ks[h]]), 0.0) for h in heads]
        gla['s0'] = [s_sc[h] for h in heads]

    def gla_state():
        gla['o_inter'] = [_bdot(gla['qt'][:, ks[h]], gla['s0'][h]) for h in heads]
        gla['s_add'] = [_bdot_tn(gla['ke'][:, ks[h]], gla['vh'][h]) for h in heads]

    def gla_finish():
        outs = [_bdot(gla['a'][h], gla['vh'][h]) + gla['o_inter'][h] for h in heads]
        gla['s_new'] = [_col_of_row(gla['e_end'][:, ks[h]]) * gla['s0'][h] + gla['s_add'][h] for h in heads]
        for h in heads:
            s_sc[h] = gla['s_new'][h]
        go = _gla_post(jnp.concatenate(outs, axis=1), og_ref[...].astype(F32), nw_ref[...])
        go_ref[...] = go.astype(go_ref.dtype)

    zs = zs_ref[...]
    sh = pltpu.roll(zs, 1, 0)
    sh = jnp.where(rowi == 0, prev_sc[...], sh)
    prev_sc[...] = zs[CHUNK - 1:CHUNK, :]
    z = zs + (sh - zs) * mu_ref[...]
    head_ones = _head_ones()
    r, lw, k_h, v, kk, b, g = _rwkv_pre(z, w0_ref[...], w2a_ref[...], a0_ref[...], g2_ref[...], kk_ref[...],
                                        ka_ref[...], head_ones)
    lw = jnp.where(pad_row, 0.0, lw)
    gla_prep()

    cl = _split_dot(_tri_incl(CHUNK), lw)
    cl_end = cl[CHUNK - 1:CHUNK, :]
    e_neg = jnp.exp(-cl)
    e_c = jnp.exp(cl_end)
    kkt = kk * jnp.exp(cl - lw)
    rt = r * jnp.exp(cl)
    bt = b * e_neg
    kt = k_h * e_neg
    bh = bt * e_c
    kh = kt * e_c

    n2 = 2 * CHUNK
    tok_r = jnp.bitwise_and(_iota((n2, n2), 0), CHUNK - 1)
    tok_c = jnp.bitwise_and(_iota((n2, n2), 1), CHUNK - 1)
    strict = tok_r > tok_c
    incl = tok_r >= tok_c

    pairs = range(RW_HEADS // 2)
    sls = [slice(j * LANES, (j + 1) * LANES) for j in pairs]
    s_kkt = [_stack2(kkt[:, sl]) for sl in sls]
    s_rt = [_stack2(rt[:, sl]) for sl in sls]
    s_v = [_stack2(v[:, sl]).astype(BF16) for sl in sls]
    rb = [jnp.concatenate([_stack2(bt[:, sl]), _stack2(kt[:, sl])], axis=0).astype(BF16) for sl in sls]
    aa = [_bdot_nt(s_kkt[j], rb[j]) for j in pairs]
    mm = [_bdot_nt(s_rt[j], rb[j]) for j in pairs]
    h_kv = [_bdot_tn(_stack2(kh[:, sls[j]]), s_v[j]) for j in pairs]
    gla_scores()
    a_b =[jnp.where(strict, aa[j][:, :n2], 0.0).astype(BF16) for j in pairs]
    a_k = [jnp.where(strict, aa[j][:, n2:], 0.0) for j in pairs]
    m_rb = [jnp.where(incl, mm[j][:, :n2], 0.0).astype(BF16) for j in pairs]
    m_rk = [jnp.where(incl, mm[j][:, n2:], 0.0) for j in pairs]
    h0 = [h_sc[j] for j in pairs]
    akv = [_bdot(a_k[j], s_v[j]) for j in pairs]
    y_kv = [_bdot(m_rk[j], s_v[j]) for j in pairs]
    y_h = [_bdot(rt[:, sls[j]], h0[j]) for j in pairs]
    x = [_bdot(s_kkt[j], h0[j]) + akv[j] for j in pairs]
    p = a_b
    x = [x[j] - _bdot(p[j], x[j]) for j in pairs]
    for step in range(5):
        p = [_bdot(p[j], p[j]).astype(BF16) for j in pairs]
        x = [x[j] + _bdot(p[j], x[j]) for j in pairs]
        if step == 1:
            gla_state()
        if step == 3:
            gla_finish()
    zq = [_bdot(m_rb[j], x[j]) for j in pairs]
    zb = [_bdot_tn(_stack2(bh[:, sls[j]]), x[j]) for j in pairs]
    for j in pairs:
        y_sc[:, sls[j]] = y_h[j] + _unstack2(y_kv[j] - zq[j])
    h_new = [_col_of_row(e_c[:, sls[j]]) * h0[j] + (h_kv[j] - zb[j]) for j in pairs]
    for j in pairs:
        h_sc[j] = h_new[j]

    o = _rwkv_post(y_sc[...], r, k_h, v, g, rk_ref[...], lw_ref[...], lb_ref[...], head_ones)
    o_ref[...] = o.astype(o_ref.dtype)

    @pl.when(last)
    def _():
        for j in pairs:
            hout_ref[0, :, sls[j]] = _unstack2(h_new[j])
        for h in heads:
            sout_ref[0, h] = gla['s_new'][h]

    @pl.when(is_meta)
    def _():
        hmeta_sc[...] = h_sc[...]
        prevmeta_sc[...] = prev_sc[...]
        smeta_sc[...] = s_sc[...]


def _prompt_recurrences(zr, zg, zb, p, q, n_batch=N_BATCH, n_chunks=N_CHUNKS, meta_block=ROW_META // CHUNK,
                        og_block=OG_BLOCK):
    row = lambda n: pl.BlockSpec((1, n), lambda i: (0, 0))
    full = lambda a, b: pl.BlockSpec((a, b), lambda i: (0, 0))
    blk = lambda i: jnp.where(i == 0, meta_block, i - 1)
    out_blk = lambda i: jnp.maximum(i - 1, 0)
    seq = lambda i: jnp.maximum(i - 1, 0) // n_chunks
    n_rows = n_batch * n_chunks * CHUNK
    return pl.pallas_call(
        functools.partial(_prompt_chunk_kernel, n_chunks=n_chunks),
        grid=(1 + n_batch * n_chunks,),
        in_specs=[pl.BlockSpec((CHUNK, W_SHIFT), lambda i: (blk(i), 0)),
                  row(W_SHIFT), row(RW_WIDTH), full(LANES, 2 * RW_WIDTH), row(RW_WIDTH), full(LANES, RW_WIDTH),
                  row(RW_WIDTH), row(RW_WIDTH), row(RW_WIDTH), row(RW_WIDTH), row(RW_WIDTH),
                  pl.BlockSpec((CHUNK, W_GLA_PAD), lambda i: (blk(i), 0)),
                  pl.BlockSpec((CHUNK, GLA_WIDTH), lambda i: (blk(i), og_block)),
                  full(LANES, GLA_KDIM), full(1, GLA_KDIM), full(1, GLA_DV)],
        out_specs=[pl.BlockSpec((CHUNK, RW_WIDTH), lambda i: (out_blk(i), 0)),
                   pl.BlockSpec((1, RW_HEAD, RW_WIDTH), lambda i: (seq(i), 0, 0)),
                   pl.BlockSpec((CHUNK, GLA_WIDTH), lambda i: (out_blk(i), 0)),
                   pl.BlockSpec((1, GLA_HEADS, GLA_DK, GLA_DV), lambda i: (seq(i), 0, 0, 0))],
        out_shape=[jax.ShapeDtypeStruct((n_rows, RW_WIDTH), BF16),
                   jax.ShapeDtypeStruct((n_batch, RW_HEAD, RW_WIDTH), F32),
                   jax.ShapeDtypeStruct((n_rows, GLA_WIDTH), BF16),
                   jax.ShapeDtypeStruct((n_batch, GLA_HEADS, GLA_DK, GLA_DV), F32)],
        scratch_shapes=[pltpu.VMEM((1, W_SHIFT), F32),
                        pltpu.VMEM((RW_HEADS // 2, LANES, LANES), F32),
                        pltpu.VMEM((RW_HEADS // 2, LANES, LANES), F32),
                        pltpu.VMEM((1, W_SHIFT), F32),
                        pltpu.VMEM((CHUNK, RW_WIDTH), F32),
                        pltpu.VMEM((GLA_HEADS, GLA_DK, GLA_DV), F32),
                        pltpu.VMEM((GLA_HEADS, GLA_DK, GLA_DV), F32)],
        compiler_params=_cparams(),
        name="prompt_recurrences",
    )(zr, p['mu'], p['w0'], p['w2a'], p['a0'], p['g2'], p['k_k'], p['k_a'], p['r_k'], p['lnx_w'], p['lnx_b'],
      zg, zb, q['gk_up'], q['gk_b'], q['norm_w'])


def _rwkv_sample_pre_kernel(zs_ref, prev_ref, mu_ref, w0_ref, w2a_ref, a0_ref, g2_ref, kk_ref, ka_ref,
                            r_ref, kh_ref, v_ref, g_ref, rt_ref, wt_ref, kht_ref, vt_ref, nkkt_ref, bt_ref):
    zs = zs_ref[...]
    z = zs + (prev_ref[...] - zs) * mu_ref[...]
    r, lw, k_h, v, kk, b, g = _rwkv_pre(z, w0_ref[...], w2a_ref[...], a0_ref[...], g2_ref[...], kk_ref[...],
                                        ka_ref[...], _head_ones())
    r_ref[...] = r
    kh_ref[...] = k_h
    v_ref[...] = v
    g_ref[...] = g
    rt_ref[...] = r.T
    wt_ref[...] = jnp.exp(lw).T
    kht_ref[...] = k_h.T
    vt_ref[...] = v.T
    nkkt_ref[...] = (-kk).T
    bt_ref[...] = b.T


def _rwkv_sample_pre(zr, prev, p):
    row = lambda n: pl.BlockSpec((1, n), lambda i: (0, 0))
    full = lambda a, b: pl.BlockSpec((a, b), lambda i: (0, 0))
    vec = jax.ShapeDtypeStruct((N_SAMPLE, RW_WIDTH), F32)
    vec_t = jax.ShapeDtypeStruct((RW_WIDTH, N_SAMPLE), F32)
    return pl.pallas_call(
        _rwkv_sample_pre_kernel,
        grid=(1,),
        in_specs=[pl.BlockSpec((N_SAMPLE, W_SHIFT), lambda i: (ROW_SAMPLE // N_SAMPLE, 0)),
                  full(N_SAMPLE, W_SHIFT),
                  row(W_SHIFT), row(RW_WIDTH), full(LANES, 2 * RW_WIDTH), row(RW_WIDTH), full(LANES, RW_WIDTH),
                  row(RW_WIDTH), row(RW_WIDTH)],
        out_specs=[full(N_SAMPLE, RW_WIDTH)] * 4 + [full(RW_WIDTH, N_SAMPLE)] * 6,
        out_shape=[vec] * 4 + [vec_t] * 6,
        compiler_params=_cparams(),
        name="rwkv_sample_pre",
    )(zr, prev, p['mu'], p['w0'], p['w2a'], p['a0'], p['g2'], p['k_k'], p['k_a'])


def _rwkv_sample_step_kernel(s_ref, r_ref, w_ref, kh_ref, v_ref, nkk_ref, b_ref, so_ref, y_ref):
    s = s_ref[0]
    sa = jnp.sum(s * nkk_ref[0][None], axis=1, keepdims=True)
    s_new = s * w_ref[0][None] + sa * b_ref[0][None] + v_ref[0] * kh_ref[0][None]
    so_ref[0] = s_new
    y_ref[0] = jnp.sum(s_new * r_ref[0][None], axis=1, keepdims=True)


def _rwkv_sample_step(state_t, r_t, w_t, kh_t, v_t, nkk_t, b_t):
    kvec = lambda t: t.reshape(RW_HEADS, RW_HEAD, N_SAMPLE)
    kspec = pl.BlockSpec((1, RW_HEAD, N_SAMPLE), lambda i: (i, 0, 0))
    vspec = pl.BlockSpec((1, RW_HEAD, 1, N_SAMPLE), lambda i: (i, 0, 0, 0))
    sspec = pl.BlockSpec((1, RW_HEAD, RW_HEAD, N_SAMPLE), lambda i: (i, 0, 0, 0))
    s_new, y = pl.pallas_call(
        _rwkv_sample_step_kernel,
        grid=(RW_HEADS,),
        in_specs=[sspec, kspec, kspec, kspec, vspec, kspec, kspec],
        out_specs=[sspec, vspec],
        out_shape=[jax.ShapeDtypeStruct(state_t.shape, F32),
                   jax.ShapeDtypeStruct((RW_HEADS, RW_HEAD, 1, N_SAMPLE), F32)],
        compiler_params=_cparams(),
        name="rwkv_sample_step",
    )(state_t, kvec(r_t), kvec(w_t), kvec(kh_t), v_t.reshape(RW_HEADS, RW_HEAD, 1, N_SAMPLE), kvec(nkk_t), kvec(b_t))
    return s_new, y.reshape(RW_WIDTH, N_SAMPLE)


def _rwkv_sample_post_kernel(yt_ref, r_ref, kh_ref, v_ref, g_ref, rk_ref, lw_ref, lb_ref, o_ref):
    o = _rwkv_post(yt_ref[...].T, r_ref[...], kh_ref[...], v_ref[...], g_ref[...], rk_ref[...], lw_ref[...],
                   lb_ref[...], _head_ones())
    o_ref[0:N_SAMPLE, :] = o.astype(o_ref.dtype)
    o_ref[N_SAMPLE:T_EXT, :] = jnp.zeros((T_EXT - N_SAMPLE, RW_WIDTH), o_ref.dtype)


def _rwkv_sample_post(y_t, r, kh, v, g, p):
    row = lambda n: pl.BlockSpec((1, n), lambda i: (0, 0))
    full = lambda a, b: pl.BlockSpec((a, b), lambda i: (0, 0))
    return pl.pallas_call(
        _rwkv_sample_post_kernel,
        grid=(1,),
        in_specs=[full(RW_WIDTH, N_SAMPLE)] + [full(N_SAMPLE, RW_WIDTH)] * 4 + [row(RW_WIDTH)] * 3,
        out_specs=full(T_EXT, RW_WIDTH),
        out_shape=jax.ShapeDtypeStruct((T_EXT, RW_WIDTH), BF16),
        compiler_params=_cparams(),
        name="rwkv_sample_post",
    )(y_t, r, kh, v, g, p['r_k'], p['lnx_w'], p['lnx_b'])


def _gla_logg(zg, gk_up, gk_b):
    gkd = zg[:, 2 * GLA_KDIM + GLA_WIDTH:2 * GLA_KDIM + GLA_WIDTH + LANES]
    x = _bdot(gkd, gk_up) + gk_b
    return (jnp.minimum(x, 0.0) - jnp.log(1.0 + jnp.exp(-jnp.abs(x)))) * (1.0 / GLA_NORMALIZER)


def _gla_post(o, og_act, norm_w):
    outs = []
    for h in range(GLA_HEADS):
        oh = o[:, h * GLA_DV:(h + 1) * GLA_DV]
        ms = jnp.mean(oh * oh, axis=-1, keepdims=True)
        outs.append(oh * lax.rsqrt(ms + RMS_EPS) * norm_w)
    return jnp.concatenate(outs, axis=1) * og_act


def _gla_sample_pre_kernel(zg_ref, gkup_ref, gkb_ref, q_ref, k_ref, eg_ref):
    zg = zg_ref[...]
    logg = _gla_logg(zg, gkup_ref[...], gkb_ref[...])
    q_ref[...] = zg[:, 0:GLA_KDIM] * (GLA_DK ** -0.5)
    k_ref[...] = zg[:, GLA_KDIM:2 * GLA_KDIM]
    eg_ref[...] = jnp.exp(logg)


def _gla_sample_pre(zg, p):
    full = lambda a, b: pl.BlockSpec((a, b), lambda i: (0, 0))
    vec = jax.ShapeDtypeStruct((N_SAMPLE, GLA_KDIM), F32)
    return pl.pallas_call(
        _gla_sample_pre_kernel,
        grid=(1,),
        in_specs=[pl.BlockSpec((N_SAMPLE, W_GLA_PAD), lambda i: (ROW_SAMPLE // N_SAMPLE, 0)),
                  full(LANES, GLA_KDIM), full(1, GLA_KDIM)],
        out_specs=[full(N_SAMPLE, GLA_KDIM)] * 3,
        out_shape=[vec] * 3,
        compiler_params=_cparams(),
        name="gla_sample_pre",
    )(zg, p['gk_up'], p['gk_b'])


GLA_STEP_BATCH = 8


def _gla_sample_step_kernel(s_ref, q_ref, k_ref, eg_ref, v_ref, so_ref, o_ref):
    eye = _iota((GLA_DK, GLA_DK), 0) == _iota((GLA_DK, GLA_DK), 1)
    for b in range(GLA_STEP_BATCH):
        s = s_ref[b]
        col = lambda ref: jnp.sum(jnp.where(eye, ref[b], 0.0), axis=-1, keepdims=True)
        s_new = col(eg_ref) * s + col(k_ref) * v_ref[b]
        so_ref[b] = s_new
        o_ref[b] = jnp.sum(col(q_ref) * s_new, axis=-2, keepdims=True)


def _gla_sample_step(state, q, k, eg, v):
    kv = lambda t: t.reshape(N_SAMPLE, GLA_HEADS, 1, GLA_DK)
    kspec = pl.BlockSpec((GLA_STEP_BATCH, GLA_HEADS, 1, GLA_DK), lambda i: (i, 0, 0, 0))
    vspec = pl.BlockSpec((GLA_STEP_BATCH, GLA_HEADS, 1, GLA_DV), lambda i: (i, 0, 0, 0))
    sspec = pl.BlockSpec((GLA_STEP_BATCH, GLA_HEADS, GLA_DK, GLA_DV), lambda i: (i, 0, 0, 0))
    s_new, o = pl.pallas_call(
        _gla_sample_step_kernel,
        grid=(N_SAMPLE // GLA_STEP_BATCH,),
        in_specs=[sspec, kspec, kspec, kspec, vspec],
        out_specs=[sspec, vspec],
        out_shape=[jax.ShapeDtypeStruct(state.shape, F32),
                   jax.ShapeDtypeStruct((N_SAMPLE, GLA_HEADS, 1, GLA_DV), F32)],
        compiler_params=_cparams(),
        name="gla_sample_step",
    )(state, kv(q), kv(k), kv(eg), v.reshape(N_SAMPLE, GLA_HEADS, 1, GLA_DV))
    return s_new, o.reshape(N_SAMPLE, GLA_WIDTH)


def _gla_sample_post_kernel(o_ref, og_ref, nw_ref, out_ref):
    o = _gla_post(o_ref[...], og_ref[0:N_SAMPLE, :].astype(F32), nw_ref[...])
    out_ref[0:N_SAMPLE, :] = o.astype(out_ref.dtype)
    out_ref[N_SAMPLE:T_EXT, :] = jnp.zeros((T_EXT - N_SAMPLE, GLA_WIDTH), out_ref.dtype)


def _gla_sample_post(o, zb, p, og_block=OG_BLOCK):
    full = lambda a, b: pl.BlockSpec((a, b), lambda i: (0, 0))
    return pl.pallas_call(
        _gla_sample_post_kernel,
        grid=(1,),
        in_specs=[full(N_SAMPLE, GLA_WIDTH),
                  pl.BlockSpec((T_EXT, GLA_WIDTH), lambda i: (T_MAIN // T_EXT, og_block)),
                  full(1, GLA_DV)],
        out_specs=full(T_EXT, GLA_WIDTH),
        out_shape=jax.ShapeDtypeStruct((T_EXT, GLA_WIDTH), BF16),
        compiler_params=_cparams(),
        name="gla_sample_post",
    )(o, zb, p['norm_w'])


def _mix_route_kernel(or_ref, ore_ref, og_ref, oge_ref, gates_ref, xp_ref, xe_ref, pr_ref, pg_ref, wo_ref, n2_ref,
                      wrh_ref, wrl_ref, br_ref, h1_ref, xn_ref, info_ref, cnt_ref, carry_sc):
    i = pl.program_id(0)
    is_main = i < T_MAIN // TM_DENSE

    @pl.when(i == 0)
    def _():
        carry_sc[...] = jnp.zeros_like(carry_sc)

    sig_r = gates_ref[:, 0:D_MODEL].astype(F32)
    sig_g = gates_ref[:, D_MODEL:2 * D_MODEL].astype(F32)
    o_r = jnp.where(is_main, or_ref[...], ore_ref[...])
    o_g = jnp.where(is_main, og_ref[...], oge_ref[...])
    m = (sig_r * jnp.dot(o_r, pr_ref[...], preferred_element_type=F32)
         + sig_g * jnp.dot(o_g, pg_ref[...], preferred_element_type=F32))
    h = jnp.where(is_main, xp_ref[...], xe_ref[...])
    h1 = h + jnp.dot(m.astype(BF16), wo_ref[...], preferred_element_type=F32)
    h1_ref[...] = h1
    ms = jnp.mean(h1 * h1, axis=-1, keepdims=True)
    xn = h1 * lax.rsqrt(ms + RMS_EPS) * n2_ref[...]
    xn_ref[...] = xn
    xh = xn.astype(BF16)
    xl = (xn - xh.astype(F32)).astype(BF16)
    lg = (jnp.dot(xh, wrh_ref[...], preferred_element_type=F32)
          + jnp.dot(xh, wrl_ref[...], preferred_element_type=F32)
          + jnp.dot(xl, wrh_ref[...], preferred_element_type=F32)) + br_ref[...]

    lane = _iota(lg.shape, 1)
    lanef = lane.astype(F32)
    is_g = jnp.logical_and(lane >= N_EXPERTS, lane < N_EXPERTS + N_GROUPS)
    gl = jnp.where(is_g, lg, NEG_BIG)
    gmax = jnp.max(gl, axis=-1, keepdims=True)
    gsel = jnp.min(jnp.where(jnp.logical_and(is_g, gl == gmax), lanef, 1e9), axis=-1, keepdims=True) - N_EXPERTS
    p_g = 1.0 / jnp.sum(jnp.exp(gl - gmax), axis=-1, keepdims=True)
    grp = (lane // EXPERTS_PER_GROUP).astype(F32)
    in_grp = jnp.logical_and(lane < N_EXPERTS, grp == gsel)
    el = jnp.where(in_grp, lg, NEG_BIG)
    m1 = jnp.max(el, axis=-1, keepdims=True)
    i1 = jnp.min(jnp.where(jnp.logical_and(in_grp, el == m1), lanef, 1e9), axis=-1, keepdims=True)
    in2 = jnp.logical_and(in_grp, lanef != i1)
    el2 = jnp.where(in2, lg, NEG_BIG)
    m2 = jnp.max(el2, axis=-1, keepdims=True)
    i2 = jnp.min(jnp.where(jnp.logical_and(in2, el2 == m2), lanef, 1e9), axis=-1, keepdims=True)
    e2 = jnp.exp(m2 - m1)
    w1 = p_g / (1.0 + e2)
    w2 = p_g * e2 / (1.0 + e2)

    oh1 = lanef == i1
    oh2 = lanef == i2
    cnt = jnp.where(jnp.logical_or(oh1, oh2), 1.0, 0.0)
    tm = cnt.shape[0]
    lstrict = (_iota((tm, tm), 0) > _iota((tm, tm), 1)).astype(BF16)
    before = jnp.dot(lstrict, cnt.astype(BF16), preferred_element_type=F32) + carry_sc[...]
    rank1 = jnp.sum(jnp.where(oh1, before, 0.0), axis=-1, keepdims=True)
    rank2 = jnp.sum(jnp.where(oh2, before, 0.0), axis=-1, keepdims=True)
    carry_sc[...] = carry_sc[...] + jnp.sum(cnt, axis=0, keepdims=True)
    cnt_ref[...] = carry_sc[...]
    info = jnp.where(lane == 0, i1, jnp.where(lane == 1, i2, jnp.where(lane == 2, w1, jnp.where(
        lane == 3, w2, jnp.where(lane == 4, rank1, jnp.where(lane == 5, rank2, 0.0))))))
    info_ref[...] = info


def _mix_route(o_r, o_r_ext, o_g, o_g_ext, zb, xp, xe, p):
    nmain = T_MAIN // TM_DENSE
    tile = lambda n: pl.BlockSpec((TM_DENSE, n), lambda i: (i, 0))
    main = lambda n: pl.BlockSpec((TM_DENSE, n), lambda i: (jnp.minimum(i, nmain - 1), 0))
    ext = lambda n: pl.BlockSpec((T_EXT, n), lambda i: (0, 0))
    const = lambda a, b: pl.BlockSpec((a, b), lambda i: (0, 0), pipeline_mode=pl.Buffered(1))
    return pl.pallas_call(
        _mix_route_kernel,
        grid=(T_ALL // TM_DENSE,),
        in_specs=[main(RW_WIDTH), ext(RW_WIDTH), main(GLA_WIDTH), ext(GLA_WIDTH), tile(2 * D_MODEL),
                  main(D_MODEL), ext(D_MODEL),
                  const(RW_WIDTH, D_MODEL), const(GLA_WIDTH, D_MODEL), const(D_MODEL, D_MODEL),
                  const(1, D_MODEL), const(D_MODEL, LANES), const(D_MODEL, LANES), const(1, LANES)],
        out_specs=[tile(D_MODEL), tile(D_MODEL), tile(LANES), pl.BlockSpec((1, LANES), lambda i: (0, 0))],
        out_shape=[jax.ShapeDtypeStruct((T_ALL, D_MODEL), F32),
                   jax.ShapeDtypeStruct((T_ALL, D_MODEL), F32),
                   jax.ShapeDtypeStruct((T_ALL, LANES), F32),
                   jax.ShapeDtypeStruct((1, LANES), F32)],
        scratch_shapes=[pltpu.VMEM((1, LANES), F32)],
        compiler_params=_cparams(),
        name="mix_route",
    )(o_r, o_r_ext, o_g, o_g_ext, zb, xp, xe, p['p_rwkv'], p['p_gla'], p['w_out'], p['norm2_w'], p['wr_hi'],
      p['wr_lo'], p['b_route'])


def _dispatch_kernel(info_ref, cnt_ref, xq_ref, xs_hbm, d_ref, te_ref, nt_ref,
                     d_vm, d_sm, tab_vm, tab_sm, zbuf, sem_t, sem_x):
    i = pl.program_id(0)
    lane1 = _iota((1, LANES), 1)
    cnt = jnp.where(lane1 < N_EXPERTS, cnt_ref[...], 0.0)
    tiles = jnp.floor((cnt + (TM_MOE - 1)) * (1.0 / TM_MOE))
    upper = (_iota((LANES, LANES), 0) < _iota((LANES, LANES), 1)).astype(BF16)
    tile_start = jnp.dot(jnp.broadcast_to(tiles, (8, LANES)).astype(BF16), upper,
                         preferred_element_type=F32)[0:1, :]
    base = tile_start * TM_MOE

    info = info_ref[...]
    lanef = _iota(info.shape, 1).astype(F32)
    pick = lambda col: jnp.sum(jnp.where(lanef == info[:, col:col + 1], base, 0.0), axis=-1, keepdims=True)
    d1 = pick(0) + info[:, 4:5]
    d2 = pick(1) + info[:, 5:6]
    tm = info.shape[0]
    eye = _iota((tm, tm), 0) == _iota((tm, tm), 1)
    to_row = lambda col: jnp.sum(jnp.where(eye, col, 0.0), axis=0, keepdims=True)
    d_row = jnp.concatenate([to_row(d1), to_row(d2)], axis=1).astype(jnp.int32)
    d_ref[0] = d_row
    d_vm[...] = d_row

    @pl.when(i == 0)
    def _():
        tile_end = tile_start + tiles
        n_tiles = jnp.sum(tiles, axis=-1, keepdims=True)
        eye_l = _iota((LANES, LANES), 0) == _iota((LANES, LANES), 1)
        end_col = jnp.sum(jnp.where(eye_l, tile_end, 0.0), axis=-1, keepdims=True)
        rowl = _iota((LANES, LANES), 0)
        tile_f = _iota((LANES, LANES), 1).astype(F32)
        te = jnp.sum(jnp.where(jnp.logical_and(rowl < N_EXPERTS, end_col <= tile_f), 1.0, 0.0), axis=0,
                     keepdims=True)
        last_e = jnp.max(jnp.where(tiles > 0.0, lane1.astype(F32), 0.0), axis=-1, keepdims=True)
        te = jnp.where(lane1.astype(F32) < n_tiles, jnp.minimum(te, N_EXPERTS - 1.0), last_e)
        te_ref[...] = te.astype(jnp.int32)
        nt_ref[...] = jnp.broadcast_to(n_tiles, (1, LANES)).astype(jnp.int32)
        zbuf[...] = jnp.zeros_like(zbuf)
        last_tile = (tile_end - 1.0).astype(jnp.int32)
        nt_row = jnp.broadcast_to(n_tiles, (1, LANES)).astype(jnp.int32)
        tab_vm[...] = jnp.concatenate([last_tile, tiles.astype(jnp.int32), nt_row, jnp.zeros((5, LANES), jnp.int32)],
                                      axis=0)
        tab_copy = pltpu.make_async_copy(tab_vm, tab_sm, sem_t)
        tab_copy.start()
        tab_copy.wait()

        def zero_tile(t):
            return pltpu.make_async_copy(zbuf, xs_hbm.at[pl.ds(t * TM_MOE, TM_MOE), :], sem_x)

        def each(fn):
            def body(r, carry):
                fn(r)
                return carry
            return body

        for e in range(N_EXPERTS):
            @pl.when(tab_sm[1, e] > 0)
            def _(e=e):
                zero_tile(tab_sm[0, e]).start()
        lax.fori_loop(tab_sm[2, 0], NT_MOE, each(lambda t: zero_tile(t).start()), 0)
        for e in range(N_EXPERTS):
            @pl.when(tab_sm[1, e] > 0)
            def _(e=e):
                zero_tile(tab_sm[0, e]).wait()
        lax.fori_loop(tab_sm[2, 0], NT_MOE, each(lambda t: zero_tile(t).wait()), 0)

    d_copy = pltpu.make_async_copy(d_vm, d_sm, sem_t)
    d_copy.start()
    d_copy.wait()

    for t in range(tm):
        for half in range(2):
            pltpu.make_async_copy(xq_ref.at[pl.ds(t, 1), :], xs_hbm.at[pl.ds(d_sm[0, half * tm + t], 1), :],
                                  sem_x).start(priority=half)
    for half in range(2):
        pltpu.make_async_copy(xq_ref, xs_hbm.at[pl.ds(0, tm), :], sem_x).wait()


def _dispatch(info, counts, xq):
    nt = T_ALL // TM_DENSE
    return pl.pallas_call(
        _dispatch_kernel,
        grid=(nt,),
        in_specs=[pl.BlockSpec((TM_DENSE, LANES), lambda i: (i, 0)),
                  pl.BlockSpec((1, LANES), lambda i: (0, 0)),
                  pl.BlockSpec((TM_DENSE, D_MODEL), lambda i: (i, 0))],
        out_specs=[pl.BlockSpec(memory_space=pl.ANY),
                   pl.BlockSpec((1, 1, 2 * TM_DENSE), lambda i: (i, 0, 0)),
                   pl.BlockSpec((1, LANES), lambda i: (0, 0)),
                   pl.BlockSpec((1, LANES), lambda i: (0, 0))],
        out_shape=[jax.ShapeDtypeStruct((NT_MOE * TM_MOE, D_MODEL), F32),
                   jax.ShapeDtypeStruct((nt, 1, 2 * TM_DENSE), jnp.int32),
                   jax.ShapeDtypeStruct((1, LANES), jnp.int32),
                   jax.ShapeDtypeStruct((1, LANES), jnp.int32)],
        scratch_shapes=[pltpu.VMEM((1, 2 * TM_DENSE), jnp.int32),
                        pltpu.SMEM((1, 2 * TM_DENSE), jnp.int32),
                        pltpu.VMEM((8, LANES), jnp.int32),
                        pltpu.SMEM((8, LANES), jnp.int32),
                        pltpu.VMEM((TM_MOE, D_MODEL), F32),
                        pltpu.SemaphoreType.DMA(()),
                        pltpu.SemaphoreType.DMA(())],
        compiler_params=_cparams(),
        name="moe_dispatch",
    )(info, counts, xq)


def _moe_kernel(te_ref, nt_ref, xs_ref, w1_hbm, w3_hbm, w2_hbm, o_ref, w1r, w3r, w2r, w1b, w3b, w2b, sem):
    i = pl.program_id(0)
    n_tiles = nt_ref[0]

    def weight_copies(e):
        return [pltpu.make_async_copy(w1_hbm.at[e], w1r, sem), pltpu.make_async_copy(w3_hbm.at[e], w3r, sem),
                pltpu.make_async_copy(w2_hbm.at[e], w2r, sem)]

    @pl.when(i < n_tiles)
    def _():
        e = te_ref[i]

        @pl.when(i == 0)
        def _():
            for c in weight_copies(e):
                c.start()

        @pl.when(jnp.logical_or(i == 0, e != te_ref[jnp.maximum(i - 1, 0)]))
        def _():
            for c in weight_copies(e):
                c.wait()
            w1b[...] = w1r[...].astype(BF16)
            w3b[...] = w3r[...].astype(BF16)
            w2b[...] = w2r[...].astype(BF16)
            nxt = lax.while_loop(
                lambda j: jnp.logical_and(j < n_tiles, te_ref[jnp.minimum(j, NT_MOE - 1)] == e),
                lambda j: j + 1, i + 1)

            @pl.when(nxt < n_tiles)
            def _():
                for c in weight_copies(te_ref[jnp.minimum(nxt, NT_MOE - 1)]):
                    c.start()

        xb = xs_ref[...].astype(BF16)
        h1 = jnp.dot(xb, w1b[...], preferred_element_type=F32)
        h3 = jnp.dot(xb, w3b[...], preferred_element_type=F32)
        hh = (h1 * jax.nn.sigmoid(h1) * h3).astype(BF16)
        o_ref[...] = jnp.dot(hh, w2b[...], preferred_element_type=F32)

    @pl.when(i >= nt_ref[0])
    def _():
        o_ref[...] = jnp.zeros_like(o_ref)


def _moe(tile_expert, n_tiles, xs, w1, w3, w2):
    grid_spec = pltpu.PrefetchScalarGridSpec(
        num_scalar_prefetch=2,
        grid=(NT_MOE,),
        in_specs=[pl.BlockSpec((TM_MOE, D_MODEL), lambda i, te, nt: (jnp.minimum(i, nt[0] - 1), 0)),
                  pl.BlockSpec(memory_space=pl.ANY),
                  pl.BlockSpec(memory_space=pl.ANY),
                  pl.BlockSpec(memory_space=pl.ANY)],
        out_specs=pl.BlockSpec((TM_MOE, D_MODEL), lambda i, te, nt: (i, 0)),
        scratch_shapes=[pltpu.VMEM((D_MODEL, D_EXPERT), F32),
                        pltpu.VMEM((D_MODEL, D_EXPERT), F32),
                        pltpu.VMEM((D_EXPERT, D_MODEL), F32),
                        pltpu.VMEM((D_MODEL, D_EXPERT), BF16),
                        pltpu.VMEM((D_MODEL, D_EXPERT), BF16),
                        pltpu.VMEM((D_EXPERT, D_MODEL), BF16),
                        pltpu.SemaphoreType.DMA(())],
    )
    return pl.pallas_call(
        _moe_kernel,
        grid_spec=grid_spec,
        out_shape=jax.ShapeDtypeStruct((NT_MOE * TM_MOE, D_MODEL), F32),
        compiler_params=_cparams(),
        name="moe_experts",
    )(tile_expert, n_tiles, xs, w1, w3, w2)


def _combine_kernel(dcur_ref, dnext_ref, h1_ref, info_ref, fw_ref, eo_hbm, yp_ref, ye_ref, buf, sem):
    i = pl.program_id(0)
    nmain = T_MAIN // TM_DENSE
    slot = lax.rem(i, 2)

    def row_copy(d_ref, s, t):
        return pltpu.make_async_copy(eo_hbm.at[pl.ds(d_ref[0, 0, t], 1), :], buf.at[s, pl.ds(t, 1), :], sem.at[s])

    def issue_all(d_ref, s):
        for t in range(2 * TM_DENSE):
            row_copy(d_ref, s, t).start(priority=t % 2)

    @pl.when(i == 0)
    def _():
        issue_all(dcur_ref, slot)

    @pl.when(i + 1 < pl.num_programs(0))
    def _():
        issue_all(dnext_ref, 1 - slot)

    pltpu.make_async_copy(eo_hbm.at[pl.ds(0, 2 * TM_DENSE), :], buf.at[slot], sem.at[slot]).wait()

    info = info_ref[...]
    y = h1_ref[...] + info[:, 2:3] * buf[slot, 0:TM_DENSE, :] + info[:, 3:4] * buf[slot, TM_DENSE:2 * TM_DENSE, :]
    ms = jnp.mean(y * y, axis=-1, keepdims=True)
    out = y * lax.rsqrt(ms + RMS_EPS) * fw_ref[...]

    @pl.when(i < nmain)
    def _():
        yp_ref[...] = out

    @pl.when(i == nmain)
    def _():
        ye_ref[...] = out


def _combine(dst, h1, info, fw, eo):
    nmain = T_MAIN // TM_DENSE
    return pl.pallas_call(
        _combine_kernel,
        grid=(T_ALL // TM_DENSE,),
        in_specs=[pl.BlockSpec((1, 1, 2 * TM_DENSE), lambda i: (i, 0, 0), memory_space=pltpu.SMEM),
                  pl.BlockSpec((1, 1, 2 * TM_DENSE), lambda i: (jnp.minimum(i + 1, nmain), 0, 0),
                               memory_space=pltpu.SMEM),
                  pl.BlockSpec((TM_DENSE, D_MODEL), lambda i: (i, 0)),
                  pl.BlockSpec((TM_DENSE, LANES), lambda i: (i, 0)),
                  pl.BlockSpec((1, D_MODEL), lambda i: (0, 0)),
                  pl.BlockSpec(memory_space=pl.ANY)],
        out_specs=[pl.BlockSpec((TM_DENSE, D_MODEL), lambda i: (jnp.minimum(i, nmain - 1), 0)),
                   pl.BlockSpec((T_EXT, D_MODEL), lambda i: (0, 0))],
        out_shape=[jax.ShapeDtypeStruct((T_MAIN, D_MODEL), F32),
                   jax.ShapeDtypeStruct((T_EXT, D_MODEL), F32)],
        scratch_shapes=[pltpu.VMEM((2, 2 * TM_DENSE, D_MODEL), F32), pltpu.SemaphoreType.DMA((2,))],
        compiler_params=_cparams(),
        name="moe_combine",
    )(dst, dst, h1, info, fw, eo)


def kernel(x_prompt, x_sample, state_rwkv, state_shift, state_gla, meta_tokens, norm1_w, w_in, mu_shift, rw_w0,
           rw_w2, rw_a0, rw_a2, rw_g2, rw_k_k, rw_k_a, rw_r_k, rw_lnx_w, rw_lnx_b, gla_gk_up, gla_gk_b, gla_norm_w,
           p_rwkv, p_gla, w_out, norm2_w, moe_w_group, moe_b_group, moe_w_router, moe_b_router, moe_w1, moe_w3,
           moe_w2, final_norm_w):
    w_t = w_in[0].T
    row = lambda t: t.reshape(1, -1)
    w2a = jnp.zeros((LANES, 2 * RW_WIDTH), F32)
    w2a = w2a.at[0:64, 0:RW_WIDTH].set(rw_w2[0]).at[64:128, RW_WIDTH:].set(rw_a2[0])
    rw = dict(mu=row(mu_shift[0][_SHIFT_PERM]), w0=row(rw_w0[0]), w2a=w2a, a0=row(rw_a0[0]), g2=rw_g2[0],
              k_k=row(rw_k_k[0]), k_a=row(rw_k_a[0]), r_k=row(rw_r_k[0]), lnx_w=row(rw_lnx_w[0]),
              lnx_b=row(rw_lnx_b[0]))
    gk_up = jnp.zeros((LANES, GLA_KDIM), F32).at[0:GLA_LORA].set(gla_gk_up[0])
    gl = dict(gk_up=gk_up, gk_b=row(gla_gk_b[0]), norm_w=row(gla_norm_w[0]))
    w_route = jnp.zeros((D_MODEL, LANES), F32)
    w_route = w_route.at[:, 0:N_EXPERTS].set(moe_w_router[0]).at[:, N_EXPERTS:N_EXPERTS + N_GROUPS].set(moe_w_group[0])
    wr_hi = w_route.astype(BF16)
    wr_lo = (w_route - wr_hi.astype(F32)).astype(BF16)
    b_route = jnp.zeros((1, LANES), F32)
    b_route = b_route.at[0, 0:N_EXPERTS].set(moe_b_router[0]).at[0, N_EXPERTS:N_EXPERTS + N_GROUPS].set(moe_b_group[0])
    mix = dict(p_rwkv=p_rwkv[0].astype(BF16), p_gla=p_gla[0].astype(BF16), w_out=w_out[0].astype(BF16),
               norm2_w=row(norm2_w[0]), wr_hi=wr_hi, wr_lo=wr_lo, b_route=b_route)

    xp = x_prompt.reshape(T_MAIN, D_MODEL)
    xe = jnp.concatenate([x_sample[:, 0, :], jnp.zeros((META_PAD, D_MODEL), F32), meta_tokens,
                          jnp.zeros((T_ALL - ROW_META - CHUNK, D_MODEL), F32)], axis=0)

    xn = _norm1(xp, xe, row(norm1_w[0]))
    zr = _proj(xn, w_t, _RWKV_TILES, W_SHIFT // 2, F32, name="proj_rwkv")
    zg = _proj(xn, w_t, _GLA_TILES, W_GLA_PAD // 2, F32, name="proj_gla")
    zb = _proj(xn, w_t, _GATE_TILES, GLA_WIDTH, BF16, silu_from=2 * D_MODEL // GLA_WIDTH, name="proj_gates")

    o_r, h_fin, o_g, s_fin = _prompt_recurrences(zr, zg, zb, rw, gl)

    r_s, kh_s, v_s, g_s, r_t, w_t, kh_t, v_t, nkk_t, b_t = _rwkv_sample_pre(zr, state_shift[0][:, _SHIFT_PERM], rw)
    rw_new_t, y_t = _rwkv_sample_step(state_rwkv[0].transpose(1, 2, 3, 0), r_t, w_t, kh_t, v_t, nkk_t, b_t)
    rw_new = rw_new_t.transpose(3, 0, 1, 2)
    o_r_ext = _rwkv_sample_post(y_t, r_s, kh_s, v_s, g_s, rw)
    q_s, k_s, eg_s = _gla_sample_pre(zg, gl)
    v_gs = zg[ROW_SAMPLE:ROW_SAMPLE + N_SAMPLE, 2 * GLA_KDIM:2 * GLA_KDIM + GLA_WIDTH]
    gla_new, og_s = _gla_sample_step(state_gla[0], q_s, k_s, eg_s, v_gs)
    o_g_ext = _gla_sample_post(og_s, zb, gl)

    h1, xn2, info, counts = _mix_route(o_r, o_r_ext, o_g, o_g_ext, zb, xp, xe, mix)
    xs, dst, tile_expert, n_tiles = _dispatch(info, counts, xn2)
    eo = _moe(tile_expert[0, :NT_MOE], n_tiles[0, :1], xs, moe_w1[0], moe_w3[0], moe_w2[0])
    y_p, y_e = _combine(dst, h1, info, row(final_norm_w), eo)

    y_prompt = y_p.reshape(N_BATCH, SEQ, D_MODEL)
    y_sample = y_e[0:N_SAMPLE].reshape(N_SAMPLE, 1, D_MODEL)
    new_rwkv_prompt = h_fin.reshape(N_BATCH, RW_HEAD, RW_HEADS, RW_HEAD).transpose(0, 2, 3, 1)[None]
    shift_rows = jnp.concatenate([zr[(b + 1) * SEQ - 1:(b + 1) * SEQ] for b in range(N_BATCH)]
                                 + [zr[ROW_SAMPLE:ROW_SAMPLE + N_SAMPLE]], axis=0)
    shift_rows = shift_rows[:, _SHIFT_INV]
    return (y_prompt, y_sample, new_rwkv_prompt, shift_rows[None, 0:N_BATCH], s_fin[None],
            rw_new[None], shift_rows[None, N_BATCH:], gla_new[None])
```

```python
import functools
import math

import numpy as np
import jax
import jax.numpy as jnp
from jax import lax
from jax.experimental import pallas as pl
from jax.experimental.pallas import tpu as pltpu

F32 = jnp.float32
BF16 = jnp.bfloat16

D_MODEL = 2048
N_BATCH = 4
SEQ = 2048
N_SAMPLE = 128
N_META = 16
RMS_EPS = 1e-6

RW_WIDTH = 1024
RW_HEAD = 64
RW_HEADS = 16
RW_GN_EPS = RW_HEAD * 1e-5
W_SHIFT = 3328
GLA_HEADS = 4
GLA_DK = 128
GLA_DV = 256
GLA_KDIM = 512
GLA_WIDTH = 1024
GLA_LORA = 16
GLA_NORMALIZER = 16.0
W_GLA_PAD = 2304
OG_BLOCK = 2 * D_MODEL // GLA_WIDTH
N_GROUPS = 4
EXPERTS_PER_GROUP = 8
N_EXPERTS = 32
D_EXPERT = 512

CHUNK = 64
N_CHUNKS = SEQ // CHUNK
META_PAD = CHUNK - N_META
T_MAIN = N_BATCH * SEQ
ROW_SAMPLE = T_MAIN
ROW_META = T_MAIN + N_SAMPLE
T_EXT = 256
T_ALL = T_MAIN + T_EXT
LANES = 128

TM_DENSE = 256
TM_MM = T_ALL // 8
TM_MOE = 256
NT_MOE = (2 * T_ALL) // TM_MOE + N_EXPERTS
NEG_BIG = -1e30

VMEM_LIMIT = 56 * 1024 * 1024

_SHIFT_PERM = np.concatenate([np.arange(0, 1024), np.arange(1088, 2112), np.arange(2112, 3136),
                              np.arange(1024, 1088), np.arange(3136, 3328)])
_SHIFT_INV = np.argsort(_SHIFT_PERM)


def _cparams(n_axes=1):
    return pltpu.CompilerParams(dimension_semantics=("arbitrary",) * n_axes, vmem_limit_bytes=VMEM_LIMIT)


def _bdot(a, b):
    return jnp.dot(a.astype(BF16), b.astype(BF16), preferred_element_type=F32)


def _bdot_nt(a, b):
    return lax.dot_general(a.astype(BF16), b.astype(BF16), (((1,), (1,)), ((), ())), preferred_element_type=F32)


def _bdot_tn(a, b):
    return lax.dot_general(a.astype(BF16), b.astype(BF16), (((0,), (0,)), ((), ())), preferred_element_type=F32)


def _split_dot(m_bf16, x):
    hi = x.astype(BF16)
    lo = (x - hi.astype(F32)).astype(BF16)
    return (jnp.dot(m_bf16, hi, preferred_element_type=F32) + jnp.dot(m_bf16, lo, preferred_element_type=F32))


def _iota(shape, dim):
    return lax.broadcasted_iota(jnp.int32, shape, dim)


def _tri_incl(n):
    return (_iota((n, n), 0) >= _iota((n, n), 1)).astype(BF16)


def _head_ones():
    return ((_iota((LANES, LANES), 0) // RW_HEAD) == (_iota((LANES, LANES), 1) // RW_HEAD)).astype(BF16)


def _seg_sum(x, bd):
    m, n = x.shape
    nb = n // LANES
    xs = jnp.concatenate([x[:, j * LANES:(j + 1) * LANES] for j in range(nb)], axis=0)
    s = jnp.dot(xs.astype(BF16), bd, preferred_element_type=F32)
    return jnp.concatenate([s[j * m:(j + 1) * m] for j in range(nb)], axis=1)


def _col_of_row(row):
    n = row.shape[-1]
    eye = _iota((n, n), 0) == _iota((n, n), 1)
    return jnp.sum(jnp.where(eye, jnp.broadcast_to(row, (n, n)), 0.0), axis=-1, keepdims=True)


def _norm1_kernel(xp_ref, xe_ref, w_ref, o_ref):
    i = pl.program_id(0)

    def f(x):
        ms = jnp.mean(x * x, axis=-1, keepdims=True)
        return (x * lax.rsqrt(ms + RMS_EPS) * w_ref[...]).astype(BF16)

    @pl.when(i < T_MAIN // TM_DENSE)
    def _():
        o_ref[...] = f(xp_ref[...])

    @pl.when(i == T_MAIN // TM_DENSE)
    def _():
        o_ref[...] = f(xe_ref[...])


def _norm1(xp, xe, w):
    nmain = T_MAIN // TM_DENSE
    return pl.pallas_call(
        _norm1_kernel,
        grid=(T_ALL // TM_DENSE,),
        in_specs=[pl.BlockSpec((TM_DENSE, D_MODEL), lambda i: (jnp.minimum(i, nmain - 1), 0)),
                  pl.BlockSpec((T_EXT, D_MODEL), lambda i: (0, 0)),
                  pl.BlockSpec((1, D_MODEL), lambda i: (0, 0))],
        out_specs=pl.BlockSpec((TM_DENSE, D_MODEL), lambda i: (i, 0)),
        out_shape=jax.ShapeDtypeStruct((T_ALL, D_MODEL), BF16),
        compiler_params=_cparams(),
        name="norm1",
    )(xp, xe, w)


def _proj_kernel(x_ref, wt_hbm, o_ref, wraw, wbf, sem, *, tiles, silu_from):
    j = pl.program_id(0)
    m = pl.program_id(1)
    tn = wbf.shape[0]

    def copies(jj):
        return [pltpu.make_async_copy(wt_hbm.at[pl.ds(src, n), :], wraw.at[pl.ds(dst, n), :], sem)
                for (src, n, dst) in tiles[jj]]

    @pl.when(jnp.logical_and(j == 0, m == 0))
    def _():
        for c in copies(0):
            c.start()

    for jj in range(len(tiles)):
        @pl.when(jnp.logical_and(j == jj, m == 0))
        def _(jj=jj):
            for c in copies(jj):
                c.wait()
            edge = 0
            for lo, hi in sorted((dst, dst + n) for (_, n, dst) in tiles[jj]) + [(tn, tn)]:
                if lo > edge:
                    wbf[edge:lo, :] = jnp.zeros((lo - edge, wbf.shape[1]), BF16)
                if hi > lo:
                    wbf[lo:hi, :] = wraw[lo:hi, :].astype(BF16)
                edge = hi
            if jj + 1 < len(tiles):
                for c in copies(jj + 1):
                    c.start()

    z = lax.dot_general(x_ref[...], wbf[...], (((1,), (1,)), ((), ())), preferred_element_type=F32)
    if silu_from is None:
        o_ref[...] = z.astype(o_ref.dtype)
    else:
        s = jax.nn.sigmoid(z)
        o_ref[...] = jnp.where(j >= silu_from, z * s, s).astype(o_ref.dtype)


def _proj(x, w_t, tiles, tn, out_dtype, silu_from=None, name="proj"):
    t, k = x.shape
    return pl.pallas_call(
        functools.partial(_proj_kernel, tiles=tiles, silu_from=silu_from),
        grid=(len(tiles), t // TM_MM),
        in_specs=[pl.BlockSpec((TM_MM, k), lambda j, m: (m, 0)),
                  pl.BlockSpec(memory_space=pl.ANY)],
        out_specs=pl.BlockSpec((TM_MM, tn), lambda j, m: (m, j)),
        out_shape=jax.ShapeDtypeStruct((t, tn * len(tiles)), out_dtype),
        scratch_shapes=[pltpu.VMEM((tn, k), F32), pltpu.VMEM((tn, k), BF16), pltpu.SemaphoreType.DMA(())],
        compiler_params=_cparams(2),
        name=name,
    )(x, w_t)


_RWKV_TILES = (((0, 1024, 0), (1088, 640, 1024)),
               ((1728, 1408, 0), (1024, 64, 1408), (3136, 192, 1472)))
_GLA_TILES = (((3328, 1152, 0),), ((4480, 912, 0),))
_GATE_TILES = tuple(((6416 + 1024 * j, 1024, 0),) for j in range(4)) + (((5392, 1024, 0),),)


def _rwkv_pre(z, w0, w2a, a0, g2, k_k, k_a, bd):
    r = z[:, 0:1024]
    k = z[:, 1024:2048]
    v = z[:, 2048:3072]
    wa = z[:, 3072:3200]
    gd = z[:, 3200:3328]
    lane = _iota(wa.shape, 1)
    wa = jnp.where(lane < 64, jnp.tanh(wa), wa)
    up = _bdot(wa, w2a)
    lw = -math.exp(-0.5) * jax.nn.sigmoid(w0 + up[:, :1024])
    a = jax.nn.sigmoid(a0 + up[:, 1024:])
    g = _bdot(jax.nn.sigmoid(gd), g2)
    kk = k * k_k
    kk = kk * lax.rsqrt(jnp.maximum(_seg_sum(kk * kk, bd), 1e-24))
    k_h = k * (1.0 + (a - 1.0) * k_a)
    return r, lw, k_h, v, kk, kk * a, g


def _rwkv_post(y, r, k_h, v, g, r_k, lnx_w, lnx_b, bd):
    mean = _seg_sum(y, bd) * (1.0 / RW_HEAD)
    d = y - mean
    var = _seg_sum(d * d, bd) * (1.0 / RW_HEAD)
    yn = d * lax.rsqrt(var + RW_GN_EPS) * lnx_w + lnx_b
    bonus = _seg_sum(r * k_h * r_k, bd) * v
    return (yn + bonus) * g


def _stack2(x):
    lane = _iota(x.shape, 1)
    return jnp.concatenate([jnp.where(lane < RW_HEAD, x, 0.0), jnp.where(lane >= RW_HEAD, x, 0.0)], axis=0)


def _unstack2(x):
    c = x.shape[0] // 2
    return x[:c] + x[c:]


def _prompt_chunk_kernel(zs_ref, mu_ref, w0_ref, w2a_ref, a0_ref, g2_ref, kk_ref, ka_ref, rk_ref, lw_ref, lb_ref,
                         zg_ref, og_ref, gkup_ref, gkb_ref, nw_ref,
                         o_ref, hout_ref, go_ref, sout_ref,
                         prev_sc, h_sc, hmeta_sc, prevmeta_sc, y_sc, s_sc, smeta_sc, *, n_chunks):
    i = pl.program_id(0)
    is_meta = i == 0
    c = lax.rem(jnp.maximum(i - 1, 0), n_chunks)
    first = jnp.logical_and(i >= 1, c == 0)
    last = jnp.logical_and(i >= 1, c == n_chunks - 1)

    @pl.when(is_meta)
    def _():
        h_sc[...] = jnp.zeros_like(h_sc)
        prev_sc[...] = jnp.zeros_like(prev_sc)
        s_sc[...] = jnp.zeros_like(s_sc)

    @pl.when(first)
    def _():
        h_sc[...] = hmeta_sc[...]
        prev_sc[...] = prevmeta_sc[...]
        s_sc[...] = smeta_sc[...]

    rowi = _iota((CHUNK, 1), 0)
    pad_row = jnp.logical_and(is_meta, rowi < META_PAD)
    heads = range(GLA_HEADS)
    ks = [slice(h * GLA_DK, (h + 1) * GLA_DK) for h in heads]
    gla = {}

    def gla_prep():
        zg = zg_ref[...]
        logg = jnp.where(pad_row, 0.0, _gla_logg(zg, gkup_ref[...], gkb_ref[...]))
        bcum = _split_dot(_tri_incl(CHUNK), logg)
        b_end = bcum[CHUNK - 1:CHUNK, :]
        gla['qt'] = zg[:, 0:GLA_KDIM] * (GLA_DK ** -0.5) * jnp.exp(bcum)
        gla['kt'] = zg[:, GLA_KDIM:2 * GLA_KDIM] * jnp.exp(-bcum)
        gla['ke'] = zg[:, GLA_KDIM:2 * GLA_KDIM] * jnp.exp(b_end - bcum)
        gla['e_end'] = jnp.exp(b_end)
        gla['vh'] = [zg[:, 2 * GLA_KDIM + h * GLA_DV:2 * GLA_KDIM + (h + 1) * GLA_DV].astype(BF16) for h in heads]

    def gla_scores():
        causal = _iota((CHUNK, CHUNK), 0) >= _iota((CHUNK, CHUNK), 1)
        gla['a'] = [jnp.where(causal, _bdot_nt(gla['qt'][:, ks[h]], gla['kt'][:, ks[h]]), 0.0) for h in heads]
        gla['s0'] = [s_sc[h] for h in heads]

    def gla_state():
        gla['o_inter'] = [_bdot(gla['qt'][:, ks[h]], gla['s0'][h]) for h in heads]
        gla['s_add'] = [_bdot_tn(gla['ke'][:, ks[h]], gla['vh'][h]) for h in heads]

    def gla_finish():
        outs = [_bdot(gla['a'][h], gla['vh'][h]) + gla['o_inter'][h] for h in heads]
        gla['s_new'] = [_col_of_row(gla['e_end'][:, ks[h]]) * gla['s0'][h] + gla['s_add'][h] for h in heads]
        for h in heads:
            s_sc[h] = gla['s_new'][h]
        go = _gla_post(jnp.concatenate(outs, axis=1), og_ref[...].astype(F32), nw_ref[...])
        go_ref[...] = go.astype(go_ref.dtype)

    zs = zs_ref[...]
    sh = pltpu.roll(zs, 1, 0)
    sh = jnp.where(rowi == 0, prev_sc[...], sh)
    prev_sc[...] = zs[CHUNK - 1:CHUNK, :]
    z = zs + (sh - zs) * mu_ref[...]
    head_ones = _head_ones()
    r, lw, k_h, v, kk, b, g = _rwkv_pre(z, w0_ref[...], w2a_ref[...], a0_ref[...], g2_ref[...], kk_ref[...],
                                        ka_ref[...], head_ones)
    lw = jnp.where(pad_row, 0.0, lw)
    gla_prep()

    cl = _split_dot(_tri_incl(CHUNK), lw)
    cl_end = cl[CHUNK - 1:CHUNK, :]
    e_neg = jnp.exp(-cl)
    e_c = jnp.exp(cl_end)
    kkt = kk * jnp.exp(cl - lw)
    rt = r * jnp.exp(cl)
    bt = b * e_neg
    kt = k_h * e_neg
    bh = bt * e_c
    kh = kt * e_c

    n2 = 2 * CHUNK
    tok_r = jnp.bitwise_and(_iota((n2, n2), 0), CHUNK - 1)
    tok_c = jnp.bitwise_and(_iota((n2, n2), 1), CHUNK - 1)
    strict = tok_r > tok_c
    incl = tok_r >= tok_c

    pairs = range(RW_HEADS // 2)
    sls = [slice(j * LANES, (j + 1) * LANES) for j in pairs]
    s_kkt = [_stack2(kkt[:, sl]) for sl in sls]
    s_rt = [_stack2(rt[:, sl]) for sl in sls]
    s_v = [_stack2(v[:, sl]).astype(BF16) for sl in sls]
    rb = [jnp.concatenate([_stack2(bt[:, sl]), _stack2(kt[:, sl])], axis=0).astype(BF16) for sl in sls]
    aa = [_bdot_nt(s_kkt[j], rb[j]) for j in pairs]
    mm = [_bdot_nt(s_rt[j], rb[j]) for j in pairs]
    h_kv = [_bdot_tn(_stack2(kh[:, sls[j]]), s_v[j]) for j in pairs]
    gla_scores()
    a_b =[jnp.where(strict, aa[j][:, :n2], 0.0).astype(BF16) for j in pairs]
    a_k = [jnp.where(strict, aa[j][:, n2:], 0.0) for j in pairs]
    m_rb = [jnp.where(incl, mm[j][:, :n2], 0.0).astype(BF16) for j in pairs]
    m_rk = [jnp.where(incl, mm[j][:, n2:], 0.0) for j in pairs]
    h0 = [h_sc[j] for j in pairs]
    akv = [_bdot(a_k[j], s_v[j]) for j in pairs]
    y_kv = [_bdot(m_rk[j], s_v[j]) for j in pairs]
    y_h = [_bdot(rt[:, sls[j]], h0[j]) for j in pairs]
    x = [_bdot(s_kkt[j], h0[j]) + akv[j] for j in pairs]
    p = a_b
    x = [x[j] - _bdot(p[j], x[j]) for j in pairs]
    for step in range(5):
        p = [_bdot(p[j], p[j]).astype(BF16) for j in pairs]
        x = [x[j] + _bdot(p[j], x[j]) for j in pairs]
        if step == 1:
            gla_state()
        if step == 3:
            gla_finish()
    zq = [_bdot(m_rb[j], x[j]) for j in pairs]
    zb = [_bdot_tn(_stack2(bh[:, sls[j]]), x[j]) for j in pairs]
    for j in pairs:
        y_sc[:, sls[j]] = y_h[j] + _unstack2(y_kv[j] - zq[j])
    h_new = [_col_of_row(e_c[:, sls[j]]) * h0[j] + (h_kv[j] - zb[j]) for j in pairs]
    for j in pairs:
        h_sc[j] = h_new[j]

    o = _rwkv_post(y_sc[...], r, k_h, v, g, rk_ref[...], lw_ref[...], lb_ref[...], head_ones)
    o_ref[...] = o.astype(o_ref.dtype)

    @pl.when(last)
    def _():
        for j in pairs:
            hout_ref[0, :, sls[j]] = _unstack2(h_new[j])
        for h in heads:
            sout_ref[0, h] = gla['s_new'][h]

    @pl.when(is_meta)
    def _():
        hmeta_sc[...] = h_sc[...]
        prevmeta_sc[...] = prev_sc[...]
        smeta_sc[...] = s_sc[...]


def _prompt_recurrences(zr, zg, zb, p, q, n_batch=N_BATCH, n_chunks=N_CHUNKS, meta_block=ROW_META // CHUNK,
                        og_block=OG_BLOCK):
    row = lambda n: pl.BlockSpec((1, n), lambda i: (0, 0))
    full = lambda a, b: pl.BlockSpec((a, b), lambda i: (0, 0))
    blk = lambda i: jnp.where(i == 0, meta_block, i - 1)
    out_blk = lambda i: jnp.maximum(i - 1, 0)
    seq = lambda i: jnp.maximum(i - 1, 0) // n_chunks
    n_rows = n_batch * n_chunks * CHUNK
    return pl.pallas_call(
        functools.partial(_prompt_chunk_kernel, n_chunks=n_chunks),
        grid=(1 + n_batch * n_chunks,),
        in_specs=[pl.BlockSpec((CHUNK, W_SHIFT), lambda i: (blk(i), 0)),
                  row(W_SHIFT), row(RW_WIDTH), full(LANES, 2 * RW_WIDTH), row(RW_WIDTH), full(LANES, RW_WIDTH),
                  row(RW_WIDTH), row(RW_WIDTH), row(RW_WIDTH), row(RW_WIDTH), row(RW_WIDTH),
                  pl.BlockSpec((CHUNK, W_GLA_PAD), lambda i: (blk(i), 0)),
                  pl.BlockSpec((CHUNK, GLA_WIDTH), lambda i: (blk(i), og_block)),
                  full(LANES, GLA_KDIM), full(1, GLA_KDIM), full(1, GLA_DV)],
        out_specs=[pl.BlockSpec((CHUNK, RW_WIDTH), lambda i: (out_blk(i), 0)),
                   pl.BlockSpec((1, RW_HEAD, RW_WIDTH), lambda i: (seq(i), 0, 0)),
                   pl.BlockSpec((CHUNK, GLA_WIDTH), lambda i: (out_blk(i), 0)),
                   pl.BlockSpec((1, GLA_HEADS, GLA_DK, GLA_DV), lambda i: (seq(i), 0, 0, 0))],
        out_shape=[jax.ShapeDtypeStruct((n_rows, RW_WIDTH), BF16),
                   jax.ShapeDtypeStruct((n_batch, RW_HEAD, RW_WIDTH), F32),
                   jax.ShapeDtypeStruct((n_rows, GLA_WIDTH), BF16),
                   jax.ShapeDtypeStruct((n_batch, GLA_HEADS, GLA_DK, GLA_DV), F32)],
        scratch_shapes=[pltpu.VMEM((1, W_SHIFT), F32),
                        pltpu.VMEM((RW_HEADS // 2, LANES, LANES), F32),
                        pltpu.VMEM((RW_HEADS // 2, LANES, LANES), F32),
                        pltpu.VMEM((1, W_SHIFT), F32),
                        pltpu.VMEM((CHUNK, RW_WIDTH), F32),
                        pltpu.VMEM((GLA_HEADS, GLA_DK, GLA_DV), F32),
                        pltpu.VMEM((GLA_HEADS, GLA_DK, GLA_DV), F32)],
        compiler_params=_cparams(),
        name="prompt_recurrences",
    )(zr, p['mu'], p['w0'], p['w2a'], p['a0'], p['g2'], p['k_k'], p['k_a'], p['r_k'], p['lnx_w'], p['lnx_b'],
      zg, zb, q['gk_up'], q['gk_b'], q['norm_w'])


def _rwkv_sample_pre_kernel(zs_ref, prev_ref, mu_ref, w0_ref, w2a_ref, a0_ref, g2_ref, kk_ref, ka_ref,
                            r_ref, kh_ref, v_ref, g_ref, rt_ref, wt_ref, kht_ref, vt_ref, nkkt_ref, bt_ref):
    zs = zs_ref[...]
    z = zs + (prev_ref[...] - zs) * mu_ref[...]
    r, lw, k_h, v, kk, b, g = _rwkv_pre(z, w0_ref[...], w2a_ref[...], a0_ref[...], g2_ref[...], kk_ref[...],
                                        ka_ref[...], _head_ones())
    r_ref[...] = r
    kh_ref[...] = k_h
    v_ref[...] = v
    g_ref[...] = g
    rt_ref[...] = r.T
    wt_ref[...] = jnp.exp(lw).T
    kht_ref[...] = k_h.T
    vt_ref[...] = v.T
    nkkt_ref[...] = (-kk).T
    bt_ref[...] = b.T


def _rwkv_sample_pre(zr, prev, p):
    row = lambda n: pl.BlockSpec((1, n), lambda i: (0, 0))
    full = lambda a, b: pl.BlockSpec((a, b), lambda i: (0, 0))
    vec = jax.ShapeDtypeStruct((N_SAMPLE, RW_WIDTH), F32)
    vec_t = jax.ShapeDtypeStruct((RW_WIDTH, N_SAMPLE), F32)
    return pl.pallas_call(
        _rwkv_sample_pre_kernel,
        grid=(1,),
        in_specs=[pl.BlockSpec((N_SAMPLE, W_SHIFT), lambda i: (ROW_SAMPLE // N_SAMPLE, 0)),
                  full(N_SAMPLE, W_SHIFT),
                  row(W_SHIFT), row(RW_WIDTH), full(LANES, 2 * RW_WIDTH), row(RW_WIDTH), full(LANES, RW_WIDTH),
                  row(RW_WIDTH), row(RW_WIDTH)],
        out_specs=[full(N_SAMPLE, RW_WIDTH)] * 4 + [full(RW_WIDTH, N_SAMPLE)] * 6,
        out_shape=[vec] * 4 + [vec_t] * 6,
        compiler_params=_cparams(),
        name="rwkv_sample_pre",
    )(zr, prev, p['mu'], p['w0'], p['w2a'], p['a0'], p['g2'], p['k_k'], p['k_a'])


def _rwkv_sample_step_kernel(s_ref, r_ref, w_ref, kh_ref, v_ref, nkk_ref, b_ref, so_ref, y_ref):
    s = s_ref[0]
    sa = jnp.sum(s * nkk_ref[0][None], axis=1, keepdims=True)
    s_new = s * w_ref[0][None] + sa * b_ref[0][None] + v_ref[0] * kh_ref[0][None]
    so_ref[0] = s_new
    y_ref[0] = jnp.sum(s_new * r_ref[0][None], axis=1, keepdims=True)


def _rwkv_sample_step(state_t, r_t, w_t, kh_t, v_t, nkk_t, b_t):
    kvec = lambda t: t.reshape(RW_HEADS, RW_HEAD, N_SAMPLE)
    kspec = pl.BlockSpec((1, RW_HEAD, N_SAMPLE), lambda i: (i, 0, 0))
    vspec = pl.BlockSpec((1, RW_HEAD, 1, N_SAMPLE), lambda i: (i, 0, 0, 0))
    sspec = pl.BlockSpec((1, RW_HEAD, RW_HEAD, N_SAMPLE), lambda i: (i, 0, 0, 0))
    s_new, y = pl.pallas_call(
        _rwkv_sample_step_kernel,
        grid=(RW_HEADS,),
        in_specs=[sspec, kspec, kspec, kspec, vspec, kspec, kspec],
        out_specs=[sspec, vspec],
        out_shape=[jax.ShapeDtypeStruct(state_t.shape, F32),
                   jax.ShapeDtypeStruct((RW_HEADS, RW_HEAD, 1, N_SAMPLE), F32)],
        compiler_params=_cparams(),
        name="rwkv_sample_step",
    )(state_t, kvec(r_t), kvec(w_t), kvec(kh_t), v_t.reshape(RW_HEADS, RW_HEAD, 1, N_SAMPLE), kvec(nkk_t), kvec(b_t))
    return s_new, y.reshape(RW_WIDTH, N_SAMPLE)


def _rwkv_sample_post_kernel(yt_ref, r_ref, kh_ref, v_ref, g_ref, rk_ref, lw_ref, lb_ref, o_ref):
    o = _rwkv_post(yt_ref[...].T, r_ref[...], kh_ref[...], v_ref[...], g_ref[...], rk_ref[...], lw_ref[...],
                   lb_ref[...], _head_ones())
    o_ref[0:N_SAMPLE, :] = o.astype(o_ref.dtype)
    o_ref[N_SAMPLE:T_EXT, :] = jnp.zeros((T_EXT - N_SAMPLE, RW_WIDTH), o_ref.dtype)


def _rwkv_sample_post(y_t, r, kh, v, g, p):
    row = lambda n: pl.BlockSpec((1, n), lambda i: (0, 0))
    full = lambda a, b: pl.BlockSpec((a, b), lambda i: (0, 0))
    return pl.pallas_call(
        _rwkv_sample_post_kernel,
        grid=(1,),
        in_specs=[full(RW_WIDTH, N_SAMPLE)] + [full(N_SAMPLE, RW_WIDTH)] * 4 + [row(RW_WIDTH)] * 3,
        out_specs=full(T_EXT, RW_WIDTH),
        out_shape=jax.ShapeDtypeStruct((T_EXT, RW_WIDTH), BF16),
        compiler_params=_cparams(),
        name="rwkv_sample_post",
    )(y_t, r, kh, v, g, p['r_k'], p['lnx_w'], p['lnx_b'])


def _gla_logg(zg, gk_up, gk_b):
    gkd = zg[:, 2 * GLA_KDIM + GLA_WIDTH:2 * GLA_KDIM + GLA_WIDTH + LANES]
    x = _bdot(gkd, gk_up) + gk_b
    return (jnp.minimum(x, 0.0) - jnp.log(1.0 + jnp.exp(-jnp.abs(x)))) * (1.0 / GLA_NORMALIZER)


def _gla_post(o, og_act, norm_w):
    outs = []
    for h in range(GLA_HEADS):
        oh = o[:, h * GLA_DV:(h + 1) * GLA_DV]
        ms = jnp.mean(oh * oh, axis=-1, keepdims=True)
        outs.append(oh * lax.rsqrt(ms + RMS_EPS) * norm_w)
    return jnp.concatenate(outs, axis=1) * og_act


def _gla_sample_pre_kernel(zg_ref, gkup_ref, gkb_ref, q_ref, k_ref, eg_ref):
    zg = zg_ref[...]
    logg = _gla_logg(zg, gkup_ref[...], gkb_ref[...])
    q_ref[...] = zg[:, 0:GLA_KDIM] * (GLA_DK ** -0.5)
    k_ref[...] = zg[:, GLA_KDIM:2 * GLA_KDIM]
    eg_ref[...] = jnp.exp(logg)


def _gla_sample_pre(zg, p):
    full = lambda a, b: pl.BlockSpec((a, b), lambda i: (0, 0))
    vec = jax.ShapeDtypeStruct((N_SAMPLE, GLA_KDIM), F32)
    return pl.pallas_call(
        _gla_sample_pre_kernel,
        grid=(1,),
        in_specs=[pl.BlockSpec((N_SAMPLE, W_GLA_PAD), lambda i: (ROW_SAMPLE // N_SAMPLE, 0)),
                  full(LANES, GLA_KDIM), full(1, GLA_KDIM)],
        out_specs=[full(N_SAMPLE, GLA_KDIM)] * 3,
        out_shape=[vec] * 3,
        compiler_params=_cparams(),
        name="gla_sample_pre",
    )(zg, p['gk_up'], p['gk_b'])


GLA_STEP_BATCH = 8


def _gla_sample_step_kernel(s_ref, q_ref, k_ref, eg_ref, v_ref, so_ref, o_ref):
    eye = _iota((GLA_DK, GLA_DK), 0) == _iota((GLA_DK, GLA_DK), 1)
    for b in range(GLA_STEP_BATCH):
        s = s_ref[b]
        col = lambda ref: jnp.sum(jnp.where(eye, ref[b], 0.0), axis=-1, keepdims=True)
        s_new = col(eg_ref) * s + col(k_ref) * v_ref[b]
        so_ref[b] = s_new
        o_ref[b] = jnp.sum(col(q_ref) * s_new, axis=-2, keepdims=True)


def _gla_sample_step(state, q, k, eg, v):
    kv = lambda t: t.reshape(N_SAMPLE, GLA_HEADS, 1, GLA_DK)
    kspec = pl.BlockSpec((GLA_STEP_BATCH, GLA_HEADS, 1, GLA_DK), lambda i: (i, 0, 0, 0))
    vspec = pl.BlockSpec((GLA_STEP_BATCH, GLA_HEADS, 1, GLA_DV), lambda i: (i, 0, 0, 0))
    sspec = pl.BlockSpec((GLA_STEP_BATCH, GLA_HEADS, GLA_DK, GLA_DV), lambda i: (i, 0, 0, 0))
    s_new, o = pl.pallas_call(
        _gla_sample_step_kernel,
        grid=(N_SAMPLE // GLA_STEP_BATCH,),
        in_specs=[sspec, kspec, kspec, kspec, vspec],
        out_specs=[sspec, vspec],
        out_shape=[jax.ShapeDtypeStruct(state.shape, F32),
                   jax.ShapeDtypeStruct((N_SAMPLE, GLA_HEADS, 1, GLA_DV), F32)],
        compiler_params=_cparams(),
        name="gla_sample_step",
    )(state, kv(q), kv(k), kv(eg), v.reshape(N_SAMPLE, GLA_HEADS, 1, GLA_DV))
    return s_new, o.reshape(N_SAMPLE, GLA_WIDTH)


def _gla_sample_post_kernel(o_ref, og_ref, nw_ref, out_ref):
    o = _gla_post(o_ref[...], og_ref[0:N_SAMPLE, :].astype(F32), nw_ref[...])
    out_ref[0:N_SAMPLE, :] = o.astype(out_ref.dtype)
    out_ref[N_SAMPLE:T_EXT, :] = jnp.zeros((T_EXT - N_SAMPLE, GLA_WIDTH), out_ref.dtype)


def _gla_sample_post(o, zb, p, og_block=OG_BLOCK):
    full = lambda a, b: pl.BlockSpec((a, b), lambda i: (0, 0))
    return pl.pallas_call(
        _gla_sample_post_kernel,
        grid=(1,),
        in_specs=[full(N_SAMPLE, GLA_WIDTH),
                  pl.BlockSpec((T_EXT, GLA_WIDTH), lambda i: (T_MAIN // T_EXT, og_block)),
                  full(1, GLA_DV)],
        out_specs=full(T_EXT, GLA_WIDTH),
        out_shape=jax.ShapeDtypeStruct((T_EXT, GLA_WIDTH), BF16),
        compiler_params=_cparams(),
        name="gla_sample_post",
    )(o, zb, p['norm_w'])


def _mix_route_kernel(or_ref, ore_ref, og_ref, oge_ref, gates_ref, xp_ref, xe_ref, pr_ref, pg_ref, wo_ref, n2_ref,
                      wrh_ref, wrl_ref, br_ref, h1_ref, xn_ref, info_ref, cnt_ref, carry_sc):
    i = pl.program_id(0)
    is_main = i < T_MAIN // TM_DENSE

    @pl.when(i == 0)
    def _():
        carry_sc[...] = jnp.zeros_like(carry_sc)

    sig_r = gates_ref[:, 0:D_MODEL].astype(F32)
    sig_g = gates_ref[:, D_MODEL:2 * D_MODEL].astype(F32)
    o_r = jnp.where(is_main, or_ref[...], ore_ref[...])
    o_g = jnp.where(is_main, og_ref[...], oge_ref[...])
    m = (sig_r * jnp.dot(o_r, pr_ref[...], preferred_element_type=F32)
         + sig_g * jnp.dot(o_g, pg_ref[...], preferred_element_type=F32))
    h = jnp.where(is_main, xp_ref[...], xe_ref[...])
    h1 = h + jnp.dot(m.astype(BF16), wo_ref[...], preferred_element_type=F32)
    h1_ref[...] = h1
    ms = jnp.mean(h1 * h1, axis=-1, keepdims=True)
    xn = h1 * lax.rsqrt(ms + RMS_EPS) * n2_ref[...]
    xn_ref[...] = xn
    xh = xn.astype(BF16)
    xl = (xn - xh.astype(F32)).astype(BF16)
    lg = (jnp.dot(xh, wrh_ref[...], preferred_element_type=F32)
          + jnp.dot(xh, wrl_ref[...], preferred_element_type=F32)
          + jnp.dot(xl, wrh_ref[...], preferred_element_type=F32)) + br_ref[...]

    lane = _iota(lg.shape, 1)
    lanef = lane.astype(F32)
    is_g = jnp.logical_and(lane >= N_EXPERTS, lane < N_EXPERTS + N_GROUPS)
    gl = jnp.where(is_g, lg, NEG_BIG)
    gmax = jnp.max(gl, axis=-1, keepdims=True)
    gsel = jnp.min(jnp.where(jnp.logical_and(is_g, gl == gmax), lanef, 1e9), axis=-1, keepdims=True) - N_EXPERTS
    p_g = 1.0 / jnp.sum(jnp.exp(gl - gmax), axis=-1, keepdims=True)
    grp = (lane // EXPERTS_PER_GROUP).astype(F32)
    in_grp = jnp.logical_and(lane < N_EXPERTS, grp == gsel)
    el = jnp.where(in_grp, lg, NEG_BIG)
    m1 = jnp.max(el, axis=-1, keepdims=True)
    i1 = jnp.min(jnp.where(jnp.logical_and(in_grp, el == m1), lanef, 1e9), axis=-1, keepdims=True)
    in2 = jnp.logical_and(in_grp, lanef != i1)
    el2 = jnp.where(in2, lg, NEG_BIG)
    m2 = jnp.max(el2, axis=-1, keepdims=True)
    i2 = jnp.min(jnp.where(jnp.logical_and(in2, el2 == m2), lanef, 1e9), axis=-1, keepdims=True)
    e2 = jnp.exp(m2 - m1)
    w1 = p_g / (1.0 + e2)
    w2 = p_g * e2 / (1.0 + e2)

    oh1 = lanef == i1
    oh2 = lanef == i2
    cnt = jnp.where(jnp.logical_or(oh1, oh2), 1.0, 0.0)
    tm = cnt.shape[0]
    lstrict = (_iota((tm, tm), 0) > _iota((tm, tm), 1)).astype(BF16)
    before = jnp.dot(lstrict, cnt.astype(BF16), preferred_element_type=F32) + carry_sc[...]
    rank1 = jnp.sum(jnp.where(oh1, before, 0.0), axis=-1, keepdims=True)
    rank2 = jnp.sum(jnp.where(oh2, before, 0.0), axis=-1, keepdims=True)
    carry_sc[...] = carry_sc[...] + jnp.sum(cnt, axis=0, keepdims=True)
    cnt_ref[...] = carry_sc[...]
    info = jnp.where(lane == 0, i1, jnp.where(lane == 1, i2, jnp.where(lane == 2, w1, jnp.where(
        lane == 3, w2, jnp.where(lane == 4, rank1, jnp.where(lane == 5, rank2, 0.0))))))
    info_ref[...] = info


def _mix_route(o_r, o_r_ext, o_g, o_g_ext, zb, xp, xe, p):
    nmain = T_MAIN // TM_DENSE
    tile = lambda n: pl.BlockSpec((TM_DENSE, n), lambda i: (i, 0))
    main = lambda n: pl.BlockSpec((TM_DENSE, n), lambda i: (jnp.minimum(i, nmain - 1), 0))
    ext = lambda n: pl.BlockSpec((T_EXT, n), lambda i: (0, 0))
    const = lambda a, b: pl.BlockSpec((a, b), lambda i: (0, 0), pipeline_mode=pl.Buffered(1))
    return pl.pallas_call(
        _mix_route_kernel,
        grid=(T_ALL // TM_DENSE,),
        in_specs=[main(RW_WIDTH), ext(RW_WIDTH), main(GLA_WIDTH), ext(GLA_WIDTH), tile(2 * D_MODEL),
                  main(D_MODEL), ext(D_MODEL),
                  const(RW_WIDTH, D_MODEL), const(GLA_WIDTH, D_MODEL), const(D_MODEL, D_MODEL),
                  const(1, D_MODEL), const(D_MODEL, LANES), const(D_MODEL, LANES), const(1, LANES)],
        out_specs=[tile(D_MODEL), tile(D_MODEL), tile(LANES), pl.BlockSpec((1, LANES), lambda i: (0, 0))],
        out_shape=[jax.ShapeDtypeStruct((T_ALL, D_MODEL), F32),
                   jax.ShapeDtypeStruct((T_ALL, D_MODEL), F32),
                   jax.ShapeDtypeStruct((T_ALL, LANES), F32),
                   jax.ShapeDtypeStruct((1, LANES), F32)],
        scratch_shapes=[pltpu.VMEM((1, LANES), F32)],
        compiler_params=_cparams(),
        name="mix_route",
    )(o_r, o_r_ext, o_g, o_g_ext, zb, xp, xe, p['p_rwkv'], p['p_gla'], p['w_out'], p['norm2_w'], p['wr_hi'],
      p['wr_lo'], p['b_route'])


def _dispatch_kernel(info_ref, cnt_ref, xq_ref, xs_hbm, d_ref, te_ref, nt_ref,
                     d_vm, d_sm, tab_vm, tab_sm, zbuf, ring, sem_t, sem_x, sem_e, sem_u):
    i = pl.program_id(0)

    def zero_tile(t, sem):
        return pltpu.make_async_copy(zbuf, xs_hbm.at[pl.ds(t * TM_MOE, TM_MOE), :], sem)

    def each(fn):
        def body(r, carry):
            fn(r)
            return carry
        return body

    lane1 = _iota((1, LANES), 1)
    cnt = jnp.where(lane1 < N_EXPERTS, cnt_ref[...], 0.0)
    tiles = jnp.floor((cnt + (TM_MOE - 1)) * (1.0 / TM_MOE))
    upper = (_iota((LANES, LANES), 0) < _iota((LANES, LANES), 1)).astype(BF16)
    tile_start = jnp.dot(jnp.broadcast_to(tiles, (8, LANES)).astype(BF16), upper,
                         preferred_element_type=F32)[0:1, :]
    base = tile_start * TM_MOE

    info = info_ref[...]
    lanef = _iota(info.shape, 1).astype(F32)
    pick = lambda col: jnp.sum(jnp.where(lanef == info[:, col:col + 1], base, 0.0), axis=-1, keepdims=True)
    d1 = pick(0) + info[:, 4:5]
    d2 = pick(1) + info[:, 5:6]
    tm = info.shape[0]
    eye = _iota((tm, tm), 0) == _iota((tm, tm), 1)
    to_row = lambda col: jnp.sum(jnp.where(eye, col, 0.0), axis=0, keepdims=True)
    d_row = jnp.concatenate([to_row(d1), to_row(d2)], axis=1).astype(jnp.int32)
    d_ref[0] = d_row
    d_vm[...] = d_row

    @pl.when(i == 0)
    def _():
        tile_end = tile_start + tiles
        n_tiles = jnp.sum(tiles, axis=-1, keepdims=True)
        eye_l = _iota((LANES, LANES), 0) == _iota((LANES, LANES), 1)
        end_col = jnp.sum(jnp.where(eye_l, tile_end, 0.0), axis=-1, keepdims=True)
        rowl = _iota((LANES, LANES), 0)
        tile_f = _iota((LANES, LANES), 1).astype(F32)
        te = jnp.sum(jnp.where(jnp.logical_and(rowl < N_EXPERTS, end_col <= tile_f), 1.0, 0.0), axis=0,
                     keepdims=True)
        last_e = jnp.max(jnp.where(tiles > 0.0, lane1.astype(F32), 0.0), axis=-1, keepdims=True)
        te = jnp.where(lane1.astype(F32) < n_tiles, jnp.minimum(te, N_EXPERTS - 1.0), last_e)
        te_ref[...] = te.astype(jnp.int32)
        nt_ref[...] = jnp.broadcast_to(n_tiles, (1, LANES)).astype(jnp.int32)
        zbuf[...] = jnp.zeros_like(zbuf)
        last_tile = (tile_end - 1.0).astype(jnp.int32)
        nt_row = jnp.broadcast_to(n_tiles, (1, LANES)).astype(jnp.int32)
        tab_vm[...] = jnp.concatenate([last_tile, tiles.astype(jnp.int32), nt_row, jnp.zeros((5, LANES), jnp.int32)],
                                      axis=0)
        tab_copy = pltpu.make_async_copy(tab_vm, tab_sm, sem_t)
        tab_copy.start()
        tab_copy.wait()

        for e in range(N_EXPERTS):
            @pl.when(tab_sm[1, e] > 0)
            def _(e=e):
                zero_tile(tab_sm[0, e], sem_e).start()
        lax.fori_loop(tab_sm[2, 0], NT_MOE, each(lambda t: zero_tile(t, sem_u).start()), 0)
        for e in range(N_EXPERTS):
            @pl.when(tab_sm[1, e] > 0)
            def _(e=e):
                zero_tile(tab_sm[0, e], sem_e).wait()

    d_copy = pltpu.make_async_copy(d_vm, d_sm, sem_t)
    d_copy.start()
    d_copy.wait()

    slot = lax.rem(i, 2)
    ring[slot] = xq_ref[...]
    for t in range(tm):
        for half in range(2):
            pltpu.make_async_copy(ring.at[slot, pl.ds(t, 1), :], xs_hbm.at[pl.ds(d_sm[0, half * tm + t], 1), :],
                                  sem_x.at[slot]).start(priority=half)

    def wait_rows(s):
        for _ in range(2):
            pltpu.make_async_copy(ring.at[s], xs_hbm.at[pl.ds(0, tm), :], sem_x.at[s]).wait()

    @pl.when(i >= 1)
    def _():
        wait_rows(1 - slot)

    @pl.when(i == pl.num_programs(0) - 1)
    def _():
        wait_rows(slot)
        lax.fori_loop(tab_sm[2, 0], NT_MOE, each(lambda t: zero_tile(t, sem_u).wait()), 0)


def _dispatch(info, counts, xq):
    nt = T_ALL // TM_DENSE
    return pl.pallas_call(
        _dispatch_kernel,
        grid=(nt,),
        in_specs=[pl.BlockSpec((TM_DENSE, LANES), lambda i: (i, 0)),
                  pl.BlockSpec((1, LANES), lambda i: (0, 0)),
                  pl.BlockSpec((TM_DENSE, D_MODEL), lambda i: (i, 0))],
        out_specs=[pl.BlockSpec(memory_space=pl.ANY),
                   pl.BlockSpec((1, 1, 2 * TM_DENSE), lambda i: (i, 0, 0)),
                   pl.BlockSpec((1, LANES), lambda i: (0, 0)),
                   pl.BlockSpec((1, LANES), lambda i: (0, 0))],
        out_shape=[jax.ShapeDtypeStruct((NT_MOE * TM_MOE, D_MODEL), F32),
                   jax.ShapeDtypeStruct((nt, 1, 2 * TM_DENSE), jnp.int32),
                   jax.ShapeDtypeStruct((1, LANES), jnp.int32),
                   jax.ShapeDtypeStruct((1, LANES), jnp.int32)],
        scratch_shapes=[pltpu.VMEM((1, 2 * TM_DENSE), jnp.int32),
                        pltpu.SMEM((1, 2 * TM_DENSE), jnp.int32),
                        pltpu.VMEM((8, LANES), jnp.int32),
                        pltpu.SMEM((8, LANES), jnp.int32),
                        pltpu.VMEM((TM_MOE, D_MODEL), F32),
                        pltpu.VMEM((2, TM_DENSE, D_MODEL), F32),
                        pltpu.SemaphoreType.DMA(()),
                        pltpu.SemaphoreType.DMA((2,)),
                        pltpu.SemaphoreType.DMA(()),
                        pltpu.SemaphoreType.DMA(())],
        compiler_params=_cparams(),
        name="moe_dispatch",
    )(info, counts, xq)


def _moe_kernel(te_ref, nt_ref, xs_ref, w1_hbm, w3_hbm, w2_hbm, o_ref, w1r, w3r, w2r, w1b, w3b, w2b, sem):
    i = pl.program_id(0)
    n_tiles = nt_ref[0]

    def weight_copies(e):
        return [pltpu.make_async_copy(w1_hbm.at[e], w1r, sem), pltpu.make_async_copy(w3_hbm.at[e], w3r, sem),
                pltpu.make_async_copy(w2_hbm.at[e], w2r, sem)]

    @pl.when(i < n_tiles)
    def _():
        e = te_ref[i]

        @pl.when(i == 0)
        def _():
            for c in weight_copies(e):
                c.start()

        @pl.when(jnp.logical_or(i == 0, e != te_ref[jnp.maximum(i - 1, 0)]))
        def _():
            for c in weight_copies(e):
                c.wait()
            w1b[...] = w1r[...].astype(BF16)
            w3b[...] = w3r[...].astype(BF16)
            w2b[...] = w2r[...].astype(BF16)
            nxt = lax.while_loop(
                lambda j: jnp.logical_and(j < n_tiles, te_ref[jnp.minimum(j, NT_MOE - 1)] == e),
                lambda j: j + 1, i + 1)

            @pl.when(nxt < n_tiles)
            def _():
                for c in weight_copies(te_ref[jnp.minimum(nxt, NT_MOE - 1)]):
                    c.start()

        xb = xs_ref[...].astype(BF16)
        h1 = jnp.dot(xb, w1b[...], preferred_element_type=F32)
        h3 = jnp.dot(xb, w3b[...], preferred_element_type=F32)
        hh = (h1 * jax.nn.sigmoid(h1) * h3).astype(BF16)
        o_ref[...] = jnp.dot(hh, w2b[...], preferred_element_type=F32)

    @pl.when(i >= nt_ref[0])
    def _():
        o_ref[...] = jnp.zeros_like(o_ref)


def _moe(tile_expert, n_tiles, xs, w1, w3, w2):
    grid_spec = pltpu.PrefetchScalarGridSpec(
        num_scalar_prefetch=2,
        grid=(NT_MOE,),
        in_specs=[pl.BlockSpec((TM_MOE, D_MODEL), lambda i, te, nt: (jnp.minimum(i, nt[0] - 1), 0)),
                  pl.BlockSpec(memory_space=pl.ANY),
                  pl.BlockSpec(memory_space=pl.ANY),
                  pl.BlockSpec(memory_space=pl.ANY)],
        out_specs=pl.BlockSpec((TM_MOE, D_MODEL), lambda i, te, nt: (i, 0)),
        scratch_shapes=[pltpu.VMEM((D_MODEL, D_EXPERT), F32),
                        pltpu.VMEM((D_MODEL, D_EXPERT), F32),
                        pltpu.VMEM((D_EXPERT, D_MODEL), F32),
                        pltpu.VMEM((D_MODEL, D_EXPERT), BF16),
                        pltpu.VMEM((D_MODEL, D_EXPERT), BF16),
                        pltpu.VMEM((D_EXPERT, D_MODEL), BF16),
                        pltpu.SemaphoreType.DMA(())],
    )
    return pl.pallas_call(
        _moe_kernel,
        grid_spec=grid_spec,
        out_shape=jax.ShapeDtypeStruct((NT_MOE * TM_MOE, D_MODEL), F32),
        compiler_params=_cparams(),
        name="moe_experts",
    )(tile_expert, n_tiles, xs, w1, w3, w2)


def _combine_kernel(dcur_ref, dnext_ref, h1_ref, info_ref, fw_ref, eo_hbm, yp_ref, ye_ref, buf, sem):
    i = pl.program_id(0)
    nmain = T_MAIN // TM_DENSE
    slot = lax.rem(i, 2)

    def row_copy(d_ref, s, t):
        return pltpu.make_async_copy(eo_hbm.at[pl.ds(d_ref[0, 0, t], 1), :], buf.at[s, pl.ds(t, 1), :], sem.at[s])

    def issue_all(d_ref, s):
        for t in range(2 * TM_DENSE):
            row_copy(d_ref, s, t).start(priority=t % 2)

    @pl.when(i == 0)
    def _():
        issue_all(dcur_ref, slot)

    @pl.when(i + 1 < pl.num_programs(0))
    def _():
        issue_all(dnext_ref, 1 - slot)

    pltpu.make_async_copy(eo_hbm.at[pl.ds(0, 2 * TM_DENSE), :], buf.at[slot], sem.at[slot]).wait()

    info = info_ref[...]
    y = h1_ref[...] + info[:, 2:3] * buf[slot, 0:TM_DENSE, :] + info[:, 3:4] * buf[slot, TM_DENSE:2 * TM_DENSE, :]
    ms = jnp.mean(y * y, axis=-1, keepdims=True)
    out = y * lax.rsqrt(ms + RMS_EPS) * fw_ref[...]

    @pl.when(i < nmain)
    def _():
        yp_ref[...] = out

    @pl.when(i == nmain)
    def _():
        ye_ref[...] = out


def _combine(dst, h1, info, fw, eo):
    nmain = T_MAIN // TM_DENSE
    return pl.pallas_call(
        _combine_kernel,
        grid=(T_ALL // TM_DENSE,),
        in_specs=[pl.BlockSpec((1, 1, 2 * TM_DENSE), lambda i: (i, 0, 0), memory_space=pltpu.SMEM),
                  pl.BlockSpec((1, 1, 2 * TM_DENSE), lambda i: (jnp.minimum(i + 1, nmain), 0, 0),
                               memory_space=pltpu.SMEM),
                  pl.BlockSpec((TM_DENSE, D_MODEL), lambda i: (i, 0)),
                  pl.BlockSpec((TM_DENSE, LANES), lambda i: (i, 0)),
                  pl.BlockSpec((1, D_MODEL), lambda i: (0, 0)),
                  pl.BlockSpec(memory_space=pl.ANY)],
        out_specs=[pl.BlockSpec((TM_DENSE, D_MODEL), lambda i: (jnp.minimum(i, nmain - 1), 0)),
                   pl.BlockSpec((T_EXT, D_MODEL), lambda i: (0, 0))],
        out_shape=[jax.ShapeDtypeStruct((T_MAIN, D_MODEL), F32),
                   jax.ShapeDtypeStruct((T_EXT, D_MODEL), F32)],
        scratch_shapes=[pltpu.VMEM((2, 2 * TM_DENSE, D_MODEL), F32), pltpu.SemaphoreType.DMA((2,))],
        compiler_params=_cparams(),
        name="moe_combine",
    )(dst, dst, h1, info, fw, eo)


def kernel(x_prompt, x_sample, state_rwkv, state_shift, state_gla, meta_tokens, norm1_w, w_in, mu_shift, rw_w0,
           rw_w2, rw_a0, rw_a2, rw_g2, rw_k_k, rw_k_a, rw_r_k, rw_lnx_w, rw_lnx_b, gla_gk_up, gla_gk_b, gla_norm_w,
           p_rwkv, p_gla, w_out, norm2_w, moe_w_group, moe_b_group, moe_w_router, moe_b_router, moe_w1, moe_w3,
           moe_w2, final_norm_w):
    w_t = w_in[0].T
    row = lambda t: t.reshape(1, -1)
    w2a = jnp.zeros((LANES, 2 * RW_WIDTH), F32)
    w2a = w2a.at[0:64, 0:RW_WIDTH].set(rw_w2[0]).at[64:128, RW_WIDTH:].set(rw_a2[0])
    rw = dict(mu=row(mu_shift[0][_SHIFT_PERM]), w0=row(rw_w0[0]), w2a=w2a, a0=row(rw_a0[0]), g2=rw_g2[0],
              k_k=row(rw_k_k[0]), k_a=row(rw_k_a[0]), r_k=row(rw_r_k[0]), lnx_w=row(rw_lnx_w[0]),
              lnx_b=row(rw_lnx_b[0]))
    gk_up = jnp.zeros((LANES, GLA_KDIM), F32).at[0:GLA_LORA].set(gla_gk_up[0])
    gl = dict(gk_up=gk_up, gk_b=row(gla_gk_b[0]), norm_w=row(gla_norm_w[0]))
    w_route = jnp.zeros((D_MODEL, LANES), F32)
    w_route = w_route.at[:, 0:N_EXPERTS].set(moe_w_router[0]).at[:, N_EXPERTS:N_EXPERTS + N_GROUPS].set(moe_w_group[0])
    wr_hi = w_route.astype(BF16)
    wr_lo = (w_route - wr_hi.astype(F32)).astype(BF16)
    b_route = jnp.zeros((1, LANES), F32)
    b_route = b_route.at[0, 0:N_EXPERTS].set(moe_b_router[0]).at[0, N_EXPERTS:N_EXPERTS + N_GROUPS].set(moe_b_group[0])
    mix = dict(p_rwkv=p_rwkv[0].astype(BF16), p_gla=p_gla[0].astype(BF16), w_out=w_out[0].astype(BF16),
               norm2_w=row(norm2_w[0]), wr_hi=wr_hi, wr_lo=wr_lo, b_route=b_route)

    xp = x_prompt.reshape(T_MAIN, D_MODEL)
    xe = jnp.concatenate([x_sample[:, 0, :], jnp.zeros((META_PAD, D_MODEL), F32), meta_tokens,
                          jnp.zeros((T_ALL - ROW_META - CHUNK, D_MODEL), F32)], axis=0)

    xn = _norm1(xp, xe, row(norm1_w[0]))
    zr = _proj(xn, w_t, _RWKV_TILES, W_SHIFT // 2, F32, name="proj_rwkv")
    zg = _proj(xn, w_t, _GLA_TILES, W_GLA_PAD // 2, F32, name="proj_gla")
    zb = _proj(xn, w_t, _GATE_TILES, GLA_WIDTH, BF16, silu_from=2 * D_MODEL // GLA_WIDTH, name="proj_gates")

    o_r, h_fin, o_g, s_fin = _prompt_recurrences(zr, zg, zb, rw, gl)

    r_s, kh_s, v_s, g_s, r_t, w_t, kh_t, v_t, nkk_t, b_t = _rwkv_sample_pre(zr, state_shift[0][:, _SHIFT_PERM], rw)
    rw_new_t, y_t = _rwkv_sample_step(state_rwkv[0].transpose(1, 2, 3, 0), r_t, w_t, kh_t, v_t, nkk_t, b_t)
    rw_new = rw_new_t.transpose(3, 0, 1, 2)
    o_r_ext = _rwkv_sample_post(y_t, r_s, kh_s, v_s, g_s, rw)
    q_s, k_s, eg_s = _gla_sample_pre(zg, gl)
    v_gs = zg[ROW_SAMPLE:ROW_SAMPLE + N_SAMPLE, 2 * GLA_KDIM:2 * GLA_KDIM + GLA_WIDTH]
    gla_new, og_s = _gla_sample_step(state_gla[0], q_s, k_s, eg_s, v_gs)
    o_g_ext = _gla_sample_post(og_s, zb, gl)

    h1, xn2, info, counts = _mix_route(o_r, o_r_ext, o_g, o_g_ext, zb, xp, xe, mix)
    xs, dst, tile_expert, n_tiles = _dispatch(info, counts, xn2)
    eo = _moe(tile_expert[0, :NT_MOE], n_tiles[0, :1], xs, moe_w1[0], moe_w3[0], moe_w2[0])
    y_p, y_e = _combine(dst, h1, info, row(final_norm_w), eo)

    y_prompt = y_p.reshape(N_BATCH, SEQ, D_MODEL)
    y_sample = y_e[0:N_SAMPLE].reshape(N_SAMPLE, 1, D_MODEL)
    new_rwkv_prompt = h_fin.reshape(N_BATCH, RW_HEAD, RW_HEADS, RW_HEAD).transpose(0, 2, 3, 1)[None]
    shift_rows = jnp.concatenate([zr[(b + 1) * SEQ - 1:(b + 1) * SEQ] for b in range(N_BATCH)]
                                 + [zr[ROW_SAMPLE:ROW_SAMPLE + N_SAMPLE]], axis=0)
    shift_rows = shift_rows[:, _SHIFT_INV]
    return (y_prompt, y_sample, new_rwkv_prompt, shift_rows[None, 0:N_BATCH], s_fin[None],
            rw_new[None], shift_rows[None, N_BATCH:], gla_new[None])
```

```python
import functools
import math

import numpy as np
import jax
import jax.numpy as jnp
from jax import lax
from jax.experimental import pallas as pl
from jax.experimental.pallas import tpu as pltpu

F32 = jnp.float32
BF16 = jnp.bfloat16

D_MODEL = 2048
N_BATCH = 4
SEQ = 2048
N_SAMPLE = 128
N_META = 16
RMS_EPS = 1e-6

RW_WIDTH = 1024
RW_HEAD = 64
RW_HEADS = 16
RW_GN_EPS = RW_HEAD * 1e-5
W_SHIFT = 3328
GLA_HEADS = 4
GLA_DK = 128
GLA_DV = 256
GLA_KDIM = 512
GLA_WIDTH = 1024
GLA_LORA = 16
GLA_NORMALIZER = 16.0
W_GLA_PAD = 2304
OG_BLOCK = 2 * D_MODEL // GLA_WIDTH
N_GROUPS = 4
EXPERTS_PER_GROUP = 8
N_EXPERTS = 32
D_EXPERT = 512

CHUNK = 64
N_CHUNKS = SEQ // CHUNK
META_PAD = CHUNK - N_META
T_MAIN = N_BATCH * SEQ
ROW_SAMPLE = T_MAIN
ROW_META = T_MAIN + N_SAMPLE
T_EXT = 256
T_ALL = T_MAIN + T_EXT
LANES = 128

TM_DENSE = 256
TM_MM = T_ALL // 8
TM_MOE = 256
NT_MOE = (2 * T_ALL) // TM_MOE + N_EXPERTS
NEG_BIG = -1e30

VMEM_LIMIT = 56 * 1024 * 1024

_SHIFT_PERM = np.concatenate([np.arange(0, 1024), np.arange(1088, 2112), np.arange(2112, 3136),
                              np.arange(1024, 1088), np.arange(3136, 3328)])
_SHIFT_INV = np.argsort(_SHIFT_PERM)


def _cparams(n_axes=1):
    return pltpu.CompilerParams(dimension_semantics=("arbitrary",) * n_axes, vmem_limit_bytes=VMEM_LIMIT)


def _bdot(a, b):
    return jnp.dot(a.astype(BF16), b.astype(BF16), preferred_element_type=F32)


def _bdot_nt(a, b):
    return lax.dot_general(a.astype(BF16), b.astype(BF16), (((1,), (1,)), ((), ())), preferred_element_type=F32)


def _bdot_tn(a, b):
    return lax.dot_general(a.astype(BF16), b.astype(BF16), (((0,), (0,)), ((), ())), preferred_element_type=F32)


def _split_dot(m_bf16, x):
    hi = x.astype(BF16)
    lo = (x - hi.astype(F32)).astype(BF16)
    return (jnp.dot(m_bf16, hi, preferred_element_type=F32) + jnp.dot(m_bf16, lo, preferred_element_type=F32))


def _iota(shape, dim):
    return lax.broadcasted_iota(jnp.int32, shape, dim)


def _tri_incl(n):
    return (_iota((n, n), 0) >= _iota((n, n), 1)).astype(BF16)


def _head_ones():
    return ((_iota((LANES, LANES), 0) // RW_HEAD) == (_iota((LANES, LANES), 1) // RW_HEAD)).astype(BF16)


def _seg_sum(x, bd):
    m, n = x.shape
    nb = n // LANES
    xs = jnp.concatenate([x[:, j * LANES:(j + 1) * LANES] for j in range(nb)], axis=0)
    s = jnp.dot(xs.astype(BF16), bd, preferred_element_type=F32)
    return jnp.concatenate([s[j * m:(j + 1) * m] for j in range(nb)], axis=1)


def _col_of_row(row):
    n = row.shape[-1]
    eye = _iota((n, n), 0) == _iota((n, n), 1)
    return jnp.sum(jnp.where(eye, jnp.broadcast_to(row, (n, n)), 0.0), axis=-1, keepdims=True)


def _norm1_kernel(xp_ref, xe_ref, w_ref, o_ref):
    i = pl.program_id(0)

    def f(x):
        ms = jnp.mean(x * x, axis=-1, keepdims=True)
        return (x * lax.rsqrt(ms + RMS_EPS) * w_ref[...]).astype(BF16)

    @pl.when(i < T_MAIN // TM_DENSE)
    def _():
        o_ref[...] = f(xp_ref[...])

    @pl.when(i == T_MAIN // TM_DENSE)
    def _():
        o_ref[...] = f(xe_ref[...])


def _norm1(xp, xe, w):
    nmain = T_MAIN // TM_DENSE
    return pl.pallas_call(
        _norm1_kernel,
        grid=(T_ALL // TM_DENSE,),
        in_specs=[pl.BlockSpec((TM_DENSE, D_MODEL), lambda i: (jnp.minimum(i, nmain - 1), 0)),
                  pl.BlockSpec((T_EXT, D_MODEL), lambda i: (0, 0)),
                  pl.BlockSpec((1, D_MODEL), lambda i: (0, 0))],
        out_specs=pl.BlockSpec((TM_DENSE, D_MODEL), lambda i: (i, 0)),
        out_shape=jax.ShapeDtypeStruct((T_ALL, D_MODEL), BF16),
        compiler_params=_cparams(),
        name="norm1",
    )(xp, xe, w)


def _proj_kernel(x_ref, wt_hbm, o_ref, wraw, wbf, sem, *, tiles, silu_from):
    j = pl.program_id(0)
    m = pl.program_id(1)
    tn = wbf.shape[0]

    def copies(jj):
        return [pltpu.make_async_copy(wt_hbm.at[pl.ds(src, n), :], wraw.at[pl.ds(dst, n), :], sem)
                for (src, n, dst) in tiles[jj]]

    @pl.when(jnp.logical_and(j == 0, m == 0))
    def _():
        for c in copies(0):
            c.start()

    for jj in range(len(tiles)):
        @pl.when(jnp.logical_and(j == jj, m == 0))
        def _(jj=jj):
            for c in copies(jj):
                c.wait()
            edge = 0
            for lo, hi in sorted((dst, dst + n) for (_, n, dst) in tiles[jj]) + [(tn, tn)]:
                if lo > edge:
                    wbf[edge:lo, :] = jnp.zeros((lo - edge, wbf.shape[1]), BF16)
                if hi > lo:
                    wbf[lo:hi, :] = wraw[lo:hi, :].astype(BF16)
                edge = hi
            if jj + 1 < len(tiles):
                for c in copies(jj + 1):
                    c.start()

    z = lax.dot_general(x_ref[...], wbf[...], (((1,), (1,)), ((), ())), preferred_element_type=F32)
    if silu_from is None:
        o_ref[...] = z.astype(o_ref.dtype)
    else:
        s = jax.nn.sigmoid(z)
        o_ref[...] = jnp.where(j >= silu_from, z * s, s).astype(o_ref.dtype)


def _proj(x, w_t, tiles, tn, out_dtype, silu_from=None, name="proj"):
    t, k = x.shape
    return pl.pallas_call(
        functools.partial(_proj_kernel, tiles=tiles, silu_from=silu_from),
        grid=(len(tiles), t // TM_MM),
        in_specs=[pl.BlockSpec((TM_MM, k), lambda j, m: (m, 0)),
                  pl.BlockSpec(memory_space=pl.ANY)],
        out_specs=pl.BlockSpec((TM_MM, tn), lambda j, m: (m, j)),
        out_shape=jax.ShapeDtypeStruct((t, tn * len(tiles)), out_dtype),
        scratch_shapes=[pltpu.VMEM((tn, k), F32), pltpu.VMEM((tn, k), BF16), pltpu.SemaphoreType.DMA(())],
        compiler_params=_cparams(2),
        name=name,
    )(x, w_t)


_RWKV_TILES = (((0, 1024, 0), (1088, 640, 1024)),
               ((1728, 1408, 0), (1024, 64, 1408), (3136, 192, 1472)))
_GLA_TILES = (((3328, 1152, 0),), ((4480, 912, 0),))
_GATE_TILES = tuple(((6416 + 1024 * j, 1024, 0),) for j in range(4)) + (((5392, 1024, 0),),)


def _rwkv_pre(z, w0, w2a, a0, g2, k_k, k_a, bd):
    r = z[:, 0:1024]
    k = z[:, 1024:2048]
    v = z[:, 2048:3072]
    wa = z[:, 3072:3200]
    gd = z[:, 3200:3328]
    lane = _iota(wa.shape, 1)
    wa = jnp.where(lane < 64, jnp.tanh(wa), wa)
    up = _bdot(wa, w2a)
    lw = -math.exp(-0.5) * jax.nn.sigmoid(w0 + up[:, :1024])
    a = jax.nn.sigmoid(a0 + up[:, 1024:])
    g = _bdot(jax.nn.sigmoid(gd), g2)
    kk = k * k_k
    kk = kk * lax.rsqrt(jnp.maximum(_seg_sum(kk * kk, bd), 1e-24))
    k_h = k * (1.0 + (a - 1.0) * k_a)
    return r, lw, k_h, v, kk, kk * a, g


def _rwkv_post(y, r, k_h, v, g, r_k, lnx_w, lnx_b, bd):
    mean = _seg_sum(y, bd) * (1.0 / RW_HEAD)
    d = y - mean
    var = _seg_sum(d * d, bd) * (1.0 / RW_HEAD)
    yn = d * lax.rsqrt(var + RW_GN_EPS) * lnx_w + lnx_b
    bonus = _seg_sum(r * k_h * r_k, bd) * v
    return (yn + bonus) * g


def _stack2(x):
    lane = _iota(x.shape, 1)
    return jnp.concatenate([jnp.where(lane < RW_HEAD, x, 0.0), jnp.where(lane >= RW_HEAD, x, 0.0)], axis=0)


def _unstack2(x):
    c = x.shape[0] // 2
    return x[:c] + x[c:]


def _prompt_chunk_kernel(zs_ref, mu_ref, w0_ref, w2a_ref, a0_ref, g2_ref, kk_ref, ka_ref, rk_ref, lw_ref, lb_ref,
                         zg_ref, og_ref, gkup_ref, gkb_ref, nw_ref,
                         o_ref, hout_ref, go_ref, sout_ref,
                         prev_sc, h_sc, hmeta_sc, prevmeta_sc, y_sc, s_sc, smeta_sc, *, n_chunks):
    i = pl.program_id(0)
    is_meta = i == 0
    c = lax.rem(jnp.maximum(i - 1, 0), n_chunks)
    first = jnp.logical_and(i >= 1, c == 0)
    last = jnp.logical_and(i >= 1, c == n_chunks - 1)

    @pl.when(is_meta)
    def _():
        h_sc[...] = jnp.zeros_like(h_sc)
        prev_sc[...] = jnp.zeros_like(prev_sc)
        s_sc[...] = jnp.zeros_like(s_sc)

    @pl.when(first)
    def _():
        h_sc[...] = hmeta_sc[...]
        prev_sc[...] = prevmeta_sc[...]
        s_sc[...] = smeta_sc[...]

    rowi = _iota((CHUNK, 1), 0)
    pad_row = jnp.logical_and(is_meta, rowi < META_PAD)
    heads = range(GLA_HEADS)
    ks = [slice(h * GLA_DK, (h + 1) * GLA_DK) for h in heads]
    gla = {}

    def gla_prep():
        zg = zg_ref[...]
        logg = jnp.where(pad_row, 0.0, _gla_logg(zg, gkup_ref[...], gkb_ref[...]))
        bcum = _split_dot(_tri_incl(CHUNK), logg)
        b_end = bcum[CHUNK - 1:CHUNK, :]
        gla['qt'] = zg[:, 0:GLA_KDIM] * (GLA_DK ** -0.5) * jnp.exp(bcum)
        gla['kt'] = zg[:, GLA_KDIM:2 * GLA_KDIM] * jnp.exp(-bcum)
        gla['ke'] = zg[:, GLA_KDIM:2 * GLA_KDIM] * jnp.exp(b_end - bcum)
        gla['e_end'] = jnp.exp(b_end)
        gla['vh'] = [zg[:, 2 * GLA_KDIM + h * GLA_DV:2 * GLA_KDIM + (h + 1) * GLA_DV].astype(BF16) for h in heads]

    def gla_scores():
        causal = _iota((CHUNK, CHUNK), 0) >= _iota((CHUNK, CHUNK), 1)
        gla['a'] = [jnp.where(causal, _bdot_nt(gla['qt'][:, ks[h]], gla['kt'][:, ks[h]]), 0.0) for h in heads]
        gla['s0'] = [s_sc[h] for h in heads]

    def gla_state():
        gla['o_inter'] = [_bdot(gla['qt'][:, ks[h]], gla['s0'][h]) for h in heads]
        gla['s_add'] = [_bdot_tn(gla['ke'][:, ks[h]], gla['vh'][h]) for h in heads]

    def gla_finish():
        outs = [_bdot(gla['a'][h], gla['vh'][h]) + gla['o_inter'][h] for h in heads]
        gla['s_new'] = [_col_of_row(gla['e_end'][:, ks[h]]) * gla['s0'][h] + gla['s_add'][h] for h in heads]
        for h in heads:
            s_sc[h] = gla['s_new'][h]
        go = _gla_post(jnp.concatenate(outs, axis=1), og_ref[...].astype(F32), nw_ref[...])
        go_ref[...] = go.astype(go_ref.dtype)

    zs = zs_ref[...]
    sh = pltpu.roll(zs, 1, 0)
    sh = jnp.where(rowi == 0, prev_sc[...], sh)
    prev_sc[...] = zs[CHUNK - 1:CHUNK, :]
    z = zs + (sh - zs) * mu_ref[...]
    head_ones = _head_ones()
    r, lw, k_h, v, kk, b, g = _rwkv_pre(z, w0_ref[...], w2a_ref[...], a0_ref[...], g2_ref[...], kk_ref[...],
                                        ka_ref[...], head_ones)
    lw = jnp.where(pad_row, 0.0, lw)
    gla_prep()

    cl = _split_dot(_tri_incl(CHUNK), lw)
    cl_end = cl[CHUNK - 1:CHUNK, :]
    e_neg = jnp.exp(-cl)
    e_c = jnp.exp(cl_end)
    kkt = kk * jnp.exp(cl - lw)
    rt = r * jnp.exp(cl)
    bt = b * e_neg
    kt = k_h * e_neg
    bh = bt * e_c
    kh = kt * e_c

    n2 = 2 * CHUNK
    tok_r = jnp.bitwise_and(_iota((n2, n2), 0), CHUNK - 1)
    tok_c = jnp.bitwise_and(_iota((n2, n2), 1), CHUNK - 1)
    strict = tok_r > tok_c
    incl = tok_r >= tok_c

    pairs = range(RW_HEADS // 2)
    sls = [slice(j * LANES, (j + 1) * LANES) for j in pairs]
    s_kkt = [_stack2(kkt[:, sl]) for sl in sls]
    s_rt = [_stack2(rt[:, sl]) for sl in sls]
    s_v = [_stack2(v[:, sl]).astype(BF16) for sl in sls]
    rb = [jnp.concatenate([_stack2(bt[:, sl]), _stack2(kt[:, sl])], axis=0).astype(BF16) for sl in sls]
    aa = [_bdot_nt(s_kkt[j], rb[j]) for j in pairs]
    mm = [_bdot_nt(s_rt[j], rb[j]) for j in pairs]
    h_kv = [_bdot_tn(_stack2(kh[:, sls[j]]), s_v[j]) for j in pairs]
    gla_scores()
    a_b =[jnp.where(strict, aa[j][:, :n2], 0.0).astype(BF16) for j in pairs]
    a_k = [jnp.where(strict, aa[j][:, n2:], 0.0) for j in pairs]
    m_rb = [jnp.where(incl, mm[j][:, :n2], 0.0).astype(BF16) for j in pairs]
    m_rk = [jnp.where(incl, mm[j][:, n2:], 0.0) for j in pairs]
    h0 = [h_sc[j] for j in pairs]
    akv = [_bdot(a_k[j], s_v[j]) for j in pairs]
    y_kv = [_bdot(m_rk[j], s_v[j]) for j in pairs]
    y_h = [_bdot(rt[:, sls[j]], h0[j]) for j in pairs]
    x = [_bdot(s_kkt[j], h0[j]) + akv[j] for j in pairs]
    p = a_b
    x = [x[j] - _bdot(p[j], x[j]) for j in pairs]
    for step in range(5):
        p = [_bdot(p[j], p[j]).astype(BF16) for j in pairs]
        x = [x[j] + _bdot(p[j], x[j]) for j in pairs]
        if step == 1:
            gla_state()
        if step == 3:
            gla_finish()
    zq = [_bdot(m_rb[j], x[j]) for j in pairs]
    zb = [_bdot_tn(_stack2(bh[:, sls[j]]), x[j]) for j in pairs]
    for j in pairs:
        y_sc[:, sls[j]] = y_h[j] + _unstack2(y_kv[j] - zq[j])
    h_new = [_col_of_row(e_c[:, sls[j]]) * h0[j] + (h_kv[j] - zb[j]) for j in pairs]
    for j in pairs:
        h_sc[j] = h_new[j]

    o = _rwkv_post(y_sc[...], r, k_h, v, g, rk_ref[...], lw_ref[...], lb_ref[...], head_ones)
    o_ref[...] = o.astype(o_ref.dtype)

    @pl.when(last)
    def _():
        for j in pairs:
            hout_ref[0, :, sls[j]] = _unstack2(h_new[j])
        for h in heads:
            sout_ref[0, h] = gla['s_new'][h]

    @pl.when(is_meta)
    def _():
        hmeta_sc[...] = h_sc[...]
        prevmeta_sc[...] = prev_sc[...]
        smeta_sc[...] = s_sc[...]


def _prompt_recurrences(zr, zg, zb, p, q, n_batch=N_BATCH, n_chunks=N_CHUNKS, meta_block=ROW_META // CHUNK,
                        og_block=OG_BLOCK):
    row = lambda n: pl.BlockSpec((1, n), lambda i: (0, 0))
    full = lambda a, b: pl.BlockSpec((a, b), lambda i: (0, 0))
    blk = lambda i: jnp.where(i == 0, meta_block, i - 1)
    out_blk = lambda i: jnp.maximum(i - 1, 0)
    seq = lambda i: jnp.maximum(i - 1, 0) // n_chunks
    n_rows = n_batch * n_chunks * CHUNK
    return pl.pallas_call(
        functools.partial(_prompt_chunk_kernel, n_chunks=n_chunks),
        grid=(1 + n_batch * n_chunks,),
        in_specs=[pl.BlockSpec((CHUNK, W_SHIFT), lambda i: (blk(i), 0)),
                  row(W_SHIFT), row(RW_WIDTH), full(LANES, 2 * RW_WIDTH), row(RW_WIDTH), full(LANES, RW_WIDTH),
                  row(RW_WIDTH), row(RW_WIDTH), row(RW_WIDTH), row(RW_WIDTH), row(RW_WIDTH),
                  pl.BlockSpec((CHUNK, W_GLA_PAD), lambda i: (blk(i), 0)),
                  pl.BlockSpec((CHUNK, GLA_WIDTH), lambda i: (blk(i), og_block)),
                  full(LANES, GLA_KDIM), full(1, GLA_KDIM), full(1, GLA_DV)],
        out_specs=[pl.BlockSpec((CHUNK, RW_WIDTH), lambda i: (out_blk(i), 0)),
                   pl.BlockSpec((1, RW_HEAD, RW_WIDTH), lambda i: (seq(i), 0, 0)),
                   pl.BlockSpec((CHUNK, GLA_WIDTH), lambda i: (out_blk(i), 0)),
                   pl.BlockSpec((1, GLA_HEADS, GLA_DK, GLA_DV), lambda i: (seq(i), 0, 0, 0))],
        out_shape=[jax.ShapeDtypeStruct((n_rows, RW_WIDTH), BF16),
                   jax.ShapeDtypeStruct((n_batch, RW_HEAD, RW_WIDTH), F32),
                   jax.ShapeDtypeStruct((n_rows, GLA_WIDTH), BF16),
                   jax.ShapeDtypeStruct((n_batch, GLA_HEADS, GLA_DK, GLA_DV), F32)],
        scratch_shapes=[pltpu.VMEM((1, W_SHIFT), F32),
                        pltpu.VMEM((RW_HEADS // 2, LANES, LANES), F32),
                        pltpu.VMEM((RW_HEADS // 2, LANES, LANES), F32),
                        pltpu.VMEM((1, W_SHIFT), F32),
                        pltpu.VMEM((CHUNK, RW_WIDTH), F32),
                        pltpu.VMEM((GLA_HEADS, GLA_DK, GLA_DV), F32),
                        pltpu.VMEM((GLA_HEADS, GLA_DK, GLA_DV), F32)],
        compiler_params=_cparams(),
        name="prompt_recurrences",
    )(zr, p['mu'], p['w0'], p['w2a'], p['a0'], p['g2'], p['k_k'], p['k_a'], p['r_k'], p['lnx_w'], p['lnx_b'],
      zg, zb, q['gk_up'], q['gk_b'], q['norm_w'])


def _rwkv_sample_pre_kernel(zs_ref, prev_ref, mu_ref, w0_ref, w2a_ref, a0_ref, g2_ref, kk_ref, ka_ref,
                            r_ref, kh_ref, v_ref, g_ref, rt_ref, wt_ref, kht_ref, vt_ref, nkkt_ref, bt_ref):
    zs = zs_ref[...]
    z = zs + (prev_ref[...] - zs) * mu_ref[...]
    r, lw, k_h, v, kk, b, g = _rwkv_pre(z, w0_ref[...], w2a_ref[...], a0_ref[...], g2_ref[...], kk_ref[...],
                                        ka_ref[...], _head_ones())
    r_ref[...] = r
    kh_ref[...] = k_h
    v_ref[...] = v
    g_ref[...] = g
    rt_ref[...] = r.T
    wt_ref[...] = jnp.exp(lw).T
    kht_ref[...] = k_h.T
    vt_ref[...] = v.T
    nkkt_ref[...] = (-kk).T
    bt_ref[...] = b.T


def _rwkv_sample_pre(zr, prev, p):
    row = lambda n: pl.BlockSpec((1, n), lambda i: (0, 0))
    full = lambda a, b: pl.BlockSpec((a, b), lambda i: (0, 0))
    vec = jax.ShapeDtypeStruct((N_SAMPLE, RW_WIDTH), F32)
    vec_t = jax.ShapeDtypeStruct((RW_WIDTH, N_SAMPLE), F32)
    return pl.pallas_call(
        _rwkv_sample_pre_kernel,
        grid=(1,),
        in_specs=[pl.BlockSpec((N_SAMPLE, W_SHIFT), lambda i: (ROW_SAMPLE // N_SAMPLE, 0)),
                  full(N_SAMPLE, W_SHIFT),
                  row(W_SHIFT), row(RW_WIDTH), full(LANES, 2 * RW_WIDTH), row(RW_WIDTH), full(LANES, RW_WIDTH),
                  row(RW_WIDTH), row(RW_WIDTH)],
        out_specs=[full(N_SAMPLE, RW_WIDTH)] * 4 + [full(RW_WIDTH, N_SAMPLE)] * 6,
        out_shape=[vec] * 4 + [vec_t] * 6,
        compiler_params=_cparams(),
        name="rwkv_sample_pre",
    )(zr, prev, p['mu'], p['w0'], p['w2a'], p['a0'], p['g2'], p['k_k'], p['k_a'])


RW_STEP_HEADS = 4


def _rwkv_sample_step_kernel(s_ref, r_ref, w_ref, kh_ref, v_ref, nkk_ref, b_ref, so_ref, y_ref):
    for h in range(RW_STEP_HEADS):
        s = s_ref[h]
        sa = jnp.sum(s * nkk_ref[h][None], axis=1, keepdims=True)
        s_new = s * w_ref[h][None] + sa * b_ref[h][None] + v_ref[h] * kh_ref[h][None]
        so_ref[h] = s_new
        y_ref[h] = jnp.sum(s_new * r_ref[h][None], axis=1, keepdims=True)


def _rwkv_sample_step(state_t, r_t, w_t, kh_t, v_t, nkk_t, b_t):
    kvec = lambda t: t.reshape(RW_HEADS, RW_HEAD, N_SAMPLE)
    kspec = pl.BlockSpec((RW_STEP_HEADS, RW_HEAD, N_SAMPLE), lambda i: (i, 0, 0))
    vspec = pl.BlockSpec((RW_STEP_HEADS, RW_HEAD, 1, N_SAMPLE), lambda i: (i, 0, 0, 0))
    sspec = pl.BlockSpec((RW_STEP_HEADS, RW_HEAD, RW_HEAD, N_SAMPLE), lambda i: (i, 0, 0, 0))
    s_new, y = pl.pallas_call(
        _rwkv_sample_step_kernel,
        grid=(RW_HEADS // RW_STEP_HEADS,),
        in_specs=[sspec, kspec, kspec, kspec, vspec, kspec, kspec],
        out_specs=[sspec, vspec],
        out_shape=[jax.ShapeDtypeStruct(state_t.shape, F32),
                   jax.ShapeDtypeStruct((RW_HEADS, RW_HEAD, 1, N_SAMPLE), F32)],
        compiler_params=_cparams(),
        name="rwkv_sample_step",
    )(state_t, kvec(r_t), kvec(w_t), kvec(kh_t), v_t.reshape(RW_HEADS, RW_HEAD, 1, N_SAMPLE), kvec(nkk_t), kvec(b_t))
    return s_new, y.reshape(RW_WIDTH, N_SAMPLE)


def _rwkv_sample_post_kernel(yt_ref, r_ref, kh_ref, v_ref, g_ref, rk_ref, lw_ref, lb_ref, o_ref):
    o = _rwkv_post(yt_ref[...].T, r_ref[...], kh_ref[...], v_ref[...], g_ref[...], rk_ref[...], lw_ref[...],
                   lb_ref[...], _head_ones())
    o_ref[0:N_SAMPLE, :] = o.astype(o_ref.dtype)
    o_ref[N_SAMPLE:T_EXT, :] = jnp.zeros((T_EXT - N_SAMPLE, RW_WIDTH), o_ref.dtype)


def _rwkv_sample_post(y_t, r, kh, v, g, p):
    row = lambda n: pl.BlockSpec((1, n), lambda i: (0, 0))
    full = lambda a, b: pl.BlockSpec((a, b), lambda i: (0, 0))
    return pl.pallas_call(
        _rwkv_sample_post_kernel,
        grid=(1,),
        in_specs=[full(RW_WIDTH, N_SAMPLE)] + [full(N_SAMPLE, RW_WIDTH)] * 4 + [row(RW_WIDTH)] * 3,
        out_specs=full(T_EXT, RW_WIDTH),
        out_shape=jax.ShapeDtypeStruct((T_EXT, RW_WIDTH), BF16),
        compiler_params=_cparams(),
        name="rwkv_sample_post",
    )(y_t, r, kh, v, g, p['r_k'], p['lnx_w'], p['lnx_b'])


def _gla_logg(zg, gk_up, gk_b):
    gkd = zg[:, 2 * GLA_KDIM + GLA_WIDTH:2 * GLA_KDIM + GLA_WIDTH + LANES]
    x = _bdot(gkd, gk_up) + gk_b
    return (jnp.minimum(x, 0.0) - jnp.log(1.0 + jnp.exp(-jnp.abs(x)))) * (1.0 / GLA_NORMALIZER)


def _gla_post(o, og_act, norm_w):
    outs = []
    for h in range(GLA_HEADS):
        oh = o[:, h * GLA_DV:(h + 1) * GLA_DV]
        ms = jnp.mean(oh * oh, axis=-1, keepdims=True)
        outs.append(oh * lax.rsqrt(ms + RMS_EPS) * norm_w)
    return jnp.concatenate(outs, axis=1) * og_act


def _gla_sample_pre_kernel(zg_ref, gkup_ref, gkb_ref, q_ref, k_ref, eg_ref):
    zg = zg_ref[...]
    logg = _gla_logg(zg, gkup_ref[...], gkb_ref[...])
    q_ref[...] = zg[:, 0:GLA_KDIM] * (GLA_DK ** -0.5)
    k_ref[...] = zg[:, GLA_KDIM:2 * GLA_KDIM]
    eg_ref[...] = jnp.exp(logg)


def _gla_sample_pre(zg, p):
    full = lambda a, b: pl.BlockSpec((a, b), lambda i: (0, 0))
    vec = jax.ShapeDtypeStruct((N_SAMPLE, GLA_KDIM), F32)
    return pl.pallas_call(
        _gla_sample_pre_kernel,
        grid=(1,),
        in_specs=[pl.BlockSpec((N_SAMPLE, W_GLA_PAD), lambda i: (ROW_SAMPLE // N_SAMPLE, 0)),
                  full(LANES, GLA_KDIM), full(1, GLA_KDIM)],
        out_specs=[full(N_SAMPLE, GLA_KDIM)] * 3,
        out_shape=[vec] * 3,
        compiler_params=_cparams(),
        name="gla_sample_pre",
    )(zg, p['gk_up'], p['gk_b'])


GLA_STEP_BATCH = 16


def _gla_sample_step_kernel(s_ref, q_ref, k_ref, eg_ref, v_ref, so_ref, o_ref):
    eye = _iota((GLA_DK, GLA_DK), 0) == _iota((GLA_DK, GLA_DK), 1)
    for b in range(GLA_STEP_BATCH):
        s = s_ref[b]
        col = lambda ref: jnp.sum(jnp.where(eye, ref[b], 0.0), axis=-1, keepdims=True)
        s_new = col(eg_ref) * s + col(k_ref) * v_ref[b]
        so_ref[b] = s_new
        o_ref[b] = jnp.sum(col(q_ref) * s_new, axis=-2, keepdims=True)


def _gla_sample_step(state, q, k, eg, v):
    kv = lambda t: t.reshape(N_SAMPLE, GLA_HEADS, 1, GLA_DK)
    kspec = pl.BlockSpec((GLA_STEP_BATCH, GLA_HEADS, 1, GLA_DK), lambda i: (i, 0, 0, 0))
    vspec = pl.BlockSpec((GLA_STEP_BATCH, GLA_HEADS, 1, GLA_DV), lambda i: (i, 0, 0, 0))
    sspec = pl.BlockSpec((GLA_STEP_BATCH, GLA_HEADS, GLA_DK, GLA_DV), lambda i: (i, 0, 0, 0))
    s_new, o = pl.pallas_call(
        _gla_sample_step_kernel,
        grid=(N_SAMPLE // GLA_STEP_BATCH,),
        in_specs=[sspec, kspec, kspec, kspec, vspec],
        out_specs=[sspec, vspec],
        out_shape=[jax.ShapeDtypeStruct(state.shape, F32),
                   jax.ShapeDtypeStruct((N_SAMPLE, GLA_HEADS, 1, GLA_DV), F32)],
        compiler_params=_cparams(),
        name="gla_sample_step",
    )(state, kv(q), kv(k), kv(eg), v.reshape(N_SAMPLE, GLA_HEADS, 1, GLA_DV))
    return s_new, o.reshape(N_SAMPLE, GLA_WIDTH)


def _gla_sample_post_kernel(o_ref, og_ref, nw_ref, out_ref):
    o = _gla_post(o_ref[...], og_ref[0:N_SAMPLE, :].astype(F32), nw_ref[...])
    out_ref[0:N_SAMPLE, :] = o.astype(out_ref.dtype)
    out_ref[N_SAMPLE:T_EXT, :] = jnp.zeros((T_EXT - N_SAMPLE, GLA_WIDTH), out_ref.dtype)


def _gla_sample_post(o, zb, p, og_block=OG_BLOCK):
    full = lambda a, b: pl.BlockSpec((a, b), lambda i: (0, 0))
    return pl.pallas_call(
        _gla_sample_post_kernel,
        grid=(1,),
        in_specs=[full(N_SAMPLE, GLA_WIDTH),
                  pl.BlockSpec((T_EXT, GLA_WIDTH), lambda i: (T_MAIN // T_EXT, og_block)),
                  full(1, GLA_DV)],
        out_specs=full(T_EXT, GLA_WIDTH),
        out_shape=jax.ShapeDtypeStruct((T_EXT, GLA_WIDTH), BF16),
        compiler_params=_cparams(),
        name="gla_sample_post",
    )(o, zb, p['norm_w'])


def _mix_route_kernel(or_ref, ore_ref, og_ref, oge_ref, gates_ref, xp_ref, xe_ref, pr_ref, pg_ref, wo_ref, n2_ref,
                      wrh_ref, wrl_ref, br_ref, h1_ref, xn_ref, info_ref, cnt_ref, carry_sc):
    i = pl.program_id(0)
    is_main = i < T_MAIN // TM_DENSE

    @pl.when(i == 0)
    def _():
        carry_sc[...] = jnp.zeros_like(carry_sc)

    sig_r = gates_ref[:, 0:D_MODEL].astype(F32)
    sig_g = gates_ref[:, D_MODEL:2 * D_MODEL].astype(F32)
    o_r = jnp.where(is_main, or_ref[...], ore_ref[...])
    o_g = jnp.where(is_main, og_ref[...], oge_ref[...])
    m = (sig_r * jnp.dot(o_r, pr_ref[...], preferred_element_type=F32)
         + sig_g * jnp.dot(o_g, pg_ref[...], preferred_element_type=F32))
    h = jnp.where(is_main, xp_ref[...], xe_ref[...])
    h1 = h + jnp.dot(m.astype(BF16), wo_ref[...], preferred_element_type=F32)
    h1_ref[...] = h1
    ms = jnp.mean(h1 * h1, axis=-1, keepdims=True)
    xn = h1 * lax.rsqrt(ms + RMS_EPS) * n2_ref[...]
    xn_ref[...] = xn
    xh = xn.astype(BF16)
    xl = (xn - xh.astype(F32)).astype(BF16)
    lg = (jnp.dot(xh, wrh_ref[...], preferred_element_type=F32)
          + jnp.dot(xh, wrl_ref[...], preferred_element_type=F32)
          + jnp.dot(xl, wrh_ref[...], preferred_element_type=F32)) + br_ref[...]

    lane = _iota(lg.shape, 1)
    lanef = lane.astype(F32)
    is_g = jnp.logical_and(lane >= N_EXPERTS, lane < N_EXPERTS + N_GROUPS)
    gl = jnp.where(is_g, lg, NEG_BIG)
    gmax = jnp.max(gl, axis=-1, keepdims=True)
    gsel = jnp.min(jnp.where(jnp.logical_and(is_g, gl == gmax), lanef, 1e9), axis=-1, keepdims=True) - N_EXPERTS
    p_g = 1.0 / jnp.sum(jnp.exp(gl - gmax), axis=-1, keepdims=True)
    grp = (lane // EXPERTS_PER_GROUP).astype(F32)
    in_grp = jnp.logical_and(lane < N_EXPERTS, grp == gsel)
    el = jnp.where(in_grp, lg, NEG_BIG)
    m1 = jnp.max(el, axis=-1, keepdims=True)
    i1 = jnp.min(jnp.where(jnp.logical_and(in_grp, el == m1), lanef, 1e9), axis=-1, keepdims=True)
    in2 = jnp.logical_and(in_grp, lanef != i1)
    el2 = jnp.where(in2, lg, NEG_BIG)
    m2 = jnp.max(el2, axis=-1, keepdims=True)
    i2 = jnp.min(jnp.where(jnp.logical_and(in2, el2 == m2), lanef, 1e9), axis=-1, keepdims=True)
    e2 = jnp.exp(m2 - m1)
    w1 = p_g / (1.0 + e2)
    w2 = p_g * e2 / (1.0 + e2)

    oh1 = lanef == i1
    oh2 = lanef == i2
    cnt = jnp.where(jnp.logical_or(oh1, oh2), 1.0, 0.0)
    tm = cnt.shape[0]
    lstrict = (_iota((tm, tm), 0) > _iota((tm, tm), 1)).astype(BF16)
    before = jnp.dot(lstrict, cnt.astype(BF16), preferred_element_type=F32) + carry_sc[...]
    rank1 = jnp.sum(jnp.where(oh1, before, 0.0), axis=-1, keepdims=True)
    rank2 = jnp.sum(jnp.where(oh2, before, 0.0), axis=-1, keepdims=True)
    carry_sc[...] = carry_sc[...] + jnp.sum(cnt, axis=0, keepdims=True)
    cnt_ref[...] = carry_sc[...]
    info = jnp.where(lane == 0, i1, jnp.where(lane == 1, i2, jnp.where(lane == 2, w1, jnp.where(
        lane == 3, w2, jnp.where(lane == 4, rank1, jnp.where(lane == 5, rank2, 0.0))))))
    info_ref[...] = info


def _mix_route(o_r, o_r_ext, o_g, o_g_ext, zb, xp, xe, p):
    nmain = T_MAIN // TM_DENSE
    tile = lambda n: pl.BlockSpec((TM_DENSE, n), lambda i: (i, 0))
    main = lambda n: pl.BlockSpec((TM_DENSE, n), lambda i: (jnp.minimum(i, nmain - 1), 0))
    ext = lambda n: pl.BlockSpec((T_EXT, n), lambda i: (0, 0))
    const = lambda a, b: pl.BlockSpec((a, b), lambda i: (0, 0), pipeline_mode=pl.Buffered(1))
    return pl.pallas_call(
        _mix_route_kernel,
        grid=(T_ALL // TM_DENSE,),
        in_specs=[main(RW_WIDTH), ext(RW_WIDTH), main(GLA_WIDTH), ext(GLA_WIDTH), tile(2 * D_MODEL),
                  main(D_MODEL), ext(D_MODEL),
                  const(RW_WIDTH, D_MODEL), const(GLA_WIDTH, D_MODEL), const(D_MODEL, D_MODEL),
                  const(1, D_MODEL), const(D_MODEL, LANES), const(D_MODEL, LANES), const(1, LANES)],
        out_specs=[tile(D_MODEL), tile(D_MODEL), tile(LANES), pl.BlockSpec((1, LANES), lambda i: (0, 0))],
        out_shape=[jax.ShapeDtypeStruct((T_ALL, D_MODEL), F32),
                   jax.ShapeDtypeStruct((T_ALL, D_MODEL), F32),
                   jax.ShapeDtypeStruct((T_ALL, LANES), F32),
                   jax.ShapeDtypeStruct((1, LANES), F32)],
        scratch_shapes=[pltpu.VMEM((1, LANES), F32)],
        compiler_params=_cparams(),
        name="mix_route",
    )(o_r, o_r_ext, o_g, o_g_ext, zb, xp, xe, p['p_rwkv'], p['p_gla'], p['w_out'], p['norm2_w'], p['wr_hi'],
      p['wr_lo'], p['b_route'])


def _dispatch_kernel(info_ref, cnt_ref, xq_ref, xs_hbm, d_ref, te_ref, nt_ref,
                     d_vm, d_sm, tab_vm, tab_sm, zbuf, ring, sem_t, sem_x, sem_e, sem_u):
    i = pl.program_id(0)

    def zero_tile(t, sem):
        return pltpu.make_async_copy(zbuf, xs_hbm.at[pl.ds(t * TM_MOE, TM_MOE), :], sem)

    def each(fn):
        def body(r, carry):
            fn(r)
            return carry
        return body

    lane1 = _iota((1, LANES), 1)
    cnt = jnp.where(lane1 < N_EXPERTS, cnt_ref[...], 0.0)
    tiles = jnp.floor((cnt + (TM_MOE - 1)) * (1.0 / TM_MOE))
    upper = (_iota((LANES, LANES), 0) < _iota((LANES, LANES), 1)).astype(BF16)
    tile_start = jnp.dot(jnp.broadcast_to(tiles, (8, LANES)).astype(BF16), upper,
                         preferred_element_type=F32)[0:1, :]
    base = tile_start * TM_MOE

    info = info_ref[...]
    lanef = _iota(info.shape, 1).astype(F32)
    pick = lambda col: jnp.sum(jnp.where(lanef == info[:, col:col + 1], base, 0.0), axis=-1, keepdims=True)
    d1 = pick(0) + info[:, 4:5]
    d2 = pick(1) + info[:, 5:6]
    tm = info.shape[0]
    eye = _iota((tm, tm), 0) == _iota((tm, tm), 1)
    to_row = lambda col: jnp.sum(jnp.where(eye, col, 0.0), axis=0, keepdims=True)
    d_row = jnp.concatenate([to_row(d1), to_row(d2)], axis=1).astype(jnp.int32)
    d_ref[0] = d_row
    d_vm[...] = d_row

    @pl.when(i == 0)
    def _():
        tile_end = tile_start + tiles
        n_tiles = jnp.sum(tiles, axis=-1, keepdims=True)
        eye_l = _iota((LANES, LANES), 0) == _iota((LANES, LANES), 1)
        end_col = jnp.sum(jnp.where(eye_l, tile_end, 0.0), axis=-1, keepdims=True)
        rowl = _iota((LANES, LANES), 0)
        tile_f = _iota((LANES, LANES), 1).astype(F32)
        te = jnp.sum(jnp.where(jnp.logical_and(rowl < N_EXPERTS, end_col <= tile_f), 1.0, 0.0), axis=0,
                     keepdims=True)
        last_e = jnp.max(jnp.where(tiles > 0.0, lane1.astype(F32), 0.0), axis=-1, keepdims=True)
        te = jnp.where(lane1.astype(F32) < n_tiles, jnp.minimum(te, N_EXPERTS - 1.0), last_e)
        te_ref[...] = te.astype(jnp.int32)
        nt_ref[...] = jnp.broadcast_to(n_tiles, (1, LANES)).astype(jnp.int32)
        zbuf[...] = jnp.zeros_like(zbuf)
        last_tile = (tile_end - 1.0).astype(jnp.int32)
        nt_row = jnp.broadcast_to(n_tiles, (1, LANES)).astype(jnp.int32)
        tab_vm[...] = jnp.concatenate([last_tile, tiles.astype(jnp.int32), nt_row, jnp.zeros((5, LANES), jnp.int32)],
                                      axis=0)
        tab_copy = pltpu.make_async_copy(tab_vm, tab_sm, sem_t)
        tab_copy.start()
        tab_copy.wait()

        for e in range(N_EXPERTS):
            @pl.when(tab_sm[1, e] > 0)
            def _(e=e):
                zero_tile(tab_sm[0, e], sem_e).start()
        lax.fori_loop(tab_sm[2, 0], NT_MOE, each(lambda t: zero_tile(t, sem_u).start()), 0)
        for e in range(N_EXPERTS):
            @pl.when(tab_sm[1, e] > 0)
            def _(e=e):
                zero_tile(tab_sm[0, e], sem_e).wait()

    d_copy = pltpu.make_async_copy(d_vm, d_sm, sem_t)
    d_copy.start()
    d_copy.wait()

    slot = lax.rem(i, 2)
    ring[slot] = xq_ref[...]
    for t in range(tm):
        for half in range(2):
            pltpu.make_async_copy(ring.at[slot, pl.ds(t, 1), :], xs_hbm.at[pl.ds(d_sm[0, half * tm + t], 1), :],
                                  sem_x.at[slot]).start(priority=half)

    def wait_rows(s):
        for _ in range(2):
            pltpu.make_async_copy(ring.at[s], xs_hbm.at[pl.ds(0, tm), :], sem_x.at[s]).wait()

    @pl.when(i >= 1)
    def _():
        wait_rows(1 - slot)

    @pl.when(i == pl.num_programs(0) - 1)
    def _():
        wait_rows(slot)
        lax.fori_loop(tab_sm[2, 0], NT_MOE, each(lambda t: zero_tile(t, sem_u).wait()), 0)


def _dispatch(info, counts, xq):
    nt = T_ALL // TM_DENSE
    return pl.pallas_call(
        _dispatch_kernel,
        grid=(nt,),
        in_specs=[pl.BlockSpec((TM_DENSE, LANES), lambda i: (i, 0)),
                  pl.BlockSpec((1, LANES), lambda i: (0, 0)),
                  pl.BlockSpec((TM_DENSE, D_MODEL), lambda i: (i, 0))],
        out_specs=[pl.BlockSpec(memory_space=pl.ANY),
                   pl.BlockSpec((1, 1, 2 * TM_DENSE), lambda i: (i, 0, 0)),
                   pl.BlockSpec((1, LANES), lambda i: (0, 0)),
                   pl.BlockSpec((1, LANES), lambda i: (0, 0))],
        out_shape=[jax.ShapeDtypeStruct((NT_MOE * TM_MOE, D_MODEL), F32),
                   jax.ShapeDtypeStruct((nt, 1, 2 * TM_DENSE), jnp.int32),
                   jax.ShapeDtypeStruct((1, LANES), jnp.int32),
                   jax.ShapeDtypeStruct((1, LANES), jnp.int32)],
        scratch_shapes=[pltpu.VMEM((1, 2 * TM_DENSE), jnp.int32),
                        pltpu.SMEM((1, 2 * TM_DENSE), jnp.int32),
                        pltpu.VMEM((8, LANES), jnp.int32),
                        pltpu.SMEM((8, LANES), jnp.int32),
                        pltpu.VMEM((TM_MOE, D_MODEL), F32),
                        pltpu.VMEM((2, TM_DENSE, D_MODEL), F32),
                        pltpu.SemaphoreType.DMA(()),
                        pltpu.SemaphoreType.DMA((2,)),
                        pltpu.SemaphoreType.DMA(()),
                        pltpu.SemaphoreType.DMA(())],
        compiler_params=_cparams(),
        name="moe_dispatch",
    )(info, counts, xq)


def _moe_kernel(te_ref, nt_ref, xs_ref, w1_hbm, w3_hbm, w2_hbm, o_ref, w1r, w3r, w2r, w1b, w3b, w2b, sem):
    i = pl.program_id(0)
    n_tiles = nt_ref[0]

    def weight_copies(e):
        return [pltpu.make_async_copy(w1_hbm.at[e], w1r, sem), pltpu.make_async_copy(w3_hbm.at[e], w3r, sem),
                pltpu.make_async_copy(w2_hbm.at[e], w2r, sem)]

    @pl.when(i < n_tiles)
    def _():
        e = te_ref[i]

        @pl.when(i == 0)
        def _():
            for c in weight_copies(e):
                c.start()

        @pl.when(jnp.logical_or(i == 0, e != te_ref[jnp.maximum(i - 1, 0)]))
        def _():
            for c in weight_copies(e):
                c.wait()
            w1b[...] = w1r[...].astype(BF16)
            w3b[...] = w3r[...].astype(BF16)
            w2b[...] = w2r[...].astype(BF16)
            nxt = lax.while_loop(
                lambda j: jnp.logical_and(j < n_tiles, te_ref[jnp.minimum(j, NT_MOE - 1)] == e),
                lambda j: j + 1, i + 1)

            @pl.when(nxt < n_tiles)
            def _():
                for c in weight_copies(te_ref[jnp.minimum(nxt, NT_MOE - 1)]):
                    c.start()

        xb = xs_ref[...].astype(BF16)
        h1 = jnp.dot(xb, w1b[...], preferred_element_type=F32)
        h3 = jnp.dot(xb, w3b[...], preferred_element_type=F32)
        hh = (h1 * jax.nn.sigmoid(h1) * h3).astype(BF16)
        o_ref[...] = jnp.dot(hh, w2b[...], preferred_element_type=F32)

    @pl.when(i >= nt_ref[0])
    def _():
        o_ref[...] = jnp.zeros_like(o_ref)


def _moe(tile_expert, n_tiles, xs, w1, w3, w2):
    grid_spec = pltpu.PrefetchScalarGridSpec(
        num_scalar_prefetch=2,
        grid=(NT_MOE,),
        in_specs=[pl.BlockSpec((TM_MOE, D_MODEL), lambda i, te, nt: (jnp.minimum(i, nt[0] - 1), 0)),
                  pl.BlockSpec(memory_space=pl.ANY),
                  pl.BlockSpec(memory_space=pl.ANY),
                  pl.BlockSpec(memory_space=pl.ANY)],
        out_specs=pl.BlockSpec((TM_MOE, D_MODEL), lambda i, te, nt: (i, 0)),
        scratch_shapes=[pltpu.VMEM((D_MODEL, D_EXPERT), F32),
                        pltpu.VMEM((D_MODEL, D_EXPERT), F32),
                        pltpu.VMEM((D_EXPERT, D_MODEL), F32),
                        pltpu.VMEM((D_MODEL, D_EXPERT), BF16),
                        pltpu.VMEM((D_MODEL, D_EXPERT), BF16),
                        pltpu.VMEM((D_EXPERT, D_MODEL), BF16),
                        pltpu.SemaphoreType.DMA(())],
    )
    return pl.pallas_call(
        _moe_kernel,
        grid_spec=grid_spec,
        out_shape=jax.ShapeDtypeStruct((NT_MOE * TM_MOE, D_MODEL), F32),
        compiler_params=_cparams(),
        name="moe_experts",
    )(tile_expert, n_tiles, xs, w1, w3, w2)


def _combine_kernel(dcur_ref, dnext_ref, h1_ref, info_ref, fw_ref, eo_hbm, yp_ref, ye_ref, buf, sem):
    i = pl.program_id(0)
    nmain = T_MAIN // TM_DENSE
    slot = lax.rem(i, 2)

    def row_copy(d_ref, s, t):
        return pltpu.make_async_copy(eo_hbm.at[pl.ds(d_ref[0, 0, t], 1), :], buf.at[s, pl.ds(t, 1), :], sem.at[s])

    def issue_all(d_ref, s):
        for t in range(2 * TM_DENSE):
            row_copy(d_ref, s, t).start(priority=t % 2)

    @pl.when(i == 0)
    def _():
        issue_all(dcur_ref, slot)

    @pl.when(i + 1 < pl.num_programs(0))
    def _():
        issue_all(dnext_ref, 1 - slot)

    pltpu.make_async_copy(eo_hbm.at[pl.ds(0, 2 * TM_DENSE), :], buf.at[slot], sem.at[slot]).wait()

    info = info_ref[...]
    y = h1_ref[...] + info[:, 2:3] * buf[slot, 0:TM_DENSE, :] + info[:, 3:4] * buf[slot, TM_DENSE:2 * TM_DENSE, :]
    ms = jnp.mean(y * y, axis=-1, keepdims=True)
    out = y * lax.rsqrt(ms + RMS_EPS) * fw_ref[...]

    @pl.when(i < nmain)
    def _():
        yp_ref[...] = out

    @pl.when(i == nmain)
    def _():
        ye_ref[...] = out


def _combine(dst, h1, info, fw, eo):
    nmain = T_MAIN // TM_DENSE
    return pl.pallas_call(
        _combine_kernel,
        grid=(T_ALL // TM_DENSE,),
        in_specs=[pl.BlockSpec((1, 1, 2 * TM_DENSE), lambda i: (i, 0, 0), memory_space=pltpu.SMEM),
                  pl.BlockSpec((1, 1, 2 * TM_DENSE), lambda i: (jnp.minimum(i + 1, nmain), 0, 0),
                               memory_space=pltpu.SMEM),
                  pl.BlockSpec((TM_DENSE, D_MODEL), lambda i: (i, 0)),
                  pl.BlockSpec((TM_DENSE, LANES), lambda i: (i, 0)),
                  pl.BlockSpec((1, D_MODEL), lambda i: (0, 0)),
                  pl.BlockSpec(memory_space=pl.ANY)],
        out_specs=[pl.BlockSpec((TM_DENSE, D_MODEL), lambda i: (jnp.minimum(i, nmain - 1), 0)),
                   pl.BlockSpec((T_EXT, D_MODEL), lambda i: (0, 0))],
        out_shape=[jax.ShapeDtypeStruct((T_MAIN, D_MODEL), F32),
                   jax.ShapeDtypeStruct((T_EXT, D_MODEL), F32)],
        scratch_shapes=[pltpu.VMEM((2, 2 * TM_DENSE, D_MODEL), F32), pltpu.SemaphoreType.DMA((2,))],
        compiler_params=_cparams(),
        name="moe_combine",
    )(dst, dst, h1, info, fw, eo)


def kernel(x_prompt, x_sample, state_rwkv, state_shift, state_gla, meta_tokens, norm1_w, w_in, mu_shift, rw_w0,
           rw_w2, rw_a0, rw_a2, rw_g2, rw_k_k, rw_k_a, rw_r_k, rw_lnx_w, rw_lnx_b, gla_gk_up, gla_gk_b, gla_norm_w,
           p_rwkv, p_gla, w_out, norm2_w, moe_w_group, moe_b_group, moe_w_router, moe_b_router, moe_w1, moe_w3,
           moe_w2, final_norm_w):
    w_t = w_in[0].T
    row = lambda t: t.reshape(1, -1)
    w2a = jnp.zeros((LANES, 2 * RW_WIDTH), F32)
    w2a = w2a.at[0:64, 0:RW_WIDTH].set(rw_w2[0]).at[64:128, RW_WIDTH:].set(rw_a2[0])
    rw = dict(mu=row(mu_shift[0][_SHIFT_PERM]), w0=row(rw_w0[0]), w2a=w2a, a0=row(rw_a0[0]), g2=rw_g2[0],
              k_k=row(rw_k_k[0]), k_a=row(rw_k_a[0]), r_k=row(rw_r_k[0]), lnx_w=row(rw_lnx_w[0]),
              lnx_b=row(rw_lnx_b[0]))
    gk_up = jnp.zeros((LANES, GLA_KDIM), F32).at[0:GLA_LORA].set(gla_gk_up[0])
    gl = dict(gk_up=gk_up, gk_b=row(gla_gk_b[0]), norm_w=row(gla_norm_w[0]))
    w_route = jnp.zeros((D_MODEL, LANES), F32)
    w_route = w_route.at[:, 0:N_EXPERTS].set(moe_w_router[0]).at[:, N_EXPERTS:N_EXPERTS + N_GROUPS].set(moe_w_group[0])
    wr_hi = w_route.astype(BF16)
    wr_lo = (w_route - wr_hi.astype(F32)).astype(BF16)
    b_route = jnp.zeros((1, LANES), F32)
    b_route = b_route.at[0, 0:N_EXPERTS].set(moe_b_router[0]).at[0, N_EXPERTS:N_EXPERTS + N_GROUPS].set(moe_b_group[0])
    mix = dict(p_rwkv=p_rwkv[0].astype(BF16), p_gla=p_gla[0].astype(BF16), w_out=w_out[0].astype(BF16),
               norm2_w=row(norm2_w[0]), wr_hi=wr_hi, wr_lo=wr_lo, b_route=b_route)

    xp = x_prompt.reshape(T_MAIN, D_MODEL)
    xe = jnp.concatenate([x_sample[:, 0, :], jnp.zeros((META_PAD, D_MODEL), F32), meta_tokens,
                          jnp.zeros((T_ALL - ROW_META - CHUNK, D_MODEL), F32)], axis=0)

    xn = _norm1(xp, xe, row(norm1_w[0]))
    zr = _proj(xn, w_t, _RWKV_TILES, W_SHIFT // 2, F32, name="proj_rwkv")
    zg = _proj(xn, w_t, _GLA_TILES, W_GLA_PAD // 2, F32, name="proj_gla")
    zb = _proj(xn, w_t, _GATE_TILES, GLA_WIDTH, BF16, silu_from=2 * D_MODEL // GLA_WIDTH, name="proj_gates")

    o_r, h_fin, o_g, s_fin = _prompt_recurrences(zr, zg, zb, rw, gl)

    r_s, kh_s, v_s, g_s, r_t, w_t, kh_t, v_t, nkk_t, b_t = _rwkv_sample_pre(zr, state_shift[0][:, _SHIFT_PERM], rw)
    rw_new_t, y_t = _rwkv_sample_step(state_rwkv[0].transpose(1, 2, 3, 0), r_t, w_t, kh_t, v_t, nkk_t, b_t)
    rw_new = rw_new_t.transpose(3, 0, 1, 2)
    o_r_ext = _rwkv_sample_post(y_t, r_s, kh_s, v_s, g_s, rw)
    q_s, k_s, eg_s = _gla_sample_pre(zg, gl)
    v_gs = zg[ROW_SAMPLE:ROW_SAMPLE + N_SAMPLE, 2 * GLA_KDIM:2 * GLA_KDIM + GLA_WIDTH]
    gla_new, og_s = _gla_sample_step(state_gla[0], q_s, k_s, eg_s, v_gs)
    o_g_ext = _gla_sample_post(og_s, zb, gl)

    h1, xn2, info, counts = _mix_route(o_r, o_r_ext, o_g, o_g_ext, zb, xp, xe, mix)
    xs, dst, tile_expert, n_tiles = _dispatch(info, counts, xn2)
    eo = _moe(tile_expert[0, :NT_MOE], n_tiles[0, :1], xs, moe_w1[0], moe_w3[0], moe_w2[0])
    y_p, y_e = _combine(dst, h1, info, row(final_norm_w), eo)

    y_prompt = y_p.reshape(N_BATCH, SEQ, D_MODEL)
    y_sample = y_e[0:N_SAMPLE].reshape(N_SAMPLE, 1, D_MODEL)
    new_rwkv_prompt = h_fin.reshape(N_BATCH, RW_HEAD, RW_HEADS, RW_HEAD).transpose(0, 2, 3, 1)[None]
    shift_rows = jnp.concatenate([zr[(b + 1) * SEQ - 1:(b + 1) * SEQ] for b in range(N_BATCH)]
                                 + [zr[ROW_SAMPLE:ROW_SAMPLE + N_SAMPLE]], axis=0)
    shift_rows = shift_rows[:, _SHIFT_INV]
    return (y_prompt, y_sample, new_rwkv_prompt, shift_rows[None, 0:N_BATCH], s_fin[None],
            rw_new[None], shift_rows[None, N_BATCH:], gla_new[None])
```
